```python
import jax, jax.numpy as jnp
from jax import lax
import numpy as np

D_MODEL = 1024
BATCH = 4
SEQ = 8192
DEPTH = 4
DEC_BATCH = 8
DEC_SEQ = 16
PAST_LEN = 2048

CHUNK = 64
N_MIXERS = 3
H_A = 8
DH_A = 128
D_A = H_A * DH_A
FOX_BIAS_INIT = 2.0
Q_BLOCK = 128
D_RNN = 1024
N_LRU_BLOCKS = 8
LRU_BLOCK = D_RNN // N_LRU_BLOCKS
CONV_W = 4
LRU_C = 8.0
H_C = 8
DH_C = 128
D_C = H_C * DH_C
H_IDX = 8
D_IDX = 64
TOPK_MAX = 256
ALPHA = (2 * DEPTH) ** 0.25
BETA = (8 * DEPTH) ** -0.25
LN_EPS = 1e-5
NEG_INF = -1e30
N_A = (DEPTH + 2) // 3
N_B = (DEPTH + 1) // 3
N_C = DEPTH // 3

kernel_name = 'hybrid_fox_rglru_dsa_stream_step'


def layer_norm(x, g, b):
    xf = x.astype(jnp.float32)
    mu = jnp.mean(xf, axis=-1, keepdims=True)
    var = jnp.mean(jnp.square(xf - mu), axis=-1, keepdims=True)
    y = (xf - mu) * lax.rsqrt(var + LN_EPS) * g.astype(jnp.float32) + b.astype(jnp.float32)
    return y.astype(x.dtype)


def alibi_slopes(n):
    return 2.0 ** (-8.0 * jnp.arange(1, n + 1, dtype=jnp.float32) / n)


def fox_attend(q, k, v, cq, ck, qpos, kpos):
    s = jnp.einsum('bqhd,bkhd->bhqk', q, k).astype(jnp.float32) * (DH_A ** -0.5)
    bias = jnp.swapaxes(cq, 1, 2)[:, :, :, None] - jnp.swapaxes(ck, 1, 2)[:, :, None, :]
    causal = kpos[None, :] <= qpos[:, None]
    s = jnp.where(causal[None, None], s + bias, NEG_INF)
    p = jax.nn.softmax(s, axis=-1)
    return jnp.einsum('bhqk,bkhd->bqhd', p.astype(v.dtype), v)


def mixer_a(x, w_in, b_f, w_out, cache):
    B, T, _ = x.shape
    proj = x @ w_in
    q, k, v, g, fl = jnp.split(proj, [D_A, 2 * D_A, 3 * D_A, 4 * D_A], axis=-1)
    q = q.reshape(B, T, H_A, DH_A)
    k = k.reshape(B, T, H_A, DH_A)
    v = v.reshape(B, T, H_A, DH_A)
    logf = jax.nn.log_sigmoid((fl + b_f).astype(jnp.float32))
    if cache is None:
        c = jnp.cumsum(logf, axis=1)
        pos = jnp.arange(T)
        nb = T // Q_BLOCK
        blocks = (q.reshape(B, nb, Q_BLOCK, H_A, DH_A).swapaxes(0, 1),
                  c.reshape(B, nb, Q_BLOCK, H_A).swapaxes(0, 1),
                  pos.reshape(nb, Q_BLOCK))
        o = lax.map(lambda blk: fox_attend(blk[0], k, v, blk[1], c, blk[2], pos), blocks)
        o = o.swapaxes(0, 1).reshape(B, T, D_A)
    else:
        ck, cv, clf = cache
        P = ck.shape[1]
        k_all = jnp.concatenate([ck.astype(k.dtype), k], axis=1)
        v_all = jnp.concatenate([cv.astype(v.dtype), v], axis=1)
        c = jnp.cumsum(jnp.concatenate([clf.astype(jnp.float32), logf], axis=1), axis=1)
        kpos = jnp.arange(P + T)
        qpos = P + jnp.arange(T)
        o = fox_attend(q, k_all, v_all, c[:, P:], c, qpos, kpos).reshape(B, T, D_A)
    y = (o * jax.nn.silu(g)) @ w_out
    return y, (k, v, logf)


def _lin_comb(left, right):
    a_l, b_l = left
    a_r, b_r = right
    return a_l * a_r, a_r * b_l + b_r


def mixer_b(x, w_in, conv_w, conv_b, w_rg, b_rg, w_ig, b_ig, lam, w_out, state):
    B, T, _ = x.shape
    u, g = jnp.split(x @ w_in, 2, axis=-1)
    if state is None:
        buf = jnp.zeros((B, CONV_W - 1, D_RNN), u.dtype)
        h0 = jnp.zeros((B, D_RNN), jnp.float32)
    else:
        buf, h0 = state
        buf = buf.astype(u.dtype)
        h0 = h0.astype(jnp.float32)
    u_ext = jnp.concatenate([buf, u], axis=1)
    uc = conv_b + u_ext[:, CONV_W - 1:] * conv_w[CONV_W - 1]
    for j in range(CONV_W - 1):
        uc = uc + u_ext[:, j:j + T] * conv_w[j]
    ub = uc.reshape(B, T, N_LRU_BLOCKS, LRU_BLOCK)
    r = jax.nn.sigmoid(jnp.einsum('btni,nij->btnj', ub, w_rg).reshape(B, T, D_RNN) + b_rg)
    i = jax.nn.sigmoid(jnp.einsum('btni,nij->btnj', ub, w_ig).reshape(B, T, D_RNN) + b_ig)
    log_a = -LRU_C * r.astype(jnp.float32) * jax.nn.softplus(-lam.astype(jnp.float32))
    a = jnp.exp(log_a)
    b = jnp.sqrt(-jnp.expm1(2.0 * log_a)) * (i * uc).astype(jnp.float32)
    b = b.at[:, 0].add(a[:, 0] * h0)
    _, h = lax.associative_scan(_lin_comb, (a, b), axis=1)
    y = (h * jax.nn.silu(g).astype(jnp.float32)) @ w_out
    return y, (u_ext[:, -(CONV_W - 1):], h[:, -1])


def dsa_attend(q, qi, wi, qpos, k, v, ki, topk):
    S = k.shape[1]
    qchunk = qpos // CHUNK
    adm = (jnp.arange(S) // CHUNK)[None, :] <= qchunk[:, None]
    dots = jnp.einsum('bqhd,bsd->bqhs', qi, ki).astype(jnp.float32)
    score = jnp.einsum('bqh,bqhs->bqs', wi.astype(jnp.float32), jax.nn.relu(dots))
    score = jnp.where(adm[None], score, NEG_INF)
    _, sel = lax.top_k(score, topk)
    valid = (sel // CHUNK) <= qchunk[None, :, None]
    gather = jax.vmap(lambda arr, ix: arr[ix])
    k_sel = gather(k, sel)
    v_sel = gather(v, sel)
    s = jnp.einsum('bqhd,bqkhd->bhqk', q, k_sel).astype(jnp.float32) * (DH_C ** -0.5)
    dist = jnp.abs(qpos[None, :, None] - sel).astype(jnp.float32)
    s = s - alibi_slopes(H_C)[None, :, None, None] * dist[:, None]
    s = jnp.where(valid[:, None], s, NEG_INF)
    p = jax.nn.softmax(s, axis=-1)
    return jnp.einsum('bhqk,bqkhd->bqhd', p.astype(v_sel.dtype), v_sel)


def mixer_c(x, w_in, w_out, cache):
    B, T, _ = x.shape
    proj = x @ w_in
    splits = [D_C, 2 * D_C, 3 * D_C, 4 * D_C, 4 * D_C + H_IDX * D_IDX, 4 * D_C + H_IDX * D_IDX + D_IDX]
    q, k, v, g, qi, ki, wi = jnp.split(proj, splits, axis=-1)
    q = q.reshape(B, T, H_C, DH_C)
    k = k.reshape(B, T, H_C, DH_C)
    v = v.reshape(B, T, H_C, DH_C)
    qi = qi.reshape(B, T, H_IDX, D_IDX)
    if cache is None:
        topk = min(TOPK_MAX, T // 4)
        nb = T // CHUNK
        blocks = (q.reshape(B, nb, CHUNK, H_C, DH_C).swapaxes(0, 1),
                  qi.reshape(B, nb, CHUNK, H_IDX, D_IDX).swapaxes(0, 1),
                  wi.reshape(B, nb, CHUNK, H_IDX).swapaxes(0, 1),
                  jnp.arange(T).reshape(nb, CHUNK))
        o = lax.map(lambda blk: dsa_attend(blk[0], blk[1], blk[2], blk[3], k, v, ki, topk), blocks)
        o = o.swapaxes(0, 1).reshape(B, T, D_C)
    else:
        ck, cv, cki = cache
        P = ck.shape[1]
        topk = min(TOPK_MAX, (P + T) // 4)
        k_all = jnp.concatenate([ck.astype(k.dtype), k], axis=1)
        v_all = jnp.concatenate([cv.astype(v.dtype), v], axis=1)
        ki_all = jnp.concatenate([cki.astype(ki.dtype), ki], axis=1)
        o = dsa_attend(q, qi, wi, P + jnp.arange(T), k_all, v_all, ki_all, topk).reshape(B, T, D_C)
    y = (o * jax.nn.silu(g)) @ w_out
    return y, (k, v, ki)


def run_trunk(x, p, caches):
    new_a, new_b, new_c = [], [], []
    for i in range(DEPTH):
        j = i // N_MIXERS
        kind = i % N_MIXERS
        if kind == 0:
            cache = None if caches is None else (caches['a_k'][j], caches['a_v'][j], caches['a_logf'][j])
            out, st = mixer_a(x, p['w_in_a'][j], p['b_f_a'][j], p['w_out_a'][j], cache)
            new_a.append(st)
        elif kind == 1:
            state = None if caches is None else (caches['b_conv'][j], caches['b_h'][j])
            out, st = mixer_b(x, p['w_in_b'][j], p['conv_w_b'][j], p['conv_b_b'][j], p['w_rg_b'][j],
                              p['b_rg_b'][j], p['w_ig_b'][j], p['b_ig_b'][j], p['lam_b'][j],
                              p['w_out_b'][j], state)
            new_b.append(st)
        else:
            cache = None if caches is None else (caches['c_k'][j], caches['c_v'][j], caches['c_kidx'][j])
            out, st = mixer_c(x, p['w_in_c'][j], p['w_out_c'][j], cache)
            new_c.append(st)
        x = layer_norm(ALPHA * x + out, p['ln_g'][i], p['ln_b'][i])
    a_k = jnp.stack([s[0] for s in new_a])
    a_v = jnp.stack([s[1] for s in new_a])
    a_logf = jnp.stack([s[2] for s in new_a])
    b_conv = jnp.stack([s[0] for s in new_b])
    b_h = jnp.stack([s[1] for s in new_b])
    c_k = jnp.stack([s[0] for s in new_c])
    c_v = jnp.stack([s[1] for s in new_c])
    c_kidx = jnp.stack([s[2] for s in new_c])
    return x, a_k, a_v, a_logf, b_conv, b_h, c_k, c_v, c_kidx


def setup_inputs(seed: int = 0) -> dict:
    key = jax.random.key(seed)
    ks = jax.random.split(key, 26)
    f32 = jnp.float32

    def nrm(k, shape, scale=1.0):
        return jax.random.normal(k, shape, f32) * scale

    d_in_a = 4 * D_A + H_A
    d_in_c = 4 * D_C + H_IDX * D_IDX + D_IDX + H_IDX
    a_base = jax.random.uniform(ks[12], (N_B, D_RNN), f32, 0.9, 0.999) ** (1.0 / LRU_C)
    return {
        'x_prompt': nrm(ks[0], (BATCH, SEQ, D_MODEL)),
        'x_sample': nrm(ks[1], (DEC_BATCH, DEC_SEQ, D_MODEL)),
        'cache_a_k': nrm(ks[2], (N_A, DEC_BATCH, PAST_LEN, H_A, DH_A)),
        'cache_a_v': nrm(ks[3], (N_A, DEC_BATCH, PAST_LEN, H_A, DH_A)),
        'cache_a_logf': jax.nn.log_sigmoid(FOX_BIAS_INIT + nrm(ks[4], (N_A, DEC_BATCH, PAST_LEN, H_A))),
        'state_b_conv': nrm(ks[5], (N_B, DEC_BATCH, CONV_W - 1, D_RNN)),
        'state_b_h': nrm(ks[6], (N_B, DEC_BATCH, D_RNN), 0.5),
        'cache_c_k': nrm(ks[7], (N_C, DEC_BATCH, PAST_LEN, H_C, DH_C)),
        'cache_c_v': nrm(ks[8], (N_C, DEC_BATCH, PAST_LEN, H_C, DH_C)),
        'cache_c_kidx': nrm(ks[9], (N_C, DEC_BATCH, PAST_LEN, D_IDX)),
        'w_in_a': nrm(ks[10], (N_A, D_MODEL, d_in_a), D_MODEL ** -0.5),
        'b_f_a': FOX_BIAS_INIT + nrm(ks[11], (N_A, H_A), 0.1),
        'w_out_a': nrm(ks[13], (N_A, D_A, D_MODEL), BETA * D_A ** -0.5),
        'w_in_b': nrm(ks[14], (N_B, D_MODEL, 2 * D_RNN), D_MODEL ** -0.5),
        'conv_w_b': nrm(ks[15], (N_B, CONV_W, D_RNN), CONV_W ** -0.5),
        'conv_b_b': nrm(ks[16], (N_B, D_RNN), 0.02),
        'w_rg_b': nrm(ks[17], (N_B, N_LRU_BLOCKS, LRU_BLOCK, LRU_BLOCK), LRU_BLOCK ** -0.5),
        'b_rg_b': nrm(ks[18], (N_B, D_RNN), 0.02),
        'w_ig_b': nrm(ks[19], (N_B, N_LRU_BLOCKS, LRU_BLOCK, LRU_BLOCK), LRU_BLOCK ** -0.5),
        'b_ig_b': nrm(ks[20], (N_B, D_RNN), 0.02),
        'lam_b': jnp.log(a_base) - jnp.log1p(-a_base),
        'w_out_b': nrm(ks[21], (N_B, D_RNN, D_MODEL), BETA * D_RNN ** -0.5),
        'w_in_c': nrm(ks[22], (N_C, D_MODEL, d_in_c), D_MODEL ** -0.5),
        'w_out_c': nrm(ks[23], (N_C, D_C, D_MODEL), BETA * D_C ** -0.5),
        'ln_g': 1.0 + nrm(ks[24], (DEPTH, D_MODEL), 0.05),
        'ln_b': nrm(ks[25], (DEPTH, D_MODEL), 0.05),
    }


def reference(x_prompt, x_sample, cache_a_k, cache_a_v, cache_a_logf, state_b_conv, state_b_h,
              cache_c_k, cache_c_v, cache_c_kidx, w_in_a, b_f_a, w_out_a, w_in_b, conv_w_b,
              conv_b_b, w_rg_b, b_rg_b, w_ig_b, b_ig_b, lam_b, w_out_b, w_in_c, w_out_c,
              ln_g, ln_b):
    p = {'w_in_a': w_in_a, 'b_f_a': b_f_a, 'w_out_a': w_out_a, 'w_in_b': w_in_b,
         'conv_w_b': conv_w_b, 'conv_b_b': conv_b_b, 'w_rg_b': w_rg_b, 'b_rg_b': b_rg_b,
         'w_ig_b': w_ig_b, 'b_ig_b': b_ig_b, 'lam_b': lam_b, 'w_out_b': w_out_b,
         'w_in_c': w_in_c, 'w_out_c': w_out_c, 'ln_g': ln_g, 'ln_b': ln_b}
    caches = {'a_k': cache_a_k, 'a_v': cache_a_v, 'a_logf': cache_a_logf,
              'b_conv': state_b_conv, 'b_h': state_b_h,
              'c_k': cache_c_k, 'c_v': cache_c_v, 'c_kidx': cache_c_kidx}
    (y_prompt, a_k_p, a_v_p, a_logf_p, b_conv_p, b_h_p,
     c_k_p, c_v_p, c_kidx_p) = run_trunk(x_prompt, p, None)
    (y_sample, a_k_s, a_v_s, a_logf_s, b_conv_s, b_h_s,
     c_k_s, c_v_s, c_kidx_s) = run_trunk(x_sample, p, caches)
    return (y_prompt, y_sample,
            a_k_p, a_v_p, a_logf_p, b_conv_p, b_h_p, c_k_p, c_v_p, c_kidx_p,
            a_k_s, a_v_s, a_logf_s, b_conv_s, b_h_s, c_k_s, c_v_s, c_kidx_s)
```

```python
import functools
import math

import jax
import jax.numpy as jnp
from jax import lax
from jax.experimental import pallas as pl
from jax.experimental.pallas import tpu as pltpu

NEG_INF = -1e30
LN_EPS = 1e-5
CHUNK = 64
CHUNK_SHIFT = 6
TOPK_MAX = 256
LRU_C = 8.0
LANES = 128
SUBLANES = 8
INT_MIN = -(2 ** 31)
VMEM_LIMIT = 56 * 1024 * 1024

F32 = jnp.float32
BF16 = jnp.bfloat16


def _cparams(sem):
    return pltpu.CompilerParams(dimension_semantics=sem, vmem_limit_bytes=VMEM_LIMIT)


def _pick(n, pref):
    if n <= pref:
        return n
    b = pref
    while n % b:
        b //= 2
    return b


def _inproj_kernel(x_ref, *refs, n_w, plan):
    w_refs, out_refs = refs[:n_w], refs[n_w:]
    xb = x_ref[0].astype(BF16)
    for w_idx, c0, width, outs in plan:
        r = jnp.dot(xb, w_refs[w_idx][:, c0:c0 + width], preferred_element_type=F32)
        for o_idx, kind in outs:
            o = out_refs[o_idx]
            if kind == "row":
                o[0] = r.astype(o.dtype)
            else:
                for h in range(width // LANES):
                    o[0, h] = r[:, h * LANES:(h + 1) * LANES].astype(o.dtype)


def _inproj(x3, ws, plan, out_defs, bm):
    bx, tx, d = x3.shape
    grid = (bx, tx // bm)
    in_specs = [pl.BlockSpec((1, bm, d), lambda b, i: (b, i, 0))]
    for w in ws:
        in_specs.append(pl.BlockSpec(w.shape, lambda b, i: (0, 0)))
    out_shape, out_specs = [], []
    for kind, width, dt in out_defs:
        if kind == "row":
            out_shape.append(jax.ShapeDtypeStruct((bx, tx, width), dt))
            out_specs.append(pl.BlockSpec((1, bm, width), lambda b, i: (b, i, 0)))
        else:
            nh = width // LANES
            out_shape.append(jax.ShapeDtypeStruct((bx, nh, tx, LANES), dt))
            out_specs.append(pl.BlockSpec((1, nh, bm, LANES), lambda b, i: (b, 0, i, 0)))
    return pl.pallas_call(
        functools.partial(_inproj_kernel, n_w=len(ws), plan=tuple(plan)),
        out_shape=out_shape, grid=grid, in_specs=in_specs, out_specs=out_specs,
        compiler_params=_cparams(("parallel", "parallel")), name="inproj",
    )(x3, *ws)


def _outproj_ln_kernel(o_ref, g_ref, x_ref, w_ref, lg_ref, lb_ref, y_ref, *, alpha):
    g = g_ref[...]
    og = (o_ref[...] * (g * jax.nn.sigmoid(g))).astype(BF16)
    y = jnp.dot(og, w_ref[...], preferred_element_type=F32)
    z = alpha * x_ref[...] + y
    mu = jnp.mean(z, axis=-1, keepdims=True)
    zc = z - mu
    var = jnp.mean(zc * zc, axis=-1, keepdims=True)
    y_ref[...] = zc * lax.rsqrt(var + LN_EPS) * lg_ref[...] + lb_ref[...]


def _outproj_ln(o2, g2, x2, w, ln_g, ln_b, alpha):
    m, d = x2.shape
    dk = o2.shape[1]
    bm = _pick(m, 512)
    row = lambda i: (i, 0)
    fixed = lambda i: (0, 0)
    return pl.pallas_call(
        functools.partial(_outproj_ln_kernel, alpha=alpha),
        out_shape=jax.ShapeDtypeStruct((m, d), F32), grid=(m // bm,),
        in_specs=[pl.BlockSpec((bm, dk), row), pl.BlockSpec((bm, dk), row), pl.BlockSpec((bm, d), row),
                  pl.BlockSpec((dk, d), fixed), pl.BlockSpec((1, d), fixed), pl.BlockSpec((1, d), fixed)],
        out_specs=pl.BlockSpec((bm, d), row),
        compiler_params=_cparams(("parallel",)), name="outproj_ln",
    )(o2, g2, x2, w, ln_g.reshape(1, d), ln_b.reshape(1, d))


def _log_sigmoid(x):
    return -(jnp.maximum(-x, 0.0) + jnp.log1p(jnp.exp(-jnp.abs(x))))


def _logf_cumsum_kernel(z_ref, bf_ref, lf_ref, c_ref, *, p0, p1):
    z = z_ref[0]
    pos = lax.broadcasted_iota(jnp.int32, z.shape, 1)
    is_new = jnp.logical_and(pos >= p0, pos < p1)
    lf = jnp.where(is_new, _log_sigmoid(z + bf_ref[...]), z)
    lf_ref[0] = lf
    c = lf
    s = 1
    while s < z.shape[1]:
        c = c + jnp.where(pos >= s, pltpu.roll(c, s, axis=1), 0.0)
        s *= 2
    c_ref[0] = c


def _logf_cumsum(z, b_f, p0, p1):
    b, h, l = z.shape
    blk = pl.BlockSpec((1, h, l), lambda i: (i, 0, 0))
    return pl.pallas_call(
        functools.partial(_logf_cumsum_kernel, p0=p0, p1=p1),
        out_shape=[jax.ShapeDtypeStruct(z.shape, F32)] * 2, grid=(b,),
        in_specs=[blk, pl.BlockSpec((h, 1), lambda i: (0, 0))], out_specs=[blk, blk],
        compiler_params=_cparams(("parallel",)), name="logf_cumsum",
    )(z, b_f.reshape(h, 1))


def _softmax_update(h, s, v, m_sc, l_sc, acc_sc):
    m_prev = m_sc[h]
    m_new = jnp.maximum(m_prev, jnp.max(s, axis=1, keepdims=True))
    p = jnp.exp(s - m_new)
    alpha = jnp.exp(m_prev - m_new)
    l_sc[h] = alpha * l_sc[h] + jnp.sum(p, axis=1, keepdims=True)
    acc_sc[h] = acc_sc[h] * alpha + jnp.dot(p.astype(BF16), v, preferred_element_type=F32)
    m_sc[h] = m_new


def _attn_init(m_sc, l_sc, acc_sc):
    m_sc[...] = jnp.full(m_sc.shape, -jnp.inf, F32)
    l_sc[...] = jnp.zeros(l_sc.shape, F32)
    acc_sc[...] = jnp.zeros(acc_sc.shape, F32)


def _attn_finish(o_ref, l_sc, acc_sc, nh):
    for h in range(nh):
        o_ref[0, :, h * LANES:(h + 1) * LANES] = acc_sc[h] / l_sc[h]


def _qk(q, k, scale):
    return lax.dot_general(q, k, (((1,), (1,)), ((), ())), preferred_element_type=F32) * scale


def _fox_kernel(q_ref, k_ref, v_ref, cq_ref, ck_ref, o_ref, m_sc, l_sc, acc_sc, cq_sc,
                *, bq, bk, q_off, scale, nh):
    qi, ki = pl.program_id(1), pl.program_id(2)
    q_first = q_off + qi * bq
    q_last = q_first + bq - 1
    k_first = ki * bk
    k_last = k_first + bk - 1

    @pl.when(ki == 0)
    def _():
        _attn_init(m_sc, l_sc, acc_sc)
        cqb = cq_ref[0]
        for h in range(nh):
            cq_sc[h] = cqb[:, h:h + 1]

    def step(masked):
        if masked:
            qpos = q_first + lax.broadcasted_iota(jnp.int32, (bq, bk), 0)
            kpos = k_first + lax.broadcasted_iota(jnp.int32, (bq, bk), 1)
            causal = kpos <= qpos

        def head(h, carry):
            s = _qk(q_ref[0, h], k_ref[0, h], scale)
            s = s + (cq_sc[h] - ck_ref[0, pl.ds(h, 1), :])
            if masked:
                s = jnp.where(causal, s, NEG_INF)
            _softmax_update(h, s, v_ref[0, h], m_sc, l_sc, acc_sc)
            return carry

        lax.fori_loop(0, nh, head, 0)

    needed = k_first <= q_last
    straddles = k_last > q_first

    @pl.when(jnp.logical_and(needed, straddles))
    def _():
        step(True)

    @pl.when(jnp.logical_and(needed, jnp.logical_not(straddles)))
    def _():
        step(False)

    @pl.when(ki == pl.num_programs(2) - 1)
    def _():
        _attn_finish(o_ref, l_sc, acc_sc, nh)


def _attn_scratch(nh, bq):
    return [pltpu.VMEM((nh, bq, 1), F32), pltpu.VMEM((nh, bq, 1), F32), pltpu.VMEM((nh, bq, LANES), F32)]


def _fox_attention(q_hm, k_hm, v_hm, cq, ck, q_off, bq, bk):
    b, nh, tq, dh = q_hm.shape
    s = k_hm.shape[2]
    nq, nk = tq // bq, s // bk

    def kmap(bb, qi, ki):
        return jnp.minimum(ki, (q_off + (qi + 1) * bq - 1) // bk)

    return pl.pallas_call(
        functools.partial(_fox_kernel, bq=bq, bk=bk, q_off=q_off, scale=dh ** -0.5, nh=nh),
        out_shape=jax.ShapeDtypeStruct((b, tq, nh * dh), F32), grid=(b, nq, nk),
        in_specs=[pl.BlockSpec((1, nh, bq, dh), lambda bb, qi, ki: (bb, 0, qi, 0)),
                  pl.BlockSpec((1, nh, bk, dh), lambda bb, qi, ki: (bb, 0, kmap(bb, qi, ki), 0)),
                  pl.BlockSpec((1, nh, bk, dh), lambda bb, qi, ki: (bb, 0, kmap(bb, qi, ki), 0)),
                  pl.BlockSpec((1, bq, nh), lambda bb, qi, ki: (bb, qi, 0)),
                  pl.BlockSpec((1, nh, bk), lambda bb, qi, ki: (bb, 0, kmap(bb, qi, ki)))],
        out_specs=pl.BlockSpec((1, bq, nh * dh), lambda bb, qi, ki: (bb, qi, 0)),
        scratch_shapes=_attn_scratch(nh, bq) + [pltpu.VMEM((nh, bq, 1), F32)],
        compiler_params=_cparams(("parallel", "parallel", "arbitrary")), name="fox_attention",
    )(q_hm, k_hm, v_hm, cq, ck)


def _dsa_select_kernel(qi_ref, wi_ref, kit_ref, out_ref, key_sc, j_sc,
                       *, bq, kb, q_off, s_valid, topk, nh_idx, wi_col0, idx_bits):
    qb = pl.program_id(1)
    q_first = q_off + qb * bq
    q_last = q_first + bq - 1
    n_adm = jnp.minimum(((q_last >> CHUNK_SHIFT) + 1) * CHUNK, s_valid)
    nkb = (jnp.maximum(n_adm, topk) + kb - 1) // kb
    nkb_total = out_ref.shape[1]
    tpb = kb // LANES
    wi = wi_ref[0]

    def admissible(kpos, qchunk):
        return jnp.logical_and((kpos >> CHUNK_SHIFT) <= qchunk, kpos < s_valid)

    def score_block(kblk, carry):
        kt = kit_ref[0, kblk]
        sc = jnp.zeros((bq, kb), F32)
        for h in range(nh_idx):
            d = jnp.dot(qi_ref[0, h], kt, preferred_element_type=F32)
            sc = sc + wi[:, wi_col0 + h:wi_col0 + h + 1] * jnp.maximum(d, 0.0)
        qchunk = (q_first + lax.broadcasted_iota(jnp.int32, (bq, kb), 0)) >> CHUNK_SHIFT
        kpos = kblk * kb + lax.broadcasted_iota(jnp.int32, (bq, kb), 1)
        sc = jnp.where(admissible(kpos, qchunk), sc + 0.0, NEG_INF)
        bits = lax.bitcast_convert_type(sc, jnp.int32)
        key = bits ^ ((bits >> 31) & 0x7FFFFFFF)
        for j in range(tpb):
            key_sc[kblk * tpb + j] = key[:, j * LANES:(j + 1) * LANES]
        return carry

    lax.fori_loop(0, nkb, score_block, 0)

    lane = lax.broadcasted_iota(jnp.int32, (bq, LANES), 1)

    def count(pred):
        def blk(kblk, acc):
            for j in range(tpb):
                t = kblk * tpb + j
                acc = acc + jnp.where(pred(key_sc[t], t * LANES + lane), 1.0, 0.0)
            return acc
        acc = lax.fori_loop(0, nkb, blk, jnp.zeros((bq, LANES), F32))
        return jnp.sum(acc, axis=1, keepdims=True)

    def bit_step(it, t):
        cand = jnp.where(it == 0, jnp.zeros_like(t), t | jnp.left_shift(jnp.int32(1), 31 - it))
        cnt = count(lambda key, idx: key >= cand)
        return jnp.where(cnt >= topk, cand, t)

    thr = lax.fori_loop(0, 32, bit_step, jnp.full((bq, LANES), INT_MIN, jnp.int32))

    n_gt = count(lambda key, idx: key > thr)
    n_ge = count(lambda key, idx: key >= thr)
    need = topk - n_gt
    j_sc[...] = jnp.full((bq, LANES), 2 ** 30, jnp.int32)

    @pl.when(jnp.max(n_ge) > topk)
    def _():
        def idx_step(it, lo):
            cand = lo | jnp.left_shift(jnp.int32(1), idx_bits - 1 - it)
            c = count(lambda key, idx: jnp.logical_and(key == thr, idx < cand))
            return jnp.where(c < need, cand, lo)
        j_sc[...] = lax.fori_loop(0, idx_bits, idx_step, jnp.zeros((bq, LANES), jnp.int32))

    j_cut = j_sc[...]
    qchunk_t = (q_first + lax.broadcasted_iota(jnp.int32, (bq, LANES), 0)) >> CHUNK_SHIFT

    def write_block(kblk, carry):
        for j in range(tpb):
            t = kblk * tpb + j
            key = key_sc[t]
            idx = t * LANES + lane
            tie = jnp.logical_and(key == thr, idx <= j_cut)
            sel = jnp.logical_and(jnp.logical_or(key > thr, tie), admissible(idx, qchunk_t))
            out_ref[0, kblk, :, j * LANES:(j + 1) * LANES] = jnp.where(sel, 0.0, NEG_INF).astype(out_ref.dtype)
        return carry

    def fill_block(kblk, carry):
        out_ref[0, kblk] = jnp.full((bq, kb), NEG_INF, out_ref.dtype)
        return carry

    lax.fori_loop(0, nkb, write_block, 0)
    lax.fori_loop(nkb, nkb_total, fill_block, 0)


def _dsa_select(qi_hm, wi_arr, kit, q_off, s_valid, topk, wi_col0, bq):
    b, nh_idx, tq, d_idx = qi_hm.shape
    nkb_total, kb = kit.shape[1], kit.shape[3]
    s_pad = nkb_total * kb
    return pl.pallas_call(
        functools.partial(_dsa_select_kernel, bq=bq, kb=kb, q_off=q_off, s_valid=s_valid, topk=topk,
                          nh_idx=nh_idx, wi_col0=wi_col0, idx_bits=max(1, (s_pad - 1).bit_length())),
        out_shape=jax.ShapeDtypeStruct((b, nkb_total, tq, kb), BF16), grid=(b, tq // bq),
        in_specs=[pl.BlockSpec((1, nh_idx, bq, d_idx), lambda bb, qb: (bb, 0, qb, 0)),
                  pl.BlockSpec((1, bq, LANES), lambda bb, qb: (bb, qb, 0)),
                  pl.BlockSpec((1, nkb_total, d_idx, kb), lambda bb, qb: (bb, 0, 0, 0))],
        out_specs=pl.BlockSpec((1, nkb_total, bq, kb), lambda bb, qb: (bb, 0, qb, 0)),
        scratch_shapes=[pltpu.VMEM((s_pad // LANES, bq, LANES), jnp.int32), pltpu.VMEM((bq, LANES), jnp.int32)],
        compiler_params=_cparams(("parallel", "arbitrary")), name="dsa_select",
    )(qi_hm, wi_arr, kit)


def _dsa_attn_kernel(sl_ref, q_ref, k_ref, v_ref, mb_ref, o_ref, m_sc, l_sc, acc_sc, base_sc, dist_sc,
                     *, bq, bk, q_off, scale, nh):
    qi, ki = pl.program_id(1), pl.program_id(2)
    q_first = q_off + qi * bq
    q_last = q_first + bq - 1
    k_first = ki * bk

    @pl.when(ki == 0)
    def _():
        _attn_init(m_sc, l_sc, acc_sc)

    @pl.when(k_first <= ((q_last >> CHUNK_SHIFT) << CHUNK_SHIFT) + CHUNK - 1)
    def _():
        base_sc[...] = mb_ref[0, 0].astype(F32)
        qpos = q_first + lax.broadcasted_iota(jnp.int32, (bq, bk), 0)
        kpos = k_first + lax.broadcasted_iota(jnp.int32, (bq, bk), 1)
        dist_sc[...] = jnp.abs(qpos - kpos).astype(F32)

        def head(h, carry):
            s = _qk(q_ref[0, h], k_ref[0, h], scale) - sl_ref[h] * dist_sc[...] + base_sc[...]
            _softmax_update(h, s, v_ref[0, h], m_sc, l_sc, acc_sc)
            return carry

        lax.fori_loop(0, nh, head, 0)

    @pl.when(ki == pl.num_programs(2) - 1)
    def _():
        _attn_finish(o_ref, l_sc, acc_sc, nh)


def _dsa_attention(q_hm, k_hm, v_hm, mask, slopes, q_off, bq):
    b, nh, tq, dh = q_hm.shape
    nk, bk = mask.shape[1], mask.shape[3]
    nq = tq // bq

    def kmap(bb, qi, ki):
        q_last = q_off + (qi + 1) * bq - 1
        return jnp.minimum(ki, (((q_last >> CHUNK_SHIFT) << CHUNK_SHIFT) + CHUNK - 1) // bk)

    return pl.pallas_call(
        functools.partial(_dsa_attn_kernel, bq=bq, bk=bk, q_off=q_off, scale=dh ** -0.5, nh=nh),
        out_shape=jax.ShapeDtypeStruct((b, tq, nh * dh), F32), grid=(b, nq, nk),
        in_specs=[pl.BlockSpec(memory_space=pltpu.SMEM),
                  pl.BlockSpec((1, nh, bq, dh), lambda bb, qi, ki: (bb, 0, qi, 0)),
                  pl.BlockSpec((1, nh, bk, dh), lambda bb, qi, ki: (bb, 0, kmap(bb, qi, ki), 0)),
                  pl.BlockSpec((1, nh, bk, dh), lambda bb, qi, ki: (bb, 0, kmap(bb, qi, ki), 0)),
                  pl.BlockSpec((1, 1, bq, bk), lambda bb, qi, ki: (bb, kmap(bb, qi, ki), qi, 0))],
        out_specs=pl.BlockSpec((1, bq, nh * dh), lambda bb, qi, ki: (bb, qi, 0)),
        scratch_shapes=_attn_scratch(nh, bq) + [pltpu.VMEM((bq, bk), F32), pltpu.VMEM((bq, bk), F32)],
        compiler_params=_cparams(("parallel", "parallel", "arbitrary")), name="dsa_attention",
    )(slopes, q_hm, k_hm, v_hm, mask)


def _rglru_kernel(u_ref, cw_ref, cb_ref, wr_ref, br_ref, wig_ref, big_ref, lam_ref, buf0_ref, h0_ref,
                  h_ref, conv_ref, hl_ref, ubuf, a_sc, b_sc, hcar, *, tt, conv_w, nblk):
    t = pl.program_id(1)
    pad = SUBLANES
    d = u_ref.shape[2]
    blk = d // nblk

    @pl.when(t == 0)
    def _():
        ubuf[0:pad, :] = buf0_ref[0]
        hcar[...] = jnp.broadcast_to(h0_ref[0], (SUBLANES, d))

    ubuf[pad:pad + tt, :] = u_ref[0]
    uc = cb_ref[...] + ubuf[pad:pad + tt, :] * cw_ref[conv_w - 1:conv_w, :]
    for j in range(conv_w - 1):
        off = pad - (conv_w - 1) + j
        uc = uc + ubuf[off:off + tt, :] * cw_ref[j:j + 1, :]

    lam = lam_ref[...]
    neg_sp = -LRU_C * (jnp.maximum(-lam, 0.0) + jnp.log1p(jnp.exp(-jnp.abs(lam))))
    for n in range(nblk):
        cs = slice(n * blk, (n + 1) * blk)
        ucn = uc[:, cs]
        ub = ucn.astype(BF16)
        r = jax.nn.sigmoid(jnp.dot(ub, wr_ref[n], preferred_element_type=F32) + br_ref[:, cs])
        i = jax.nn.sigmoid(jnp.dot(ub, wig_ref[n], preferred_element_type=F32) + big_ref[:, cs])
        log_a = r * neg_sp[:, cs]
        a = jnp.exp(log_a)
        a_sc[:, cs] = a
        b_sc[:, cs] = jnp.sqrt(-jnp.tanh(log_a) * (a * a + 1.0)) * (i * ucn)

    row = lax.broadcasted_iota(jnp.int32, (SUBLANES, d), 0)

    def group(gi, hprev):
        r0 = pl.multiple_of(gi * SUBLANES, SUBLANES)
        av = a_sc[pl.ds(r0, SUBLANES), :]
        bv = b_sc[pl.ds(r0, SUBLANES), :]
        s = 1
        while s < SUBLANES:
            a_sh = pltpu.roll(av, s, axis=0)
            b_sh = pltpu.roll(bv, s, axis=0)
            m = row >= s
            bv = jnp.where(m, av * b_sh + bv, bv)
            av = jnp.where(m, av * a_sh, av)
            s *= 2
        hrows = av * hprev + bv
        h_ref[0, pl.ds(r0, SUBLANES), :] = hrows
        return jnp.broadcast_to(hrows[SUBLANES - 1:SUBLANES, :], (SUBLANES, d))

    hlast = lax.fori_loop(0, tt // SUBLANES, group, hcar[...])
    hcar[...] = hlast
    hl_ref[0] = hlast[0:1, :]
    tail = ubuf[tt:tt + pad, :]
    conv_ref[0] = tail
    ubuf[0:pad, :] = tail


def _rglru(u, conv_w, conv_b, w_rg, b_rg, w_ig, b_ig, lam, buf0, h0, tt):
    b, t, d = u.shape
    cw = conv_w.shape[0]
    nblk, blk = w_rg.shape[0], w_rg.shape[1]
    vec = lambda a: a.reshape(1, d)
    fixed2 = lambda bb, ti: (0, 0)
    fixed3 = lambda bb, ti: (0, 0, 0)
    perb = lambda bb, ti: (bb, 0, 0)
    return pl.pallas_call(
        functools.partial(_rglru_kernel, tt=tt, conv_w=cw, nblk=nblk),
        out_shape=[jax.ShapeDtypeStruct((b, t, d), F32), jax.ShapeDtypeStruct((b, SUBLANES, d), F32),
                   jax.ShapeDtypeStruct((b, 1, d), F32)],
        grid=(b, t // tt),
        in_specs=[pl.BlockSpec((1, tt, d), lambda bb, ti: (bb, ti, 0)),
                  pl.BlockSpec((cw, d), fixed2), pl.BlockSpec((1, d), fixed2),
                  pl.BlockSpec((nblk, blk, blk), fixed3), pl.BlockSpec((1, d), fixed2),
                  pl.BlockSpec((nblk, blk, blk), fixed3), pl.BlockSpec((1, d), fixed2),
                  pl.BlockSpec((1, d), fixed2),
                  pl.BlockSpec((1, SUBLANES, d), perb), pl.BlockSpec((1, 1, d), perb)],
        out_specs=[pl.BlockSpec((1, tt, d), lambda bb, ti: (bb, ti, 0)),
                   pl.BlockSpec((1, SUBLANES, d), perb), pl.BlockSpec((1, 1, d), perb)],
        scratch_shapes=[pltpu.VMEM((SUBLANES + tt, d), F32), pltpu.VMEM((tt, d), F32),
                        pltpu.VMEM((tt, d), F32), pltpu.VMEM((SUBLANES, d), F32)],
        compiler_params=_cparams(("parallel", "arbitrary")), name="rglru",
    )(u, conv_w, vec(conv_b), w_rg.astype(BF16), vec(b_rg), w_ig.astype(BF16), vec(b_ig), vec(lam), buf0, h0)


def _pad_to(a, axis, n):
    extra = n - a.shape[axis]
    if extra == 0:
        return a
    widths = [(0, 0)] * a.ndim
    widths[axis] = (0, extra)
    return jnp.pad(a, widths)


def _round_up(n, m):
    return (n + m - 1) // m * m


class _Group:
    def __init__(self, x, past):
        self.b, self.t, self.d = x.shape
        self.past = past
        self.flat = past > 0
        if self.flat:
            self.s_valid = past + self.t
            self.s_pad = _round_up(self.s_valid, LANES)
            self.bq = self.t
            self.bk = self.s_pad
        else:
            self.s_valid = self.s_pad = self.t
            self.bq = _pick(self.t, 512)
            self.bk = _pick(self.t, 512)

    def proj_view(self, x):
        return x.reshape(1, self.b * self.t, self.d) if self.flat else x

    def proj_bm(self):
        return self.b * self.t if self.flat else _pick(self.t, 256)

    def rows(self, a):
        return a.reshape(self.b, self.t, a.shape[-1])

    def heads(self, a):
        if not self.flat:
            return a
        nh = a.shape[1]
        return a.reshape(nh, self.b, self.t, a.shape[-1]).transpose(1, 0, 2, 3)

    def keys_hm(self, new_hm, cache):
        if cache is None:
            return new_hm
        c = cache.astype(BF16).transpose(0, 2, 1, 3)
        return _pad_to(jnp.concatenate([c, new_hm], axis=2), 2, self.s_pad)


def _mixer_a(grp, x, w_main, w_f, b_f, cache):
    nh = b_f.shape[0]
    da = w_main.shape[1] // 4
    plan = [(0, 0, da, [(0, "head")]),
            (0, da, da, [(1, "head"), (3, "row")]),
            (0, 2 * da, da, [(2, "head"), (4, "row")]),
            (0, 3 * da, da, [(5, "row")]),
            (1, 0, LANES, [(6, "row")])]
    outs = [("head", da, BF16)] * 3 + [("row", da, F32)] * 3 + [("row", LANES, F32)]
    q_hm, k_hm, v_hm, k, v, g, fl = _inproj(grp.proj_view(x), [w_main, w_f], plan, outs, grp.proj_bm())
    q_hm, k_hm, v_hm = grp.heads(q_hm), grp.heads(k_hm), grp.heads(v_hm)
    k, v, g = grp.rows(k), grp.rows(v), grp.rows(g)
    z = grp.rows(fl)[:, :, :nh].transpose(0, 2, 1)
    if cache is not None:
        ck_, cv_, clf = cache
        z = jnp.concatenate([clf.astype(F32).transpose(0, 2, 1), z], axis=2)
        z = _pad_to(z, 2, grp.s_pad)
        k_all, v_all = grp.keys_hm(k_hm, ck_), grp.keys_hm(v_hm, cv_)
    else:
        k_all, v_all = k_hm, v_hm
    lf_all, c_all = _logf_cumsum(z, b_f, grp.past, grp.s_valid)
    logf = lf_all[:, :, grp.past:grp.s_valid].transpose(0, 2, 1)
    cq = c_all[:, :, grp.past:grp.s_valid].transpose(0, 2, 1)
    o = _fox_attention(q_hm, k_all, v_all, cq, c_all, grp.past, grp.bq, grp.bk)
    dh = da // nh
    st = (k.reshape(grp.b, grp.t, nh, dh), v.reshape(grp.b, grp.t, nh, dh), logf)
    return o, g, st


def _mixer_b(grp, x, w_in, conv_w, conv_b, w_rg, b_rg, w_ig, b_ig, lam, state):
    dr = w_in.shape[1] // 2
    plan = [(0, 0, dr, [(0, "row")]), (0, dr, dr, [(1, "row")])]
    u, g = _inproj(grp.proj_view(x), [w_in], plan, [("row", dr, F32)] * 2, grp.proj_bm())
    u, g = grp.rows(u), grp.rows(g)
    cw = conv_w.shape[0]
    if state is None:
        buf0 = jnp.zeros((grp.b, SUBLANES, dr), F32)
        h0 = jnp.zeros((grp.b, 1, dr), F32)
    else:
        buf, h0 = state
        buf0 = jnp.pad(buf.astype(F32), ((0, 0), (SUBLANES - (cw - 1), 0), (0, 0)))
        h0 = h0.astype(F32).reshape(grp.b, 1, dr)
    h, tail, hl = _rglru(u, conv_w, conv_b, w_rg, b_rg, w_ig, b_ig, lam, buf0, h0, _pick(grp.t, 256))
    return h, g, (tail[:, SUBLANES - (cw - 1):], hl[:, 0])


def _mixer_c(grp, x, w_main, w_idx, nh, nh_idx, d_idx, cache):
    dc = w_main.shape[1] // 4
    wq = nh_idx * d_idx
    plan = [(0, 0, dc, [(0, "head")]),
            (0, dc, dc, [(1, "head"), (3, "row")]),
            (0, 2 * dc, dc, [(2, "head"), (4, "row")]),
            (0, 3 * dc, dc, [(5, "row")]),
            (1, 0, wq, [(6, "row")]),
            (1, wq, LANES, [(7, "row")])]
    outs = [("head", dc, BF16)] * 3 + [("row", dc, F32)] * 3 + [("row", wq, BF16), ("row", LANES, F32)]
    q_hm, k_hm, v_hm, k, v, g, qi, kw = _inproj(grp.proj_view(x), [w_main, w_idx], plan, outs, grp.proj_bm())
    q_hm, k_hm, v_hm = grp.heads(q_hm), grp.heads(k_hm), grp.heads(v_hm)
    k, v, g, qi, kw = grp.rows(k), grp.rows(v), grp.rows(g), grp.rows(qi), grp.rows(kw)
    ki = kw[:, :, :d_idx]
    qi_hm = qi.reshape(grp.b, grp.t, nh_idx, d_idx).transpose(0, 2, 1, 3)
    if cache is not None:
        ck_, cv_, cki = cache
        k_all, v_all = grp.keys_hm(k_hm, ck_), grp.keys_hm(v_hm, cv_)
        ki_all = _pad_to(jnp.concatenate([cki.astype(F32), ki], axis=1), 1, grp.s_pad)
    else:
        k_all, v_all, ki_all = k_hm, v_hm, ki
    kb = grp.bk
    kit = ki_all.astype(BF16).reshape(grp.b, grp.s_pad // kb, kb, d_idx).transpose(0, 1, 3, 2)
    topk = min(TOPK_MAX, grp.s_valid // 4)
    mask = _dsa_select(qi_hm, kw, kit, grp.past, grp.s_valid, topk, d_idx, _pick(grp.t, 128))
    slopes = 2.0 ** (-8.0 * jnp.arange(1, nh + 1, dtype=F32) / nh)
    o = _dsa_attention(q_hm, k_all, v_all, mask, slopes, grp.past, _pick(grp.t, 256))
    dh = dc // nh
    st = (k.reshape(grp.b, grp.t, nh, dh), v.reshape(grp.b, grp.t, nh, dh), ki)
    return o, g, st


def _run_trunk(x, p, caches, past):
    depth = p["ln_g"].shape[0]
    alpha = (2 * depth) ** 0.25
    grp = _Group(x, past)
    new_a, new_b, new_c = [], [], []
    for i in range(depth):
        j, kind = i // 3, i % 3
        if kind == 0:
            cache = None if caches is None else (caches["a_k"][j], caches["a_v"][j], caches["a_logf"][j])
            o, g, st = _mixer_a(grp, x, p["w_main_a"][j], p["w_f_a"][j], p["b_f_a"][j], cache)
            new_a.append(st)
            w_out = p["w_out_a"][j]
        elif kind == 1:
            state = None if caches is None else (caches["b_conv"][j], caches["b_h"][j])
            o, g, st = _mixer_b(grp, x, p["w_in_b"][j], p["conv_w_b"][j], p["conv_b_b"][j], p["w_rg_b"][j],
                                p["b_rg_b"][j], p["w_ig_b"][j], p["b_ig_b"][j], p["lam_b"][j], state)
            new_b.append(st)
            w_out = p["w_out_b"][j]
        else:
            cache = None if caches is None else (caches["c_k"][j], caches["c_v"][j], caches["c_kidx"][j])
            o, g, st = _mixer_c(grp, x, p["w_main_c"][j], p["w_idx_c"][j], p["h_c"], p["h_idx"], p["d_idx"], cache)
            new_c.append(st)
            w_out = p["w_out_c"][j]
        m = grp.b * grp.t
        x = _outproj_ln(o.reshape(m, -1), g.reshape(m, -1), x.reshape(m, grp.d), w_out,
                        p["ln_g"][i], p["ln_b"][i], alpha).reshape(grp.b, grp.t, grp.d)
    stack = lambda sts, n: jnp.stack([s[n] for s in sts])
    return (x, stack(new_a, 0), stack(new_a, 1), stack(new_a, 2), stack(new_b, 0), stack(new_b, 1),
            stack(new_c, 0), stack(new_c, 1), stack(new_c, 2))


def kernel(x_prompt, x_sample, cache_a_k, cache_a_v, cache_a_logf, state_b_conv, state_b_h, cache_c_k, cache_c_v, cache_c_kidx, w_in_a, b_f_a, w_out_a, w_in_b, conv_w_b, conv_b_b, w_rg_b, b_rg_b, w_ig_b, b_ig_b, lam_b, w_out_b, w_in_c, w_out_c, ln_g, ln_b):
    h_a = b_f_a.shape[1]
    d_a = w_out_a.shape[1]
    d_c = w_out_c.shape[1]
    h_c = cache_c_k.shape[3]
    d_idx = cache_c_kidx.shape[-1]
    h_idx = (w_in_c.shape[2] - 4 * d_c - d_idx) // (d_idx + 1)
    assert d_a // h_a == LANES and d_c // h_c == LANES, "head width must equal the lane count"
    assert w_in_a.shape[2] == 4 * d_a + h_a and d_idx + h_idx <= LANES
    past = cache_a_k.shape[2]
    assert past % CHUNK == 0 and past > 0

    w_idx = w_in_c[:, :, 4 * d_c:]
    w_idx = _pad_to(w_idx, 2, h_idx * d_idx + LANES)
    p = {"w_main_a": w_in_a[:, :, :4 * d_a].astype(BF16),
         "w_f_a": _pad_to(w_in_a[:, :, 4 * d_a:], 2, LANES).astype(BF16),
         "b_f_a": b_f_a, "w_out_a": w_out_a.astype(BF16),
         "w_in_b": w_in_b.astype(BF16), "conv_w_b": conv_w_b, "conv_b_b": conv_b_b, "w_rg_b": w_rg_b,
         "b_rg_b": b_rg_b, "w_ig_b": w_ig_b, "b_ig_b": b_ig_b, "lam_b": lam_b, "w_out_b": w_out_b.astype(BF16),
         "w_main_c": w_in_c[:, :, :4 * d_c].astype(BF16), "w_idx_c": w_idx.astype(BF16),
         "w_out_c": w_out_c.astype(BF16), "ln_g": ln_g, "ln_b": ln_b,
         "h_c": h_c, "h_idx": h_idx, "d_idx": d_idx}
    caches = {"a_k": cache_a_k, "a_v": cache_a_v, "a_logf": cache_a_logf, "b_conv": state_b_conv,
              "b_h": state_b_h, "c_k": cache_c_k, "c_v": cache_c_v, "c_kidx": cache_c_kidx}
    outs_p = _run_trunk(x_prompt, p, None, 0)
    outs_s = _run_trunk(x_sample, p, caches, past)
    return (outs_p[0], outs_s[0]) + outs_p[1:] + outs_s[1:]
```

```python
import functools
import math

import jax
import jax.numpy as jnp
from jax import lax
from jax.experimental import pallas as pl
from jax.experimental.pallas import tpu as pltpu

NEG_INF = -1e30
LN_EPS = 1e-5
CHUNK = 64
CHUNK_SHIFT = 6
TOPK_MAX = 256
LRU_C = 8.0
LANES = 128
SUBLANES = 8
BF16_ROWS = 16
PLANE_ROWS = 32 * SUBLANES
INT_MIN = -(2 ** 31)
VMEM_LIMIT = 56 * 1024 * 1024
LOG2E = math.log2(math.e)

F32 = jnp.float32
BF16 = jnp.bfloat16


def _cparams(sem, flags=None):
    return pltpu.CompilerParams(dimension_semantics=sem, vmem_limit_bytes=VMEM_LIMIT, flags=flags)


def _pick(n, pref):
    if n <= pref:
        return n
    b = pref
    while n % b:
        b //= 2
    return b


def _inproj_kernel(x_ref, *refs, n_w, plan):
    w_refs, out_refs = refs[:n_w], refs[n_w:]
    xb = x_ref[0].astype(BF16)
    for w_idx, c0, width, scale, outs in plan:
        r = jnp.dot(xb, w_refs[w_idx][:, c0:c0 + width], preferred_element_type=F32)
        if scale != 1.0:
            r = r * scale
        for o_idx, kind in outs:
            o = out_refs[o_idx]
            if kind == "row":
                o[0] = r.astype(o.dtype)
            elif kind == "head":
                for h in range(width // LANES):
                    o[0, h] = r[:, h * LANES:(h + 1) * LANES].astype(o.dtype)
            else:
                for h in range(width // LANES):
                    o[0, h] = r[:, h * LANES:(h + 1) * LANES].T.astype(o.dtype)


def _inproj(x3, ws, plan, out_defs, bm):
    bx, tx, d = x3.shape
    grid = (bx, tx // bm)
    in_specs = [pl.BlockSpec((1, bm, d), lambda b, i: (b, i, 0))]
    for w in ws:
        in_specs.append(pl.BlockSpec(w.shape, lambda b, i: (0, 0)))
    out_shape, out_specs = [], []
    for kind, width, dt in out_defs:
        if kind == "row":
            out_shape.append(jax.ShapeDtypeStruct((bx, tx, width), dt))
            out_specs.append(pl.BlockSpec((1, bm, width), lambda b, i: (b, i, 0)))
        elif kind == "head":
            nh = width // LANES
            out_shape.append(jax.ShapeDtypeStruct((bx, nh, tx, LANES), dt))
            out_specs.append(pl.BlockSpec((1, nh, bm, LANES), lambda b, i: (b, 0, i, 0)))
        else:
            nh = width // LANES
            out_shape.append(jax.ShapeDtypeStruct((bx, nh, LANES, tx), dt))
            out_specs.append(pl.BlockSpec((1, nh, LANES, bm), lambda b, i: (b, 0, 0, i)))
    return pl.pallas_call(
        functools.partial(_inproj_kernel, n_w=len(ws), plan=tuple(plan)),
        out_shape=out_shape, grid=grid, in_specs=in_specs, out_specs=out_specs,
        compiler_params=_cparams(("parallel", "parallel")), name="inproj",
    )(x3, *ws)


def _outproj_ln_kernel(o_ref, g_ref, x_ref, w_ref, lg_ref, lb_ref, y_ref, *, alpha):
    g = g_ref[...]
    og = (o_ref[...] * (g * jax.nn.sigmoid(g))).astype(BF16)
    y = jnp.dot(og, w_ref[...], preferred_element_type=F32)
    z = alpha * x_ref[...] + y
    mu = jnp.mean(z, axis=-1, keepdims=True)
    zc = z - mu
    var = jnp.mean(zc * zc, axis=-1, keepdims=True)
    y_ref[...] = zc * lax.rsqrt(var + LN_EPS) * lg_ref[...] + lb_ref[...]


def _outproj_ln(o2, g2, x2, w, ln_g, ln_b, alpha):
    m, d = x2.shape
    dk = o2.shape[1]
    bm = _pick(m, 512)
    row = lambda i: (i, 0)
    fixed = lambda i: (0, 0)
    return pl.pallas_call(
        functools.partial(_outproj_ln_kernel, alpha=alpha),
        out_shape=jax.ShapeDtypeStruct((m, d), F32), grid=(m // bm,),
        in_specs=[pl.BlockSpec((bm, dk), row), pl.BlockSpec((bm, dk), row), pl.BlockSpec((bm, d), row),
                  pl.BlockSpec((dk, d), fixed), pl.BlockSpec((1, d), fixed), pl.BlockSpec((1, d), fixed)],
        out_specs=pl.BlockSpec((bm, d), row),
        compiler_params=_cparams(("parallel",)), name="outproj_ln",
    )(o2, g2, x2, w, ln_g.reshape(1, d), ln_b.reshape(1, d))


def _log_sigmoid(x):
    return -(jnp.maximum(-x, 0.0) + jnp.log1p(jnp.exp(-jnp.abs(x))))


def _logf_cumsum_kernel(z_ref, bf_ref, lf_ref, c_ref, *, p0, p1):
    z = z_ref[0]
    pos = lax.broadcasted_iota(jnp.int32, z.shape, 1)
    is_new = jnp.logical_and(pos >= p0, pos < p1)
    lf = jnp.where(is_new, _log_sigmoid(z + bf_ref[...]), z)
    lf_ref[0] = lf
    c = lf
    s = 1
    while s < z.shape[1]:
        c = c + jnp.where(pos >= s, pltpu.roll(c, s, axis=1), 0.0)
        s *= 2
    c_ref[0] = c


def _logf_cumsum(z, b_f, p0, p1):
    b, h, l = z.shape
    blk = pl.BlockSpec((1, h, l), lambda i: (i, 0, 0))
    return pl.pallas_call(
        functools.partial(_logf_cumsum_kernel, p0=p0, p1=p1),
        out_shape=[jax.ShapeDtypeStruct(z.shape, F32)] * 2, grid=(b,),
        in_specs=[blk, pl.BlockSpec((h, 1), lambda i: (0, 0))], out_specs=[blk, blk],
        compiler_params=_cparams(("parallel",)), name="logf_cumsum",
    )(z, b_f.reshape(h, 1))


def _score_store(u, slot, s_sc, mc_sc):
    s_sc[slot] = u
    mc_sc[slot] = jnp.broadcast_to(jnp.max(u, axis=0, keepdims=True), mc_sc.shape[1:])


def _softmax_update(h, slot, vt, s_sc, mc_sc, m_sc, acc_sc, row_shift=None):
    u = s_sc[slot]
    m_prev = m_sc[h]
    m_cur = mc_sc[slot]
    if row_shift is not None:
        m_cur = m_cur + row_shift
    m_new = jnp.maximum(m_prev, m_cur)
    m_row = m_new[0:1]
    p = jnp.exp2(u - (m_row if row_shift is None else m_row - row_shift))
    alpha = jnp.exp2(m_prev - m_new)
    vt1 = jnp.concatenate([vt, jnp.ones((BF16_ROWS, vt.shape[1]), BF16)], axis=0)
    acc_sc[h] = acc_sc[h] * alpha[0:1] + jnp.dot(vt1, p.astype(BF16), preferred_element_type=F32)
    m_sc[h] = m_new


def _pipelined_heads(nh, qk_phase, sm_phase):
    qk_phase(0, 0)
    for h in range(nh - 1):
        qk_phase(h + 1, (h + 1) % 2)
        sm_phase(h, h % 2)
    sm_phase(nh - 1, (nh - 1) % 2)


def _attn_init(m_sc, acc_sc):
    m_sc[...] = jnp.full(m_sc.shape, -jnp.inf, F32)
    acc_sc[...] = jnp.zeros(acc_sc.shape, F32)


def _attn_finish(o_ref, acc_sc, nh, dh):
    for h in range(nh):
        a = acc_sc[h]
        o_ref[0, :, h * dh:(h + 1) * dh] = (a[0:dh] / a[dh:dh + 1]).T


def _attn_scratch(nh, bq, bk, dh):
    return [pltpu.VMEM((nh, SUBLANES, bq), F32), pltpu.VMEM((nh, dh + BF16_ROWS, bq), F32),
            pltpu.VMEM((2, bk, bq), F32), pltpu.VMEM((2, SUBLANES, bq), F32)]


def _fox_kernel(qt_ref, k_ref, vt_ref, cq_ref, ck_ref, o_ref, m_sc, acc_sc, s_sc, mc_sc, kb_sc,
                *, bq, bk, q_off, nh):
    qi, ki = pl.program_id(1), pl.program_id(2)
    q_first = q_off + qi * bq
    q_last = q_first + bq - 1
    k_first = ki * bk
    k_last = k_first + bk - 1

    @pl.when(ki == 0)
    def _():
        _attn_init(m_sc, acc_sc)

    def step(masked):
        ckb = ck_ref[0]
        for h in range(nh):
            kb_sc[h] = jnp.broadcast_to(ckb[:, h:h + 1] * (-LOG2E), (bk, LANES))
        if masked:
            kpos = k_first + lax.broadcasted_iota(jnp.int32, (bk, bq), 0)
            qpos = q_first + lax.broadcasted_iota(jnp.int32, (bk, bq), 1)
            causal = kpos <= qpos

        def qk_phase(h, slot):
            st = jnp.dot(k_ref[0, h], qt_ref[0, h], preferred_element_type=F32)
            u = st + jnp.tile(kb_sc[h], (1, bq // LANES))
            if masked:
                u = jnp.where(causal, u, NEG_INF)
            _score_store(u, slot, s_sc, mc_sc)

        def sm_phase(h, slot):
            cq2 = cq_ref[0, pl.ds(h, 1), :] * LOG2E
            _softmax_update(h, slot, vt_ref[0, h], s_sc, mc_sc, m_sc, acc_sc, row_shift=cq2)

        _pipelined_heads(nh, qk_phase, sm_phase)

    needed = k_first <= q_last
    straddles = k_last > q_first

    @pl.when(jnp.logical_and(needed, straddles))
    def _():
        step(True)

    @pl.when(jnp.logical_and(needed, jnp.logical_not(straddles)))
    def _():
        step(False)

    @pl.when(ki == pl.num_programs(2) - 1)
    def _():
        _attn_finish(o_ref, acc_sc, nh, k_ref.shape[3])


def _fox_attention(qt_hm, k_hm, vt_hm, cq, ck, q_off, bq, bk):
    b, nh, dh, tq = qt_hm.shape
    s = k_hm.shape[2]
    nq, nk = tq // bq, s // bk

    def kmap(qi, ki):
        return jnp.minimum(ki, (q_off + (qi + 1) * bq - 1) // bk)

    return pl.pallas_call(
        functools.partial(_fox_kernel, bq=bq, bk=bk, q_off=q_off, nh=nh),
        out_shape=jax.ShapeDtypeStruct((b, tq, nh * dh), F32), grid=(b, nq, nk),
        in_specs=[pl.BlockSpec((1, nh, dh, bq), lambda bb, qi, ki: (bb, 0, 0, qi)),
                  pl.BlockSpec((1, nh, bk, dh), lambda bb, qi, ki: (bb, 0, kmap(qi, ki), 0)),
                  pl.BlockSpec((1, nh, dh, bk), lambda bb, qi, ki: (bb, 0, 0, kmap(qi, ki))),
                  pl.BlockSpec((1, nh, bq), lambda bb, qi, ki: (bb, 0, qi)),
                  pl.BlockSpec((1, bk, nh), lambda bb, qi, ki: (bb, kmap(qi, ki), 0))],
        out_specs=pl.BlockSpec((1, bq, nh * dh), lambda bb, qi, ki: (bb, qi, 0)),
        scratch_shapes=_attn_scratch(nh, bq, bk, dh) + [pltpu.VMEM((nh, bk, LANES), F32)],
        compiler_params=_cparams(("parallel", "parallel", "arbitrary")), name="fox_attention",
    )(qt_hm, k_hm, vt_hm, cq, ck)


def _bit_planes(words):
    a = list(words)
    j, m = 16, 0x0000FFFF
    while j:
        k = 0
        while k < 32:
            t = (a[k] ^ lax.shift_right_logical(a[k + j], jnp.int32(j))) & jnp.int32(m)
            a[k] = a[k] ^ t
            a[k + j] = a[k + j] ^ lax.shift_left(t, jnp.int32(j))
            k = (k + j + 1) & ~j
        j >>= 1
        m = (m ^ (m << j)) & 0xFFFFFFFF
    return a


def _dsa_select_kernel(qit_ref, wit_ref, ki_ref, out_ref, key_sc, pl_sc, act_sc, j_sc,
                       *, bq, kb, q_off, s_valid, topk, nh_idx, idx_bits):
    qb = pl.program_id(1)
    q_first = q_off + qb * bq
    q_last = q_first + bq - 1
    n_adm = jnp.minimum(((q_last >> CHUNK_SHIFT) + 1) * CHUNK, s_valid)
    nkb = (jnp.maximum(n_adm, topk) + kb - 1) // kb
    nkb_total = key_sc.shape[0] // kb
    gpb = kb // PLANE_ROWS
    qchunk = (q_first + lax.broadcasted_iota(jnp.int32, (kb, bq), 1)) >> CHUNK_SHIFT
    row = lax.broadcasted_iota(jnp.int32, (kb, bq), 0)

    def admissible(kpos):
        return jnp.logical_and((kpos >> CHUNK_SHIFT) <= qchunk, kpos < s_valid)

    def rows(kblk):
        return pl.ds(pl.multiple_of(kblk * kb, kb), kb)

    def score_block(kblk, carry):
        kblock = ki_ref[0, rows(kblk), :]
        sc = jnp.zeros((kb, bq), F32)
        for h in range(nh_idx):
            d = jnp.dot(kblock, qit_ref[0, h], preferred_element_type=F32)
            sc = sc + wit_ref[0, h:h + 1, :] * jnp.maximum(d, 0.0)
        sc = jnp.where(admissible(kblk * kb + row), sc + 0.0, NEG_INF)
        bits = lax.bitcast_convert_type(sc, jnp.int32)
        key = bits ^ ((bits >> 31) & 0x7FFFFFFF)
        key_sc[rows(kblk), :] = key
        for g in range(gpb):
            base = g * PLANE_ROWS
            planes = _bit_planes([key[base + SUBLANES * j:base + SUBLANES * (j + 1), :] for j in range(32)])
            planes[0] = ~planes[0]
            for i in range(32):
                pl_sc[kblk * gpb + g, i] = planes[i]
            act_sc[kblk * gpb + g] = jnp.full((SUBLANES, bq), -1, jnp.int32)
        return carry

    lax.fori_loop(0, nkb, score_block, 0)

    def bit_step(it, carry):
        thr_u, above = carry

        def cnt_blk(kblk, acc):
            for g in range(gpb):
                gi = kblk * gpb + g
                acc = acc + lax.population_count(act_sc[gi] & pl_sc[gi, it])
            return acc

        acc = lax.fori_loop(0, nkb, cnt_blk, jnp.zeros((SUBLANES, bq), jnp.int32))
        c = jnp.sum(acc.astype(F32), axis=0, keepdims=True)
        take = (above + c) >= topk

        def upd_blk(kblk, carry2):
            for g in range(gpb):
                gi = kblk * gpb + g
                a = act_sc[gi]
                x = a & pl_sc[gi, it]
                act_sc[gi] = jnp.where(take, x, a ^ x)
            return carry2

        lax.fori_loop(0, nkb, upd_blk, 0)
        bit = lax.shift_left(jnp.int32(1), 31 - it)
        return thr_u | jnp.where(take, bit, 0), above + jnp.where(take, 0.0, c)

    thr_u, n_gt = lax.fori_loop(0, 32, bit_step, (jnp.zeros((1, bq), jnp.int32), jnp.zeros((1, bq), F32)))
    thr = thr_u ^ INT_MIN

    def eq_blk(kblk, acc):
        for g in range(gpb):
            acc = acc + lax.population_count(act_sc[kblk * gpb + g])
        return acc

    n_eq = jnp.sum(lax.fori_loop(0, nkb, eq_blk, jnp.zeros((SUBLANES, bq), jnp.int32)).astype(F32),
                   axis=0, keepdims=True)
    need = topk - n_gt
    has_excess = jnp.max(n_eq - need) > 0.0
    j_sc[...] = jnp.full(j_sc.shape, 2 ** 30, jnp.int32)

    @pl.when(has_excess)
    def _():
        def count_ties_below(cand):
            def blk(kblk, acc):
                hit = jnp.logical_and(key_sc[rows(kblk), :] == thr, kblk * kb + row < cand)
                ind = jnp.where(hit, 1.0, 0.0)
                for r in range(kb // SUBLANES):
                    acc = acc + ind[r * SUBLANES:(r + 1) * SUBLANES]
                return acc
            return jnp.sum(lax.fori_loop(0, nkb, blk, jnp.zeros((SUBLANES, bq), F32)), axis=0, keepdims=True)

        def idx_step(it, lo):
            cand = lo | lax.shift_left(jnp.int32(1), idx_bits - 1 - it)
            return jnp.where(count_ties_below(cand) < need, cand, lo)

        lo = lax.fori_loop(0, idx_bits, idx_step, jnp.zeros((1, bq), jnp.int32))
        j_sc[...] = jnp.broadcast_to(lo, j_sc.shape)

    j_cut = j_sc[0:1, :]

    def write_fast(kblk, carry):
        out_ref[0, 0, rows(kblk), :] = jnp.where(key_sc[rows(kblk), :] >= thr, 0.0, NEG_INF).astype(out_ref.dtype)
        return carry

    def write_block(kblk, carry):
        key = key_sc[rows(kblk), :]
        kpos = kblk * kb + row
        tie = jnp.logical_and(key == thr, kpos <= j_cut)
        sel = jnp.logical_and(jnp.logical_or(key > thr, tie), admissible(kpos))
        out_ref[0, 0, rows(kblk), :] = jnp.where(sel, 0.0, NEG_INF).astype(out_ref.dtype)
        return carry

    def fill_block(kblk, carry):
        out_ref[0, 0, rows(kblk), :] = jnp.full((kb, bq), NEG_INF, out_ref.dtype)
        return carry

    n_full = jnp.minimum(((q_first >> CHUNK_SHIFT) + 1) * CHUNK, s_valid) // kb
    n_fast = jnp.where(has_excess, 0, n_full)
    lax.fori_loop(0, n_fast, write_fast, 0)
    lax.fori_loop(n_fast, nkb, write_block, 0)
    lax.fori_loop(nkb, nkb_total, fill_block, 0)


def _dsa_select(qit_hm, wit, ki, q_off, s_valid, topk, bq, kb):
    b, nh_idx, d_idx, tq = qit_hm.shape
    s_pad = ki.shape[1]
    return pl.pallas_call(
        functools.partial(_dsa_select_kernel, bq=bq, kb=kb, q_off=q_off, s_valid=s_valid, topk=topk,
                          nh_idx=nh_idx, idx_bits=max(1, (s_pad - 1).bit_length())),
        out_shape=jax.ShapeDtypeStruct((b, tq // bq, s_pad, bq), BF16), grid=(b, tq // bq),
        in_specs=[pl.BlockSpec((1, nh_idx, d_idx, bq), lambda bb, qb: (bb, 0, 0, qb)),
                  pl.BlockSpec((1, nh_idx, bq), lambda bb, qb: (bb, 0, qb)),
                  pl.BlockSpec((1, s_pad, d_idx), lambda bb, qb: (bb, 0, 0))],
        out_specs=pl.BlockSpec((1, 1, s_pad, bq), lambda bb, qb: (bb, qb, 0, 0)),
        scratch_shapes=[pltpu.VMEM((s_pad, bq), jnp.int32),
                        pltpu.VMEM((s_pad // PLANE_ROWS, 32, SUBLANES, bq), jnp.int32),
                        pltpu.VMEM((s_pad // PLANE_ROWS, SUBLANES, bq), jnp.int32),
                        pltpu.VMEM((SUBLANES, bq), jnp.int32)],
        compiler_params=_cparams(("parallel", "arbitrary")), name="dsa_select",
    )(qit_hm, wit, ki)


def _dsa_attn_kernel(sl_ref, qt_ref, k_ref, vt_ref, mb_ref, o_ref, m_sc, acc_sc, s_sc, mc_sc,
                     base_sc, dist_sc, *, bq, bk, q_off, nh):
    qi, ki = pl.program_id(1), pl.program_id(2)
    q_first = q_off + qi * bq
    q_last = q_first + bq - 1
    k_first = ki * bk

    @pl.when(ki == 0)
    def _():
        _attn_init(m_sc, acc_sc)

    @pl.when(k_first <= ((q_last >> CHUNK_SHIFT) << CHUNK_SHIFT) + CHUNK - 1)
    def _():
        base_sc[...] = mb_ref[0, 0].astype(F32)
        kpos = k_first + lax.broadcasted_iota(jnp.int32, (bk, bq), 0)
        qpos = q_first + lax.broadcasted_iota(jnp.int32, (bk, bq), 1)
        dist_sc[...] = jnp.abs(qpos - kpos).astype(F32)

        def qk_phase(h, slot):
            st = jnp.dot(k_ref[0, h], qt_ref[0, h], preferred_element_type=F32)
            _score_store(st - sl_ref[h] * dist_sc[...] + base_sc[...], slot, s_sc, mc_sc)

        def sm_phase(h, slot):
            _softmax_update(h, slot, vt_ref[0, h], s_sc, mc_sc, m_sc, acc_sc)

        _pipelined_heads(nh, qk_phase, sm_phase)

    @pl.when(ki == pl.num_programs(2) - 1)
    def _():
        _attn_finish(o_ref, acc_sc, nh, k_ref.shape[3])


def _dsa_attention(qt_hm, k_hm, vt_hm, mask, slopes2, q_off, bk):
    b, nh, dh, tq = qt_hm.shape
    nq, s, bq = mask.shape[1], mask.shape[2], mask.shape[3]
    nk = s // bk

    def kmap(qi, ki):
        q_last = q_off + (qi + 1) * bq - 1
        return jnp.minimum(ki, (((q_last >> CHUNK_SHIFT) << CHUNK_SHIFT) + CHUNK - 1) // bk)

    return pl.pallas_call(
        functools.partial(_dsa_attn_kernel, bq=bq, bk=bk, q_off=q_off, nh=nh),
        out_shape=jax.ShapeDtypeStruct((b, tq, nh * dh), F32), grid=(b, nq, nk),
        in_specs=[pl.BlockSpec(memory_space=pltpu.SMEM),
                  pl.BlockSpec((1, nh, dh, bq), lambda bb, qi, ki: (bb, 0, 0, qi)),
                  pl.BlockSpec((1, nh, bk, dh), lambda bb, qi, ki: (bb, 0, kmap(qi, ki), 0)),
                  pl.BlockSpec((1, nh, dh, bk), lambda bb, qi, ki: (bb, 0, 0, kmap(qi, ki))),
                  pl.BlockSpec((1, 1, bk, bq), lambda bb, qi, ki: (bb, qi, kmap(qi, ki), 0))],
        out_specs=pl.BlockSpec((1, bq, nh * dh), lambda bb, qi, ki: (bb, qi, 0)),
        scratch_shapes=_attn_scratch(nh, bq, bk, dh) + [pltpu.VMEM((bk, bq), F32), pltpu.VMEM((bk, bq), F32)],
        compiler_params=_cparams(("parallel", "parallel", "arbitrary")), name="dsa_attention",
    )(slopes2, qt_hm, k_hm, vt_hm, mask)


def _rglru_kernel(u_ref, cw_ref, cb_ref, wr_ref, br_ref, wig_ref, big_ref, lam_ref, buf0_ref, h0_ref,
                  h_ref, conv_ref, hl_ref, ubuf, a_sc, b_sc, hcar, *, tt, conv_w, nblk):
    t = pl.program_id(1)
    pad = SUBLANES
    d = u_ref.shape[2]
    blk = d // nblk

    @pl.when(t == 0)
    def _():
        ubuf[0:pad, :] = buf0_ref[0]
        hcar[...] = jnp.broadcast_to(h0_ref[0], (SUBLANES, d))

    ubuf[pad:pad + tt, :] = u_ref[0]
    uc = cb_ref[...] + ubuf[pad:pad + tt, :] * cw_ref[conv_w - 1:conv_w, :]
    for j in range(conv_w - 1):
        off = pad - (conv_w - 1) + j
        uc = uc + ubuf[off:off + tt, :] * cw_ref[j:j + 1, :]

    lam = lam_ref[...]
    neg_sp = -LRU_C * (jnp.maximum(-lam, 0.0) + jnp.log1p(jnp.exp(-jnp.abs(lam))))
    for n in range(nblk):
        cs = slice(n * blk, (n + 1) * blk)
        ucn = uc[:, cs]
        ub = ucn.astype(BF16)
        r = jax.nn.sigmoid(jnp.dot(ub, wr_ref[n], preferred_element_type=F32) + br_ref[:, cs])
        i = jax.nn.sigmoid(jnp.dot(ub, wig_ref[n], preferred_element_type=F32) + big_ref[:, cs])
        log_a = r * neg_sp[:, cs]
        a = jnp.exp(log_a)
        a_sc[:, cs] = a
        b_sc[:, cs] = jnp.sqrt(-jnp.tanh(log_a) * (a * a + 1.0)) * (i * ucn)

    row = lax.broadcasted_iota(jnp.int32, (SUBLANES, d), 0)

    def group(gi, hprev):
        r0 = pl.multiple_of(gi * SUBLANES, SUBLANES)
        av = a_sc[pl.ds(r0, SUBLANES), :]
        bv = b_sc[pl.ds(r0, SUBLANES), :]
        s = 1
        while s < SUBLANES:
            a_sh = pltpu.roll(av, s, axis=0)
            b_sh = pltpu.roll(bv, s, axis=0)
            m = row >= s
            bv = jnp.where(m, av * b_sh + bv, bv)
            av = jnp.where(m, av * a_sh, av)
            s *= 2
        hrows = av * hprev + bv
        h_ref[0, pl.ds(r0, SUBLANES), :] = hrows
        return jnp.broadcast_to(hrows[SUBLANES - 1:SUBLANES, :], (SUBLANES, d))

    hlast = lax.fori_loop(0, tt // SUBLANES, group, hcar[...])
    hcar[...] = hlast
    hl_ref[0] = hlast[0:1, :]
    tail = ubuf[tt:tt + pad, :]
    conv_ref[0] = tail
    ubuf[0:pad, :] = tail


def _rglru(u, conv_w, conv_b, w_rg, b_rg, w_ig, b_ig, lam, buf0, h0, tt):
    b, t, d = u.shape
    cw = conv_w.shape[0]
    nblk, blk = w_rg.shape[0], w_rg.shape[1]
    vec = lambda a: a.reshape(1, d)
    fixed2 = lambda bb, ti: (0, 0)
    fixed3 = lambda bb, ti: (0, 0, 0)
    perb = lambda bb, ti: (bb, 0, 0)
    return pl.pallas_call(
        functools.partial(_rglru_kernel, tt=tt, conv_w=cw, nblk=nblk),
        out_shape=[jax.ShapeDtypeStruct((b, t, d), F32), jax.ShapeDtypeStruct((b, SUBLANES, d), F32),
                   jax.ShapeDtypeStruct((b, 1, d), F32)],
        grid=(b, t // tt),
        in_specs=[pl.BlockSpec((1, tt, d), lambda bb, ti: (bb, ti, 0)),
                  pl.BlockSpec((cw, d), fixed2), pl.BlockSpec((1, d), fixed2),
                  pl.BlockSpec((nblk, blk, blk), fixed3), pl.BlockSpec((1, d), fixed2),
                  pl.BlockSpec((nblk, blk, blk), fixed3), pl.BlockSpec((1, d), fixed2),
                  pl.BlockSpec((1, d), fixed2),
                  pl.BlockSpec((1, SUBLANES, d), perb), pl.BlockSpec((1, 1, d), perb)],
        out_specs=[pl.BlockSpec((1, tt, d), lambda bb, ti: (bb, ti, 0)),
                   pl.BlockSpec((1, SUBLANES, d), perb), pl.BlockSpec((1, 1, d), perb)],
        scratch_shapes=[pltpu.VMEM((SUBLANES + tt, d), F32), pltpu.VMEM((tt, d), F32),
                        pltpu.VMEM((tt, d), F32), pltpu.VMEM((SUBLANES, d), F32)],
        compiler_params=_cparams(("parallel", "arbitrary")), name="rglru",
    )(u, conv_w, vec(conv_b), w_rg.astype(BF16), vec(b_rg), w_ig.astype(BF16), vec(b_ig), vec(lam), buf0, h0)


def _pad_to(a, axis, n):
    extra = n - a.shape[axis]
    if extra == 0:
        return a
    widths = [(0, 0)] * a.ndim
    widths[axis] = (0, extra)
    return jnp.pad(a, widths)


def _round_up(n, m):
    return (n + m - 1) // m * m


class _Group:
    def __init__(self, x, past):
        self.b, self.t, self.d = x.shape
        self.past = past
        self.flat = past > 0
        self.s_valid = past + self.t
        if self.flat:
            self.tq = _round_up(self.t, LANES)
            self.s_pad = _round_up(self.s_valid, PLANE_ROWS)
            self.fox_bq = self.dsa_bq = self.tq
            self.bk = self.s_pad
        else:
            self.tq = self.s_pad = self.t
            self.fox_bq = _pick(self.t, 512)
            self.dsa_bq = _pick(self.t, 256)
            self.bk = _pick(self.t, 512)

    def proj_view(self, x):
        return x.reshape(1, self.b * self.t, self.d) if self.flat else x

    def proj_bm(self):
        return self.b * self.t if self.flat else _pick(self.t, 256)

    def rows(self, a):
        return a.reshape(self.b, self.t, a.shape[-1])

    def heads(self, a):
        if not self.flat:
            return a
        return a.reshape(a.shape[1], self.b, self.t, a.shape[-1]).transpose(1, 0, 2, 3)

    def heads_t(self, a):
        if not self.flat:
            return a
        return a.reshape(a.shape[1], a.shape[2], self.b, self.t).transpose(2, 0, 1, 3)

    def pad_q(self, a):
        return _pad_to(a, a.ndim - 1, self.tq)

    def keys(self, new_hm, cache):
        if cache is None:
            return new_hm
        c = cache.astype(BF16).transpose(0, 2, 1, 3)
        return _pad_to(jnp.concatenate([c, new_hm], axis=2), 2, self.s_pad)

    def keys_t(self, new_t, cache):
        if cache is None:
            return new_t
        c = cache.astype(BF16).transpose(0, 2, 3, 1)
        return _pad_to(jnp.concatenate([c, new_t], axis=3), 3, self.s_pad)


def _qkvg_plan(width, dh):
    plan = [(0, 0, width, dh ** -0.5 * LOG2E, [(0, "headT")]),
            (0, width, width, 1.0, [(1, "head"), (3, "row")]),
            (0, 2 * width, width, 1.0, [(2, "headT"), (4, "row")]),
            (0, 3 * width, width, 1.0, [(5, "row")])]
    outs = [("headT", width, BF16), ("head", width, BF16), ("headT", width, BF16)] + [("row", width, F32)] * 3
    return plan, outs


def _mixer_a(grp, x, w_main, w_f, b_f, cache):
    nh = b_f.shape[0]
    da = w_main.shape[1] // 4
    plan, outs = _qkvg_plan(da, da // nh)
    plan = plan + [(1, 0, LANES, 1.0, [(6, "row")])]
    outs = outs + [("row", LANES, F32)]
    qt, k_hm, vt, k, v, g, fl = _inproj(grp.proj_view(x), [w_main, w_f], plan, outs, grp.proj_bm())
    qt, k_hm, vt = grp.pad_q(grp.heads_t(qt)), grp.heads(k_hm), grp.heads_t(vt)
    k, v, g = grp.rows(k), grp.rows(v), grp.rows(g)
    z = grp.rows(fl)[:, :, :nh].transpose(0, 2, 1)
    ck_, cv_ = (None, None) if cache is None else cache[:2]
    if cache is not None:
        z = _pad_to(jnp.concatenate([cache[2].astype(F32).transpose(0, 2, 1), z], axis=2), 2, grp.s_pad)
    lf_all, c_all = _logf_cumsum(z, b_f, grp.past, grp.s_valid)
    logf = lf_all[:, :, grp.past:grp.s_valid].transpose(0, 2, 1)
    cq = grp.pad_q(c_all[:, :, grp.past:grp.s_valid])
    o = _fox_attention(qt, grp.keys(k_hm, ck_), grp.keys_t(vt, cv_), cq, c_all.transpose(0, 2, 1),
                       grp.past, grp.fox_bq, grp.bk)[:, :grp.t]
    dh = da // nh
    return o, g, (k.reshape(grp.b, grp.t, nh, dh), v.reshape(grp.b, grp.t, nh, dh), logf)


def _mixer_b(grp, x, w_in, conv_w, conv_b, w_rg, b_rg, w_ig, b_ig, lam, state):
    dr = w_in.shape[1] // 2
    plan = [(0, 0, dr, 1.0, [(0, "row")]), (0, dr, dr, 1.0, [(1, "row")])]
    u, g = _inproj(grp.proj_view(x), [w_in], plan, [("row", dr, F32)] * 2, grp.proj_bm())
    u, g = grp.rows(u), grp.rows(g)
    cw = conv_w.shape[0]
    if state is None:
        buf0 = jnp.zeros((grp.b, SUBLANES, dr), F32)
        h0 = jnp.zeros((grp.b, 1, dr), F32)
    else:
        buf, h0 = state
        buf0 = jnp.pad(buf.astype(F32), ((0, 0), (SUBLANES - (cw - 1), 0), (0, 0)))
        h0 = h0.astype(F32).reshape(grp.b, 1, dr)
    h, tail, hl = _rglru(u, conv_w, conv_b, w_rg, b_rg, w_ig, b_ig, lam, buf0, h0, _pick(grp.t, 256))
    return h, g, (tail[:, SUBLANES - (cw - 1):], hl[:, 0])


def _mixer_c(grp, x, w_main, w_idx, nh, nh_idx, d_idx, cache):
    dc = w_main.shape[1] // 4
    wq = nh_idx * d_idx
    plan, outs = _qkvg_plan(dc, dc // nh)
    plan = plan + [(1, 0, wq, 1.0, [(6, "row")]), (1, wq, LANES, 1.0, [(7, "row")])]
    outs = outs + [("row", wq, BF16), ("row", LANES, F32)]
    qt, k_hm, vt, k, v, g, qi, kw = _inproj(grp.proj_view(x), [w_main, w_idx], plan, outs, grp.proj_bm())
    qt, k_hm, vt = grp.pad_q(grp.heads_t(qt)), grp.heads(k_hm), grp.heads_t(vt)
    k, v, g, qi, kw = grp.rows(k), grp.rows(v), grp.rows(g), grp.rows(qi), grp.rows(kw)
    ki = kw[:, :, :d_idx]
    wit = grp.pad_q(kw[:, :, d_idx:d_idx + nh_idx].transpose(0, 2, 1))
    qit = grp.pad_q(qi.reshape(grp.b, grp.t, nh_idx, d_idx).transpose(0, 2, 3, 1))
    ck_, cv_ = (None, None) if cache is None else cache[:2]
    ki_all = ki if cache is None else _pad_to(jnp.concatenate([cache[2].astype(F32), ki], axis=1), 1, grp.s_pad)
    topk = min(TOPK_MAX, grp.s_valid // 4)
    mask = _dsa_select(qit, wit, ki_all.astype(BF16), grp.past, grp.s_valid, topk, grp.dsa_bq, grp.bk)
    slopes2 = 2.0 ** (-8.0 * jnp.arange(1, nh + 1, dtype=F32) / nh) * LOG2E
    o = _dsa_attention(qt, grp.keys(k_hm, ck_), grp.keys_t(vt, cv_), mask, slopes2, grp.past, grp.bk)[:, :grp.t]
    dh = dc // nh
    return o, g, (k.reshape(grp.b, grp.t, nh, dh), v.reshape(grp.b, grp.t, nh, dh), ki)


def _run_trunk(x, p, caches, past):
    depth = p["ln_g"].shape[0]
    alpha = (2 * depth) ** 0.25
    grp = _Group(x, past)
    new_a, new_b, new_c = [], [], []
    for i in range(depth):
        j, kind = i // 3, i % 3
        if kind == 0:
            cache = None if caches is None else (caches["a_k"][j], caches["a_v"][j], caches["a_logf"][j])
            o, g, st = _mixer_a(grp, x, p["w_main_a"][j], p["w_f_a"][j], p["b_f_a"][j], cache)
            new_a.append(st)
            w_out = p["w_out_a"][j]
        elif kind == 1:
            state = None if caches is None else (caches["b_conv"][j], caches["b_h"][j])
            o, g, st = _mixer_b(grp, x, p["w_in_b"][j], p["conv_w_b"][j], p["conv_b_b"][j], p["w_rg_b"][j],
                                p["b_rg_b"][j], p["w_ig_b"][j], p["b_ig_b"][j], p["lam_b"][j], state)
            new_b.append(st)
            w_out = p["w_out_b"][j]
        else:
            cache = None if caches is None else (caches["c_k"][j], caches["c_v"][j], caches["c_kidx"][j])
            o, g, st = _mixer_c(grp, x, p["w_main_c"][j], p["w_idx_c"][j], p["h_c"], p["h_idx"], p["d_idx"], cache)
            new_c.append(st)
            w_out = p["w_out_c"][j]
        m = grp.b * grp.t
        x = _outproj_ln(o.reshape(m, -1), g.reshape(m, -1), x.reshape(m, grp.d), w_out,
                        p["ln_g"][i], p["ln_b"][i], alpha).reshape(grp.b, grp.t, grp.d)
    stack = lambda sts, n: jnp.stack([s[n] for s in sts])
    return (x, stack(new_a, 0), stack(new_a, 1), stack(new_a, 2), stack(new_b, 0), stack(new_b, 1),
            stack(new_c, 0), stack(new_c, 1), stack(new_c, 2))


def kernel(x_prompt, x_sample, cache_a_k, cache_a_v, cache_a_logf, state_b_conv, state_b_h, cache_c_k, cache_c_v, cache_c_kidx, w_in_a, b_f_a, w_out_a, w_in_b, conv_w_b, conv_b_b, w_rg_b, b_rg_b, w_ig_b, b_ig_b, lam_b, w_out_b, w_in_c, w_out_c, ln_g, ln_b):
    h_a = b_f_a.shape[1]
    d_a = w_out_a.shape[1]
    d_c = w_out_c.shape[1]
    h_c = cache_c_k.shape[3]
    d_idx = cache_c_kidx.shape[-1]
    h_idx = (w_in_c.shape[2] - 4 * d_c - d_idx) // (d_idx + 1)
    assert d_a // h_a == LANES and d_c // h_c == LANES, "head width must equal the lane count"
    assert w_in_a.shape[2] == 4 * d_a + h_a and d_idx + h_idx <= LANES
    past = cache_a_k.shape[2]
    assert past % CHUNK == 0 and past > 0

    w_idx = w_in_c[:, :, 4 * d_c:]
    w_idx = _pad_to(w_idx, 2, h_idx * d_idx + LANES)
    p = {"w_main_a": w_in_a[:, :, :4 * d_a].astype(BF16),
         "w_f_a": _pad_to(w_in_a[:, :, 4 * d_a:], 2, LANES).astype(BF16),
         "b_f_a": b_f_a, "w_out_a": w_out_a.astype(BF16),
         "w_in_b": w_in_b.astype(BF16), "conv_w_b": conv_w_b, "conv_b_b": conv_b_b, "w_rg_b": w_rg_b,
         "b_rg_b": b_rg_b, "w_ig_b": w_ig_b, "b_ig_b": b_ig_b, "lam_b": lam_b, "w_out_b": w_out_b.astype(BF16),
         "w_main_c": w_in_c[:, :, :4 * d_c].astype(BF16), "w_idx_c": w_idx.astype(BF16),
         "w_out_c": w_out_c.astype(BF16), "ln_g": ln_g, "ln_b": ln_b,
         "h_c": h_c, "h_idx": h_idx, "d_idx": d_idx}
    caches = {"a_k": cache_a_k, "a_v": cache_a_v, "a_logf": cache_a_logf, "b_conv": state_b_conv,
              "b_h": state_b_h, "c_k": cache_c_k, "c_v": cache_c_v, "c_kidx": cache_c_kidx}
    outs_p = _run_trunk(x_prompt, p, None, 0)
    outs_s = _run_trunk(x_sample, p, caches, past)
    return (outs_p[0], outs_s[0]) + outs_p[1:] + outs_s[1:]
```

```python
import functools
import math

import jax
import jax.numpy as jnp
import numpy as np
from jax import lax
from jax.experimental import pallas as pl
from jax.experimental.pallas import tpu as pltpu

NEG_INF = -1e30
LN_EPS = 1e-5
CHUNK = 64
CHUNK_SHIFT = 6
TOPK_MAX = 256
LRU_C = 8.0
LANES = 128
SUBLANES = 8
BF16_ROWS = 16
PLANE_ROWS = 32 * SUBLANES
INT_MIN = -(2 ** 31)
VMEM_LIMIT = 56 * 1024 * 1024
LOG2E = math.log2(math.e)

F32 = jnp.float32
BF16 = jnp.bfloat16


def _cparams(sem, flags=None):
    return pltpu.CompilerParams(dimension_semantics=sem, vmem_limit_bytes=VMEM_LIMIT, flags=flags)


def _pick(n, pref):
    if n <= pref:
        return n
    b = pref
    while n % b:
        b //= 2
    return b


def _inproj_kernel(x_ref, *refs, n_w, plan):
    w_refs, out_refs = refs[:n_w], refs[n_w:]
    xb = x_ref[0].astype(BF16)
    for w_idx, c0, width, scale, outs in plan:
        r = jnp.dot(xb, w_refs[w_idx][:, c0:c0 + width], preferred_element_type=F32)
        if scale != 1.0:
            r = r * scale
        for o_idx, kind in outs:
            o = out_refs[o_idx]
            if kind == "row":
                o[0] = r.astype(o.dtype)
            elif kind == "head":
                for h in range(width // LANES):
                    o[0, h] = r[:, h * LANES:(h + 1) * LANES].astype(o.dtype)
            else:
                for h in range(width // LANES):
                    o[0, h] = r[:, h * LANES:(h + 1) * LANES].T.astype(o.dtype)


def _inproj(x3, ws, plan, out_defs, bm):
    bx, tx, d = x3.shape
    grid = (bx, tx // bm)
    in_specs = [pl.BlockSpec((1, bm, d), lambda b, i: (b, i, 0))]
    for w in ws:
        in_specs.append(pl.BlockSpec(w.shape, lambda b, i: (0, 0)))
    out_shape, out_specs = [], []
    for kind, width, dt in out_defs:
        if kind == "row":
            out_shape.append(jax.ShapeDtypeStruct((bx, tx, width), dt))
            out_specs.append(pl.BlockSpec((1, bm, width), lambda b, i: (b, i, 0)))
        elif kind == "head":
            nh = width // LANES
            out_shape.append(jax.ShapeDtypeStruct((bx, nh, tx, LANES), dt))
            out_specs.append(pl.BlockSpec((1, nh, bm, LANES), lambda b, i: (b, 0, i, 0)))
        else:
            nh = width // LANES
            out_shape.append(jax.ShapeDtypeStruct((bx, nh, LANES, tx), dt))
            out_specs.append(pl.BlockSpec((1, nh, LANES, bm), lambda b, i: (b, 0, 0, i)))
    return pl.pallas_call(
        functools.partial(_inproj_kernel, n_w=len(ws), plan=tuple(plan)),
        out_shape=out_shape, grid=grid, in_specs=in_specs, out_specs=out_specs,
        compiler_params=_cparams(("parallel", "parallel")), name="inproj",
    )(x3, *ws)


def _outproj_ln_kernel(o_ref, g_ref, x_ref, w_ref, lg_ref, lb_ref, y_ref, *, alpha):
    g = g_ref[...]
    og = (o_ref[...] * (g * jax.nn.sigmoid(g))).astype(BF16)
    y = jnp.dot(og, w_ref[...], preferred_element_type=F32)
    z = alpha * x_ref[...] + y
    mu = jnp.mean(z, axis=-1, keepdims=True)
    zc = z - mu
    var = jnp.mean(zc * zc, axis=-1, keepdims=True)
    y_ref[...] = zc * lax.rsqrt(var + LN_EPS) * lg_ref[...] + lb_ref[...]


def _outproj_ln(o2, g2, x2, w, ln_g, ln_b, alpha):
    m, d = x2.shape
    dk = o2.shape[1]
    bm = _pick(m, 512)
    row = lambda i: (i, 0)
    fixed = lambda i: (0, 0)
    return pl.pallas_call(
        functools.partial(_outproj_ln_kernel, alpha=alpha),
        out_shape=jax.ShapeDtypeStruct((m, d), F32), grid=(m // bm,),
        in_specs=[pl.BlockSpec((bm, dk), row), pl.BlockSpec((bm, dk), row), pl.BlockSpec((bm, d), row),
                  pl.BlockSpec((dk, d), fixed), pl.BlockSpec((1, d), fixed), pl.BlockSpec((1, d), fixed)],
        out_specs=pl.BlockSpec((bm, d), row),
        compiler_params=_cparams(("parallel",)), name="outproj_ln",
    )(o2, g2, x2, w, ln_g.reshape(1, d), ln_b.reshape(1, d))


def _log_sigmoid(x):
    return -(jnp.maximum(-x, 0.0) + jnp.log1p(jnp.exp(-jnp.abs(x))))


def _logf_cumsum_kernel(z_ref, bf_ref, lf_ref, c_ref, *, p0, p1):
    z = z_ref[0]
    pos = lax.broadcasted_iota(jnp.int32, z.shape, 1)
    is_new = jnp.logical_and(pos >= p0, pos < p1)
    lf = jnp.where(is_new, _log_sigmoid(z + bf_ref[...]), z)
    lf_ref[0] = lf
    c = lf
    s = 1
    while s < z.shape[1]:
        c = c + jnp.where(pos >= s, pltpu.roll(c, s, axis=1), 0.0)
        s *= 2
    c_ref[0] = c


def _logf_cumsum(z, b_f, p0, p1):
    b, h, l = z.shape
    blk = pl.BlockSpec((1, h, l), lambda i: (i, 0, 0))
    return pl.pallas_call(
        functools.partial(_logf_cumsum_kernel, p0=p0, p1=p1),
        out_shape=[jax.ShapeDtypeStruct(z.shape, F32)] * 2, grid=(b,),
        in_specs=[blk, pl.BlockSpec((h, 1), lambda i: (0, 0))], out_specs=[blk, blk],
        compiler_params=_cparams(("parallel",)), name="logf_cumsum",
    )(z, b_f.reshape(h, 1))


def _score_store(u, slot, s_sc, mc_sc):
    s_sc[slot] = u
    mc_sc[slot] = jnp.broadcast_to(jnp.max(u, axis=0, keepdims=True), mc_sc.shape[1:])


def _softmax_update(h, slot, vt, s_sc, mc_sc, m_sc, acc_sc, row_shift=None):
    u = s_sc[slot]
    m_prev = m_sc[h]
    m_cur = mc_sc[slot]
    if row_shift is not None:
        m_cur = m_cur + row_shift
    m_new = jnp.maximum(m_prev, m_cur)
    m_row = m_new[0:1]
    p = jnp.exp2(u - (m_row if row_shift is None else m_row - row_shift))
    alpha = jnp.exp2(m_prev - m_new)
    vt1 = jnp.concatenate([vt, jnp.ones((BF16_ROWS, vt.shape[1]), BF16)], axis=0)
    acc_sc[h] = acc_sc[h] * alpha[0:1] + jnp.dot(vt1, p.astype(BF16), preferred_element_type=F32)
    m_sc[h] = m_new


def _scores(k, kx, qt, qx):
    return jnp.dot(jnp.concatenate([k, kx], axis=1), jnp.concatenate([qt, qx], axis=0),
                   preferred_element_type=F32)


def _pipelined_heads(nh, qk_phase, sm_phase):
    qk_phase(0, 0)
    for h in range(nh - 1):
        qk_phase(h + 1, (h + 1) % 2)
        sm_phase(h, h % 2)
    sm_phase(nh - 1, (nh - 1) % 2)


def _attn_init(m_sc, acc_sc):
    m_sc[...] = jnp.full(m_sc.shape, -jnp.inf, F32)
    acc_sc[...] = jnp.zeros(acc_sc.shape, F32)


def _attn_finish(o_ref, acc_sc, nh, dh):
    for h in range(nh):
        a = acc_sc[h]
        o_ref[0, :, h * dh:(h + 1) * dh] = (a[0:dh] / a[dh:dh + 1]).T


def _attn_scratch(nh, bq, bk, dh):
    return [pltpu.VMEM((nh, SUBLANES, bq), F32), pltpu.VMEM((nh, dh + BF16_ROWS, bq), F32),
            pltpu.VMEM((2, bk, bq), F32), pltpu.VMEM((2, SUBLANES, bq), F32)]


def _fox_kernel(qt_ref, k_ref, vt_ref, cq_ref, kx_ref, qx_ref, o_ref, m_sc, acc_sc, s_sc, mc_sc,
                *, bq, bk, q_off, nh):
    qi, ki = pl.program_id(1), pl.program_id(2)
    q_first = q_off + qi * bq
    q_last = q_first + bq - 1
    k_first = ki * bk
    k_last = k_first + bk - 1

    @pl.when(ki == 0)
    def _():
        _attn_init(m_sc, acc_sc)

    def step(masked):
        kx = kx_ref[0]
        if masked:
            kpos = k_first + lax.broadcasted_iota(jnp.int32, (bk, bq), 0)
            qpos = q_first + lax.broadcasted_iota(jnp.int32, (bk, bq), 1)
            causal = kpos <= qpos

        def qk_phase(h, slot):
            u = _scores(k_ref[0, h], kx, qt_ref[0, h], qx_ref[h])
            if masked:
                u = jnp.where(causal, u, NEG_INF)
            _score_store(u, slot, s_sc, mc_sc)

        def sm_phase(h, slot):
            cq2 = cq_ref[0, pl.ds(h, 1), :] * LOG2E
            _softmax_update(h, slot, vt_ref[0, h], s_sc, mc_sc, m_sc, acc_sc, row_shift=cq2)

        _pipelined_heads(nh, qk_phase, sm_phase)

    needed = k_first <= q_last
    straddles = k_last > q_first

    @pl.when(jnp.logical_and(needed, straddles))
    def _():
        step(True)

    @pl.when(jnp.logical_and(needed, jnp.logical_not(straddles)))
    def _():
        step(False)

    @pl.when(ki == pl.num_programs(2) - 1)
    def _():
        _attn_finish(o_ref, acc_sc, nh, k_ref.shape[3])


def _split3(x):
    def top_half(v):
        bits = lax.bitcast_convert_type(v, jnp.uint32) & jnp.uint32(0xFFFF0000)
        return lax.bitcast_convert_type(bits, F32)

    hi = top_half(x)
    r1 = x - hi
    mid = top_half(r1)
    return hi.astype(BF16), mid.astype(BF16), (r1 - mid).astype(BF16)


def _fox_attention(qt_hm, k_hm, vt_hm, cq, ck, q_off, bq, bk):
    b, nh, dh, tq = qt_hm.shape
    s = k_hm.shape[2]
    nq, nk = tq // bq, s // bk
    w = LANES // nh
    assert w >= 3
    kx = jnp.stack(_split3(ck * (-LOG2E)), axis=-1)
    kx = _pad_to(kx, 3, w).transpose(0, 2, 1, 3).reshape(b, s, LANES)
    rows = jnp.arange(LANES)[None, :, None]
    heads = jnp.arange(nh)[:, None, None]
    qx = jnp.broadcast_to(jnp.logical_and(rows >= heads * w, rows < heads * w + 3), (nh, LANES, bq)).astype(BF16)

    def kmap(qi, ki):
        return jnp.minimum(ki, (q_off + (qi + 1) * bq - 1) // bk)

    return pl.pallas_call(
        functools.partial(_fox_kernel, bq=bq, bk=bk, q_off=q_off, nh=nh),
        out_shape=jax.ShapeDtypeStruct((b, tq, nh * dh), F32), grid=(b, nq, nk),
        in_specs=[pl.BlockSpec((1, nh, dh, bq), lambda bb, qi, ki: (bb, 0, 0, qi)),
                  pl.BlockSpec((1, nh, bk, dh), lambda bb, qi, ki: (bb, 0, kmap(qi, ki), 0)),
                  pl.BlockSpec((1, nh, dh, bk), lambda bb, qi, ki: (bb, 0, 0, kmap(qi, ki))),
                  pl.BlockSpec((1, nh, bq), lambda bb, qi, ki: (bb, 0, qi)),
                  pl.BlockSpec((1, bk, LANES), lambda bb, qi, ki: (bb, kmap(qi, ki), 0)),
                  pl.BlockSpec((nh, LANES, bq), lambda bb, qi, ki: (0, 0, 0))],
        out_specs=pl.BlockSpec((1, bq, nh * dh), lambda bb, qi, ki: (bb, qi, 0)),
        scratch_shapes=_attn_scratch(nh, bq, bk, dh),
        compiler_params=_cparams(("parallel", "parallel", "arbitrary")), name="fox_attention",
    )(qt_hm, k_hm, vt_hm, cq, kx, qx)


def _bit_planes(words):
    a = list(words)
    j, m = 16, 0x0000FFFF
    while j:
        k = 0
        while k < 32:
            t = (a[k] ^ lax.shift_right_logical(a[k + j], jnp.int32(j))) & jnp.int32(m)
            a[k] = a[k] ^ t
            a[k + j] = a[k + j] ^ lax.shift_left(t, jnp.int32(j))
            k = (k + j + 1) & ~j
        j >>= 1
        m = (m ^ (m << j)) & 0xFFFFFFFF
    return a


def _dsa_select_kernel(qit_ref, wit_ref, ki_ref, out_ref, key_sc, pl_sc, act_sc, j_sc,
                       *, bq, kb, q_off, s_valid, topk, nh_idx, idx_bits):
    qb = pl.program_id(1)
    q_first = q_off + qb * bq
    q_last = q_first + bq - 1
    n_adm = jnp.minimum(((q_last >> CHUNK_SHIFT) + 1) * CHUNK, s_valid)
    nkb = (jnp.maximum(n_adm, topk) + kb - 1) // kb
    nkb_total = key_sc.shape[0] // kb
    gpb = kb // PLANE_ROWS
    qchunk = (q_first + lax.broadcasted_iota(jnp.int32, (kb, bq), 1)) >> CHUNK_SHIFT
    row = lax.broadcasted_iota(jnp.int32, (kb, bq), 0)

    def admissible(kpos):
        return jnp.logical_and((kpos >> CHUNK_SHIFT) <= qchunk, kpos < s_valid)

    def rows(kblk):
        return pl.ds(pl.multiple_of(kblk * kb, kb), kb)

    def score_block(kblk, carry):
        kblock = ki_ref[0, rows(kblk), :]
        sc = jnp.zeros((kb, bq), F32)
        for h in range(nh_idx):
            d = jnp.dot(kblock, qit_ref[0, h], preferred_element_type=F32)
            sc = sc + wit_ref[0, h:h + 1, :] * jnp.maximum(d, 0.0)
        sc = jnp.where(admissible(kblk * kb + row), sc + 0.0, NEG_INF)
        bits = lax.bitcast_convert_type(sc, jnp.int32)
        key = bits ^ ((bits >> 31) & 0x7FFFFFFF)
        key_sc[rows(kblk), :] = key
        for g in range(gpb):
            base = g * PLANE_ROWS
            planes = _bit_planes([key[base + SUBLANES * j:base + SUBLANES * (j + 1), :] for j in range(32)])
            planes[0] = ~planes[0]
            for i in range(32):
                pl_sc[kblk * gpb + g, i] = planes[i]
            act_sc[kblk * gpb + g] = jnp.full((SUBLANES, bq), -1, jnp.int32)
        return carry

    lax.fori_loop(0, nkb, score_block, 0)

    def bit_step(it, carry):
        thr_u, above = carry

        def cnt_blk(kblk, acc):
            for g in range(gpb):
                gi = kblk * gpb + g
                acc = acc + lax.population_count(act_sc[gi] & pl_sc[gi, it])
            return acc

        acc = lax.fori_loop(0, nkb, cnt_blk, jnp.zeros((SUBLANES, bq), jnp.int32))
        c = jnp.sum(acc.astype(F32), axis=0, keepdims=True)
        take = (above + c) >= topk

        def upd_blk(kblk, carry2):
            for g in range(gpb):
                gi = kblk * gpb + g
                a = act_sc[gi]
                x = a & pl_sc[gi, it]
                act_sc[gi] = jnp.where(take, x, a ^ x)
            return carry2

        lax.fori_loop(0, nkb, upd_blk, 0)
        bit = lax.shift_left(jnp.int32(1), 31 - it)
        return thr_u | jnp.where(take, bit, 0), above + jnp.where(take, 0.0, c)

    thr_u, n_gt = lax.fori_loop(0, 32, bit_step, (jnp.zeros((1, bq), jnp.int32), jnp.zeros((1, bq), F32)))
    thr = thr_u ^ INT_MIN

    def eq_blk(kblk, acc):
        for g in range(gpb):
            acc = acc + lax.population_count(act_sc[kblk * gpb + g])
        return acc

    n_eq = jnp.sum(lax.fori_loop(0, nkb, eq_blk, jnp.zeros((SUBLANES, bq), jnp.int32)).astype(F32),
                   axis=0, keepdims=True)
    need = topk - n_gt
    has_excess = jnp.max(n_eq - need) > 0.0
    j_sc[...] = jnp.full(j_sc.shape, 2 ** 30, jnp.int32)

    @pl.when(has_excess)
    def _():
        def count_ties_below(cand):
            def blk(kblk, acc):
                hit = jnp.logical_and(key_sc[rows(kblk), :] == thr, kblk * kb + row < cand)
                ind = jnp.where(hit, 1.0, 0.0)
                for r in range(kb // SUBLANES):
                    acc = acc + ind[r * SUBLANES:(r + 1) * SUBLANES]
                return acc
            return jnp.sum(lax.fori_loop(0, nkb, blk, jnp.zeros((SUBLANES, bq), F32)), axis=0, keepdims=True)

        def idx_step(it, lo):
            cand = lo | lax.shift_left(jnp.int32(1), idx_bits - 1 - it)
            return jnp.where(count_ties_below(cand) < need, cand, lo)

        lo = lax.fori_loop(0, idx_bits, idx_step, jnp.zeros((1, bq), jnp.int32))
        j_sc[...] = jnp.broadcast_to(lo, j_sc.shape)

    j_cut = j_sc[0:1, :]

    def write_fast(kblk, carry):
        out_ref[0, 0, rows(kblk), :] = jnp.where(key_sc[rows(kblk), :] >= thr, 0.0, NEG_INF).astype(out_ref.dtype)
        return carry

    def write_block(kblk, carry):
        key = key_sc[rows(kblk), :]
        kpos = kblk * kb + row
        tie = jnp.logical_and(key == thr, kpos <= j_cut)
        sel = jnp.logical_and(jnp.logical_or(key > thr, tie), admissible(kpos))
        out_ref[0, 0, rows(kblk), :] = jnp.where(sel, 0.0, NEG_INF).astype(out_ref.dtype)
        return carry

    def fill_block(kblk, carry):
        out_ref[0, 0, rows(kblk), :] = jnp.full((kb, bq), NEG_INF, out_ref.dtype)
        return carry

    n_full = jnp.minimum(((q_first >> CHUNK_SHIFT) + 1) * CHUNK, s_valid) // kb
    n_fast = jnp.where(has_excess, 0, n_full)
    lax.fori_loop(0, n_fast, write_fast, 0)
    lax.fori_loop(n_fast, nkb, write_block, 0)
    lax.fori_loop(nkb, nkb_total, fill_block, 0)


def _dsa_select(qit_hm, wit, ki, q_off, s_valid, topk, bq, kb):
    b, nh_idx, d_idx, tq = qit_hm.shape
    s_pad = ki.shape[1]
    return pl.pallas_call(
        functools.partial(_dsa_select_kernel, bq=bq, kb=kb, q_off=q_off, s_valid=s_valid, topk=topk,
                          nh_idx=nh_idx, idx_bits=max(1, (s_pad - 1).bit_length())),
        out_shape=jax.ShapeDtypeStruct((b, tq // bq, s_pad, bq), BF16), grid=(b, tq // bq),
        in_specs=[pl.BlockSpec((1, nh_idx, d_idx, bq), lambda bb, qb: (bb, 0, 0, qb)),
                  pl.BlockSpec((1, nh_idx, bq), lambda bb, qb: (bb, 0, qb)),
                  pl.BlockSpec((1, s_pad, d_idx), lambda bb, qb: (bb, 0, 0))],
        out_specs=pl.BlockSpec((1, 1, s_pad, bq), lambda bb, qb: (bb, qb, 0, 0)),
        scratch_shapes=[pltpu.VMEM((s_pad, bq), jnp.int32),
                        pltpu.VMEM((s_pad // PLANE_ROWS, 32, SUBLANES, bq), jnp.int32),
                        pltpu.VMEM((s_pad // PLANE_ROWS, SUBLANES, bq), jnp.int32),
                        pltpu.VMEM((SUBLANES, bq), jnp.int32)],
        compiler_params=_cparams(("parallel", "arbitrary")), name="dsa_select",
    )(qit_hm, wit, ki)


def _dsa_attn_kernel(sl_ref, qt_ref, k_ref, vt_ref, mb_ref, qx_ref, o_ref, m_sc, acc_sc, s_sc, mc_sc,
                     base_sc, t_sc, *, bq, bk, q_off, nh):
    qi, ki = pl.program_id(1), pl.program_id(2)
    q_first = q_off + qi * bq
    q_last = q_first + bq - 1
    k_first = ki * bk
    k_last = k_first + bk - 1
    qposf = (q_first + lax.broadcasted_iota(jnp.int32, (1, bq), 1)).astype(F32)

    @pl.when(ki == 0)
    def _():
        _attn_init(m_sc, acc_sc)

    def step(past):
        base_sc[...] = jnp.concatenate([mb_ref[0, j] for j in range(mb_ref.shape[1])], axis=1).astype(F32)
        if past:
            kpos = k_first + lax.broadcasted_iota(jnp.int32, (bk, LANES), 0)
            lane = lax.broadcasted_iota(jnp.int32, (bk, LANES), 1)
            kx = jnp.where(lane < 3, kpos >> CHUNK_SHIFT, jnp.where(lane < 6, kpos & (CHUNK - 1), 0)).astype(BF16)
        else:
            kpos = k_first + lax.broadcasted_iota(jnp.int32, (bk, bq), 0)
            qpos = q_first + lax.broadcasted_iota(jnp.int32, (bk, bq), 1)
            t_sc[...] = jnp.minimum(kpos, 2 * qpos - kpos).astype(F32)

        def qk_phase(h, slot):
            if past:
                u = _scores(k_ref[0, h], kx, qt_ref[0, h], qx_ref[h]) + base_sc[...]
            else:
                st = jnp.dot(k_ref[0, h], qt_ref[0, h], preferred_element_type=F32)
                u = st + sl_ref[h] * t_sc[...] + base_sc[...]
            _score_store(u, slot, s_sc, mc_sc)

        def sm_phase(h, slot):
            _softmax_update(h, slot, vt_ref[0, h], s_sc, mc_sc, m_sc, acc_sc, row_shift=-sl_ref[h] * qposf)

        _pipelined_heads(nh, qk_phase, sm_phase)

    needed = k_first <= ((q_last >> CHUNK_SHIFT) << CHUNK_SHIFT) + CHUNK - 1
    past = k_last <= q_first

    @pl.when(past)
    def _():
        step(True)

    @pl.when(jnp.logical_and(needed, jnp.logical_not(past)))
    def _():
        step(False)

    @pl.when(ki == pl.num_programs(2) - 1)
    def _():
        _attn_finish(o_ref, acc_sc, nh, k_ref.shape[3])


def _dsa_attention(qt_hm, k_hm, vt_hm, mask, q_off, bq, bk):
    b, nh, dh, tq = qt_hm.shape
    s, mq = mask.shape[2], mask.shape[3]
    nq, nk, nsub = tq // bq, s // bk, bq // mq
    slopes2_np = (2.0 ** (-8.0 * np.arange(1, nh + 1) / nh) * LOG2E).astype(np.float32)
    slopes2 = jnp.asarray(slopes2_np)
    rest, pieces = slopes2_np, []
    for _ in range(3):
        piece = rest.astype(BF16).astype(np.float32)
        pieces.append(piece)
        rest = rest - piece
    col = np.zeros((nh, LANES), np.float32)
    col[:, 0:3] = np.stack(pieces, axis=1) * CHUNK
    col[:, 3:6] = np.stack(pieces, axis=1)
    qx = jnp.broadcast_to(jnp.asarray(col.astype(BF16))[:, :, None], (nh, LANES, bq))

    def kmap(qi, ki):
        q_last = q_off + (qi + 1) * bq - 1
        return jnp.minimum(ki, (((q_last >> CHUNK_SHIFT) << CHUNK_SHIFT) + CHUNK - 1) // bk)

    return pl.pallas_call(
        functools.partial(_dsa_attn_kernel, bq=bq, bk=bk, q_off=q_off, nh=nh),
        out_shape=jax.ShapeDtypeStruct((b, tq, nh * dh), F32), grid=(b, nq, nk),
        in_specs=[pl.BlockSpec(memory_space=pltpu.SMEM),
                  pl.BlockSpec((1, nh, dh, bq), lambda bb, qi, ki: (bb, 0, 0, qi)),
                  pl.BlockSpec((1, nh, bk, dh), lambda bb, qi, ki: (bb, 0, kmap(qi, ki), 0)),
                  pl.BlockSpec((1, nh, dh, bk), lambda bb, qi, ki: (bb, 0, 0, kmap(qi, ki))),
                  pl.BlockSpec((1, nsub, bk, mq), lambda bb, qi, ki: (bb, qi, kmap(qi, ki), 0)),
                  pl.BlockSpec((nh, LANES, bq), lambda bb, qi, ki: (0, 0, 0))],
        out_specs=pl.BlockSpec((1, bq, nh * dh), lambda bb, qi, ki: (bb, qi, 0)),
        scratch_shapes=_attn_scratch(nh, bq, bk, dh) + [pltpu.VMEM((bk, bq), F32), pltpu.VMEM((bk, bq), F32)],
        compiler_params=_cparams(("parallel", "parallel", "arbitrary")), name="dsa_attention",
    )(slopes2, qt_hm, k_hm, vt_hm, mask, qx)


def _rglru_kernel(u_ref, cw_ref, cb_ref, wr_ref, br_ref, wig_ref, big_ref, lam_ref, buf0_ref, h0_ref,
                  h_ref, conv_ref, hl_ref, ubuf, a_sc, b_sc, hcar, *, tt, conv_w, nblk):
    t = pl.program_id(1)
    pad = SUBLANES
    d = u_ref.shape[2]
    blk = d // nblk

    @pl.when(t == 0)
    def _():
        ubuf[0:pad, :] = buf0_ref[0]
        hcar[...] = jnp.broadcast_to(h0_ref[0], (SUBLANES, d))

    ubuf[pad:pad + tt, :] = u_ref[0]
    uc = cb_ref[...] + ubuf[pad:pad + tt, :] * cw_ref[conv_w - 1:conv_w, :]
    for j in range(conv_w - 1):
        off = pad - (conv_w - 1) + j
        uc = uc + ubuf[off:off + tt, :] * cw_ref[j:j + 1, :]

    lam = lam_ref[...]
    neg_sp = -LRU_C * (jnp.maximum(-lam, 0.0) + jnp.log1p(jnp.exp(-jnp.abs(lam))))
    for n in range(nblk):
        cs = slice(n * blk, (n + 1) * blk)
        ucn = uc[:, cs]
        ub = ucn.astype(BF16)
        r = jax.nn.sigmoid(jnp.dot(ub, wr_ref[n], preferred_element_type=F32) + br_ref[:, cs])
        i = jax.nn.sigmoid(jnp.dot(ub, wig_ref[n], preferred_element_type=F32) + big_ref[:, cs])
        log_a = r * neg_sp[:, cs]
        a = jnp.exp(log_a)
        a_sc[:, cs] = a
        b_sc[:, cs] = jnp.sqrt(-jnp.tanh(log_a) * (a * a + 1.0)) * (i * ucn)

    row = lax.broadcasted_iota(jnp.int32, (SUBLANES, d), 0)

    def group(gi, hprev):
        r0 = pl.multiple_of(gi * SUBLANES, SUBLANES)
        av = a_sc[pl.ds(r0, SUBLANES), :]
        bv = b_sc[pl.ds(r0, SUBLANES), :]
        s = 1
        while s < SUBLANES:
            a_sh = pltpu.roll(av, s, axis=0)
            b_sh = pltpu.roll(bv, s, axis=0)
            m = row >= s
            bv = jnp.where(m, av * b_sh + bv, bv)
            av = jnp.where(m, av * a_sh, av)
            s *= 2
        hrows = av * hprev + bv
        h_ref[0, pl.ds(r0, SUBLANES), :] = hrows
        return jnp.broadcast_to(hrows[SUBLANES - 1:SUBLANES, :], (SUBLANES, d))

    hlast = lax.fori_loop(0, tt // SUBLANES, group, hcar[...])
    hcar[...] = hlast
    hl_ref[0] = hlast[0:1, :]
    tail = ubuf[tt:tt + pad, :]
    conv_ref[0] = tail
    ubuf[0:pad, :] = tail


def _rglru(u, conv_w, conv_b, w_rg, b_rg, w_ig, b_ig, lam, buf0, h0, tt):
    b, t, d = u.shape
    cw = conv_w.shape[0]
    nblk, blk = w_rg.shape[0], w_rg.shape[1]
    vec = lambda a: a.reshape(1, d)
    fixed2 = lambda bb, ti: (0, 0)
    fixed3 = lambda bb, ti: (0, 0, 0)
    perb = lambda bb, ti: (bb, 0, 0)
    return pl.pallas_call(
        functools.partial(_rglru_kernel, tt=tt, conv_w=cw, nblk=nblk),
        out_shape=[jax.ShapeDtypeStruct((b, t, d), F32), jax.ShapeDtypeStruct((b, SUBLANES, d), F32),
                   jax.ShapeDtypeStruct((b, 1, d), F32)],
        grid=(b, t // tt),
        in_specs=[pl.BlockSpec((1, tt, d), lambda bb, ti: (bb, ti, 0)),
                  pl.BlockSpec((cw, d), fixed2), pl.BlockSpec((1, d), fixed2),
                  pl.BlockSpec((nblk, blk, blk), fixed3), pl.BlockSpec((1, d), fixed2),
                  pl.BlockSpec((nblk, blk, blk), fixed3), pl.BlockSpec((1, d), fixed2),
                  pl.BlockSpec((1, d), fixed2),
                  pl.BlockSpec((1, SUBLANES, d), perb), pl.BlockSpec((1, 1, d), perb)],
        out_specs=[pl.BlockSpec((1, tt, d), lambda bb, ti: (bb, ti, 0)),
                   pl.BlockSpec((1, SUBLANES, d), perb), pl.BlockSpec((1, 1, d), perb)],
        scratch_shapes=[pltpu.VMEM((SUBLANES + tt, d), F32), pltpu.VMEM((tt, d), F32),
                        pltpu.VMEM((tt, d), F32), pltpu.VMEM((SUBLANES, d), F32)],
        compiler_params=_cparams(("parallel", "arbitrary")), name="rglru",
    )(u, conv_w, vec(conv_b), w_rg.astype(BF16), vec(b_rg), w_ig.astype(BF16), vec(b_ig), vec(lam), buf0, h0)


def _pad_to(a, axis, n):
    extra = n - a.shape[axis]
    if extra == 0:
        return a
    widths = [(0, 0)] * a.ndim
    widths[axis] = (0, extra)
    return jnp.pad(a, widths)


def _round_up(n, m):
    return (n + m - 1) // m * m


class _Group:
    def __init__(self, x, past):
        self.b, self.t, self.d = x.shape
        self.past = past
        self.flat = past > 0
        self.s_valid = past + self.t
        if self.flat:
            self.tq = _round_up(self.t, LANES)
            self.s_pad = _round_up(self.s_valid, PLANE_ROWS)
            self.fox_bq = self.dsa_bq = self.tq
            self.bk = self.s_pad
        else:
            self.tq = self.s_pad = self.t
            self.fox_bq = _pick(self.t, 512)
            self.dsa_bq = _pick(self.t, 256)
            self.bk = _pick(self.t, 512)

    def proj_view(self, x):
        return x.reshape(1, self.b * self.t, self.d) if self.flat else x

    def proj_bm(self):
        return self.b * self.t if self.flat else _pick(self.t, 256)

    def rows(self, a):
        return a.reshape(self.b, self.t, a.shape[-1])

    def heads(self, a):
        if not self.flat:
            return a
        return a.reshape(a.shape[1], self.b, self.t, a.shape[-1]).transpose(1, 0, 2, 3)

    def heads_t(self, a):
        if not self.flat:
            return a
        return a.reshape(a.shape[1], a.shape[2], self.b, self.t).transpose(2, 0, 1, 3)

    def pad_q(self, a):
        return _pad_to(a, a.ndim - 1, self.tq)

    def keys(self, new_hm, cache):
        if cache is None:
            return new_hm
        c = cache.astype(BF16).transpose(0, 2, 1, 3)
        return _pad_to(jnp.concatenate([c, new_hm], axis=2), 2, self.s_pad)

    def keys_t(self, new_t, cache):
        if cache is None:
            return new_t
        c = cache.astype(BF16).transpose(0, 2, 3, 1)
        return _pad_to(jnp.concatenate([c, new_t], axis=3), 3, self.s_pad)


def _qkvg_plan(width, dh):
    plan = [(0, 0, width, dh ** -0.5 * LOG2E, [(0, "headT")]),
            (0, width, width, 1.0, [(1, "head"), (3, "row")]),
            (0, 2 * width, width, 1.0, [(2, "headT"), (4, "row")]),
            (0, 3 * width, width, 1.0, [(5, "row")])]
    outs = [("headT", width, BF16), ("head", width, BF16), ("headT", width, BF16)] + [("row", width, F32)] * 3
    return plan, outs


def _mixer_a(grp, x, w_main, w_f, b_f, cache):
    nh = b_f.shape[0]
    da = w_main.shape[1] // 4
    plan, outs = _qkvg_plan(da, da // nh)
    plan = plan + [(1, 0, LANES, 1.0, [(6, "row")])]
    outs = outs + [("row", LANES, F32)]
    qt, k_hm, vt, k, v, g, fl = _inproj(grp.proj_view(x), [w_main, w_f], plan, outs, grp.proj_bm())
    qt, k_hm, vt = grp.pad_q(grp.heads_t(qt)), grp.heads(k_hm), grp.heads_t(vt)
    k, v, g = grp.rows(k), grp.rows(v), grp.rows(g)
    z = grp.rows(fl)[:, :, :nh].transpose(0, 2, 1)
    ck_, cv_ = (None, None) if cache is None else cache[:2]
    if cache is not None:
        z = _pad_to(jnp.concatenate([cache[2].astype(F32).transpose(0, 2, 1), z], axis=2), 2, grp.s_pad)
    lf_all, c_all = _logf_cumsum(z, b_f, grp.past, grp.s_valid)
    logf = lf_all[:, :, grp.past:grp.s_valid].transpose(0, 2, 1)
    cq = grp.pad_q(c_all[:, :, grp.past:grp.s_valid])
    o = _fox_attention(qt, grp.keys(k_hm, ck_), grp.keys_t(vt, cv_), cq, c_all,
                       grp.past, grp.fox_bq, grp.bk)[:, :grp.t]
    dh = da // nh
    return o, g, (k.reshape(grp.b, grp.t, nh, dh), v.reshape(grp.b, grp.t, nh, dh), logf)


def _mixer_b(grp, x, w_in, conv_w, conv_b, w_rg, b_rg, w_ig, b_ig, lam, state):
    dr = w_in.shape[1] // 2
    plan = [(0, 0, dr, 1.0, [(0, "row")]), (0, dr, dr, 1.0, [(1, "row")])]
    u, g = _inproj(grp.proj_view(x), [w_in], plan, [("row", dr, F32)] * 2, grp.proj_bm())
    u, g = grp.rows(u), grp.rows(g)
    cw = conv_w.shape[0]
    if state is None:
        buf0 = jnp.zeros((grp.b, SUBLANES, dr), F32)
        h0 = jnp.zeros((grp.b, 1, dr), F32)
    else:
        buf, h0 = state
        buf0 = jnp.pad(buf.astype(F32), ((0, 0), (SUBLANES - (cw - 1), 0), (0, 0)))
        h0 = h0.astype(F32).reshape(grp.b, 1, dr)
    h, tail, hl = _rglru(u, conv_w, conv_b, w_rg, b_rg, w_ig, b_ig, lam, buf0, h0, _pick(grp.t, 256))
    return h, g, (tail[:, SUBLANES - (cw - 1):], hl[:, 0])


def _mixer_c(grp, x, w_main, w_idx, nh, nh_idx, d_idx, cache):
    dc = w_main.shape[1] // 4
    wq = nh_idx * d_idx
    plan, outs = _qkvg_plan(dc, dc // nh)
    plan = plan + [(1, 0, wq, 1.0, [(6, "row")]), (1, wq, LANES, 1.0, [(7, "row")])]
    outs = outs + [("row", wq, BF16), ("row", LANES, F32)]
    qt, k_hm, vt, k, v, g, qi, kw = _inproj(grp.proj_view(x), [w_main, w_idx], plan, outs, grp.proj_bm())
    qt, k_hm, vt = grp.pad_q(grp.heads_t(qt)), grp.heads(k_hm), grp.heads_t(vt)
    k, v, g, qi, kw = grp.rows(k), grp.rows(v), grp.rows(g), grp.rows(qi), grp.rows(kw)
    ki = kw[:, :, :d_idx]
    wit = grp.pad_q(kw[:, :, d_idx:d_idx + nh_idx].transpose(0, 2, 1))
    qit = grp.pad_q(qi.reshape(grp.b, grp.t, nh_idx, d_idx).transpose(0, 2, 3, 1))
    ck_, cv_ = (None, None) if cache is None else cache[:2]
    ki_all = ki if cache is None else _pad_to(jnp.concatenate([cache[2].astype(F32), ki], axis=1), 1, grp.s_pad)
    topk = min(TOPK_MAX, grp.s_valid // 4)
    mask = _dsa_select(qit, wit, ki_all.astype(BF16), grp.past, grp.s_valid, topk, grp.dsa_bq, grp.bk)
    o = _dsa_attention(qt, grp.keys(k_hm, ck_), grp.keys_t(vt, cv_), mask, grp.past, grp.fox_bq, grp.bk)[:, :grp.t]
    dh = dc // nh
    return o, g, (k.reshape(grp.b, grp.t, nh, dh), v.reshape(grp.b, grp.t, nh, dh), ki)


def _run_trunk(x, p, caches, past):
    depth = p["ln_g"].shape[0]
    alpha = (2 * depth) ** 0.25
    grp = _Group(x, past)
    new_a, new_b, new_c = [], [], []
    for i in range(depth):
        j, kind = i // 3, i % 3
        if kind == 0:
            cache = None if caches is None else (caches["a_k"][j], caches["a_v"][j], caches["a_logf"][j])
            o, g, st = _mixer_a(grp, x, p["w_main_a"][j], p["w_f_a"][j], p["b_f_a"][j], cache)
            new_a.append(st)
            w_out = p["w_out_a"][j]
        elif kind == 1:
            state = None if caches is None else (caches["b_conv"][j], caches["b_h"][j])
            o, g, st = _mixer_b(grp, x, p["w_in_b"][j], p["conv_w_b"][j], p["conv_b_b"][j], p["w_rg_b"][j],
                                p["b_rg_b"][j], p["w_ig_b"][j], p["b_ig_b"][j], p["lam_b"][j], state)
            new_b.append(st)
            w_out = p["w_out_b"][j]
        else:
            cache = None if caches is None else (caches["c_k"][j], caches["c_v"][j], caches["c_kidx"][j])
            o, g, st = _mixer_c(grp, x, p["w_main_c"][j], p["w_idx_c"][j], p["h_c"], p["h_idx"], p["d_idx"], cache)
            new_c.append(st)
            w_out = p["w_out_c"][j]
        m = grp.b * grp.t
        x = _outproj_ln(o.reshape(m, -1), g.reshape(m, -1), x.reshape(m, grp.d), w_out,
                        p["ln_g"][i], p["ln_b"][i], alpha).reshape(grp.b, grp.t, grp.d)
    stack = lambda sts, n: jnp.stack([s[n] for s in sts])
    return (x, stack(new_a, 0), stack(new_a, 1), stack(new_a, 2), stack(new_b, 0), stack(new_b, 1),
            stack(new_c, 0), stack(new_c, 1), stack(new_c, 2))


def kernel(x_prompt, x_sample, cache_a_k, cache_a_v, cache_a_logf, state_b_conv, state_b_h, cache_c_k, cache_c_v, cache_c_kidx, w_in_a, b_f_a, w_out_a, w_in_b, conv_w_b, conv_b_b, w_rg_b, b_rg_b, w_ig_b, b_ig_b, lam_b, w_out_b, w_in_c, w_out_c, ln_g, ln_b):
    h_a = b_f_a.shape[1]
    d_a = w_out_a.shape[1]
    d_c = w_out_c.shape[1]
    h_c = cache_c_k.shape[3]
    d_idx = cache_c_kidx.shape[-1]
    h_idx = (w_in_c.shape[2] - 4 * d_c - d_idx) // (d_idx + 1)
    assert d_a // h_a == LANES and d_c // h_c == LANES, "head width must equal the lane count"
    assert w_in_a.shape[2] == 4 * d_a + h_a and d_idx + h_idx <= LANES
    past = cache_a_k.shape[2]
    assert past % CHUNK == 0 and past > 0

    w_idx = w_in_c[:, :, 4 * d_c:]
    w_idx = _pad_to(w_idx, 2, h_idx * d_idx + LANES)
    p = {"w_main_a": w_in_a[:, :, :4 * d_a].astype(BF16),
         "w_f_a": _pad_to(w_in_a[:, :, 4 * d_a:], 2, LANES).astype(BF16),
         "b_f_a": b_f_a, "w_out_a": w_out_a.astype(BF16),
         "w_in_b": w_in_b.astype(BF16), "conv_w_b": conv_w_b, "conv_b_b": conv_b_b, "w_rg_b": w_rg_b,
         "b_rg_b": b_rg_b, "w_ig_b": w_ig_b, "b_ig_b": b_ig_b, "lam_b": lam_b, "w_out_b": w_out_b.astype(BF16),
         "w_main_c": w_in_c[:, :, :4 * d_c].astype(BF16), "w_idx_c": w_idx.astype(BF16),
         "w_out_c": w_out_c.astype(BF16), "ln_g": ln_g, "ln_b": ln_b,
         "h_c": h_c, "h_idx": h_idx, "d_idx": d_idx}
    caches = {"a_k": cache_a_k, "a_v": cache_a_v, "a_logf": cache_a_logf, "b_conv": state_b_conv,
              "b_h": state_b_h, "c_k": cache_c_k, "c_v": cache_c_v, "c_kidx": cache_c_kidx}
    outs_p = _run_trunk(x_prompt, p, None, 0)
    outs_s = _run_trunk(x_sample, p, caches, past)
    return (outs_p[0], outs_s[0]) + outs_p[1:] + outs_s[1:]
```

```python
import functools
import math

import jax
import jax.numpy as jnp
import numpy as np
from jax import lax
from jax.experimental import pallas as pl
from jax.experimental.pallas import tpu as pltpu

NEG_INF = -1e30
LN_EPS = 1e-5
CHUNK = 64
CHUNK_SHIFT = 6
TOPK_MAX = 256
LRU_C = 8.0
LANES = 128
SUBLANES = 8
BF16_ROWS = 16
PLANE_ROWS = 32 * SUBLANES
INT_MIN = -(2 ** 31)
VMEM_LIMIT = 56 * 1024 * 1024
LOG2E = math.log2(math.e)

F32 = jnp.float32
BF16 = jnp.bfloat16


def _cparams(sem, flags=None):
    return pltpu.CompilerParams(dimension_semantics=sem, vmem_limit_bytes=VMEM_LIMIT, flags=flags)


def _pick(n, pref):
    if n <= pref:
        return n
    b = pref
    while n % b:
        b //= 2
    return b


def _inproj_kernel(x_ref, *refs, n_w, n_alias, plan):
    w_refs, out_refs = refs[:n_w], refs[n_w + n_alias:]
    xb = x_ref[0].astype(BF16)
    for w_idx, c0, width, scale, outs in plan:
        r = jnp.dot(xb, w_refs[w_idx][:, c0:c0 + width], preferred_element_type=F32)
        if scale != 1.0:
            r = r * scale
        for o_idx, kind in outs:
            o = out_refs[o_idx]
            if kind == "row":
                o[0] = r.astype(o.dtype)
            elif kind == "stack":
                o[0, 0] = r.astype(o.dtype)
            else:
                for h in range(width // LANES):
                    o[0, h] = r[:, h * LANES:(h + 1) * LANES].T.astype(o.dtype)


def _inproj(x3, ws, plan, out_defs, bm, layer=(0, 1), stacked=None):
    bx, tx, d = x3.shape
    j, n_layers = layer
    grid = (bx, tx // bm)
    in_specs = [pl.BlockSpec((1, bm, d), lambda b, i: (b, i, 0))]
    for w in ws:
        in_specs.append(pl.BlockSpec(w.shape, lambda b, i: (0, 0)))
    stacked = list(stacked or [])
    in_specs += [pl.BlockSpec(memory_space=pl.ANY)] * len(stacked)
    out_shape, out_specs, aliases = [], [], {}
    for o_idx, (kind, width, dt) in enumerate(out_defs):
        if kind == "row":
            out_shape.append(jax.ShapeDtypeStruct((bx, tx, width), dt))
            out_specs.append(pl.BlockSpec((1, bm, width), lambda b, i: (b, i, 0)))
        elif kind == "stack":
            out_shape.append(jax.ShapeDtypeStruct((n_layers, bx, tx, width), dt))
            out_specs.append(pl.BlockSpec((1, 1, bm, width), lambda b, i: (j, b, i, 0)))
            if stacked:
                aliases[1 + len(ws) + len(aliases)] = o_idx
        else:
            nh = width // LANES
            out_shape.append(jax.ShapeDtypeStruct((bx, nh, LANES, tx), dt))
            out_specs.append(pl.BlockSpec((1, nh, LANES, bm), lambda b, i: (b, 0, 0, i)))
    assert len(aliases) == len(stacked)
    return pl.pallas_call(
        functools.partial(_inproj_kernel, n_w=len(ws), n_alias=len(stacked), plan=tuple(plan)),
        out_shape=out_shape, grid=grid, in_specs=in_specs, out_specs=out_specs, input_output_aliases=aliases,
        compiler_params=_cparams(("parallel", "parallel")), name="inproj",
    )(x3, *ws, *stacked)


def _outproj_ln_kernel(o_ref, g_ref, x_ref, w_ref, lg_ref, lb_ref, y_ref, *, alpha):
    g = g_ref[...].astype(F32)
    og = (o_ref[...].astype(F32) * (g * jax.nn.sigmoid(g))).astype(BF16)
    y = jnp.dot(og, w_ref[...], preferred_element_type=F32)
    z = alpha * x_ref[...] + y
    mu = jnp.mean(z, axis=-1, keepdims=True)
    zc = z - mu
    var = jnp.mean(zc * zc, axis=-1, keepdims=True)
    y_ref[...] = zc * lax.rsqrt(var + LN_EPS) * lg_ref[...] + lb_ref[...]


def _outproj_ln(o2, g2, x2, w, ln_g, ln_b, alpha):
    m, d = x2.shape
    dk = o2.shape[1]
    bm = _pick(m, 512)
    row = lambda i: (i, 0)
    fixed = lambda i: (0, 0)
    return pl.pallas_call(
        functools.partial(_outproj_ln_kernel, alpha=alpha),
        out_shape=jax.ShapeDtypeStruct((m, d), F32), grid=(m // bm,),
        in_specs=[pl.BlockSpec((bm, dk), row), pl.BlockSpec((bm, dk), row), pl.BlockSpec((bm, d), row),
                  pl.BlockSpec((dk, d), fixed), pl.BlockSpec((1, d), fixed), pl.BlockSpec((1, d), fixed)],
        out_specs=pl.BlockSpec((bm, d), row),
        compiler_params=_cparams(("parallel",)), name="outproj_ln",
    )(o2, g2, x2, w, ln_g.reshape(1, d), ln_b.reshape(1, d))


def _log_sigmoid(x):
    return -(jnp.maximum(-x, 0.0) + jnp.log1p(jnp.exp(-jnp.abs(x))))


def _logf_cumsum_kernel(z_ref, bf_ref, lf_ref, c_ref, *, p0, p1):
    z = z_ref[0]
    pos = lax.broadcasted_iota(jnp.int32, z.shape, 1)
    is_new = jnp.logical_and(pos >= p0, pos < p1)
    lf = jnp.where(is_new, _log_sigmoid(z + bf_ref[...]), z)
    lf_ref[0] = lf
    c = lf
    s = 1
    while s < z.shape[1]:
        c = c + jnp.where(pos >= s, pltpu.roll(c, s, axis=1), 0.0)
        s *= 2
    c_ref[0] = c


def _logf_cumsum(z, b_f, p0, p1):
    b, h, l = z.shape
    blk = pl.BlockSpec((1, h, l), lambda i: (i, 0, 0))
    return pl.pallas_call(
        functools.partial(_logf_cumsum_kernel, p0=p0, p1=p1),
        out_shape=[jax.ShapeDtypeStruct(z.shape, F32)] * 2, grid=(b,),
        in_specs=[blk, pl.BlockSpec((h, 1), lambda i: (0, 0))], out_specs=[blk, blk],
        compiler_params=_cparams(("parallel",)), name="logf_cumsum",
    )(z, b_f.reshape(h, 1))


def _score_store(u, slot, s_sc, mc_sc):
    s_sc[slot] = u
    mc_sc[slot] = jnp.broadcast_to(jnp.max(u, axis=0, keepdims=True), mc_sc.shape[1:])


def _softmax_update(h, slot, vt, s_sc, mc_sc, m_sc, acc_sc, row_shift=None):
    u = s_sc[slot]
    m_prev = m_sc[h]
    m_cur = mc_sc[slot]
    if row_shift is not None:
        m_cur = m_cur + row_shift
    m_new = jnp.maximum(m_prev, m_cur)
    m_row = m_new[0:1]
    p = jnp.exp2(u - (m_row if row_shift is None else m_row - row_shift))
    alpha = jnp.exp2(m_prev - m_new)
    vt1 = jnp.concatenate([vt, jnp.ones((BF16_ROWS, vt.shape[1]), BF16)], axis=0)
    acc_sc[h] = acc_sc[h] * alpha[0:1] + jnp.dot(vt1, p.astype(BF16), preferred_element_type=F32)
    m_sc[h] = m_new


def _head(k_ref, h):
    dh = LANES
    return k_ref[0, :, h * dh:(h + 1) * dh]


def _scores(k, kx, qt, qx):
    return jnp.dot(jnp.concatenate([k, kx], axis=1), jnp.concatenate([qt, qx], axis=0),
                   preferred_element_type=F32)


def _pipelined_heads(nh, qk_phase, sm_phase):
    qk_phase(0, 0)
    for h in range(nh - 1):
        qk_phase(h + 1, (h + 1) % 2)
        sm_phase(h, h % 2)
    sm_phase(nh - 1, (nh - 1) % 2)


def _attn_init(m_sc, acc_sc):
    m_sc[...] = jnp.full(m_sc.shape, -jnp.inf, F32)
    acc_sc[...] = jnp.zeros(acc_sc.shape, F32)


def _attn_finish(o_ref, acc_sc, nh, dh):
    for h in range(nh):
        a = acc_sc[h]
        o_ref[0, :, h * dh:(h + 1) * dh] = (a[0:dh] / a[dh:dh + 1]).T.astype(o_ref.dtype)


def _attn_scratch(nh, bq, bk, dh):
    return [pltpu.VMEM((nh, SUBLANES, bq), F32), pltpu.VMEM((nh, dh + BF16_ROWS, bq), F32),
            pltpu.VMEM((2, bk, bq), F32), pltpu.VMEM((2, SUBLANES, bq), F32)]


def _fox_kernel(qt_ref, k_ref, vt_ref, cq_ref, kx_ref, qx_ref, o_ref, m_sc, acc_sc, s_sc, mc_sc,
                *, bq, bk, q_off, nh):
    qi, ki = pl.program_id(1), pl.program_id(2)
    q_first = q_off + qi * bq
    q_last = q_first + bq - 1
    k_first = ki * bk
    k_last = k_first + bk - 1

    @pl.when(ki == 0)
    def _():
        _attn_init(m_sc, acc_sc)

    def step(masked):
        kx = kx_ref[0]
        if masked:
            kpos = k_first + lax.broadcasted_iota(jnp.int32, (bk, bq), 0)
            qpos = q_first + lax.broadcasted_iota(jnp.int32, (bk, bq), 1)
            causal = kpos <= qpos

        def qk_phase(h, slot):
            u = _scores(_head(k_ref, h), kx, qt_ref[0, h], qx_ref[h])
            if masked:
                u = jnp.where(causal, u, NEG_INF)
            _score_store(u, slot, s_sc, mc_sc)

        def sm_phase(h, slot):
            cq2 = cq_ref[0, pl.ds(h, 1), :] * LOG2E
            _softmax_update(h, slot, vt_ref[0, h], s_sc, mc_sc, m_sc, acc_sc, row_shift=cq2)

        _pipelined_heads(nh, qk_phase, sm_phase)

    needed = k_first <= q_last
    straddles = k_last > q_first

    @pl.when(jnp.logical_and(needed, straddles))
    def _():
        step(True)

    @pl.when(jnp.logical_and(needed, jnp.logical_not(straddles)))
    def _():
        step(False)

    @pl.when(ki == pl.num_programs(2) - 1)
    def _():
        _attn_finish(o_ref, acc_sc, nh, vt_ref.shape[2])


def _split3(x):
    def top_half(v):
        bits = lax.bitcast_convert_type(v, jnp.uint32) & jnp.uint32(0xFFFF0000)
        return lax.bitcast_convert_type(bits, F32)

    hi = top_half(x)
    r1 = x - hi
    mid = top_half(r1)
    return hi.astype(BF16), mid.astype(BF16), (r1 - mid).astype(BF16)


def _fox_attention(qt_hm, k_rm, vt_hm, cq, ck, q_off, bq, bk):
    b, nh, dh, tq = qt_hm.shape
    s = k_rm.shape[1]
    nq, nk = tq // bq, s // bk
    w = LANES // nh
    assert w >= 3
    kx = jnp.stack(_split3(ck * (-LOG2E)), axis=-1)
    kx = _pad_to(kx, 3, w).transpose(0, 2, 1, 3).reshape(b, s, LANES)
    rows = jnp.arange(LANES)[None, :, None]
    heads = jnp.arange(nh)[:, None, None]
    qx = jnp.broadcast_to(jnp.logical_and(rows >= heads * w, rows < heads * w + 3), (nh, LANES, bq)).astype(BF16)

    def kmap(qi, ki):
        return jnp.minimum(ki, (q_off + (qi + 1) * bq - 1) // bk)

    return pl.pallas_call(
        functools.partial(_fox_kernel, bq=bq, bk=bk, q_off=q_off, nh=nh),
        out_shape=jax.ShapeDtypeStruct((b, tq, nh * dh), BF16), grid=(b, nq, nk),
        in_specs=[pl.BlockSpec((1, nh, dh, bq), lambda bb, qi, ki: (bb, 0, 0, qi)),
                  pl.BlockSpec((1, bk, nh * dh), lambda bb, qi, ki: (bb, kmap(qi, ki), 0)),
                  pl.BlockSpec((1, nh, dh, bk), lambda bb, qi, ki: (bb, 0, 0, kmap(qi, ki))),
                  pl.BlockSpec((1, nh, bq), lambda bb, qi, ki: (bb, 0, qi)),
                  pl.BlockSpec((1, bk, LANES), lambda bb, qi, ki: (bb, kmap(qi, ki), 0)),
                  pl.BlockSpec((nh, LANES, bq), lambda bb, qi, ki: (0, 0, 0))],
        out_specs=pl.BlockSpec((1, bq, nh * dh), lambda bb, qi, ki: (bb, qi, 0)),
        scratch_shapes=_attn_scratch(nh, bq, bk, dh),
        compiler_params=_cparams(("parallel", "parallel", "arbitrary")), name="fox_attention",
    )(qt_hm, k_rm, vt_hm, cq, kx, qx)


def _bit_planes(words):
    a = list(words)
    j, m = 16, 0x0000FFFF
    while j:
        k = 0
        while k < 32:
            t = (a[k] ^ lax.shift_right_logical(a[k + j], jnp.int32(j))) & jnp.int32(m)
            a[k] = a[k] ^ t
            a[k + j] = a[k + j] ^ lax.shift_left(t, jnp.int32(j))
            k = (k + j + 1) & ~j
        j >>= 1
        m = (m ^ (m << j)) & 0xFFFFFFFF
    return a


def _dsa_select_kernel(qit_ref, wit_ref, ki_ref, out_ref, key_sc, pl_sc, act_sc, j_sc,
                       *, bq, kb, q_off, s_valid, topk, nh_idx, idx_bits):
    qb = pl.program_id(1)
    q_first = q_off + qb * bq
    q_last = q_first + bq - 1
    n_adm = jnp.minimum(((q_last >> CHUNK_SHIFT) + 1) * CHUNK, s_valid)
    nkb = (jnp.maximum(n_adm, topk) + kb - 1) // kb
    nkb_total = key_sc.shape[0] // kb
    gpb = kb // PLANE_ROWS
    nkb4 = nkb // 4
    qchunk = (q_first + lax.broadcasted_iota(jnp.int32, (kb, bq), 1)) >> CHUNK_SHIFT
    row = lax.broadcasted_iota(jnp.int32, (kb, bq), 0)

    def admissible(kpos):
        return jnp.logical_and((kpos >> CHUNK_SHIFT) <= qchunk, kpos < s_valid)

    def rows(kblk):
        return pl.ds(pl.multiple_of(kblk * kb, kb), kb)

    def score_block(kblk, carry):
        kblock = ki_ref[0, rows(kblk), :]
        sc = jnp.zeros((kb, bq), F32)
        for h in range(nh_idx):
            d = jnp.dot(kblock, qit_ref[0, h], preferred_element_type=F32)
            sc = sc + wit_ref[0, h:h + 1, :] * jnp.maximum(d, 0.0)
        sc = jnp.where(admissible(kblk * kb + row), sc + 0.0, NEG_INF)
        bits = lax.bitcast_convert_type(sc, jnp.int32)
        key = bits ^ ((bits >> 31) & 0x7FFFFFFF)
        key_sc[rows(kblk), :] = key
        for g in range(gpb):
            base = g * PLANE_ROWS
            planes = _bit_planes([key[base + SUBLANES * j:base + SUBLANES * (j + 1), :] for j in range(32)])
            planes[0] = ~planes[0]
            for i in range(32):
                pl_sc[kblk * gpb + g, i] = planes[i]
            act_sc[kblk * gpb + g] = jnp.full((SUBLANES, bq), -1, jnp.int32)
        return carry

    lax.fori_loop(0, nkb, score_block, 0)

    def bit_step(it, carry):
        thr_u, above = carry

        def cnt_groups(g0, n, acc):
            for g in range(n):
                acc = acc + lax.population_count(act_sc[g0 + g] & pl_sc[g0 + g, it])
            return acc

        acc = lax.fori_loop(0, nkb4, lambda i, a: cnt_groups(i * gpb * 4, gpb * 4, a),
                            jnp.zeros((SUBLANES, bq), jnp.int32))
        acc = lax.fori_loop(nkb4 * 4, nkb, lambda i, a: cnt_groups(i * gpb, gpb, a), acc)
        c = jnp.sum(acc.astype(F32), axis=0, keepdims=True)
        take = (above + c) >= topk

        def upd_groups(g0, n, carry2):
            for g in range(n):
                a = act_sc[g0 + g]
                x = a & pl_sc[g0 + g, it]
                act_sc[g0 + g] = jnp.where(take, x, a ^ x)
            return carry2

        lax.fori_loop(0, nkb4, lambda i, c2: upd_groups(i * gpb * 4, gpb * 4, c2), 0)
        lax.fori_loop(nkb4 * 4, nkb, lambda i, c2: upd_groups(i * gpb, gpb, c2), 0)
        bit = lax.shift_left(jnp.int32(1), 31 - it)
        return thr_u | jnp.where(take, bit, 0), above + jnp.where(take, 0.0, c)

    thr_u, n_gt = lax.fori_loop(0, 32, bit_step, (jnp.zeros((1, bq), jnp.int32), jnp.zeros((1, bq), F32)))
    thr = thr_u ^ INT_MIN

    def eq_blk(kblk, acc):
        for g in range(gpb):
            acc = acc + lax.population_count(act_sc[kblk * gpb + g])
        return acc

    n_eq = jnp.sum(lax.fori_loop(0, nkb, eq_blk, jnp.zeros((SUBLANES, bq), jnp.int32)).astype(F32),
                   axis=0, keepdims=True)
    need = topk - n_gt
    has_excess = jnp.max(n_eq - need) > 0.0
    j_sc[...] = jnp.full(j_sc.shape, 2 ** 30, jnp.int32)

    @pl.when(has_excess)
    def _():
        def count_ties_below(cand):
            def blk(kblk, acc):
                hit = jnp.logical_and(key_sc[rows(kblk), :] == thr, kblk * kb + row < cand)
                ind = jnp.where(hit, 1.0, 0.0)
                for r in range(kb // SUBLANES):
                    acc = acc + ind[r * SUBLANES:(r + 1) * SUBLANES]
                return acc
            return jnp.sum(lax.fori_loop(0, nkb, blk, jnp.zeros((SUBLANES, bq), F32)), axis=0, keepdims=True)

        def idx_step(it, lo):
            cand = lo | lax.shift_left(jnp.int32(1), idx_bits - 1 - it)
            return jnp.where(count_ties_below(cand) < need, cand, lo)

        lo = lax.fori_loop(0, idx_bits, idx_step, jnp.zeros((1, bq), jnp.int32))
        j_sc[...] = jnp.broadcast_to(lo, j_sc.shape)

    j_cut = j_sc[0:1, :]

    def write_fast(kblk, carry):
        out_ref[0, 0, rows(kblk), :] = jnp.where(key_sc[rows(kblk), :] >= thr, 0.0, NEG_INF).astype(out_ref.dtype)
        return carry

    def write_block(kblk, carry):
        key = key_sc[rows(kblk), :]
        kpos = kblk * kb + row
        tie = jnp.logical_and(key == thr, kpos <= j_cut)
        sel = jnp.logical_and(jnp.logical_or(key > thr, tie), admissible(kpos))
        out_ref[0, 0, rows(kblk), :] = jnp.where(sel, 0.0, NEG_INF).astype(out_ref.dtype)
        return carry

    def fill_block(kblk, carry):
        out_ref[0, 0, rows(kblk), :] = jnp.full((kb, bq), NEG_INF, out_ref.dtype)
        return carry

    n_full = jnp.minimum(((q_first >> CHUNK_SHIFT) + 1) * CHUNK, s_valid) // kb
    n_fast = jnp.where(has_excess, 0, n_full)
    lax.fori_loop(0, n_fast, write_fast, 0)
    lax.fori_loop(n_fast, nkb, write_block, 0)
    lax.fori_loop(nkb, nkb_total, fill_block, 0)


def _dsa_select(qit_hm, wit, ki, q_off, s_valid, topk, bq, kb):
    b, nh_idx, d_idx, tq = qit_hm.shape
    s_pad = ki.shape[1]
    return pl.pallas_call(
        functools.partial(_dsa_select_kernel, bq=bq, kb=kb, q_off=q_off, s_valid=s_valid, topk=topk,
                          nh_idx=nh_idx, idx_bits=max(1, (s_pad - 1).bit_length())),
        out_shape=jax.ShapeDtypeStruct((b, tq // bq, s_pad, bq), BF16), grid=(b, tq // bq),
        in_specs=[pl.BlockSpec((1, nh_idx, d_idx, bq), lambda bb, qb: (bb, 0, 0, qb)),
                  pl.BlockSpec((1, nh_idx, bq), lambda bb, qb: (bb, 0, qb)),
                  pl.BlockSpec((1, s_pad, d_idx), lambda bb, qb: (bb, 0, 0))],
        out_specs=pl.BlockSpec((1, 1, s_pad, bq), lambda bb, qb: (bb, qb, 0, 0)),
        scratch_shapes=[pltpu.VMEM((s_pad, bq), jnp.int32),
                        pltpu.VMEM((s_pad // PLANE_ROWS, 32, SUBLANES, bq), jnp.int32),
                        pltpu.VMEM((s_pad // PLANE_ROWS, SUBLANES, bq), jnp.int32),
                        pltpu.VMEM((SUBLANES, bq), jnp.int32)],
        compiler_params=_cparams(("parallel", "arbitrary")), name="dsa_select",
    )(qit_hm, wit, ki)


def _dsa_attn_kernel(sl_ref, qt_ref, k_ref, vt_ref, mb_ref, qx_ref, o_ref, m_sc, acc_sc, s_sc, mc_sc,
                     base_sc, t_sc, *, bq, bk, q_off, nh):
    qi, ki = pl.program_id(1), pl.program_id(2)
    q_first = q_off + qi * bq
    q_last = q_first + bq - 1
    k_first = ki * bk
    k_last = k_first + bk - 1
    qposf = (q_first + lax.broadcasted_iota(jnp.int32, (1, bq), 1)).astype(F32)

    @pl.when(ki == 0)
    def _():
        _attn_init(m_sc, acc_sc)

    def step(past):
        base_sc[...] = jnp.concatenate([mb_ref[0, j] for j in range(mb_ref.shape[1])], axis=1).astype(F32)
        if past:
            kpos = k_first + lax.broadcasted_iota(jnp.int32, (bk, LANES), 0)
            lane = lax.broadcasted_iota(jnp.int32, (bk, LANES), 1)
            kx = jnp.where(lane < 3, kpos >> CHUNK_SHIFT, jnp.where(lane < 6, kpos & (CHUNK - 1), 0)).astype(BF16)
        else:
            kpos = k_first + lax.broadcasted_iota(jnp.int32, (bk, bq), 0)
            qpos = q_first + lax.broadcasted_iota(jnp.int32, (bk, bq), 1)
            t_sc[...] = jnp.minimum(kpos, 2 * qpos - kpos).astype(F32)

        def qk_phase(h, slot):
            if past:
                u = _scores(_head(k_ref, h), kx, qt_ref[0, h], qx_ref[h]) + base_sc[...]
            else:
                st = jnp.dot(_head(k_ref, h), qt_ref[0, h], preferred_element_type=F32)
                u = st + sl_ref[h] * t_sc[...] + base_sc[...]
            _score_store(u, slot, s_sc, mc_sc)

        def sm_phase(h, slot):
            _softmax_update(h, slot, vt_ref[0, h], s_sc, mc_sc, m_sc, acc_sc, row_shift=-sl_ref[h] * qposf)

        _pipelined_heads(nh, qk_phase, sm_phase)

    needed = k_first <= ((q_last >> CHUNK_SHIFT) << CHUNK_SHIFT) + CHUNK - 1
    past = k_last <= q_first

    @pl.when(past)
    def _():
        step(True)

    @pl.when(jnp.logical_and(needed, jnp.logical_not(past)))
    def _():
        step(False)

    @pl.when(ki == pl.num_programs(2) - 1)
    def _():
        _attn_finish(o_ref, acc_sc, nh, vt_ref.shape[2])


def _dsa_attention(qt_hm, k_rm, vt_hm, mask, q_off, bq, bk):
    b, nh, dh, tq = qt_hm.shape
    s, mq = mask.shape[2], mask.shape[3]
    nq, nk, nsub = tq // bq, s // bk, bq // mq
    slopes2_np = (2.0 ** (-8.0 * np.arange(1, nh + 1) / nh) * LOG2E).astype(np.float32)
    slopes2 = jnp.asarray(slopes2_np)
    rest, pieces = slopes2_np, []
    for _ in range(3):
        piece = rest.astype(BF16).astype(np.float32)
        pieces.append(piece)
        rest = rest - piece
    col = np.zeros((nh, LANES), np.float32)
    col[:, 0:3] = np.stack(pieces, axis=1) * CHUNK
    col[:, 3:6] = np.stack(pieces, axis=1)
    qx = jnp.broadcast_to(jnp.asarray(col.astype(BF16))[:, :, None], (nh, LANES, bq))

    def kmap(qi, ki):
        q_last = q_off + (qi + 1) * bq - 1
        return jnp.minimum(ki, (((q_last >> CHUNK_SHIFT) << CHUNK_SHIFT) + CHUNK - 1) // bk)

    return pl.pallas_call(
        functools.partial(_dsa_attn_kernel, bq=bq, bk=bk, q_off=q_off, nh=nh),
        out_shape=jax.ShapeDtypeStruct((b, tq, nh * dh), BF16), grid=(b, nq, nk),
        in_specs=[pl.BlockSpec(memory_space=pltpu.SMEM),
                  pl.BlockSpec((1, nh, dh, bq), lambda bb, qi, ki: (bb, 0, 0, qi)),
                  pl.BlockSpec((1, bk, nh * dh), lambda bb, qi, ki: (bb, kmap(qi, ki), 0)),
                  pl.BlockSpec((1, nh, dh, bk), lambda bb, qi, ki: (bb, 0, 0, kmap(qi, ki))),
                  pl.BlockSpec((1, nsub, bk, mq), lambda bb, qi, ki: (bb, qi, kmap(qi, ki), 0)),
                  pl.BlockSpec((nh, LANES, bq), lambda bb, qi, ki: (0, 0, 0))],
        out_specs=pl.BlockSpec((1, bq, nh * dh), lambda bb, qi, ki: (bb, qi, 0)),
        scratch_shapes=_attn_scratch(nh, bq, bk, dh) + [pltpu.VMEM((bk, bq), F32), pltpu.VMEM((bk, bq), F32)],
        compiler_params=_cparams(("parallel", "parallel", "arbitrary")), name="dsa_attention",
    )(slopes2, qt_hm, k_rm, vt_hm, mask, qx)


def _rglru_kernel(u_ref, cw_ref, cb_ref, wr_ref, br_ref, wig_ref, big_ref, lam_ref, buf0_ref, h0_ref,
                  h_ref, conv_ref, hl_ref, ubuf, a_sc, b_sc, hbuf, hcar, *, tt, conv_w, nblk):
    t = pl.program_id(1)
    pad = SUBLANES
    d = u_ref.shape[2]
    blk = d // nblk

    @pl.when(t == 0)
    def _():
        ubuf[0:pad, :] = buf0_ref[0]
        hcar[...] = jnp.broadcast_to(h0_ref[0], (SUBLANES, d))

    ubuf[pad:pad + tt, :] = u_ref[0]
    uc = cb_ref[...] + ubuf[pad:pad + tt, :] * cw_ref[conv_w - 1:conv_w, :]
    for j in range(conv_w - 1):
        off = pad - (conv_w - 1) + j
        uc = uc + ubuf[off:off + tt, :] * cw_ref[j:j + 1, :]

    lam = lam_ref[...]
    neg_sp = -LRU_C * (jnp.maximum(-lam, 0.0) + jnp.log1p(jnp.exp(-jnp.abs(lam))))
    for n in range(nblk):
        cs = slice(n * blk, (n + 1) * blk)
        ucn = uc[:, cs]
        ub = ucn.astype(BF16)
        r = jax.nn.sigmoid(jnp.dot(ub, wr_ref[n], preferred_element_type=F32) + br_ref[:, cs])
        i = jax.nn.sigmoid(jnp.dot(ub, wig_ref[n], preferred_element_type=F32) + big_ref[:, cs])
        log_a = r * neg_sp[:, cs]
        a = jnp.exp(log_a)
        a_sc[:, cs] = a
        b_sc[:, cs] = jnp.sqrt(-jnp.tanh(log_a) * (a * a + 1.0)) * (i * ucn)

    row = lax.broadcasted_iota(jnp.int32, (SUBLANES, d), 0)

    def group(gi, hprev):
        r0 = pl.multiple_of(gi * SUBLANES, SUBLANES)
        av = a_sc[pl.ds(r0, SUBLANES), :]
        bv = b_sc[pl.ds(r0, SUBLANES), :]
        s = 1
        while s < SUBLANES:
            a_sh = pltpu.roll(av, s, axis=0)
            b_sh = pltpu.roll(bv, s, axis=0)
            m = row >= s
            bv = jnp.where(m, av * b_sh + bv, bv)
            av = jnp.where(m, av * a_sh, av)
            s *= 2
        hrows = av * hprev + bv
        hbuf[pl.ds(r0, SUBLANES), :] = hrows
        return jnp.broadcast_to(hrows[SUBLANES - 1:SUBLANES, :], (SUBLANES, d))

    hlast = lax.fori_loop(0, tt // SUBLANES, group, hcar[...])
    h_ref[0] = hbuf[...].astype(h_ref.dtype)
    hcar[...] = hlast
    hl_ref[0] = hlast[0:1, :]
    tail = ubuf[tt:tt + pad, :]
    conv_ref[0] = tail
    ubuf[0:pad, :] = tail


def _rglru(u, conv_w, conv_b, w_rg, b_rg, w_ig, b_ig, lam, buf0, h0, tt):
    b, t, d = u.shape
    cw = conv_w.shape[0]
    nblk, blk = w_rg.shape[0], w_rg.shape[1]
    vec = lambda a: a.reshape(1, d)
    fixed2 = lambda bb, ti: (0, 0)
    fixed3 = lambda bb, ti: (0, 0, 0)
    perb = lambda bb, ti: (bb, 0, 0)
    return pl.pallas_call(
        functools.partial(_rglru_kernel, tt=tt, conv_w=cw, nblk=nblk),
        out_shape=[jax.ShapeDtypeStruct((b, t, d), BF16), jax.ShapeDtypeStruct((b, SUBLANES, d), F32),
                   jax.ShapeDtypeStruct((b, 1, d), F32)],
        grid=(b, t // tt),
        in_specs=[pl.BlockSpec((1, tt, d), lambda bb, ti: (bb, ti, 0)),
                  pl.BlockSpec((cw, d), fixed2), pl.BlockSpec((1, d), fixed2),
                  pl.BlockSpec((nblk, blk, blk), fixed3), pl.BlockSpec((1, d), fixed2),
                  pl.BlockSpec((nblk, blk, blk), fixed3), pl.BlockSpec((1, d), fixed2),
                  pl.BlockSpec((1, d), fixed2),
                  pl.BlockSpec((1, SUBLANES, d), perb), pl.BlockSpec((1, 1, d), perb)],
        out_specs=[pl.BlockSpec((1, tt, d), lambda bb, ti: (bb, ti, 0)),
                   pl.BlockSpec((1, SUBLANES, d), perb), pl.BlockSpec((1, 1, d), perb)],
        scratch_shapes=[pltpu.VMEM((SUBLANES + tt, d), F32), pltpu.VMEM((tt, d), F32),
                        pltpu.VMEM((tt, d), F32), pltpu.VMEM((tt, d), F32), pltpu.VMEM((SUBLANES, d), F32)],
        compiler_params=_cparams(("parallel", "arbitrary")), name="rglru",
    )(u, conv_w, vec(conv_b), w_rg.astype(BF16), vec(b_rg), w_ig.astype(BF16), vec(b_ig), vec(lam), buf0, h0)


def _pad_to(a, axis, n):
    extra = n - a.shape[axis]
    if extra == 0:
        return a
    widths = [(0, 0)] * a.ndim
    widths[axis] = (0, extra)
    return jnp.pad(a, widths)


def _round_up(n, m):
    return (n + m - 1) // m * m


class _Group:
    def __init__(self, x, past):
        self.b, self.t, self.d = x.shape
        self.past = past
        self.flat = past > 0
        self.s_valid = past + self.t
        if self.flat:
            self.tq = _round_up(self.t, LANES)
            self.s_pad = _round_up(self.s_valid, PLANE_ROWS)
            self.fox_bq = self.dsa_bq = self.tq
            self.bk = self.s_pad
        else:
            self.tq = self.s_pad = self.t
            self.fox_bq = _pick(self.t, 512)
            self.dsa_bq = _pick(self.t, 256)
            self.bk = _pick(self.t, 512)

    def proj_view(self, x):
        return x.reshape(1, self.b * self.t, self.d) if self.flat else x

    def proj_bm(self):
        return self.b * self.t if self.flat else _pick(self.t, 256)

    def rows(self, a):
        return a.reshape(self.b, self.t, a.shape[-1])

    def heads_t(self, a):
        if not self.flat:
            return a
        return a.reshape(a.shape[1], a.shape[2], self.b, self.t).transpose(2, 0, 1, 3)

    def pad_q(self, a):
        return _pad_to(a, a.ndim - 1, self.tq)

    def keys(self, new_rm, cache):
        if cache is None:
            return new_rm
        c = cache.astype(BF16).reshape(self.b, self.past, -1)
        return _pad_to(jnp.concatenate([c, new_rm], axis=1), 1, self.s_pad)

    def keys_t(self, new_t, cache):
        if cache is None:
            return new_t
        c = cache.astype(BF16).transpose(0, 2, 3, 1)
        return _pad_to(jnp.concatenate([c, new_t], axis=3), 3, self.s_pad)


def _qkvg_plan(width, dh):
    plan = [(0, 0, width, dh ** -0.5 * LOG2E, [(0, "headT")]),
            (0, width, width, 1.0, [(1, "row"), (3, "stack")]),
            (0, 2 * width, width, 1.0, [(2, "headT"), (4, "stack")]),
            (0, 3 * width, width, 1.0, [(5, "row")])]
    outs = [("headT", width, BF16), ("row", width, BF16), ("headT", width, BF16),
            ("stack", width, F32), ("stack", width, F32), ("row", width, BF16)]
    return plan, outs


def _mixer_a(grp, x, w_main, w_f, b_f, cache, layer, stacked):
    nh = b_f.shape[0]
    da = w_main.shape[1] // 4
    plan, outs = _qkvg_plan(da, da // nh)
    plan = plan + [(1, 0, LANES, 1.0, [(6, "row")])]
    outs = outs + [("row", LANES, F32)]
    qt, k_rm, vt, k, v, g, fl = _inproj(grp.proj_view(x), [w_main, w_f], plan, outs, grp.proj_bm(), layer, stacked)
    qt, k_rm, vt, g = grp.pad_q(grp.heads_t(qt)), grp.rows(k_rm), grp.heads_t(vt), grp.rows(g)
    z = grp.rows(fl)[:, :, :nh].transpose(0, 2, 1)
    ck_, cv_ = (None, None) if cache is None else cache[:2]
    if cache is not None:
        z = _pad_to(jnp.concatenate([cache[2].astype(F32).transpose(0, 2, 1), z], axis=2), 2, grp.s_pad)
    lf_all, c_all = _logf_cumsum(z, b_f, grp.past, grp.s_valid)
    logf = lf_all[:, :, grp.past:grp.s_valid].transpose(0, 2, 1)
    cq = grp.pad_q(c_all[:, :, grp.past:grp.s_valid])
    o = _fox_attention(qt, grp.keys(k_rm, ck_), grp.keys_t(vt, cv_), cq, c_all,
                       grp.past, grp.fox_bq, grp.bk)[:, :grp.t]
    return o, g, ([k, v], logf)


def _mixer_b(grp, x, w_in, conv_w, conv_b, w_rg, b_rg, w_ig, b_ig, lam, state):
    dr = w_in.shape[1] // 2
    plan = [(0, 0, dr, 1.0, [(0, "row")]), (0, dr, dr, 1.0, [(1, "row")])]
    u, g = _inproj(grp.proj_view(x), [w_in], plan, [("row", dr, F32), ("row", dr, BF16)], grp.proj_bm())
    u, g = grp.rows(u), grp.rows(g)
    cw = conv_w.shape[0]
    if state is None:
        buf0 = jnp.zeros((grp.b, SUBLANES, dr), F32)
        h0 = jnp.zeros((grp.b, 1, dr), F32)
    else:
        buf, h0 = state
        buf0 = jnp.pad(buf.astype(F32), ((0, 0), (SUBLANES - (cw - 1), 0), (0, 0)))
        h0 = h0.astype(F32).reshape(grp.b, 1, dr)
    h, tail, hl = _rglru(u, conv_w, conv_b, w_rg, b_rg, w_ig, b_ig, lam, buf0, h0, _pick(grp.t, 256))
    return h, g, (tail[:, SUBLANES - (cw - 1):], hl[:, 0])


def _mixer_c(grp, x, w_main, w_idx, nh, nh_idx, d_idx, cache, layer, stacked):
    dc = w_main.shape[1] // 4
    wq = nh_idx * d_idx
    plan, outs = _qkvg_plan(dc, dc // nh)
    plan = plan + [(1, 0, wq, 1.0, [(6, "row")]), (1, wq, LANES, 1.0, [(7, "row")])]
    outs = outs + [("row", wq, BF16), ("row", LANES, F32)]
    qt, k_rm, vt, k, v, g, qi, kw = _inproj(grp.proj_view(x), [w_main, w_idx], plan, outs, grp.proj_bm(), layer,
                                            stacked)
    qt, k_rm, vt = grp.pad_q(grp.heads_t(qt)), grp.rows(k_rm), grp.heads_t(vt)
    g, qi, kw = grp.rows(g), grp.rows(qi), grp.rows(kw)
    ki = kw[:, :, :d_idx]
    wit = grp.pad_q(kw[:, :, d_idx:d_idx + nh_idx].transpose(0, 2, 1))
    qit = grp.pad_q(qi.reshape(grp.b, grp.t, nh_idx, d_idx).transpose(0, 2, 3, 1))
    ck_, cv_ = (None, None) if cache is None else cache[:2]
    ki_all = ki if cache is None else _pad_to(jnp.concatenate([cache[2].astype(F32), ki], axis=1), 1, grp.s_pad)
    topk = min(TOPK_MAX, grp.s_valid // 4)
    mask = _dsa_select(qit, wit, ki_all.astype(BF16), grp.past, grp.s_valid, topk, grp.dsa_bq, grp.bk)
    o = _dsa_attention(qt, grp.keys(k_rm, ck_), grp.keys_t(vt, cv_), mask, grp.past, grp.fox_bq, grp.bk)[:, :grp.t]
    return o, g, ([k, v], ki)


def _run_trunk(x, p, caches, past):
    depth = p["ln_g"].shape[0]
    alpha = (2 * depth) ** 0.25
    grp = _Group(x, past)
    n_a, n_c = (depth + 2) // 3, depth // 3
    kv_a, kv_c = None, None
    new_a, new_b, new_c = [], [], []
    for i in range(depth):
        j, kind = i // 3, i % 3
        if kind == 0:
            cache = None if caches is None else (caches["a_k"][j], caches["a_v"][j], caches["a_logf"][j])
            o, g, (kv_a, logf) = _mixer_a(grp, x, p["w_main_a"][j], p["w_f_a"][j], p["b_f_a"][j], cache,
                                          (j, n_a), kv_a)
            new_a.append(logf)
            w_out = p["w_out_a"][j]
        elif kind == 1:
            state = None if caches is None else (caches["b_conv"][j], caches["b_h"][j])
            o, g, st = _mixer_b(grp, x, p["w_in_b"][j], p["conv_w_b"][j], p["conv_b_b"][j], p["w_rg_b"][j],
                                p["b_rg_b"][j], p["w_ig_b"][j], p["b_ig_b"][j], p["lam_b"][j], state)
            new_b.append(st)
            w_out = p["w_out_b"][j]
        else:
            cache = None if caches is None else (caches["c_k"][j], caches["c_v"][j], caches["c_kidx"][j])
            o, g, (kv_c, ki) = _mixer_c(grp, x, p["w_main_c"][j], p["w_idx_c"][j], p["h_c"], p["h_idx"], p["d_idx"],
                                        cache, (j, n_c), kv_c)
            new_c.append(ki)
            w_out = p["w_out_c"][j]
        m = grp.b * grp.t
        x = _outproj_ln(o.reshape(m, -1), g.reshape(m, -1), x.reshape(m, grp.d), w_out,
                        p["ln_g"][i], p["ln_b"][i], alpha).reshape(grp.b, grp.t, grp.d)
    stack = lambda sts, n: jnp.stack([s[n] for s in sts])
    heads = lambda a, nh: a.reshape(a.shape[0], grp.b, grp.t, nh, a.shape[-1] // nh)
    h_a = p["b_f_a"].shape[1]
    return (x, heads(kv_a[0], h_a), heads(kv_a[1], h_a), jnp.stack(new_a), stack(new_b, 0), stack(new_b, 1),
            heads(kv_c[0], p["h_c"]), heads(kv_c[1], p["h_c"]), jnp.stack(new_c))


def kernel(x_prompt, x_sample, cache_a_k, cache_a_v, cache_a_logf, state_b_conv, state_b_h, cache_c_k, cache_c_v, cache_c_kidx, w_in_a, b_f_a, w_out_a, w_in_b, conv_w_b, conv_b_b, w_rg_b, b_rg_b, w_ig_b, b_ig_b, lam_b, w_out_b, w_in_c, w_out_c, ln_g, ln_b):
    h_a = b_f_a.shape[1]
    d_a = w_out_a.shape[1]
    d_c = w_out_c.shape[1]
    h_c = cache_c_k.shape[3]
    d_idx = cache_c_kidx.shape[-1]
    h_idx = (w_in_c.shape[2] - 4 * d_c - d_idx) // (d_idx + 1)
    assert d_a // h_a == LANES and d_c // h_c == LANES, "head width must equal the lane count"
    assert w_in_a.shape[2] == 4 * d_a + h_a and d_idx + h_idx <= LANES
    past = cache_a_k.shape[2]
    assert past % CHUNK == 0 and past > 0

    w_idx = w_in_c[:, :, 4 * d_c:]
    w_idx = _pad_to(w_idx, 2, h_idx * d_idx + LANES)
    p = {"w_main_a": w_in_a[:, :, :4 * d_a].astype(BF16),
         "w_f_a": _pad_to(w_in_a[:, :, 4 * d_a:], 2, LANES).astype(BF16),
         "b_f_a": b_f_a, "w_out_a": w_out_a.astype(BF16),
         "w_in_b": w_in_b.astype(BF16), "conv_w_b": conv_w_b, "conv_b_b": conv_b_b, "w_rg_b": w_rg_b,
         "b_rg_b": b_rg_b, "w_ig_b": w_ig_b, "b_ig_b": b_ig_b, "lam_b": lam_b, "w_out_b": w_out_b.astype(BF16),
         "w_main_c": w_in_c[:, :, :4 * d_c].astype(BF16), "w_idx_c": w_idx.astype(BF16),
         "w_out_c": w_out_c.astype(BF16), "ln_g": ln_g, "ln_b": ln_b,
         "h_c": h_c, "h_idx": h_idx, "d_idx": d_idx}
    caches = {"a_k": cache_a_k, "a_v": cache_a_v, "a_logf": cache_a_logf, "b_conv": state_b_conv,
              "b_h": state_b_h, "c_k": cache_c_k, "c_v": cache_c_v, "c_kidx": cache_c_kidx}
    outs_p = _run_trunk(x_prompt, p, None, 0)
    outs_s = _run_trunk(x_sample, p, caches, past)
    return (outs_p[0], outs_s[0]) + outs_p[1:] + outs_s[1:]
```

```python
import functools
import math

import jax
import jax.numpy as jnp
import numpy as np
from jax import lax
from jax.experimental import pallas as pl
from jax.experimental.pallas import tpu as pltpu

NEG_INF = -1e30
LN_EPS = 1e-5
CHUNK = 64
CHUNK_SHIFT = 6
TOPK_MAX = 256
LRU_C = 8.0
LANES = 128
SUBLANES = 8
BF16_ROWS = 16
PLANE_ROWS = 32 * SUBLANES
INT_MIN = -(2 ** 31)
VMEM_LIMIT = 56 * 1024 * 1024
LOG2E = math.log2(math.e)

F32 = jnp.float32
BF16 = jnp.bfloat16


def _cparams(sem, flags=None):
    return pltpu.CompilerParams(dimension_semantics=sem, vmem_limit_bytes=VMEM_LIMIT, flags=flags)


def _pick(n, pref):
    if n <= pref:
        return n
    b = pref
    while n % b:
        b //= 2
    return b


def _inproj_kernel(x_ref, *refs, n_w, n_alias, plan):
    w_refs, out_refs = refs[:n_w], refs[n_w + n_alias:]
    xb = x_ref[0].astype(BF16)
    for w_idx, c0, width, scale, outs in plan:
        r = jnp.dot(xb, w_refs[w_idx][:, c0:c0 + width], preferred_element_type=F32)
        if scale != 1.0:
            r = r * scale
        for o_idx, kind in outs:
            o = out_refs[o_idx]
            if kind == "row":
                o[0] = r.astype(o.dtype)
            elif kind == "stack":
                o[0, 0] = r.astype(o.dtype).reshape(o.shape[2:])
            else:
                for h in range(width // LANES):
                    o[0, h] = r[:, h * LANES:(h + 1) * LANES].T.astype(o.dtype)


def _inproj(x3, ws, plan, out_defs, bm, layer=(0, 1), stacked=None):
    bx, tx, d = x3.shape
    j, n_layers = layer
    grid = (bx, tx // bm)
    in_specs = [pl.BlockSpec((1, bm, d), lambda b, i: (b, i, 0))]
    for w in ws:
        in_specs.append(pl.BlockSpec(w.shape, lambda b, i: (0, 0)))
    stacked = list(stacked or [])
    in_specs += [pl.BlockSpec(memory_space=pl.ANY)] * len(stacked)
    out_shape, out_specs, aliases = [], [], {}
    for o_idx, (kind, width, dt) in enumerate(out_defs):
        if kind == "row":
            out_shape.append(jax.ShapeDtypeStruct((bx, tx, width), dt))
            out_specs.append(pl.BlockSpec((1, bm, width), lambda b, i: (b, i, 0)))
        elif kind == "stack":
            out_shape.append(jax.ShapeDtypeStruct((n_layers, bx, tx, width // LANES, LANES), dt))
            out_specs.append(pl.BlockSpec((1, 1, bm, width // LANES, LANES), lambda b, i: (j, b, i, 0, 0)))
            if stacked:
                aliases[1 + len(ws) + len(aliases)] = o_idx
        else:
            nh = width // LANES
            out_shape.append(jax.ShapeDtypeStruct((bx, nh, LANES, tx), dt))
            out_specs.append(pl.BlockSpec((1, nh, LANES, bm), lambda b, i: (b, 0, 0, i)))
    assert len(aliases) == len(stacked)
    return pl.pallas_call(
        functools.partial(_inproj_kernel, n_w=len(ws), n_alias=len(stacked), plan=tuple(plan)),
        out_shape=out_shape, grid=grid, in_specs=in_specs, out_specs=out_specs, input_output_aliases=aliases,
        compiler_params=_cparams(("parallel", "parallel")), name="inproj",
    )(x3, *ws, *stacked)


def _outproj_ln_kernel(o_ref, g_ref, x_ref, w_ref, lg_ref, lb_ref, y_ref, *, alpha):
    g = g_ref[...].astype(F32)
    og = (o_ref[...].astype(F32) * (g * jax.nn.sigmoid(g))).astype(BF16)
    y = jnp.dot(og, w_ref[...], preferred_element_type=F32)
    z = alpha * x_ref[...] + y
    mu = jnp.mean(z, axis=-1, keepdims=True)
    zc = z - mu
    var = jnp.mean(zc * zc, axis=-1, keepdims=True)
    y_ref[...] = zc * lax.rsqrt(var + LN_EPS) * lg_ref[...] + lb_ref[...]


def _outproj_ln(o2, g2, x2, w, ln_g, ln_b, alpha):
    m, d = x2.shape
    dk = o2.shape[1]
    bm = _pick(m, 512)
    row = lambda i: (i, 0)
    fixed = lambda i: (0, 0)
    return pl.pallas_call(
        functools.partial(_outproj_ln_kernel, alpha=alpha),
        out_shape=jax.ShapeDtypeStruct((m, d), F32), grid=(m // bm,),
        in_specs=[pl.BlockSpec((bm, dk), row), pl.BlockSpec((bm, dk), row), pl.BlockSpec((bm, d), row),
                  pl.BlockSpec((dk, d), fixed), pl.BlockSpec((1, d), fixed), pl.BlockSpec((1, d), fixed)],
        out_specs=pl.BlockSpec((bm, d), row),
        compiler_params=_cparams(("parallel",)), name="outproj_ln",
    )(o2, g2, x2, w, ln_g.reshape(1, d), ln_b.reshape(1, d))


def _log_sigmoid(x):
    return -(jnp.maximum(-x, 0.0) + jnp.log1p(jnp.exp(-jnp.abs(x))))


def _logf_cumsum_kernel(z_ref, bf_ref, lf_ref, c_ref, *, p0, p1):
    z = z_ref[0]
    pos = lax.broadcasted_iota(jnp.int32, z.shape, 1)
    is_new = jnp.logical_and(pos >= p0, pos < p1)
    lf = jnp.where(is_new, _log_sigmoid(z + bf_ref[...]), z)
    lf_ref[0] = lf
    c = lf
    s = 1
    while s < z.shape[1]:
        c = c + jnp.where(pos >= s, pltpu.roll(c, s, axis=1), 0.0)
        s *= 2
    c_ref[0] = c


def _logf_cumsum(z, b_f, p0, p1):
    b, h, l = z.shape
    blk = pl.BlockSpec((1, h, l), lambda i: (i, 0, 0))
    return pl.pallas_call(
        functools.partial(_logf_cumsum_kernel, p0=p0, p1=p1),
        out_shape=[jax.ShapeDtypeStruct(z.shape, F32)] * 2, grid=(b,),
        in_specs=[blk, pl.BlockSpec((h, 1), lambda i: (0, 0))], out_specs=[blk, blk],
        compiler_params=_cparams(("parallel",)), name="logf_cumsum",
    )(z, b_f.reshape(h, 1))


def _score_store(u, slot, s_sc, mc_sc):
    s_sc[slot] = u
    mc_sc[slot] = jnp.broadcast_to(jnp.max(u, axis=0, keepdims=True), mc_sc.shape[1:])


def _softmax_update(h, slot, vt, s_sc, mc_sc, m_sc, acc_sc, row_shift=None):
    u = s_sc[slot]
    m_prev = m_sc[h]
    m_cur = mc_sc[slot]
    if row_shift is not None:
        m_cur = m_cur + row_shift
    m_new = jnp.maximum(m_prev, m_cur)
    m_row = m_new[0:1]
    p = jnp.exp2(u - (m_row if row_shift is None else m_row - row_shift))
    alpha = jnp.exp2(m_prev - m_new)
    vt1 = jnp.concatenate([vt, jnp.ones((BF16_ROWS, vt.shape[1]), BF16)], axis=0)
    acc_sc[h] = acc_sc[h] * alpha[0:1] + jnp.dot(vt1, p.astype(BF16), preferred_element_type=F32)
    m_sc[h] = m_new


def _head(k_ref, h):
    dh = LANES
    return k_ref[0, :, h * dh:(h + 1) * dh]


def _head_t(v_ref, h, v_rows):
    if not v_rows:
        return v_ref[0, h]
    return _head(v_ref, h).astype(F32).T.astype(BF16)


def _scores(k, kx, qt, qx):
    return jnp.dot(jnp.concatenate([k, kx], axis=1), jnp.concatenate([qt, qx], axis=0),
                   preferred_element_type=F32)


def _pipelined_heads(nh, qk_phase, sm_phase):
    qk_phase(0, 0)
    for h in range(nh - 1):
        qk_phase(h + 1, (h + 1) % 2)
        sm_phase(h, h % 2)
    sm_phase(nh - 1, (nh - 1) % 2)


def _attn_init(m_sc, acc_sc):
    m_sc[...] = jnp.full(m_sc.shape, -jnp.inf, F32)
    acc_sc[...] = jnp.zeros(acc_sc.shape, F32)


def _attn_finish(o_ref, acc_sc, nh, dh):
    for h in range(nh):
        a = acc_sc[h]
        o_ref[0, :, h * dh:(h + 1) * dh] = (a[0:dh] / a[dh:dh + 1]).T.astype(o_ref.dtype)


def _attn_scratch(nh, bq, bk, dh):
    return [pltpu.VMEM((nh, SUBLANES, bq), F32), pltpu.VMEM((nh, dh + BF16_ROWS, bq), F32),
            pltpu.VMEM((2, bk, bq), F32), pltpu.VMEM((2, SUBLANES, bq), F32)]


def _fox_kernel(qt_ref, k_ref, v_ref, cq_ref, kx_ref, qx_ref, o_ref, m_sc, acc_sc, s_sc, mc_sc,
                *, bq, bk, q_off, nh, v_rows):
    qi, ki = pl.program_id(1), pl.program_id(2)
    q_first = q_off + qi * bq
    q_last = q_first + bq - 1
    k_first = ki * bk
    k_last = k_first + bk - 1

    @pl.when(ki == 0)
    def _():
        _attn_init(m_sc, acc_sc)

    def step(masked):
        kx = kx_ref[0]
        if masked:
            kpos = k_first + lax.broadcasted_iota(jnp.int32, (bk, bq), 0)
            qpos = q_first + lax.broadcasted_iota(jnp.int32, (bk, bq), 1)
            causal = kpos <= qpos

        def qk_phase(h, slot):
            u = _scores(_head(k_ref, h), kx, qt_ref[0, h], qx_ref[h])
            if masked:
                u = jnp.where(causal, u, NEG_INF)
            _score_store(u, slot, s_sc, mc_sc)

        def sm_phase(h, slot):
            cq2 = cq_ref[0, pl.ds(h, 1), :] * LOG2E
            _softmax_update(h, slot, _head_t(v_ref, h, v_rows), s_sc, mc_sc, m_sc, acc_sc, row_shift=cq2)

        _pipelined_heads(nh, qk_phase, sm_phase)

    needed = k_first <= q_last
    straddles = k_last > q_first

    @pl.when(jnp.logical_and(needed, straddles))
    def _():
        step(True)

    @pl.when(jnp.logical_and(needed, jnp.logical_not(straddles)))
    def _():
        step(False)

    @pl.when(ki == pl.num_programs(2) - 1)
    def _():
        _attn_finish(o_ref, acc_sc, nh, LANES)


def _split3(x):
    def top_half(v):
        bits = lax.bitcast_convert_type(v, jnp.uint32) & jnp.uint32(0xFFFF0000)
        return lax.bitcast_convert_type(bits, F32)

    hi = top_half(x)
    r1 = x - hi
    mid = top_half(r1)
    return hi.astype(BF16), mid.astype(BF16), (r1 - mid).astype(BF16)


def _v_spec(v, nh, dh, bk, kmap):
    if v.ndim == 3:
        return True, pl.BlockSpec((1, bk, nh * dh), lambda bb, qi, ki: (bb, kmap(qi, ki), 0))
    return False, pl.BlockSpec((1, nh, dh, bk), lambda bb, qi, ki: (bb, 0, 0, kmap(qi, ki)))


def _fox_attention(qt_hm, k_rm, v, cq, ck, q_off, bq, bk):
    b, nh, dh, tq = qt_hm.shape
    s = k_rm.shape[1]
    nq, nk = tq // bq, s // bk
    w = LANES // nh
    assert w >= 3
    kx = jnp.stack(_split3(ck * (-LOG2E)), axis=-1)
    kx = _pad_to(kx, 3, w).transpose(0, 2, 1, 3).reshape(b, s, LANES)
    rows = jnp.arange(LANES)[None, :, None]
    heads = jnp.arange(nh)[:, None, None]
    qx = jnp.broadcast_to(jnp.logical_and(rows >= heads * w, rows < heads * w + 3), (nh, LANES, bq)).astype(BF16)

    def kmap(qi, ki):
        return jnp.minimum(ki, (q_off + (qi + 1) * bq - 1) // bk)

    v_rows, v_spec = _v_spec(v, nh, dh, bk, kmap)
    return pl.pallas_call(
        functools.partial(_fox_kernel, bq=bq, bk=bk, q_off=q_off, nh=nh, v_rows=v_rows),
        out_shape=jax.ShapeDtypeStruct((b, tq, nh * dh), BF16), grid=(b, nq, nk),
        in_specs=[pl.BlockSpec((1, nh, dh, bq), lambda bb, qi, ki: (bb, 0, 0, qi)),
                  pl.BlockSpec((1, bk, nh * dh), lambda bb, qi, ki: (bb, kmap(qi, ki), 0)),
                  v_spec,
                  pl.BlockSpec((1, nh, bq), lambda bb, qi, ki: (bb, 0, qi)),
                  pl.BlockSpec((1, bk, LANES), lambda bb, qi, ki: (bb, kmap(qi, ki), 0)),
                  pl.BlockSpec((nh, LANES, bq), lambda bb, qi, ki: (0, 0, 0))],
        out_specs=pl.BlockSpec((1, bq, nh * dh), lambda bb, qi, ki: (bb, qi, 0)),
        scratch_shapes=_attn_scratch(nh, bq, bk, dh),
        compiler_params=_cparams(("parallel", "parallel", "arbitrary")), name="fox_attention",
    )(qt_hm, k_rm, v, cq, kx, qx)


def _bit_planes(words):
    a = list(words)
    j, m = 16, 0x0000FFFF
    while j:
        k = 0
        while k < 32:
            t = (a[k] ^ lax.shift_right_logical(a[k + j], jnp.int32(j))) & jnp.int32(m)
            a[k] = a[k] ^ t
            a[k + j] = a[k + j] ^ lax.shift_left(t, jnp.int32(j))
            k = (k + j + 1) & ~j
        j >>= 1
        m = (m ^ (m << j)) & 0xFFFFFFFF
    return a


def _dsa_select_kernel(qit_ref, wit_ref, ki_ref, out_ref, key_sc, pl_sc, act_sc, j_sc,
                       *, bq, kb, q_off, s_valid, topk, nh_idx, idx_bits):
    qb = pl.program_id(1)
    q_first = q_off + qb * bq
    q_last = q_first + bq - 1
    n_adm = jnp.minimum(((q_last >> CHUNK_SHIFT) + 1) * CHUNK, s_valid)
    nkb = (jnp.maximum(n_adm, topk) + kb - 1) // kb
    nkb_total = key_sc.shape[0] // kb
    gpb = kb // PLANE_ROWS
    nkb4 = nkb // 4
    qchunk = (q_first + lax.broadcasted_iota(jnp.int32, (kb, bq), 1)) >> CHUNK_SHIFT
    row = lax.broadcasted_iota(jnp.int32, (kb, bq), 0)

    def admissible(kpos):
        return jnp.logical_and((kpos >> CHUNK_SHIFT) <= qchunk, kpos < s_valid)

    def rows(kblk):
        return pl.ds(pl.multiple_of(kblk * kb, kb), kb)

    def score_block(kblk, carry):
        kblock = ki_ref[0, rows(kblk), :]
        sc = jnp.zeros((kb, bq), F32)
        for h in range(nh_idx):
            d = jnp.dot(kblock, qit_ref[0, h], preferred_element_type=F32)
            sc = sc + wit_ref[0, h:h + 1, :] * jnp.maximum(d, 0.0)
        sc = jnp.where(admissible(kblk * kb + row), sc + 0.0, NEG_INF)
        bits = lax.bitcast_convert_type(sc, jnp.int32)
        key = bits ^ ((bits >> 31) & 0x7FFFFFFF)
        key_sc[rows(kblk), :] = key
        for g in range(gpb):
            base = g * PLANE_ROWS
            planes = _bit_planes([key[base + SUBLANES * j:base + SUBLANES * (j + 1), :] for j in range(32)])
            planes[0] = ~planes[0]
            for i in range(32):
                pl_sc[kblk * gpb + g, i] = planes[i]
            act_sc[kblk * gpb + g] = jnp.full((SUBLANES, bq), -1, jnp.int32)
        return carry

    lax.fori_loop(0, nkb, score_block, 0)

    def bit_step(it, carry):
        thr_u, above = carry

        def cnt_groups(g0, n, acc):
            for g in range(n):
                acc = acc + lax.population_count(act_sc[g0 + g] & pl_sc[g0 + g, it])
            return acc

        acc = lax.fori_loop(0, nkb4, lambda i, a: cnt_groups(i * gpb * 4, gpb * 4, a),
                            jnp.zeros((SUBLANES, bq), jnp.int32))
        acc = lax.fori_loop(nkb4 * 4, nkb, lambda i, a: cnt_groups(i * gpb, gpb, a), acc)
        c = jnp.sum(acc.astype(F32), axis=0, keepdims=True)
        take = (above + c) >= topk

        def upd_groups(g0, n, carry2):
            for g in range(n):
                a = act_sc[g0 + g]
                x = a & pl_sc[g0 + g, it]
                act_sc[g0 + g] = jnp.where(take, x, a ^ x)
            return carry2

        lax.fori_loop(0, nkb4, lambda i, c2: upd_groups(i * gpb * 4, gpb * 4, c2), 0)
        lax.fori_loop(nkb4 * 4, nkb, lambda i, c2: upd_groups(i * gpb, gpb, c2), 0)
        bit = lax.shift_left(jnp.int32(1), 31 - it)
        return thr_u | jnp.where(take, bit, 0), above + jnp.where(take, 0.0, c)

    thr_u, n_gt = lax.fori_loop(0, 32, bit_step, (jnp.zeros((1, bq), jnp.int32), jnp.zeros((1, bq), F32)))
    thr = thr_u ^ INT_MIN

    def eq_blk(kblk, acc):
        for g in range(gpb):
            acc = acc + lax.population_count(act_sc[kblk * gpb + g])
        return acc

    n_eq = jnp.sum(lax.fori_loop(0, nkb, eq_blk, jnp.zeros((SUBLANES, bq), jnp.int32)).astype(F32),
                   axis=0, keepdims=True)
    need = topk - n_gt
    has_excess = jnp.max(n_eq - need) > 0.0
    j_sc[...] = jnp.full(j_sc.shape, 2 ** 30, jnp.int32)

    @pl.when(has_excess)
    def _():
        def count_ties_below(cand):
            def blk(kblk, acc):
                hit = jnp.logical_and(key_sc[rows(kblk), :] == thr, kblk * kb + row < cand)
                ind = jnp.where(hit, 1.0, 0.0)
                for r in range(kb // SUBLANES):
                    acc = acc + ind[r * SUBLANES:(r + 1) * SUBLANES]
                return acc
            return jnp.sum(lax.fori_loop(0, nkb, blk, jnp.zeros((SUBLANES, bq), F32)), axis=0, keepdims=True)

        def idx_step(it, lo):
            cand = lo | lax.shift_left(jnp.int32(1), idx_bits - 1 - it)
            return jnp.where(count_ties_below(cand) < need, cand, lo)

        lo = lax.fori_loop(0, idx_bits, idx_step, jnp.zeros((1, bq), jnp.int32))
        j_sc[...] = jnp.broadcast_to(lo, j_sc.shape)

    j_cut = j_sc[0:1, :]

    def write_fast(kblk, carry):
        out_ref[0, 0, rows(kblk), :] = jnp.where(key_sc[rows(kblk), :] >= thr, 0.0, NEG_INF).astype(out_ref.dtype)
        return carry

    def write_block(kblk, carry):
        key = key_sc[rows(kblk), :]
        kpos = kblk * kb + row
        tie = jnp.logical_and(key == thr, kpos <= j_cut)
        sel = jnp.logical_and(jnp.logical_or(key > thr, tie), admissible(kpos))
        out_ref[0, 0, rows(kblk), :] = jnp.where(sel, 0.0, NEG_INF).astype(out_ref.dtype)
        return carry

    def fill_block(kblk, carry):
        out_ref[0, 0, rows(kblk), :] = jnp.full((kb, bq), NEG_INF, out_ref.dtype)
        return carry

    n_full = jnp.minimum(((q_first >> CHUNK_SHIFT) + 1) * CHUNK, s_valid) // kb
    n_fast = jnp.where(has_excess, 0, n_full)
    lax.fori_loop(0, n_fast, write_fast, 0)
    lax.fori_loop(n_fast, nkb, write_block, 0)
    lax.fori_loop(nkb, nkb_total, fill_block, 0)


def _dsa_select(qit_hm, wit, ki, q_off, s_valid, topk, bq, kb):
    b, nh_idx, d_idx, tq = qit_hm.shape
    s_pad = ki.shape[1]
    return pl.pallas_call(
        functools.partial(_dsa_select_kernel, bq=bq, kb=kb, q_off=q_off, s_valid=s_valid, topk=topk,
                          nh_idx=nh_idx, idx_bits=max(1, (s_pad - 1).bit_length())),
        out_shape=jax.ShapeDtypeStruct((b, tq // bq, s_pad, bq), BF16), grid=(b, tq // bq),
        in_specs=[pl.BlockSpec((1, nh_idx, d_idx, bq), lambda bb, qb: (bb, 0, 0, qb)),
                  pl.BlockSpec((1, nh_idx, bq), lambda bb, qb: (bb, 0, qb)),
                  pl.BlockSpec((1, s_pad, d_idx), lambda bb, qb: (bb, 0, 0))],
        out_specs=pl.BlockSpec((1, 1, s_pad, bq), lambda bb, qb: (bb, qb, 0, 0)),
        scratch_shapes=[pltpu.VMEM((s_pad, bq), jnp.int32),
                        pltpu.VMEM((s_pad // PLANE_ROWS, 32, SUBLANES, bq), jnp.int32),
                        pltpu.VMEM((s_pad // PLANE_ROWS, SUBLANES, bq), jnp.int32),
                        pltpu.VMEM((SUBLANES, bq), jnp.int32)],
        compiler_params=_cparams(("parallel", "arbitrary")), name="dsa_select",
    )(qit_hm, wit, ki)


def _dsa_attn_kernel(sl_ref, qt_ref, k_ref, v_ref, mb_ref, qx_ref, o_ref, m_sc, acc_sc, s_sc, mc_sc,
                     base_sc, t_sc, *, bq, bk, q_off, nh, v_rows):
    qi, ki = pl.program_id(1), pl.program_id(2)
    q_first = q_off + qi * bq
    q_last = q_first + bq - 1
    k_first = ki * bk
    k_last = k_first + bk - 1
    qposf = (q_first + lax.broadcasted_iota(jnp.int32, (1, bq), 1)).astype(F32)

    @pl.when(ki == 0)
    def _():
        _attn_init(m_sc, acc_sc)

    def step(past):
        base_sc[...] = jnp.concatenate([mb_ref[0, j] for j in range(mb_ref.shape[1])], axis=1).astype(F32)
        if past:
            kpos = k_first + lax.broadcasted_iota(jnp.int32, (bk, LANES), 0)
            lane = lax.broadcasted_iota(jnp.int32, (bk, LANES), 1)
            kx = jnp.where(lane < 3, kpos >> CHUNK_SHIFT, jnp.where(lane < 6, kpos & (CHUNK - 1), 0)).astype(BF16)
        else:
            kpos = k_first + lax.broadcasted_iota(jnp.int32, (bk, bq), 0)
            qpos = q_first + lax.broadcasted_iota(jnp.int32, (bk, bq), 1)
            t_sc[...] = jnp.minimum(kpos, 2 * qpos - kpos).astype(F32)

        def qk_phase(h, slot):
            if past:
                u = _scores(_head(k_ref, h), kx, qt_ref[0, h], qx_ref[h]) + base_sc[...]
            else:
                st = jnp.dot(_head(k_ref, h), qt_ref[0, h], preferred_element_type=F32)
                u = st + sl_ref[h] * t_sc[...] + base_sc[...]
            _score_store(u, slot, s_sc, mc_sc)

        def sm_phase(h, slot):
            _softmax_update(h, slot, _head_t(v_ref, h, v_rows), s_sc, mc_sc, m_sc, acc_sc,
                            row_shift=-sl_ref[h] * qposf)

        _pipelined_heads(nh, qk_phase, sm_phase)

    needed = k_first <= ((q_last >> CHUNK_SHIFT) << CHUNK_SHIFT) + CHUNK - 1
    past = k_last <= q_first

    @pl.when(past)
    def _():
        step(True)

    @pl.when(jnp.logical_and(needed, jnp.logical_not(past)))
    def _():
        step(False)

    @pl.when(ki == pl.num_programs(2) - 1)
    def _():
        _attn_finish(o_ref, acc_sc, nh, LANES)


def _dsa_attention(qt_hm, k_rm, v, mask, q_off, bq, bk):
    b, nh, dh, tq = qt_hm.shape
    s, mq = mask.shape[2], mask.shape[3]
    nq, nk, nsub = tq // bq, s // bk, bq // mq
    slopes2_np = (2.0 ** (-8.0 * np.arange(1, nh + 1) / nh) * LOG2E).astype(np.float32)
    slopes2 = jnp.asarray(slopes2_np)
    rest, pieces = slopes2_np, []
    for _ in range(3):
        piece = rest.astype(BF16).astype(np.float32)
        pieces.append(piece)
        rest = rest - piece
    col = np.zeros((nh, LANES), np.float32)
    col[:, 0:3] = np.stack(pieces, axis=1) * CHUNK
    col[:, 3:6] = np.stack(pieces, axis=1)
    qx = jnp.broadcast_to(jnp.asarray(col.astype(BF16))[:, :, None], (nh, LANES, bq))

    def kmap(qi, ki):
        q_last = q_off + (qi + 1) * bq - 1
        return jnp.minimum(ki, (((q_last >> CHUNK_SHIFT) << CHUNK_SHIFT) + CHUNK - 1) // bk)

    v_rows, v_spec = _v_spec(v, nh, dh, bk, kmap)
    return pl.pallas_call(
        functools.partial(_dsa_attn_kernel, bq=bq, bk=bk, q_off=q_off, nh=nh, v_rows=v_rows),
        out_shape=jax.ShapeDtypeStruct((b, tq, nh * dh), BF16), grid=(b, nq, nk),
        in_specs=[pl.BlockSpec(memory_space=pltpu.SMEM),
                  pl.BlockSpec((1, nh, dh, bq), lambda bb, qi, ki: (bb, 0, 0, qi)),
                  pl.BlockSpec((1, bk, nh * dh), lambda bb, qi, ki: (bb, kmap(qi, ki), 0)),
                  v_spec,
                  pl.BlockSpec((1, nsub, bk, mq), lambda bb, qi, ki: (bb, qi, kmap(qi, ki), 0)),
                  pl.BlockSpec((nh, LANES, bq), lambda bb, qi, ki: (0, 0, 0))],
        out_specs=pl.BlockSpec((1, bq, nh * dh), lambda bb, qi, ki: (bb, qi, 0)),
        scratch_shapes=_attn_scratch(nh, bq, bk, dh) + [pltpu.VMEM((bk, bq), F32), pltpu.VMEM((bk, bq), F32)],
        compiler_params=_cparams(("parallel", "parallel", "arbitrary")), name="dsa_attention",
    )(slopes2, qt_hm, k_rm, v, mask, qx)


def _rglru_kernel(u_ref, cw_ref, cb_ref, wr_ref, br_ref, wig_ref, big_ref, lam_ref, buf0_ref, h0_ref,
                  h_ref, conv_ref, hl_ref, ubuf, a_sc, b_sc, hbuf, hcar, *, tt, conv_w, nblk):
    t = pl.program_id(1)
    pad = SUBLANES
    d = u_ref.shape[2]
    blk = d // nblk

    @pl.when(t == 0)
    def _():
        ubuf[0:pad, :] = buf0_ref[0]
        hcar[...] = jnp.broadcast_to(h0_ref[0], (SUBLANES, d))

    ubuf[pad:pad + tt, :] = u_ref[0]
    uc = cb_ref[...] + ubuf[pad:pad + tt, :] * cw_ref[conv_w - 1:conv_w, :]
    for j in range(conv_w - 1):
        off = pad - (conv_w - 1) + j
        uc = uc + ubuf[off:off + tt, :] * cw_ref[j:j + 1, :]

    lam = lam_ref[...]
    neg_sp = -LRU_C * (jnp.maximum(-lam, 0.0) + jnp.log1p(jnp.exp(-jnp.abs(lam))))
    for n in range(nblk):
        cs = slice(n * blk, (n + 1) * blk)
        ucn = uc[:, cs]
        ub = ucn.astype(BF16)
        r = jax.nn.sigmoid(jnp.dot(ub, wr_ref[n], preferred_element_type=F32) + br_ref[:, cs])
        i = jax.nn.sigmoid(jnp.dot(ub, wig_ref[n], preferred_element_type=F32) + big_ref[:, cs])
        log_a = r * neg_sp[:, cs]
        a = jnp.exp(log_a)
        a_sc[:, cs] = a
        b_sc[:, cs] = jnp.sqrt(-jnp.tanh(log_a) * (a * a + 1.0)) * (i * ucn)

    row = lax.broadcasted_iota(jnp.int32, (SUBLANES, d), 0)

    def group(gi, hprev):
        r0 = pl.multiple_of(gi * SUBLANES, SUBLANES)
        av = a_sc[pl.ds(r0, SUBLANES), :]
        bv = b_sc[pl.ds(r0, SUBLANES), :]
        s = 1
        while s < SUBLANES:
            a_sh = pltpu.roll(av, s, axis=0)
            b_sh = pltpu.roll(bv, s, axis=0)
            m = row >= s
            bv = jnp.where(m, av * b_sh + bv, bv)
            av = jnp.where(m, av * a_sh, av)
            s *= 2
        hrows = av * hprev + bv
        hbuf[pl.ds(r0, SUBLANES), :] = hrows
        return jnp.broadcast_to(hrows[SUBLANES - 1:SUBLANES, :], (SUBLANES, d))

    hlast = lax.fori_loop(0, tt // SUBLANES, group, hcar[...])
    h_ref[0] = hbuf[...].astype(h_ref.dtype)
    hcar[...] = hlast
    hl_ref[0] = hlast[0:1, :]
    tail = ubuf[tt:tt + pad, :]
    conv_ref[0] = tail
    ubuf[0:pad, :] = tail


def _rglru(u, conv_w, conv_b, w_rg, b_rg, w_ig, b_ig, lam, buf0, h0, tt):
    b, t, d = u.shape
    cw = conv_w.shape[0]
    nblk, blk = w_rg.shape[0], w_rg.shape[1]
    vec = lambda a: a.reshape(1, d)
    fixed2 = lambda bb, ti: (0, 0)
    fixed3 = lambda bb, ti: (0, 0, 0)
    perb = lambda bb, ti: (bb, 0, 0)
    return pl.pallas_call(
        functools.partial(_rglru_kernel, tt=tt, conv_w=cw, nblk=nblk),
        out_shape=[jax.ShapeDtypeStruct((b, t, d), BF16), jax.ShapeDtypeStruct((b, SUBLANES, d), F32),
                   jax.ShapeDtypeStruct((b, 1, d), F32)],
        grid=(b, t // tt),
        in_specs=[pl.BlockSpec((1, tt, d), lambda bb, ti: (bb, ti, 0)),
                  pl.BlockSpec((cw, d), fixed2), pl.BlockSpec((1, d), fixed2),
                  pl.BlockSpec((nblk, blk, blk), fixed3), pl.BlockSpec((1, d), fixed2),
                  pl.BlockSpec((nblk, blk, blk), fixed3), pl.BlockSpec((1, d), fixed2),
                  pl.BlockSpec((1, d), fixed2),
                  pl.BlockSpec((1, SUBLANES, d), perb), pl.BlockSpec((1, 1, d), perb)],
        out_specs=[pl.BlockSpec((1, tt, d), lambda bb, ti: (bb, ti, 0)),
                   pl.BlockSpec((1, SUBLANES, d), perb), pl.BlockSpec((1, 1, d), perb)],
        scratch_shapes=[pltpu.VMEM((SUBLANES + tt, d), F32), pltpu.VMEM((tt, d), F32),
                        pltpu.VMEM((tt, d), F32), pltpu.VMEM((tt, d), F32), pltpu.VMEM((SUBLANES, d), F32)],
        compiler_params=_cparams(("parallel", "arbitrary")), name="rglru",
    )(u, conv_w, vec(conv_b), w_rg.astype(BF16), vec(b_rg), w_ig.astype(BF16), vec(b_ig), vec(lam), buf0, h0)


def _pad_to(a, axis, n):
    extra = n - a.shape[axis]
    if extra == 0:
        return a
    widths = [(0, 0)] * a.ndim
    widths[axis] = (0, extra)
    return jnp.pad(a, widths)


def _round_up(n, m):
    return (n + m - 1) // m * m


class _Group:
    def __init__(self, x, past):
        self.b, self.t, self.d = x.shape
        self.past = past
        self.flat = past > 0
        self.s_valid = past + self.t
        if self.flat:
            self.tq = _round_up(self.t, LANES)
            self.s_pad = _round_up(self.s_valid, PLANE_ROWS)
            self.fox_bq = self.dsa_bq = self.tq
            self.bk = self.s_pad
        else:
            self.tq = self.s_pad = self.t
            self.fox_bq = _pick(self.t, 512)
            self.dsa_bq = _pick(self.t, 256)
            self.bk = _pick(self.t, 512)

    def proj_view(self, x):
        return x.reshape(1, self.b * self.t, self.d) if self.flat else x

    def proj_bm(self):
        return self.b * self.t if self.flat else _pick(self.t, 256)

    def rows(self, a):
        return a.reshape(self.b, self.t, a.shape[-1])

    def heads_t(self, a):
        if not self.flat:
            return a
        return a.reshape(a.shape[1], a.shape[2], self.b, self.t).transpose(2, 0, 1, 3)

    def pad_q(self, a):
        return _pad_to(a, a.ndim - 1, self.tq)

    def keys(self, new_rm, cache):
        if cache is None:
            return new_rm
        c = cache.astype(BF16).reshape(self.b, self.past, -1)
        return _pad_to(jnp.concatenate([c, new_rm], axis=1), 1, self.s_pad)

    def values(self, new, cache):
        return new if cache is None else self.keys(self.rows(new), cache)


def _qkvg_plan(width, dh, v_rows):
    plan = [(0, 0, width, dh ** -0.5 * LOG2E, [(0, "headT")]),
            (0, width, width, 1.0, [(1, "row"), (3, "stack")]),
            (0, 2 * width, width, 1.0, [(2, "row" if v_rows else "headT"), (4, "stack")]),
            (0, 3 * width, width, 1.0, [(5, "row")])]
    outs = [("headT", width, BF16), ("row", width, BF16), ("row" if v_rows else "headT", width, BF16),
            ("stack", width, F32), ("stack", width, F32), ("row", width, BF16)]
    return plan, outs


def _mixer_a(grp, x, w_main, w_f, b_f, cache, layer, stacked):
    nh = b_f.shape[0]
    da = w_main.shape[1] // 4
    plan, outs = _qkvg_plan(da, da // nh, grp.flat)
    plan = plan + [(1, 0, LANES, 1.0, [(6, "row")])]
    outs = outs + [("row", LANES, F32)]
    qt, k_rm, vt, k, v, g, fl = _inproj(grp.proj_view(x), [w_main, w_f], plan, outs, grp.proj_bm(), layer, stacked)
    qt, k_rm, g = grp.pad_q(grp.heads_t(qt)), grp.rows(k_rm), grp.rows(g)
    z = grp.rows(fl)[:, :, :nh].transpose(0, 2, 1)
    ck_, cv_ = (None, None) if cache is None else cache[:2]
    if cache is not None:
        z = _pad_to(jnp.concatenate([cache[2].astype(F32).transpose(0, 2, 1), z], axis=2), 2, grp.s_pad)
    lf_all, c_all = _logf_cumsum(z, b_f, grp.past, grp.s_valid)
    logf = lf_all[:, :, grp.past:grp.s_valid].transpose(0, 2, 1)
    cq = grp.pad_q(c_all[:, :, grp.past:grp.s_valid])
    o = _fox_attention(qt, grp.keys(k_rm, ck_), grp.values(vt, cv_), cq, c_all,
                       grp.past, grp.fox_bq, grp.bk)[:, :grp.t]
    return o, g, ([k, v], logf)


def _mixer_b(grp, x, w_in, conv_w, conv_b, w_rg, b_rg, w_ig, b_ig, lam, state):
    dr = w_in.shape[1] // 2
    plan = [(0, 0, dr, 1.0, [(0, "row")]), (0, dr, dr, 1.0, [(1, "row")])]
    u, g = _inproj(grp.proj_view(x), [w_in], plan, [("row", dr, F32), ("row", dr, BF16)], grp.proj_bm())
    u, g = grp.rows(u), grp.rows(g)
    cw = conv_w.shape[0]
    if state is None:
        buf0 = jnp.zeros((grp.b, SUBLANES, dr), F32)
        h0 = jnp.zeros((grp.b, 1, dr), F32)
    else:
        buf, h0 = state
        buf0 = jnp.pad(buf.astype(F32), ((0, 0), (SUBLANES - (cw - 1), 0), (0, 0)))
        h0 = h0.astype(F32).reshape(grp.b, 1, dr)
    h, tail, hl = _rglru(u, conv_w, conv_b, w_rg, b_rg, w_ig, b_ig, lam, buf0, h0, _pick(grp.t, 256))
    return h, g, (tail[:, SUBLANES - (cw - 1):], hl[:, 0])


def _mixer_c(grp, x, w_main, w_idx, nh, nh_idx, d_idx, cache, layer, stacked):
    dc = w_main.shape[1] // 4
    wq = nh_idx * d_idx
    plan, outs = _qkvg_plan(dc, dc // nh, grp.flat)
    plan = plan + [(1, 0, wq, 1.0, [(6, "row")]), (1, wq, LANES, 1.0, [(7, "row")])]
    outs = outs + [("row", wq, BF16), ("row", LANES, F32)]
    qt, k_rm, vt, k, v, g, qi, kw = _inproj(grp.proj_view(x), [w_main, w_idx], plan, outs, grp.proj_bm(), layer,
                                            stacked)
    qt, k_rm = grp.pad_q(grp.heads_t(qt)), grp.rows(k_rm)
    g, qi, kw = grp.rows(g), grp.rows(qi), grp.rows(kw)
    ki = kw[:, :, :d_idx]
    wit = grp.pad_q(kw[:, :, d_idx:d_idx + nh_idx].transpose(0, 2, 1))
    qit = grp.pad_q(qi.reshape(grp.b, grp.t, nh_idx, d_idx).transpose(0, 2, 3, 1))
    ck_, cv_ = (None, None) if cache is None else cache[:2]
    ki_all = ki if cache is None else _pad_to(jnp.concatenate([cache[2].astype(F32), ki], axis=1), 1, grp.s_pad)
    topk = min(TOPK_MAX, grp.s_valid // 4)
    mask = _dsa_select(qit, wit, ki_all.astype(BF16), grp.past, grp.s_valid, topk, grp.dsa_bq, grp.bk)
    o = _dsa_attention(qt, grp.keys(k_rm, ck_), grp.values(vt, cv_), mask, grp.past, grp.fox_bq, grp.bk)[:, :grp.t]
    return o, g, ([k, v], ki)


def _run_trunk(x, p, caches, past):
    depth = p["ln_g"].shape[0]
    alpha = (2 * depth) ** 0.25
    grp = _Group(x, past)
    n_a, n_c = (depth + 2) // 3, depth // 3
    kv_a, kv_c = None, None
    new_a, new_b, new_c = [], [], []
    for i in range(depth):
        j, kind = i // 3, i % 3
        if kind == 0:
            cache = None if caches is None else (caches["a_k"][j], caches["a_v"][j], caches["a_logf"][j])
            o, g, (kv_a, logf) = _mixer_a(grp, x, p["w_main_a"][j], p["w_f_a"][j], p["b_f_a"][j], cache,
                                          (j, n_a), kv_a)
            new_a.append(logf)
            w_out = p["w_out_a"][j]
        elif kind == 1:
            state = None if caches is None else (caches["b_conv"][j], caches["b_h"][j])
            o, g, st = _mixer_b(grp, x, p["w_in_b"][j], p["conv_w_b"][j], p["conv_b_b"][j], p["w_rg_b"][j],
                                p["b_rg_b"][j], p["w_ig_b"][j], p["b_ig_b"][j], p["lam_b"][j], state)
            new_b.append(st)
            w_out = p["w_out_b"][j]
        else:
            cache = None if caches is None else (caches["c_k"][j], caches["c_v"][j], caches["c_kidx"][j])
            o, g, (kv_c, ki) = _mixer_c(grp, x, p["w_main_c"][j], p["w_idx_c"][j], p["h_c"], p["h_idx"], p["d_idx"],
                                        cache, (j, n_c), kv_c)
            new_c.append(ki)
            w_out = p["w_out_c"][j]
        m = grp.b * grp.t
        x = _outproj_ln(o.reshape(m, -1), g.reshape(m, -1), x.reshape(m, grp.d), w_out,
                        p["ln_g"][i], p["ln_b"][i], alpha).reshape(grp.b, grp.t, grp.d)
    stack = lambda sts, n: jnp.stack([s[n] for s in sts])
    heads = lambda a, nh: a.reshape(a.shape[0], grp.b, grp.t, nh, a.shape[-1] * a.shape[-2] // nh)
    h_a = p["b_f_a"].shape[1]
    return (x, heads(kv_a[0], h_a), heads(kv_a[1], h_a), jnp.stack(new_a), stack(new_b, 0), stack(new_b, 1),
            heads(kv_c[0], p["h_c"]), heads(kv_c[1], p["h_c"]), jnp.stack(new_c))


def kernel(x_prompt, x_sample, cache_a_k, cache_a_v, cache_a_logf, state_b_conv, state_b_h, cache_c_k, cache_c_v, cache_c_kidx, w_in_a, b_f_a, w_out_a, w_in_b, conv_w_b, conv_b_b, w_rg_b, b_rg_b, w_ig_b, b_ig_b, lam_b, w_out_b, w_in_c, w_out_c, ln_g, ln_b):
    h_a = b_f_a.shape[1]
    d_a = w_out_a.shape[1]
    d_c = w_out_c.shape[1]
    h_c = cache_c_k.shape[3]
    d_idx = cache_c_kidx.shape[-1]
    h_idx = (w_in_c.shape[2] - 4 * d_c - d_idx) // (d_idx + 1)
    assert d_a // h_a == LANES and d_c // h_c == LANES, "head width must equal the lane count"
    assert w_in_a.shape[2] == 4 * d_a + h_a and d_idx + h_idx <= LANES
    past = cache_a_k.shape[2]
    assert past % CHUNK == 0 and past > 0

    w_idx = w_in_c[:, :, 4 * d_c:]
    w_idx = _pad_to(w_idx, 2, h_idx * d_idx + LANES)
    p = {"w_main_a": w_in_a[:, :, :4 * d_a].astype(BF16),
         "w_f_a": _pad_to(w_in_a[:, :, 4 * d_a:], 2, LANES).astype(BF16),
         "b_f_a": b_f_a, "w_out_a": w_out_a.astype(BF16),
         "w_in_b": w_in_b.astype(BF16), "conv_w_b": conv_w_b, "conv_b_b": conv_b_b, "w_rg_b": w_rg_b,
         "b_rg_b": b_rg_b, "w_ig_b": w_ig_b, "b_ig_b": b_ig_b, "lam_b": lam_b, "w_out_b": w_out_b.astype(BF16),
         "w_main_c": w_in_c[:, :, :4 * d_c].astype(BF16), "w_idx_c": w_idx.astype(BF16),
         "w_out_c": w_out_c.astype(BF16), "ln_g": ln_g, "ln_b": ln_b,
         "h_c": h_c, "h_idx": h_idx, "d_idx": d_idx}
    caches = {"a_k": cache_a_k, "a_v": cache_a_v, "a_logf": cache_a_logf, "b_conv": state_b_conv,
              "b_h": state_b_h, "c_k": cache_c_k, "c_v": cache_c_v, "c_kidx": cache_c_kidx}
    outs_p = _run_trunk(x_prompt, p, None, 0)
    outs_s = _run_trunk(x_sample, p, caches, past)
    return (outs_p[0], outs_s[0]) + outs_p[1:] + outs_s[1:]
```

```python
import functools
import math

import jax
import jax.numpy as jnp
import numpy as np
from jax import lax
from jax.experimental import pallas as pl
from jax.experimental.pallas import tpu as pltpu

NEG_INF = -1e30
LN_EPS = 1e-5
CHUNK = 64
CHUNK_SHIFT = 6
TOPK_MAX = 256
LRU_C = 8.0
LANES = 128
SUBLANES = 8
BF16_ROWS = 16
PLANE_ROWS = 32 * SUBLANES
INT_MIN = -(2 ** 31)
VMEM_LIMIT = 56 * 1024 * 1024
LOG2E = math.log2(math.e)

F32 = jnp.float32
BF16 = jnp.bfloat16


def _cparams(sem, flags=None):
    return pltpu.CompilerParams(dimension_semantics=sem, vmem_limit_bytes=VMEM_LIMIT, flags=flags)


def _pick(n, pref):
    if n <= pref:
        return n
    b = pref
    while n % b:
        b //= 2
    return b


def _inproj_kernel(x_ref, *refs, n_w, n_alias, plan):
    w_refs, out_refs = refs[:n_w], refs[n_w + n_alias:]
    xb = x_ref[0].astype(BF16)
    for w_idx, c0, width, scale, outs in plan:
        r = jnp.dot(xb, w_refs[w_idx][:, c0:c0 + width], preferred_element_type=F32)
        if scale != 1.0:
            r = r * scale
        for o_idx, kind in outs:
            o = out_refs[o_idx]
            if kind == "row":
                o[0] = r.astype(o.dtype)
            elif kind == "stack":
                o[0, 0] = r.astype(o.dtype).reshape(o.shape[2:])
            else:
                for h in range(width // LANES):
                    o[0, h] = r[:, h * LANES:(h + 1) * LANES].T.astype(o.dtype)


def _inproj(x3, ws, plan, out_defs, bm, layer=(0, 1), stacked=None):
    bx, tx, d = x3.shape
    j, n_layers = layer
    grid = (bx, tx // bm)
    in_specs = [pl.BlockSpec((1, bm, d), lambda b, i: (b, i, 0))]
    for w in ws:
        in_specs.append(pl.BlockSpec(w.shape, lambda b, i: (0, 0)))
    stacked = list(stacked or [])
    in_specs += [pl.BlockSpec(memory_space=pl.ANY)] * len(stacked)
    out_shape, out_specs, aliases = [], [], {}
    for o_idx, (kind, width, dt) in enumerate(out_defs):
        if kind == "row":
            out_shape.append(jax.ShapeDtypeStruct((bx, tx, width), dt))
            out_specs.append(pl.BlockSpec((1, bm, width), lambda b, i: (b, i, 0)))
        elif kind == "stack":
            out_shape.append(jax.ShapeDtypeStruct((n_layers, bx, tx, width // LANES, LANES), dt))
            out_specs.append(pl.BlockSpec((1, 1, bm, width // LANES, LANES), lambda b, i: (j, b, i, 0, 0)))
            if stacked:
                aliases[1 + len(ws) + len(aliases)] = o_idx
        else:
            nh = width // LANES
            out_shape.append(jax.ShapeDtypeStruct((bx, nh, LANES, tx), dt))
            out_specs.append(pl.BlockSpec((1, nh, LANES, bm), lambda b, i: (b, 0, 0, i)))
    assert len(aliases) == len(stacked)
    return pl.pallas_call(
        functools.partial(_inproj_kernel, n_w=len(ws), n_alias=len(stacked), plan=tuple(plan)),
        out_shape=out_shape, grid=grid, in_specs=in_specs, out_specs=out_specs, input_output_aliases=aliases,
        compiler_params=_cparams(("parallel", "parallel")), name="inproj",
    )(x3, *ws, *stacked)


def _outproj_ln_kernel(o_ref, g_ref, x_ref, w_ref, lg_ref, lb_ref, y_ref, *, alpha):
    g = g_ref[...].astype(F32)
    og = (o_ref[...].astype(F32) * (g * jax.nn.sigmoid(g))).astype(BF16)
    y = jnp.dot(og, w_ref[...], preferred_element_type=F32)
    z = alpha * x_ref[...] + y
    mu = jnp.mean(z, axis=-1, keepdims=True)
    zc = z - mu
    var = jnp.mean(zc * zc, axis=-1, keepdims=True)
    y_ref[...] = zc * lax.rsqrt(var + LN_EPS) * lg_ref[...] + lb_ref[...]


def _outproj_ln(o2, g2, x2, w, ln_g, ln_b, alpha):
    m, d = x2.shape
    dk = o2.shape[1]
    bm = _pick(m, 512)
    row = lambda i: (i, 0)
    fixed = lambda i: (0, 0)
    return pl.pallas_call(
        functools.partial(_outproj_ln_kernel, alpha=alpha),
        out_shape=jax.ShapeDtypeStruct((m, d), F32), grid=(m // bm,),
        in_specs=[pl.BlockSpec((bm, dk), row), pl.BlockSpec((bm, dk), row), pl.BlockSpec((bm, d), row),
                  pl.BlockSpec((dk, d), fixed), pl.BlockSpec((1, d), fixed), pl.BlockSpec((1, d), fixed)],
        out_specs=pl.BlockSpec((bm, d), row),
        compiler_params=_cparams(("parallel",)), name="outproj_ln",
    )(o2, g2, x2, w, ln_g.reshape(1, d), ln_b.reshape(1, d))


def _log_sigmoid(x):
    return -(jnp.maximum(-x, 0.0) + jnp.log1p(jnp.exp(-jnp.abs(x))))


def _logf_cumsum_kernel(z_ref, bf_ref, lf_ref, c_ref, *, p0, p1):
    z = z_ref[0]
    pos = lax.broadcasted_iota(jnp.int32, z.shape, 1)
    is_new = jnp.logical_and(pos >= p0, pos < p1)
    lf = jnp.where(is_new, _log_sigmoid(z + bf_ref[...]), z)
    lf_ref[0] = lf
    c = lf
    s = 1
    while s < z.shape[1]:
        c = c + jnp.where(pos >= s, pltpu.roll(c, s, axis=1), 0.0)
        s *= 2
    c_ref[0] = c


def _logf_cumsum(z, b_f, p0, p1):
    b, h, l = z.shape
    blk = pl.BlockSpec((1, h, l), lambda i: (i, 0, 0))
    return pl.pallas_call(
        functools.partial(_logf_cumsum_kernel, p0=p0, p1=p1),
        out_shape=[jax.ShapeDtypeStruct(z.shape, F32)] * 2, grid=(b,),
        in_specs=[blk, pl.BlockSpec((h, 1), lambda i: (0, 0))], out_specs=[blk, blk],
        compiler_params=_cparams(("parallel",)), name="logf_cumsum",
    )(z, b_f.reshape(h, 1))


def _score_store(u, slot, s_sc, mc_sc):
    s_sc[slot] = u
    mc_sc[slot] = jnp.broadcast_to(jnp.max(u, axis=0, keepdims=True), mc_sc.shape[1:])


def _softmax_update(h, slot, vt, s_sc, mc_sc, m_sc, acc_sc, row_shift=None):
    u = s_sc[slot]
    m_prev = m_sc[h]
    m_cur = mc_sc[slot]
    if row_shift is not None:
        m_cur = m_cur + row_shift
    m_new = jnp.maximum(m_prev, m_cur)
    m_row = m_new[0:1]
    p = jnp.exp2(u - (m_row if row_shift is None else m_row - row_shift))
    alpha = jnp.exp2(m_prev - m_new)
    vt1 = jnp.concatenate([vt, jnp.ones((BF16_ROWS, vt.shape[1]), BF16)], axis=0)
    acc_sc[h] = acc_sc[h] * alpha[0:1] + jnp.dot(vt1, p.astype(BF16), preferred_element_type=F32)
    m_sc[h] = m_new


def _head(k_ref, h):
    dh = LANES
    return k_ref[0, :, h * dh:(h + 1) * dh]


def _head_t(v_ref, h, v_rows):
    if not v_rows:
        return v_ref[0, h]
    return _head(v_ref, h).astype(F32).T.astype(BF16)


def _scores(k, kx, qt, qx):
    return jnp.dot(jnp.concatenate([k, kx], axis=1), jnp.concatenate([qt, qx], axis=0),
                   preferred_element_type=F32)


def _pipelined_heads(nh, qk_phase, sm_phase):
    qk_phase(0, 0)
    for h in range(nh - 1):
        qk_phase(h + 1, (h + 1) % 2)
        sm_phase(h, h % 2)
    sm_phase(nh - 1, (nh - 1) % 2)


def _attn_init(m_sc, acc_sc):
    m_sc[...] = jnp.full(m_sc.shape, -jnp.inf, F32)
    acc_sc[...] = jnp.zeros(acc_sc.shape, F32)


def _attn_finish(o_ref, acc_sc, nh, dh):
    for h in range(nh):
        a = acc_sc[h]
        o_ref[0, :, h * dh:(h + 1) * dh] = (a[0:dh] / a[dh:dh + 1]).T.astype(o_ref.dtype)


def _attn_scratch(nh, bq, bk, dh):
    return [pltpu.VMEM((nh, SUBLANES, bq), F32), pltpu.VMEM((nh, dh + BF16_ROWS, bq), F32),
            pltpu.VMEM((2, bk, bq), F32), pltpu.VMEM((2, SUBLANES, bq), F32)]


def _fox_kernel(qt_ref, k_ref, v_ref, cq_ref, kx_ref, qx_ref, o_ref, m_sc, acc_sc, s_sc, mc_sc,
                *, bq, bk, q_off, nh, v_rows):
    qi, ki = pl.program_id(1), pl.program_id(2)
    q_first = q_off + qi * bq
    q_last = q_first + bq - 1
    k_first = ki * bk
    k_last = k_first + bk - 1

    @pl.when(ki == 0)
    def _():
        _attn_init(m_sc, acc_sc)

    def step(masked):
        kx = kx_ref[0]
        if masked:
            kpos = k_first + lax.broadcasted_iota(jnp.int32, (bk, bq), 0)
            qpos = q_first + lax.broadcasted_iota(jnp.int32, (bk, bq), 1)
            causal = kpos <= qpos

        def qk_phase(h, slot):
            u = _scores(_head(k_ref, h), kx, qt_ref[0, h], qx_ref[h])
            if masked:
                u = jnp.where(causal, u, NEG_INF)
            _score_store(u, slot, s_sc, mc_sc)

        def sm_phase(h, slot):
            cq2 = cq_ref[0, pl.ds(h, 1), :] * LOG2E
            _softmax_update(h, slot, _head_t(v_ref, h, v_rows), s_sc, mc_sc, m_sc, acc_sc, row_shift=cq2)

        _pipelined_heads(nh, qk_phase, sm_phase)

    needed = k_first <= q_last
    straddles = k_last > q_first

    @pl.when(jnp.logical_and(needed, straddles))
    def _():
        step(True)

    @pl.when(jnp.logical_and(needed, jnp.logical_not(straddles)))
    def _():
        step(False)

    @pl.when(ki == pl.num_programs(2) - 1)
    def _():
        _attn_finish(o_ref, acc_sc, nh, LANES)


def _split3(x):
    def top_half(v):
        bits = lax.bitcast_convert_type(v, jnp.uint32) & jnp.uint32(0xFFFF0000)
        return lax.bitcast_convert_type(bits, F32)

    hi = top_half(x)
    r1 = x - hi
    mid = top_half(r1)
    return hi.astype(BF16), mid.astype(BF16), (r1 - mid).astype(BF16)


def _v_spec(v, nh, dh, bk, kmap):
    if v.ndim == 3:
        return True, pl.BlockSpec((1, bk, nh * dh), lambda bb, qi, ki: (bb, kmap(qi, ki), 0))
    return False, pl.BlockSpec((1, nh, dh, bk), lambda bb, qi, ki: (bb, 0, 0, kmap(qi, ki)))


def _fox_attention(qt_hm, k_rm, v, cq, ck, q_off, bq, bk):
    b, nh, dh, tq = qt_hm.shape
    s = k_rm.shape[1]
    nq, nk = tq // bq, s // bk
    w = LANES // nh
    assert w >= 3
    kx = jnp.stack(_split3(ck * (-LOG2E)), axis=-1)
    kx = _pad_to(kx, 3, w).transpose(0, 2, 1, 3).reshape(b, s, LANES)
    rows = jnp.arange(LANES)[None, :, None]
    heads = jnp.arange(nh)[:, None, None]
    qx = jnp.broadcast_to(jnp.logical_and(rows >= heads * w, rows < heads * w + 3), (nh, LANES, bq)).astype(BF16)

    def kmap(qi, ki):
        return jnp.minimum(ki, (q_off + (qi + 1) * bq - 1) // bk)

    v_rows, v_spec = _v_spec(v, nh, dh, bk, kmap)
    return pl.pallas_call(
        functools.partial(_fox_kernel, bq=bq, bk=bk, q_off=q_off, nh=nh, v_rows=v_rows),
        out_shape=jax.ShapeDtypeStruct((b, tq, nh * dh), BF16), grid=(b, nq, nk),
        in_specs=[pl.BlockSpec((1, nh, dh, bq), lambda bb, qi, ki: (bb, 0, 0, qi)),
                  pl.BlockSpec((1, bk, nh * dh), lambda bb, qi, ki: (bb, kmap(qi, ki), 0)),
                  v_spec,
                  pl.BlockSpec((1, nh, bq), lambda bb, qi, ki: (bb, 0, qi)),
                  pl.BlockSpec((1, bk, LANES), lambda bb, qi, ki: (bb, kmap(qi, ki), 0)),
                  pl.BlockSpec((nh, LANES, bq), lambda bb, qi, ki: (0, 0, 0))],
        out_specs=pl.BlockSpec((1, bq, nh * dh), lambda bb, qi, ki: (bb, qi, 0)),
        scratch_shapes=_attn_scratch(nh, bq, bk, dh),
        compiler_params=_cparams(("parallel", "parallel", "arbitrary")), name="fox_attention",
    )(qt_hm, k_rm, v, cq, kx, qx)


def _bit_planes(words):
    a = list(words)
    j, m = 16, 0x0000FFFF
    while j:
        k = 0
        while k < 32:
            t = (a[k] ^ lax.shift_right_logical(a[k + j], jnp.int32(j))) & jnp.int32(m)
            a[k] = a[k] ^ t
            a[k + j] = a[k + j] ^ lax.shift_left(t, jnp.int32(j))
            k = (k + j + 1) & ~j
        j >>= 1
        m = (m ^ (m << j)) & 0xFFFFFFFF
    return a


def _dsa_select_kernel(qit_ref, wit_ref, ki_ref, out_ref, key_sc, pl_sc, act_sc, j_sc,
                       *, bq, kb, q_off, s_valid, topk, nh_idx, idx_bits):
    qb = pl.program_id(1)
    q_first = q_off + qb * bq
    q_last = q_first + bq - 1
    n_adm = jnp.minimum(((q_last >> CHUNK_SHIFT) + 1) * CHUNK, s_valid)
    nkb = (jnp.maximum(n_adm, topk) + kb - 1) // kb
    nkb_total = key_sc.shape[0] // kb
    gpb = kb // PLANE_ROWS
    nkb4 = nkb // 4
    qchunk = (q_first + lax.broadcasted_iota(jnp.int32, (kb, bq), 1)) >> CHUNK_SHIFT
    row = lax.broadcasted_iota(jnp.int32, (kb, bq), 0)

    def admissible(kpos):
        return jnp.logical_and((kpos >> CHUNK_SHIFT) <= qchunk, kpos < s_valid)

    def rows(kblk):
        return pl.ds(pl.multiple_of(kblk * kb, kb), kb)

    def score_block(kblk, carry):
        kblock = ki_ref[0, rows(kblk), :]
        sc = jnp.zeros((kb, bq), F32)
        for h in range(nh_idx):
            d = jnp.dot(kblock, qit_ref[0, h], preferred_element_type=F32)
            sc = sc + wit_ref[0, h:h + 1, :] * jnp.maximum(d, 0.0)
        sc = jnp.where(admissible(kblk * kb + row), sc + 0.0, NEG_INF)
        bits = lax.bitcast_convert_type(sc, jnp.int32)
        key = bits ^ ((bits >> 31) & 0x7FFFFFFF)
        key_sc[rows(kblk), :] = key
        for g in range(gpb):
            base = g * PLANE_ROWS
            planes = _bit_planes([key[base + SUBLANES * j:base + SUBLANES * (j + 1), :] for j in range(32)])
            planes[0] = ~planes[0]
            for i in range(32):
                pl_sc[kblk * gpb + g, i] = planes[i]
            act_sc[kblk * gpb + g] = jnp.full((SUBLANES, bq), -1, jnp.int32)
        return carry

    lax.fori_loop(0, nkb, score_block, 0)

    def bit_step(it, carry):
        thr_u, above = carry

        def cnt_groups(g0, n, acc):
            for g in range(n):
                acc = acc + lax.population_count(act_sc[g0 + g] & pl_sc[g0 + g, it])
            return acc

        acc = lax.fori_loop(0, nkb4, lambda i, a: cnt_groups(i * gpb * 4, gpb * 4, a),
                            jnp.zeros((SUBLANES, bq), jnp.int32))
        acc = lax.fori_loop(nkb4 * 4, nkb, lambda i, a: cnt_groups(i * gpb, gpb, a), acc)
        c = jnp.sum(acc.astype(F32), axis=0, keepdims=True)
        take = (above + c) >= topk

        def upd_groups(g0, n, carry2):
            for g in range(n):
                a = act_sc[g0 + g]
                x = a & pl_sc[g0 + g, it]
                act_sc[g0 + g] = jnp.where(take, x, a ^ x)
            return carry2

        lax.fori_loop(0, nkb4, lambda i, c2: upd_groups(i * gpb * 4, gpb * 4, c2), 0)
        lax.fori_loop(nkb4 * 4, nkb, lambda i, c2: upd_groups(i * gpb, gpb, c2), 0)
        bit = lax.shift_left(jnp.int32(1), 31 - it)
        return thr_u | jnp.where(take, bit, 0), above + jnp.where(take, 0.0, c)

    thr_u, n_gt = lax.fori_loop(0, 32, bit_step, (jnp.zeros((1, bq), jnp.int32), jnp.zeros((1, bq), F32)))
    thr = thr_u ^ INT_MIN

    def eq_blk(kblk, acc):
        for g in range(gpb):
            acc = acc + lax.population_count(act_sc[kblk * gpb + g])
        return acc

    n_eq = jnp.sum(lax.fori_loop(0, nkb, eq_blk, jnp.zeros((SUBLANES, bq), jnp.int32)).astype(F32),
                   axis=0, keepdims=True)
    need = topk - n_gt
    has_excess = jnp.max(n_eq - need) > 0.0
    j_sc[...] = jnp.full(j_sc.shape, 2 ** 30, jnp.int32)

    @pl.when(has_excess)
    def _():
        def count_ties_below(cand):
            def blk(kblk, acc):
                hit = jnp.logical_and(key_sc[rows(kblk), :] == thr, kblk * kb + row < cand)
                ind = jnp.where(hit, 1.0, 0.0)
                for r in range(kb // SUBLANES):
                    acc = acc + ind[r * SUBLANES:(r + 1) * SUBLANES]
                return acc
            return jnp.sum(lax.fori_loop(0, nkb, blk, jnp.zeros((SUBLANES, bq), F32)), axis=0, keepdims=True)

        def idx_step(it, lo):
            cand = lo | lax.shift_left(jnp.int32(1), idx_bits - 1 - it)
            return jnp.where(count_ties_below(cand) < need, cand, lo)

        lo = lax.fori_loop(0, idx_bits, idx_step, jnp.zeros((1, bq), jnp.int32))
        j_sc[...] = jnp.broadcast_to(lo, j_sc.shape)

    j_cut = j_sc[0:1, :]

    def write_fast(kblk, carry):
        out_ref[0, 0, rows(kblk), :] = jnp.where(key_sc[rows(kblk), :] >= thr, 0.0, NEG_INF).astype(out_ref.dtype)
        return carry

    def write_block(kblk, carry):
        key = key_sc[rows(kblk), :]
        kpos = kblk * kb + row
        tie = jnp.logical_and(key == thr, kpos <= j_cut)
        sel = jnp.logical_and(jnp.logical_or(key > thr, tie), admissible(kpos))
        out_ref[0, 0, rows(kblk), :] = jnp.where(sel, 0.0, NEG_INF).astype(out_ref.dtype)
        return carry

    def fill_block(kblk, carry):
        out_ref[0, 0, rows(kblk), :] = jnp.full((kb, bq), NEG_INF, out_ref.dtype)
        return carry

    n_full = jnp.minimum(((q_first >> CHUNK_SHIFT) + 1) * CHUNK, s_valid) // kb
    n_fast = jnp.where(has_excess, 0, n_full)
    lax.fori_loop(0, n_fast, write_fast, 0)
    lax.fori_loop(n_fast, nkb, write_block, 0)
    lax.fori_loop(nkb, nkb_total, fill_block, 0)


def _dsa_select(qit_hm, wit, ki, q_off, s_valid, topk, bq, kb):
    b, nh_idx, d_idx, tq = qit_hm.shape
    s_pad = ki.shape[1]
    return pl.pallas_call(
        functools.partial(_dsa_select_kernel, bq=bq, kb=kb, q_off=q_off, s_valid=s_valid, topk=topk,
                          nh_idx=nh_idx, idx_bits=max(1, (s_pad - 1).bit_length())),
        out_shape=jax.ShapeDtypeStruct((b, tq // bq, s_pad, bq), BF16), grid=(b, tq // bq),
        in_specs=[pl.BlockSpec((1, nh_idx, d_idx, bq), lambda bb, qb: (bb, 0, 0, qb)),
                  pl.BlockSpec((1, nh_idx, bq), lambda bb, qb: (bb, 0, qb)),
                  pl.BlockSpec((1, s_pad, d_idx), lambda bb, qb: (bb, 0, 0))],
        out_specs=pl.BlockSpec((1, 1, s_pad, bq), lambda bb, qb: (bb, qb, 0, 0)),
        scratch_shapes=[pltpu.VMEM((s_pad, bq), jnp.int32),
                        pltpu.VMEM((s_pad // PLANE_ROWS, 32, SUBLANES, bq), jnp.int32),
                        pltpu.VMEM((s_pad // PLANE_ROWS, SUBLANES, bq), jnp.int32),
                        pltpu.VMEM((SUBLANES, bq), jnp.int32)],
        compiler_params=_cparams(("parallel", "arbitrary")), name="dsa_select",
    )(qit_hm, wit, ki)


def _dsa_attn_kernel(sl_ref, qt_ref, k_ref, v_ref, mb_ref, qx_ref, o_ref, m_sc, acc_sc, s_sc, mc_sc,
                     base_sc, t_sc, *, bq, bk, q_off, nh, v_rows):
    qi, ki = pl.program_id(1), pl.program_id(2)
    q_first = q_off + qi * bq
    q_last = q_first + bq - 1
    k_first = ki * bk
    k_last = k_first + bk - 1
    qposf = (q_first + lax.broadcasted_iota(jnp.int32, (1, bq), 1)).astype(F32)

    @pl.when(ki == 0)
    def _():
        _attn_init(m_sc, acc_sc)

    def step(past):
        base_sc[...] = jnp.concatenate([mb_ref[0, j] for j in range(mb_ref.shape[1])], axis=1).astype(F32)
        if past:
            kpos = k_first + lax.broadcasted_iota(jnp.int32, (bk, LANES), 0)
            lane = lax.broadcasted_iota(jnp.int32, (bk, LANES), 1)
            kx = jnp.where(lane < 3, kpos >> CHUNK_SHIFT, jnp.where(lane < 6, kpos & (CHUNK - 1), 0)).astype(BF16)
        else:
            kpos = k_first + lax.broadcasted_iota(jnp.int32, (bk, bq), 0)
            qpos = q_first + lax.broadcasted_iota(jnp.int32, (bk, bq), 1)
            t_sc[...] = jnp.minimum(kpos, 2 * qpos - kpos).astype(F32)

        def qk_phase(h, slot):
            if past:
                u = _scores(_head(k_ref, h), kx, qt_ref[0, h], qx_ref[h]) + base_sc[...]
            else:
                st = jnp.dot(_head(k_ref, h), qt_ref[0, h], preferred_element_type=F32)
                u = st + sl_ref[h] * t_sc[...] + base_sc[...]
            _score_store(u, slot, s_sc, mc_sc)

        def sm_phase(h, slot):
            _softmax_update(h, slot, _head_t(v_ref, h, v_rows), s_sc, mc_sc, m_sc, acc_sc,
                            row_shift=-sl_ref[h] * qposf)

        _pipelined_heads(nh, qk_phase, sm_phase)

    needed = k_first <= ((q_last >> CHUNK_SHIFT) << CHUNK_SHIFT) + CHUNK - 1
    past = k_last <= q_first

    @pl.when(past)
    def _():
        step(True)

    @pl.when(jnp.logical_and(needed, jnp.logical_not(past)))
    def _():
        step(False)

    @pl.when(ki == pl.num_programs(2) - 1)
    def _():
        _attn_finish(o_ref, acc_sc, nh, LANES)


def _dsa_attention(qt_hm, k_rm, v, mask, q_off, bq, bk):
    b, nh, dh, tq = qt_hm.shape
    s, mq = mask.shape[2], mask.shape[3]
    nq, nk, nsub = tq // bq, s // bk, bq // mq
    slopes2_np = (2.0 ** (-8.0 * np.arange(1, nh + 1) / nh) * LOG2E).astype(np.float32)
    slopes2 = jnp.asarray(slopes2_np)
    rest, pieces = slopes2_np, []
    for _ in range(3):
        piece = rest.astype(BF16).astype(np.float32)
        pieces.append(piece)
        rest = rest - piece
    col = np.zeros((nh, LANES), np.float32)
    col[:, 0:3] = np.stack(pieces, axis=1) * CHUNK
    col[:, 3:6] = np.stack(pieces, axis=1)
    qx = jnp.broadcast_to(jnp.asarray(col.astype(BF16))[:, :, None], (nh, LANES, bq))

    def kmap(qi, ki):
        q_last = q_off + (qi + 1) * bq - 1
        return jnp.minimum(ki, (((q_last >> CHUNK_SHIFT) << CHUNK_SHIFT) + CHUNK - 1) // bk)

    v_rows, v_spec = _v_spec(v, nh, dh, bk, kmap)
    return pl.pallas_call(
        functools.partial(_dsa_attn_kernel, bq=bq, bk=bk, q_off=q_off, nh=nh, v_rows=v_rows),
        out_shape=jax.ShapeDtypeStruct((b, tq, nh * dh), BF16), grid=(b, nq, nk),
        in_specs=[pl.BlockSpec(memory_space=pltpu.SMEM),
                  pl.BlockSpec((1, nh, dh, bq), lambda bb, qi, ki: (bb, 0, 0, qi)),
                  pl.BlockSpec((1, bk, nh * dh), lambda bb, qi, ki: (bb, kmap(qi, ki), 0)),
                  v_spec,
                  pl.BlockSpec((1, nsub, bk, mq), lambda bb, qi, ki: (bb, qi, kmap(qi, ki), 0)),
                  pl.BlockSpec((nh, LANES, bq), lambda bb, qi, ki: (0, 0, 0))],
        out_specs=pl.BlockSpec((1, bq, nh * dh), lambda bb, qi, ki: (bb, qi, 0)),
        scratch_shapes=_attn_scratch(nh, bq, bk, dh) + [pltpu.VMEM((bk, bq), F32), pltpu.VMEM((bk, bq), F32)],
        compiler_params=_cparams(("parallel", "parallel", "arbitrary")), name="dsa_attention",
    )(slopes2, qt_hm, k_rm, v, mask, qx)


def _rglru_kernel(u_ref, cw_ref, cb_ref, wr_ref, br_ref, wig_ref, big_ref, lam_ref, buf0_ref, h0_ref,
                  h_ref, conv_ref, hl_ref, ubuf, a_sc, b_sc, hbuf, hcar, *, tt, conv_w, nblk):
    t = pl.program_id(1)
    pad = SUBLANES
    d = u_ref.shape[2]
    blk = d // nblk

    @pl.when(t == 0)
    def _():
        ubuf[0:pad, :] = buf0_ref[0]
        hcar[...] = jnp.broadcast_to(h0_ref[0], (SUBLANES, d))

    ubuf[pad:pad + tt, :] = u_ref[0]
    uc = cb_ref[...] + ubuf[pad:pad + tt, :] * cw_ref[conv_w - 1:conv_w, :]
    for j in range(conv_w - 1):
        off = pad - (conv_w - 1) + j
        uc = uc + ubuf[off:off + tt, :] * cw_ref[j:j + 1, :]

    lam = lam_ref[...]
    neg_sp = -LRU_C * (jnp.maximum(-lam, 0.0) + jnp.log1p(jnp.exp(-jnp.abs(lam))))
    for n in range(nblk):
        cs = slice(n * blk, (n + 1) * blk)
        ucn = uc[:, cs]
        ub = ucn.astype(BF16)
        r = jax.nn.sigmoid(jnp.dot(ub, wr_ref[n], preferred_element_type=F32) + br_ref[:, cs])
        i = jax.nn.sigmoid(jnp.dot(ub, wig_ref[n], preferred_element_type=F32) + big_ref[:, cs])
        log_a = r * neg_sp[:, cs]
        a = jnp.exp(log_a)
        a_sc[:, cs] = a
        b_sc[:, cs] = jnp.sqrt(-jnp.tanh(log_a) * (a * a + 1.0)) * (i * ucn)

    row = lax.broadcasted_iota(jnp.int32, (SUBLANES, d), 0)

    def group(gi, hprev):
        r0 = pl.multiple_of(gi * SUBLANES, SUBLANES)
        av = a_sc[pl.ds(r0, SUBLANES), :]
        bv = b_sc[pl.ds(r0, SUBLANES), :]
        s = 1
        while s < SUBLANES:
            a_sh = pltpu.roll(av, s, axis=0)
            b_sh = pltpu.roll(bv, s, axis=0)
            m = row >= s
            bv = jnp.where(m, av * b_sh + bv, bv)
            av = jnp.where(m, av * a_sh, av)
            s *= 2
        hrows = av * hprev + bv
        hbuf[pl.ds(r0, SUBLANES), :] = hrows
        return jnp.broadcast_to(hrows[SUBLANES - 1:SUBLANES, :], (SUBLANES, d))

    hlast = lax.fori_loop(0, tt // SUBLANES, group, hcar[...])
    h_ref[0] = hbuf[...].astype(h_ref.dtype)
    hcar[...] = hlast
    hl_ref[0] = hlast[0:1, :]
    tail = ubuf[tt:tt + pad, :]
    conv_ref[0] = tail
    ubuf[0:pad, :] = tail


def _rglru(u, conv_w, conv_b, w_rg, b_rg, w_ig, b_ig, lam, buf0, h0, tt):
    b, t, d = u.shape
    cw = conv_w.shape[0]
    nblk, blk = w_rg.shape[0], w_rg.shape[1]
    vec = lambda a: a.reshape(1, d)
    fixed2 = lambda bb, ti: (0, 0)
    fixed3 = lambda bb, ti: (0, 0, 0)
    perb = lambda bb, ti: (bb, 0, 0)
    return pl.pallas_call(
        functools.partial(_rglru_kernel, tt=tt, conv_w=cw, nblk=nblk),
        out_shape=[jax.ShapeDtypeStruct((b, t, d), BF16), jax.ShapeDtypeStruct((b, SUBLANES, d), F32),
                   jax.ShapeDtypeStruct((b, 1, d), F32)],
        grid=(b, t // tt),
        in_specs=[pl.BlockSpec((1, tt, d), lambda bb, ti: (bb, ti, 0)),
                  pl.BlockSpec((cw, d), fixed2), pl.BlockSpec((1, d), fixed2),
                  pl.BlockSpec((nblk, blk, blk), fixed3), pl.BlockSpec((1, d), fixed2),
                  pl.BlockSpec((nblk, blk, blk), fixed3), pl.BlockSpec((1, d), fixed2),
                  pl.BlockSpec((1, d), fixed2),
                  pl.BlockSpec((1, SUBLANES, d), perb), pl.BlockSpec((1, 1, d), perb)],
        out_specs=[pl.BlockSpec((1, tt, d), lambda bb, ti: (bb, ti, 0)),
                   pl.BlockSpec((1, SUBLANES, d), perb), pl.BlockSpec((1, 1, d), perb)],
        scratch_shapes=[pltpu.VMEM((SUBLANES + tt, d), F32), pltpu.VMEM((tt, d), F32),
                        pltpu.VMEM((tt, d), F32), pltpu.VMEM((tt, d), F32), pltpu.VMEM((SUBLANES, d), F32)],
        compiler_params=_cparams(("parallel", "arbitrary")), name="rglru",
    )(u, conv_w, vec(conv_b), w_rg.astype(BF16), vec(b_rg), w_ig.astype(BF16), vec(b_ig), vec(lam), buf0, h0)


def _pack_rows_kernel(c_ref, n_ref, o_ref):
    p, t = c_ref.shape[1], n_ref.shape[1]
    o_ref[0, 0:p, :] = c_ref[0].reshape(p, o_ref.shape[2]).astype(o_ref.dtype)
    o_ref[0, p:p + t, :] = n_ref[0]
    o_ref[0, p + t:, :] = jnp.zeros((o_ref.shape[1] - p - t, o_ref.shape[2]), o_ref.dtype)


def _pack_rows(cache, new, s_pad):
    b, p, nh, dh = cache.shape
    t, w = new.shape[1], new.shape[2]
    return pl.pallas_call(
        _pack_rows_kernel, out_shape=jax.ShapeDtypeStruct((b, s_pad, w), BF16), grid=(b,),
        in_specs=[pl.BlockSpec((1, p, nh, dh), lambda i: (i, 0, 0, 0)), pl.BlockSpec((1, t, w), lambda i: (i, 0, 0))],
        out_specs=pl.BlockSpec((1, s_pad, w), lambda i: (i, 0, 0)),
        compiler_params=_cparams(("parallel",)), name="pack_rows",
    )(cache, new)


def _pad_to(a, axis, n):
    extra = n - a.shape[axis]
    if extra == 0:
        return a
    widths = [(0, 0)] * a.ndim
    widths[axis] = (0, extra)
    return jnp.pad(a, widths)


def _round_up(n, m):
    return (n + m - 1) // m * m


class _Group:
    def __init__(self, x, past):
        self.b, self.t, self.d = x.shape
        self.past = past
        self.flat = past > 0
        self.s_valid = past + self.t
        if self.flat:
            self.tq = _round_up(self.t, LANES)
            self.s_pad = _round_up(self.s_valid, PLANE_ROWS)
            self.fox_bq = self.dsa_bq = self.tq
            self.bk = self.s_pad
        else:
            self.tq = self.s_pad = self.t
            self.fox_bq = _pick(self.t, 512)
            self.dsa_bq = _pick(self.t, 256)
            self.bk = _pick(self.t, 512)

    def proj_view(self, x):
        return x.reshape(1, self.b * self.t, self.d) if self.flat else x

    def proj_bm(self):
        return self.b * self.t if self.flat else _pick(self.t, 256)

    def rows(self, a):
        return a.reshape(self.b, self.t, a.shape[-1])

    def heads_t(self, a):
        if not self.flat:
            return a
        return a.reshape(a.shape[1], a.shape[2], self.b, self.t).transpose(2, 0, 1, 3)

    def pad_q(self, a):
        return _pad_to(a, a.ndim - 1, self.tq)

    def keys(self, new_rm, cache):
        return new_rm if cache is None else _pack_rows(cache, new_rm, self.s_pad)

    def values(self, new, cache):
        return new if cache is None else self.keys(self.rows(new), cache)


def _qkvg_plan(width, dh, v_rows):
    plan = [(0, 0, width, dh ** -0.5 * LOG2E, [(0, "headT")]),
            (0, width, width, 1.0, [(1, "row"), (3, "stack")]),
            (0, 2 * width, width, 1.0, [(2, "row" if v_rows else "headT"), (4, "stack")]),
            (0, 3 * width, width, 1.0, [(5, "row")])]
    outs = [("headT", width, BF16), ("row", width, BF16), ("row" if v_rows else "headT", width, BF16),
            ("stack", width, F32), ("stack", width, F32), ("row", width, BF16)]
    return plan, outs


def _mixer_a(grp, x, w_main, w_f, b_f, cache, layer, stacked):
    nh = b_f.shape[0]
    da = w_main.shape[1] // 4
    plan, outs = _qkvg_plan(da, da // nh, grp.flat)
    plan = plan + [(1, 0, LANES, 1.0, [(6, "row")])]
    outs = outs + [("row", LANES, F32)]
    qt, k_rm, vt, k, v, g, fl = _inproj(grp.proj_view(x), [w_main, w_f], plan, outs, grp.proj_bm(), layer, stacked)
    qt, k_rm, g = grp.pad_q(grp.heads_t(qt)), grp.rows(k_rm), grp.rows(g)
    z = grp.rows(fl)[:, :, :nh].transpose(0, 2, 1)
    ck_, cv_ = (None, None) if cache is None else cache[:2]
    if cache is not None:
        z = _pad_to(jnp.concatenate([cache[2].astype(F32).transpose(0, 2, 1), z], axis=2), 2, grp.s_pad)
    lf_all, c_all = _logf_cumsum(z, b_f, grp.past, grp.s_valid)
    logf = lf_all[:, :, grp.past:grp.s_valid].transpose(0, 2, 1)
    cq = grp.pad_q(c_all[:, :, grp.past:grp.s_valid])
    o = _fox_attention(qt, grp.keys(k_rm, ck_), grp.values(vt, cv_), cq, c_all,
                       grp.past, grp.fox_bq, grp.bk)[:, :grp.t]
    return o, g, ([k, v], logf)


def _mixer_b(grp, x, w_in, conv_w, conv_b, w_rg, b_rg, w_ig, b_ig, lam, state):
    dr = w_in.shape[1] // 2
    plan = [(0, 0, dr, 1.0, [(0, "row")]), (0, dr, dr, 1.0, [(1, "row")])]
    u, g = _inproj(grp.proj_view(x), [w_in], plan, [("row", dr, F32), ("row", dr, BF16)], grp.proj_bm())
    u, g = grp.rows(u), grp.rows(g)
    cw = conv_w.shape[0]
    if state is None:
        buf0 = jnp.zeros((grp.b, SUBLANES, dr), F32)
        h0 = jnp.zeros((grp.b, 1, dr), F32)
    else:
        buf, h0 = state
        buf0 = jnp.pad(buf.astype(F32), ((0, 0), (SUBLANES - (cw - 1), 0), (0, 0)))
        h0 = h0.astype(F32).reshape(grp.b, 1, dr)
    h, tail, hl = _rglru(u, conv_w, conv_b, w_rg, b_rg, w_ig, b_ig, lam, buf0, h0, _pick(grp.t, 256))
    return h, g, (tail[:, SUBLANES - (cw - 1):], hl[:, 0])


def _mixer_c(grp, x, w_main, w_idx, nh, nh_idx, d_idx, cache, layer, stacked):
    dc = w_main.shape[1] // 4
    wq = nh_idx * d_idx
    plan, outs = _qkvg_plan(dc, dc // nh, grp.flat)
    plan = plan + [(1, 0, wq, 1.0, [(6, "row")]), (1, wq, LANES, 1.0, [(7, "row")])]
    outs = outs + [("row", wq, BF16), ("row", LANES, F32)]
    qt, k_rm, vt, k, v, g, qi, kw = _inproj(grp.proj_view(x), [w_main, w_idx], plan, outs, grp.proj_bm(), layer,
                                            stacked)
    qt, k_rm = grp.pad_q(grp.heads_t(qt)), grp.rows(k_rm)
    g, qi, kw = grp.rows(g), grp.rows(qi), grp.rows(kw)
    ki = kw[:, :, :d_idx]
    wit = grp.pad_q(kw[:, :, d_idx:d_idx + nh_idx].transpose(0, 2, 1))
    qit = grp.pad_q(qi.reshape(grp.b, grp.t, nh_idx, d_idx).transpose(0, 2, 3, 1))
    ck_, cv_ = (None, None) if cache is None else cache[:2]
    ki_all = ki if cache is None else _pad_to(jnp.concatenate([cache[2].astype(F32), ki], axis=1), 1, grp.s_pad)
    topk = min(TOPK_MAX, grp.s_valid // 4)
    mask = _dsa_select(qit, wit, ki_all.astype(BF16), grp.past, grp.s_valid, topk, grp.dsa_bq, grp.bk)
    o = _dsa_attention(qt, grp.keys(k_rm, ck_), grp.values(vt, cv_), mask, grp.past, grp.fox_bq, grp.bk)[:, :grp.t]
    return o, g, ([k, v], ki)


def _run_trunk(x, p, caches, past):
    depth = p["ln_g"].shape[0]
    alpha = (2 * depth) ** 0.25
    grp = _Group(x, past)
    n_a, n_c = (depth + 2) // 3, depth // 3
    kv_a, kv_c = None, None
    new_a, new_b, new_c = [], [], []
    for i in range(depth):
        j, kind = i // 3, i % 3
        if kind == 0:
            cache = None if caches is None else (caches["a_k"][j], caches["a_v"][j], caches["a_logf"][j])
            o, g, (kv_a, logf) = _mixer_a(grp, x, p["w_main_a"][j], p["w_f_a"][j], p["b_f_a"][j], cache,
                                          (j, n_a), kv_a)
            new_a.append(logf)
            w_out = p["w_out_a"][j]
        elif kind == 1:
            state = None if caches is None else (caches["b_conv"][j], caches["b_h"][j])
            o, g, st = _mixer_b(grp, x, p["w_in_b"][j], p["conv_w_b"][j], p["conv_b_b"][j], p["w_rg_b"][j],
                                p["b_rg_b"][j], p["w_ig_b"][j], p["b_ig_b"][j], p["lam_b"][j], state)
            new_b.append(st)
            w_out = p["w_out_b"][j]
        else:
            cache = None if caches is None else (caches["c_k"][j], caches["c_v"][j], caches["c_kidx"][j])
            o, g, (kv_c, ki) = _mixer_c(grp, x, p["w_main_c"][j], p["w_idx_c"][j], p["h_c"], p["h_idx"], p["d_idx"],
                                        cache, (j, n_c), kv_c)
            new_c.append(ki)
            w_out = p["w_out_c"][j]
        m = grp.b * grp.t
        x = _outproj_ln(o.reshape(m, -1), g.reshape(m, -1), x.reshape(m, grp.d), w_out,
                        p["ln_g"][i], p["ln_b"][i], alpha).reshape(grp.b, grp.t, grp.d)
    stack = lambda sts, n: jnp.stack([s[n] for s in sts])
    heads = lambda a, nh: a.reshape(a.shape[0], grp.b, grp.t, nh, a.shape[-1] * a.shape[-2] // nh)
    h_a = p["b_f_a"].shape[1]
    return (x, heads(kv_a[0], h_a), heads(kv_a[1], h_a), jnp.stack(new_a), stack(new_b, 0), stack(new_b, 1),
            heads(kv_c[0], p["h_c"]), heads(kv_c[1], p["h_c"]), jnp.stack(new_c))


def kernel(x_prompt, x_sample, cache_a_k, cache_a_v, cache_a_logf, state_b_conv, state_b_h, cache_c_k, cache_c_v, cache_c_kidx, w_in_a, b_f_a, w_out_a, w_in_b, conv_w_b, conv_b_b, w_rg_b, b_rg_b, w_ig_b, b_ig_b, lam_b, w_out_b, w_in_c, w_out_c, ln_g, ln_b):
    h_a = b_f_a.shape[1]
    d_a = w_out_a.shape[1]
    d_c = w_out_c.shape[1]
    h_c = cache_c_k.shape[3]
    d_idx = cache_c_kidx.shape[-1]
    h_idx = (w_in_c.shape[2] - 4 * d_c - d_idx) // (d_idx + 1)
    assert d_a // h_a == LANES and d_c // h_c == LANES, "head width must equal the lane count"
    assert w_in_a.shape[2] == 4 * d_a + h_a and d_idx + h_idx <= LANES
    past = cache_a_k.shape[2]
    assert past % CHUNK == 0 and past > 0

    w_idx = w_in_c[:, :, 4 * d_c:]
    w_idx = _pad_to(w_idx, 2, h_idx * d_idx + LANES)
    p = {"w_main_a": w_in_a[:, :, :4 * d_a].astype(BF16),
         "w_f_a": _pad_to(w_in_a[:, :, 4 * d_a:], 2, LANES).astype(BF16),
         "b_f_a": b_f_a, "w_out_a": w_out_a.astype(BF16),
         "w_in_b": w_in_b.astype(BF16), "conv_w_b": conv_w_b, "conv_b_b": conv_b_b, "w_rg_b": w_rg_b,
         "b_rg_b": b_rg_b, "w_ig_b": w_ig_b, "b_ig_b": b_ig_b, "lam_b": lam_b, "w_out_b": w_out_b.astype(BF16),
         "w_main_c": w_in_c[:, :, :4 * d_c].astype(BF16), "w_idx_c": w_idx.astype(BF16),
         "w_out_c": w_out_c.astype(BF16), "ln_g": ln_g, "ln_b": ln_b,
         "h_c": h_c, "h_idx": h_idx, "d_idx": d_idx}
    caches = {"a_k": cache_a_k, "a_v": cache_a_v, "a_logf": cache_a_logf, "b_conv": state_b_conv,
              "b_h": state_b_h, "c_k": cache_c_k, "c_v": cache_c_v, "c_kidx": cache_c_kidx}
    outs_p = _run_trunk(x_prompt, p, None, 0)
    outs_s = _run_trunk(x_sample, p, caches, past)
    return (outs_p[0], outs_s[0]) + outs_p[1:] + outs_s[1:]
```

```python
import functools
import math

import jax
import jax.numpy as jnp
import numpy as np
from jax import lax
from jax.experimental import pallas as pl
from jax.experimental.pallas import tpu as pltpu

NEG_INF = -1e30
LN_EPS = 1e-5
CHUNK = 64
CHUNK_SHIFT = 6
TOPK_MAX = 256
LRU_C = 8.0
LANES = 128
SUBLANES = 8
BF16_ROWS = 16
PLANE_ROWS = 32 * SUBLANES
INT_MIN = -(2 ** 31)
VMEM_LIMIT = 56 * 1024 * 1024
LOG2E = math.log2(math.e)

F32 = jnp.float32
BF16 = jnp.bfloat16


def _cparams(sem, flags=None):
    return pltpu.CompilerParams(dimension_semantics=sem, vmem_limit_bytes=VMEM_LIMIT, flags=flags)


def _pick(n, pref):
    if n <= pref:
        return n
    b = pref
    while n % b:
        b //= 2
    return b


def _inproj_kernel(x_ref, *refs, n_w, n_alias, layer, plan):
    w_refs, out_refs = refs[:n_w], refs[n_w + n_alias:]
    xb = x_ref[0].astype(BF16)
    for w_idx, c0, width, scale, outs in plan:
        r = jnp.dot(xb, w_refs[w_idx][:, c0:c0 + width], preferred_element_type=F32)
        if scale != 1.0:
            r = r * scale
        for o_idx, kind in outs:
            o = out_refs[o_idx]
            if kind == "row":
                o[0] = r.astype(o.dtype)
            elif kind == "stack":
                val = r.astype(o.dtype).reshape(o.shape[2:])
                if o.shape[0] == 1:
                    o[0, 0] = val
                else:
                    for l in range(o.shape[0]):
                        o[l, 0] = val if l == layer else jnp.zeros_like(val)
            else:
                for h in range(width // LANES):
                    o[0, h] = r[:, h * LANES:(h + 1) * LANES].T.astype(o.dtype)


def _inproj(x3, ws, plan, out_defs, bm, layer=(0, 1), stacked=None):
    bx, tx, d = x3.shape
    j, n_layers = layer
    grid = (bx, tx // bm)
    in_specs = [pl.BlockSpec((1, bm, d), lambda b, i: (b, i, 0))]
    for w in ws:
        in_specs.append(pl.BlockSpec(w.shape, lambda b, i: (0, 0)))
    stacked = list(stacked or [])
    in_specs += [pl.BlockSpec(memory_space=pl.ANY)] * len(stacked)
    out_shape, out_specs, aliases = [], [], {}
    for o_idx, (kind, width, dt) in enumerate(out_defs):
        if kind == "row":
            out_shape.append(jax.ShapeDtypeStruct((bx, tx, width), dt))
            out_specs.append(pl.BlockSpec((1, bm, width), lambda b, i: (b, i, 0)))
        elif kind == "stack":
            out_shape.append(jax.ShapeDtypeStruct((n_layers, bx, tx, width // LANES, LANES), dt))
            if stacked:
                out_specs.append(pl.BlockSpec((1, 1, bm, width // LANES, LANES), lambda b, i: (j, b, i, 0, 0)))
                aliases[1 + len(ws) + len(aliases)] = o_idx
            else:
                out_specs.append(pl.BlockSpec((n_layers, 1, bm, width // LANES, LANES), lambda b, i: (0, b, i, 0, 0)))
        else:
            nh = width // LANES
            out_shape.append(jax.ShapeDtypeStruct((bx, nh, LANES, tx), dt))
            out_specs.append(pl.BlockSpec((1, nh, LANES, bm), lambda b, i: (b, 0, 0, i)))
    assert len(aliases) == len(stacked)
    return pl.pallas_call(
        functools.partial(_inproj_kernel, n_w=len(ws), n_alias=len(stacked), layer=j, plan=tuple(plan)),
        out_shape=out_shape, grid=grid, in_specs=in_specs, out_specs=out_specs, input_output_aliases=aliases,
        compiler_params=_cparams(("parallel", "parallel")), name="inproj",
    )(x3, *ws, *stacked)


def _outproj_ln_kernel(o_ref, g_ref, x_ref, w_ref, lg_ref, lb_ref, y_ref, *, alpha):
    g = g_ref[...].astype(F32)
    og = (o_ref[...].astype(F32) * (g * jax.nn.sigmoid(g))).astype(BF16)
    y = jnp.dot(og, w_ref[...], preferred_element_type=F32)
    z = alpha * x_ref[...] + y
    mu = jnp.mean(z, axis=-1, keepdims=True)
    zc = z - mu
    var = jnp.mean(zc * zc, axis=-1, keepdims=True)
    y_ref[...] = zc * lax.rsqrt(var + LN_EPS) * lg_ref[...] + lb_ref[...]


def _outproj_ln(o2, g2, x2, w, ln_g, ln_b, alpha):
    m, d = x2.shape
    dk = o2.shape[1]
    bm = _pick(m, 512)
    row = lambda i: (i, 0)
    fixed = lambda i: (0, 0)
    return pl.pallas_call(
        functools.partial(_outproj_ln_kernel, alpha=alpha),
        out_shape=jax.ShapeDtypeStruct((m, d), F32), grid=(m // bm,),
        in_specs=[pl.BlockSpec((bm, dk), row), pl.BlockSpec((bm, dk), row), pl.BlockSpec((bm, d), row),
                  pl.BlockSpec((dk, d), fixed), pl.BlockSpec((1, d), fixed), pl.BlockSpec((1, d), fixed)],
        out_specs=pl.BlockSpec((bm, d), row),
        compiler_params=_cparams(("parallel",)), name="outproj_ln",
    )(o2, g2, x2, w, ln_g.reshape(1, d), ln_b.reshape(1, d))


def _log_sigmoid(x):
    return -(jnp.maximum(-x, 0.0) + jnp.log1p(jnp.exp(-jnp.abs(x))))


def _logf_cumsum_kernel(z_ref, bf_ref, lf_ref, c_ref, *, p0, p1):
    z = z_ref[0]
    pos = lax.broadcasted_iota(jnp.int32, z.shape, 1)
    is_new = jnp.logical_and(pos >= p0, pos < p1)
    lf = jnp.where(is_new, _log_sigmoid(z + bf_ref[...]), z)
    lf_ref[0] = lf
    c = lf
    s = 1
    while s < z.shape[1]:
        c = c + jnp.where(pos >= s, pltpu.roll(c, s, axis=1), 0.0)
        s *= 2
    c_ref[0] = c


def _logf_cumsum(z, b_f, p0, p1):
    b, h, l = z.shape
    blk = pl.BlockSpec((1, h, l), lambda i: (i, 0, 0))
    return pl.pallas_call(
        functools.partial(_logf_cumsum_kernel, p0=p0, p1=p1),
        out_shape=[jax.ShapeDtypeStruct(z.shape, F32)] * 2, grid=(b,),
        in_specs=[blk, pl.BlockSpec((h, 1), lambda i: (0, 0))], out_specs=[blk, blk],
        compiler_params=_cparams(("parallel",)), name="logf_cumsum",
    )(z, b_f.reshape(h, 1))


def _score_store(u, slot, s_sc, mc_sc):
    s_sc[slot] = u
    mc_sc[slot] = jnp.broadcast_to(jnp.max(u, axis=0, keepdims=True), mc_sc.shape[1:])


def _softmax_update(h, slot, vt, s_sc, mc_sc, m_sc, acc_sc, row_shift=None):
    u = s_sc[slot]
    m_prev = m_sc[h]
    m_cur = mc_sc[slot]
    if row_shift is not None:
        m_cur = m_cur + row_shift
    m_new = jnp.maximum(m_prev, m_cur)
    m_row = m_new[0:1]
    p = jnp.exp2(u - (m_row if row_shift is None else m_row - row_shift))
    alpha = jnp.exp2(m_prev - m_new)
    vt1 = jnp.concatenate([vt, jnp.ones((BF16_ROWS, vt.shape[1]), BF16)], axis=0)
    acc_sc[h] = acc_sc[h] * alpha[0:1] + jnp.dot(vt1, p.astype(BF16), preferred_element_type=F32)
    m_sc[h] = m_new


def _head(k_ref, h):
    dh = LANES
    return k_ref[0, :, h * dh:(h + 1) * dh]


def _head_t(v_ref, h, v_rows):
    if not v_rows:
        return v_ref[0, h]
    return _head(v_ref, h).astype(F32).T.astype(BF16)


def _scores(k, kx, qt, qx):
    return jnp.dot(jnp.concatenate([k, kx], axis=1), jnp.concatenate([qt, qx], axis=0),
                   preferred_element_type=F32)


def _pipelined_heads(nh, qk_phase, sm_phase):
    qk_phase(0, 0)
    for h in range(nh - 1):
        qk_phase(h + 1, (h + 1) % 2)
        sm_phase(h, h % 2)
    sm_phase(nh - 1, (nh - 1) % 2)


def _attn_init(m_sc, acc_sc):
    m_sc[...] = jnp.full(m_sc.shape, -jnp.inf, F32)
    acc_sc[...] = jnp.zeros(acc_sc.shape, F32)


def _attn_finish(o_ref, acc_sc, nh, dh):
    for h in range(nh):
        a = acc_sc[h]
        o_ref[0, :, h * dh:(h + 1) * dh] = (a[0:dh] / a[dh:dh + 1]).T.astype(o_ref.dtype)


def _attn_scratch(nh, bq, bk, dh):
    return [pltpu.VMEM((nh, SUBLANES, bq), F32), pltpu.VMEM((nh, dh + BF16_ROWS, bq), F32),
            pltpu.VMEM((2, bk, bq), F32), pltpu.VMEM((2, SUBLANES, bq), F32)]


def _fox_kernel(qt_ref, k_ref, v_ref, cq_ref, kx_ref, qx_ref, o_ref, m_sc, acc_sc, s_sc, mc_sc,
                *, bq, bk, q_off, nh, v_rows):
    qi, ki = pl.program_id(1), pl.program_id(2)
    q_first = q_off + qi * bq
    q_last = q_first + bq - 1
    k_first = ki * bk
    k_last = k_first + bk - 1

    @pl.when(ki == 0)
    def _():
        _attn_init(m_sc, acc_sc)

    def step(masked):
        kx = kx_ref[0]
        if masked:
            kpos = k_first + lax.broadcasted_iota(jnp.int32, (bk, bq), 0)
            qpos = q_first + lax.broadcasted_iota(jnp.int32, (bk, bq), 1)
            causal = kpos <= qpos

        def qk_phase(h, slot):
            u = _scores(_head(k_ref, h), kx, qt_ref[0, h], qx_ref[h])
            if masked:
                u = jnp.where(causal, u, NEG_INF)
            _score_store(u, slot, s_sc, mc_sc)

        def sm_phase(h, slot):
            cq2 = cq_ref[0, pl.ds(h, 1), :] * LOG2E
            _softmax_update(h, slot, _head_t(v_ref, h, v_rows), s_sc, mc_sc, m_sc, acc_sc, row_shift=cq2)

        _pipelined_heads(nh, qk_phase, sm_phase)

    needed = k_first <= q_last
    straddles = k_last > q_first

    @pl.when(jnp.logical_and(needed, straddles))
    def _():
        step(True)

    @pl.when(jnp.logical_and(needed, jnp.logical_not(straddles)))
    def _():
        step(False)

    @pl.when(ki == pl.num_programs(2) - 1)
    def _():
        _attn_finish(o_ref, acc_sc, nh, LANES)


def _split3(x):
    def top_half(v):
        bits = lax.bitcast_convert_type(v, jnp.uint32) & jnp.uint32(0xFFFF0000)
        return lax.bitcast_convert_type(bits, F32)

    hi = top_half(x)
    r1 = x - hi
    mid = top_half(r1)
    return hi.astype(BF16), mid.astype(BF16), (r1 - mid).astype(BF16)


def _v_spec(v, nh, dh, bk, kmap):
    if v.ndim == 3:
        return True, pl.BlockSpec((1, bk, nh * dh), lambda bb, qi, ki: (bb, kmap(qi, ki), 0))
    return False, pl.BlockSpec((1, nh, dh, bk), lambda bb, qi, ki: (bb, 0, 0, kmap(qi, ki)))


def _fox_attention(qt_hm, k_rm, v, cq, ck, q_off, bq, bk):
    b, nh, dh, tq = qt_hm.shape
    s = k_rm.shape[1]
    nq, nk = tq // bq, s // bk
    w = LANES // nh
    assert w >= 3
    kx = jnp.stack(_split3(ck * (-LOG2E)), axis=-1)
    kx = _pad_to(kx, 3, w).transpose(0, 2, 1, 3).reshape(b, s, LANES)
    rows = jnp.arange(LANES)[None, :, None]
    heads = jnp.arange(nh)[:, None, None]
    qx = jnp.broadcast_to(jnp.logical_and(rows >= heads * w, rows < heads * w + 3), (nh, LANES, bq)).astype(BF16)

    def kmap(qi, ki):
        return jnp.minimum(ki, (q_off + (qi + 1) * bq - 1) // bk)

    v_rows, v_spec = _v_spec(v, nh, dh, bk, kmap)
    return pl.pallas_call(
        functools.partial(_fox_kernel, bq=bq, bk=bk, q_off=q_off, nh=nh, v_rows=v_rows),
        out_shape=jax.ShapeDtypeStruct((b, tq, nh * dh), BF16), grid=(b, nq, nk),
        in_specs=[pl.BlockSpec((1, nh, dh, bq), lambda bb, qi, ki: (bb, 0, 0, qi)),
                  pl.BlockSpec((1, bk, nh * dh), lambda bb, qi, ki: (bb, kmap(qi, ki), 0)),
                  v_spec,
                  pl.BlockSpec((1, nh, bq), lambda bb, qi, ki: (bb, 0, qi)),
                  pl.BlockSpec((1, bk, LANES), lambda bb, qi, ki: (bb, kmap(qi, ki), 0)),
                  pl.BlockSpec((nh, LANES, bq), lambda bb, qi, ki: (0, 0, 0))],
        out_specs=pl.BlockSpec((1, bq, nh * dh), lambda bb, qi, ki: (bb, qi, 0)),
        scratch_shapes=_attn_scratch(nh, bq, bk, dh),
        compiler_params=_cparams(("parallel", "parallel", "arbitrary")), name="fox_attention",
    )(qt_hm, k_rm, v, cq, kx, qx)


def _bit_planes(words):
    a = list(words)
    j, m = 16, 0x0000FFFF
    while j:
        k = 0
        while k < 32:
            t = (a[k] ^ lax.shift_right_logical(a[k + j], jnp.int32(j))) & jnp.int32(m)
            a[k] = a[k] ^ t
            a[k + j] = a[k + j] ^ lax.shift_left(t, jnp.int32(j))
            k = (k + j + 1) & ~j
        j >>= 1
        m = (m ^ (m << j)) & 0xFFFFFFFF
    return a


def _dsa_select_kernel(qit_ref, wit_ref, ki_ref, out_ref, key_sc, pl_sc, act_sc, j_sc,
                       *, bq, kb, q_off, s_valid, topk, nh_idx, idx_bits):
    qb = pl.program_id(1)
    q_first = q_off + qb * bq
    q_last = q_first + bq - 1
    n_adm = jnp.minimum(((q_last >> CHUNK_SHIFT) + 1) * CHUNK, s_valid)
    nkb = (jnp.maximum(n_adm, topk) + kb - 1) // kb
    nkb_total = key_sc.shape[0] // kb
    gpb = kb // PLANE_ROWS
    nkb4 = nkb // 4
    qchunk = (q_first + lax.broadcasted_iota(jnp.int32, (kb, bq), 1)) >> CHUNK_SHIFT
    row = lax.broadcasted_iota(jnp.int32, (kb, bq), 0)

    def admissible(kpos):
        return jnp.logical_and((kpos >> CHUNK_SHIFT) <= qchunk, kpos < s_valid)

    def rows(kblk):
        return pl.ds(pl.multiple_of(kblk * kb, kb), kb)

    def score_block(kblk, carry):
        kblock = ki_ref[0, rows(kblk), :]
        sc = jnp.zeros((kb, bq), F32)
        for h in range(nh_idx):
            d = jnp.dot(kblock, qit_ref[0, h], preferred_element_type=F32)
            sc = sc + wit_ref[0, h:h + 1, :] * jnp.maximum(d, 0.0)
        sc = jnp.where(admissible(kblk * kb + row), sc + 0.0, NEG_INF)
        bits = lax.bitcast_convert_type(sc, jnp.int32)
        key = bits ^ ((bits >> 31) & 0x7FFFFFFF)
        key_sc[rows(kblk), :] = key
        for g in range(gpb):
            base = g * PLANE_ROWS
            planes = _bit_planes([key[base + SUBLANES * j:base + SUBLANES * (j + 1), :] for j in range(32)])
            planes[0] = ~planes[0]
            for i in range(32):
                pl_sc[kblk * gpb + g, i] = planes[i]
            act_sc[kblk * gpb + g] = jnp.full((SUBLANES, bq), -1, jnp.int32)
        return carry

    lax.fori_loop(0, nkb, score_block, 0)

    def bit_step(it, carry):
        thr_u, above = carry

        def cnt_groups(g0, n, acc):
            for g in range(n):
                acc = acc + lax.population_count(act_sc[g0 + g] & pl_sc[g0 + g, it])
            return acc

        acc = lax.fori_loop(0, nkb4, lambda i, a: cnt_groups(i * gpb * 4, gpb * 4, a),
                            jnp.zeros((SUBLANES, bq), jnp.int32))
        acc = lax.fori_loop(nkb4 * 4, nkb, lambda i, a: cnt_groups(i * gpb, gpb, a), acc)
        c = jnp.sum(acc.astype(F32), axis=0, keepdims=True)
        take = (above + c) >= topk

        def upd_groups(g0, n, carry2):
            for g in range(n):
                a = act_sc[g0 + g]
                x = a & pl_sc[g0 + g, it]
                act_sc[g0 + g] = jnp.where(take, x, a ^ x)
            return carry2

        lax.fori_loop(0, nkb4, lambda i, c2: upd_groups(i * gpb * 4, gpb * 4, c2), 0)
        lax.fori_loop(nkb4 * 4, nkb, lambda i, c2: upd_groups(i * gpb, gpb, c2), 0)
        bit = lax.shift_left(jnp.int32(1), 31 - it)
        return thr_u | jnp.where(take, bit, 0), above + jnp.where(take, 0.0, c)

    thr_u, n_gt = lax.fori_loop(0, 32, bit_step, (jnp.zeros((1, bq), jnp.int32), jnp.zeros((1, bq), F32)))
    thr = thr_u ^ INT_MIN

    def eq_blk(kblk, acc):
        for g in range(gpb):
            acc = acc + lax.population_count(act_sc[kblk * gpb + g])
        return acc

    n_eq = jnp.sum(lax.fori_loop(0, nkb, eq_blk, jnp.zeros((SUBLANES, bq), jnp.int32)).astype(F32),
                   axis=0, keepdims=True)
    need = topk - n_gt
    has_excess = jnp.max(n_eq - need) > 0.0
    j_sc[...] = jnp.full(j_sc.shape, 2 ** 30, jnp.int32)

    @pl.when(has_excess)
    def _():
        def count_ties_below(cand):
            def blk(kblk, acc):
                hit = jnp.logical_and(key_sc[rows(kblk), :] == thr, kblk * kb + row < cand)
                ind = jnp.where(hit, 1.0, 0.0)
                for r in range(kb // SUBLANES):
                    acc = acc + ind[r * SUBLANES:(r + 1) * SUBLANES]
                return acc
            return jnp.sum(lax.fori_loop(0, nkb, blk, jnp.zeros((SUBLANES, bq), F32)), axis=0, keepdims=True)

        def idx_step(it, lo):
            cand = lo | lax.shift_left(jnp.int32(1), idx_bits - 1 - it)
            return jnp.where(count_ties_below(cand) < need, cand, lo)

        lo = lax.fori_loop(0, idx_bits, idx_step, jnp.zeros((1, bq), jnp.int32))
        j_sc[...] = jnp.broadcast_to(lo, j_sc.shape)

    j_cut = j_sc[0:1, :]

    def write_fast(kblk, carry):
        out_ref[0, 0, rows(kblk), :] = jnp.where(key_sc[rows(kblk), :] >= thr, 0.0, NEG_INF).astype(out_ref.dtype)
        return carry

    def write_block(kblk, carry):
        key = key_sc[rows(kblk), :]
        kpos = kblk * kb + row
        tie = jnp.logical_and(key == thr, kpos <= j_cut)
        sel = jnp.logical_and(jnp.logical_or(key > thr, tie), admissible(kpos))
        out_ref[0, 0, rows(kblk), :] = jnp.where(sel, 0.0, NEG_INF).astype(out_ref.dtype)
        return carry

    def fill_block(kblk, carry):
        out_ref[0, 0, rows(kblk), :] = jnp.full((kb, bq), NEG_INF, out_ref.dtype)
        return carry

    n_full = jnp.minimum(((q_first >> CHUNK_SHIFT) + 1) * CHUNK, s_valid) // kb
    n_fast = jnp.where(has_excess, 0, n_full)
    lax.fori_loop(0, n_fast, write_fast, 0)
    lax.fori_loop(n_fast, nkb, write_block, 0)
    lax.fori_loop(nkb, nkb_total, fill_block, 0)


def _dsa_select(qit_hm, wit, ki, q_off, s_valid, topk, bq, kb):
    b, nh_idx, d_idx, tq = qit_hm.shape
    s_pad = ki.shape[1]
    return pl.pallas_call(
        functools.partial(_dsa_select_kernel, bq=bq, kb=kb, q_off=q_off, s_valid=s_valid, topk=topk,
                          nh_idx=nh_idx, idx_bits=max(1, (s_pad - 1).bit_length())),
        out_shape=jax.ShapeDtypeStruct((b, tq // bq, s_pad, bq), BF16), grid=(b, tq // bq),
        in_specs=[pl.BlockSpec((1, nh_idx, d_idx, bq), lambda bb, qb: (bb, 0, 0, qb)),
                  pl.BlockSpec((1, nh_idx, bq), lambda bb, qb: (bb, 0, qb)),
                  pl.BlockSpec((1, s_pad, d_idx), lambda bb, qb: (bb, 0, 0))],
        out_specs=pl.BlockSpec((1, 1, s_pad, bq), lambda bb, qb: (bb, qb, 0, 0)),
        scratch_shapes=[pltpu.VMEM((s_pad, bq), jnp.int32),
                        pltpu.VMEM((s_pad // PLANE_ROWS, 32, SUBLANES, bq), jnp.int32),
                        pltpu.VMEM((s_pad // PLANE_ROWS, SUBLANES, bq), jnp.int32),
                        pltpu.VMEM((SUBLANES, bq), jnp.int32)],
        compiler_params=_cparams(("parallel", "arbitrary")), name="dsa_select",
    )(qit_hm, wit, ki)


def _dsa_attn_kernel(sl_ref, qt_ref, k_ref, v_ref, mb_ref, qx_ref, o_ref, m_sc, acc_sc, s_sc, mc_sc,
                     base_sc, t_sc, *, bq, bk, q_off, nh, v_rows):
    qi, ki = pl.program_id(1), pl.program_id(2)
    q_first = q_off + qi * bq
    q_last = q_first + bq - 1
    k_first = ki * bk
    k_last = k_first + bk - 1
    qposf = (q_first + lax.broadcasted_iota(jnp.int32, (1, bq), 1)).astype(F32)

    @pl.when(ki == 0)
    def _():
        _attn_init(m_sc, acc_sc)

    def step(past):
        base_sc[...] = jnp.concatenate([mb_ref[0, j] for j in range(mb_ref.shape[1])], axis=1).astype(F32)
        if past:
            kpos = k_first + lax.broadcasted_iota(jnp.int32, (bk, LANES), 0)
            lane = lax.broadcasted_iota(jnp.int32, (bk, LANES), 1)
            kx = jnp.where(lane < 3, kpos >> CHUNK_SHIFT, jnp.where(lane < 6, kpos & (CHUNK - 1), 0)).astype(BF16)
        else:
            kpos = k_first + lax.broadcasted_iota(jnp.int32, (bk, bq), 0)
            qpos = q_first + lax.broadcasted_iota(jnp.int32, (bk, bq), 1)
            t_sc[...] = jnp.minimum(kpos, 2 * qpos - kpos).astype(F32)

        def qk_phase(h, slot):
            if past:
                u = _scores(_head(k_ref, h), kx, qt_ref[0, h], qx_ref[h]) + base_sc[...]
            else:
                st = jnp.dot(_head(k_ref, h), qt_ref[0, h], preferred_element_type=F32)
                u = st + sl_ref[h] * t_sc[...] + base_sc[...]
            _score_store(u, slot, s_sc, mc_sc)

        def sm_phase(h, slot):
            _softmax_update(h, slot, _head_t(v_ref, h, v_rows), s_sc, mc_sc, m_sc, acc_sc,
                            row_shift=-sl_ref[h] * qposf)

        _pipelined_heads(nh, qk_phase, sm_phase)

    needed = k_first <= ((q_last >> CHUNK_SHIFT) << CHUNK_SHIFT) + CHUNK - 1
    past = k_last <= q_first

    @pl.when(past)
    def _():
        step(True)

    @pl.when(jnp.logical_and(needed, jnp.logical_not(past)))
    def _():
        step(False)

    @pl.when(ki == pl.num_programs(2) - 1)
    def _():
        _attn_finish(o_ref, acc_sc, nh, LANES)


def _dsa_attention(qt_hm, k_rm, v, mask, q_off, bq, bk):
    b, nh, dh, tq = qt_hm.shape
    s, mq = mask.shape[2], mask.shape[3]
    nq, nk, nsub = tq // bq, s // bk, bq // mq
    slopes2_np = (2.0 ** (-8.0 * np.arange(1, nh + 1) / nh) * LOG2E).astype(np.float32)
    slopes2 = jnp.asarray(slopes2_np)
    rest, pieces = slopes2_np, []
    for _ in range(3):
        piece = rest.astype(BF16).astype(np.float32)
        pieces.append(piece)
        rest = rest - piece
    col = np.zeros((nh, LANES), np.float32)
    col[:, 0:3] = np.stack(pieces, axis=1) * CHUNK
    col[:, 3:6] = np.stack(pieces, axis=1)
    qx = jnp.broadcast_to(jnp.asarray(col.astype(BF16))[:, :, None], (nh, LANES, bq))

    def kmap(qi, ki):
        q_last = q_off + (qi + 1) * bq - 1
        return jnp.minimum(ki, (((q_last >> CHUNK_SHIFT) << CHUNK_SHIFT) + CHUNK - 1) // bk)

    v_rows, v_spec = _v_spec(v, nh, dh, bk, kmap)
    return pl.pallas_call(
        functools.partial(_dsa_attn_kernel, bq=bq, bk=bk, q_off=q_off, nh=nh, v_rows=v_rows),
        out_shape=jax.ShapeDtypeStruct((b, tq, nh * dh), BF16), grid=(b, nq, nk),
        in_specs=[pl.BlockSpec(memory_space=pltpu.SMEM),
                  pl.BlockSpec((1, nh, dh, bq), lambda bb, qi, ki: (bb, 0, 0, qi)),
                  pl.BlockSpec((1, bk, nh * dh), lambda bb, qi, ki: (bb, kmap(qi, ki), 0)),
                  v_spec,
                  pl.BlockSpec((1, nsub, bk, mq), lambda bb, qi, ki: (bb, qi, kmap(qi, ki), 0)),
                  pl.BlockSpec((nh, LANES, bq), lambda bb, qi, ki: (0, 0, 0))],
        out_specs=pl.BlockSpec((1, bq, nh * dh), lambda bb, qi, ki: (bb, qi, 0)),
        scratch_shapes=_attn_scratch(nh, bq, bk, dh) + [pltpu.VMEM((bk, bq), F32), pltpu.VMEM((bk, bq), F32)],
        compiler_params=_cparams(("parallel", "parallel", "arbitrary")), name="dsa_attention",
    )(slopes2, qt_hm, k_rm, v, mask, qx)


def _rglru_kernel(u_ref, cw_ref, cb_ref, wr_ref, br_ref, wig_ref, big_ref, lam_ref, buf0_ref, h0_ref,
                  h_ref, conv_ref, hl_ref, ubuf, a_sc, b_sc, hbuf, hcar, *, tt, conv_w, nblk):
    t = pl.program_id(1)
    pad = SUBLANES
    d = u_ref.shape[2]
    blk = d // nblk

    @pl.when(t == 0)
    def _():
        ubuf[0:pad, :] = buf0_ref[0]
        hcar[...] = jnp.broadcast_to(h0_ref[0], (SUBLANES, d))

    ubuf[pad:pad + tt, :] = u_ref[0]
    uc = cb_ref[...] + ubuf[pad:pad + tt, :] * cw_ref[conv_w - 1:conv_w, :]
    for j in range(conv_w - 1):
        off = pad - (conv_w - 1) + j
        uc = uc + ubuf[off:off + tt, :] * cw_ref[j:j + 1, :]

    lam = lam_ref[...]
    neg_sp = -LRU_C * (jnp.maximum(-lam, 0.0) + jnp.log1p(jnp.exp(-jnp.abs(lam))))
    for n in range(nblk):
        cs = slice(n * blk, (n + 1) * blk)
        ucn = uc[:, cs]
        ub = ucn.astype(BF16)
        r = jax.nn.sigmoid(jnp.dot(ub, wr_ref[n], preferred_element_type=F32) + br_ref[:, cs])
        i = jax.nn.sigmoid(jnp.dot(ub, wig_ref[n], preferred_element_type=F32) + big_ref[:, cs])
        log_a = r * neg_sp[:, cs]
        a = jnp.exp(log_a)
        a_sc[:, cs] = a
        b_sc[:, cs] = jnp.sqrt(-jnp.tanh(log_a) * (a * a + 1.0)) * (i * ucn)

    row = lax.broadcasted_iota(jnp.int32, (SUBLANES, d), 0)

    def group(gi, hprev):
        r0 = pl.multiple_of(gi * SUBLANES, SUBLANES)
        av = a_sc[pl.ds(r0, SUBLANES), :]
        bv = b_sc[pl.ds(r0, SUBLANES), :]
        s = 1
        while s < SUBLANES:
            a_sh = pltpu.roll(av, s, axis=0)
            b_sh = pltpu.roll(bv, s, axis=0)
            m = row >= s
            bv = jnp.where(m, av * b_sh + bv, bv)
            av = jnp.where(m, av * a_sh, av)
            s *= 2
        hrows = av * hprev + bv
        hbuf[pl.ds(r0, SUBLANES), :] = hrows
        return jnp.broadcast_to(hrows[SUBLANES - 1:SUBLANES, :], (SUBLANES, d))

    hlast = lax.fori_loop(0, tt // SUBLANES, group, hcar[...])
    h_ref[0] = hbuf[...].astype(h_ref.dtype)
    hcar[...] = hlast
    hl_ref[0] = hlast[0:1, :]
    tail = ubuf[tt:tt + pad, :]
    conv_ref[0] = tail
    ubuf[0:pad, :] = tail


def _rglru(u, conv_w, conv_b, w_rg, b_rg, w_ig, b_ig, lam, buf0, h0, tt):
    b, t, d = u.shape
    cw = conv_w.shape[0]
    nblk, blk = w_rg.shape[0], w_rg.shape[1]
    vec = lambda a: a.reshape(1, d)
    fixed2 = lambda bb, ti: (0, 0)
    fixed3 = lambda bb, ti: (0, 0, 0)
    perb = lambda bb, ti: (bb, 0, 0)
    return pl.pallas_call(
        functools.partial(_rglru_kernel, tt=tt, conv_w=cw, nblk=nblk),
        out_shape=[jax.ShapeDtypeStruct((b, t, d), BF16), jax.ShapeDtypeStruct((b, SUBLANES, d), F32),
                   jax.ShapeDtypeStruct((b, 1, d), F32)],
        grid=(b, t // tt),
        in_specs=[pl.BlockSpec((1, tt, d), lambda bb, ti: (bb, ti, 0)),
                  pl.BlockSpec((cw, d), fixed2), pl.BlockSpec((1, d), fixed2),
                  pl.BlockSpec((nblk, blk, blk), fixed3), pl.BlockSpec((1, d), fixed2),
                  pl.BlockSpec((nblk, blk, blk), fixed3), pl.BlockSpec((1, d), fixed2),
                  pl.BlockSpec((1, d), fixed2),
                  pl.BlockSpec((1, SUBLANES, d), perb), pl.BlockSpec((1, 1, d), perb)],
        out_specs=[pl.BlockSpec((1, tt, d), lambda bb, ti: (bb, ti, 0)),
                   pl.BlockSpec((1, SUBLANES, d), perb), pl.BlockSpec((1, 1, d), perb)],
        scratch_shapes=[pltpu.VMEM((SUBLANES + tt, d), F32), pltpu.VMEM((tt, d), F32),
                        pltpu.VMEM((tt, d), F32), pltpu.VMEM((tt, d), F32), pltpu.VMEM((SUBLANES, d), F32)],
        compiler_params=_cparams(("parallel", "arbitrary")), name="rglru",
    )(u, conv_w, vec(conv_b), w_rg.astype(BF16), vec(b_rg), w_ig.astype(BF16), vec(b_ig), vec(lam), buf0, h0)


def _pack_rows_kernel(c_ref, n_ref, o_ref):
    p, t = c_ref.shape[1], n_ref.shape[1]
    o_ref[0, 0:p, :] = c_ref[0].reshape(p, o_ref.shape[2]).astype(o_ref.dtype)
    o_ref[0, p:p + t, :] = n_ref[0]
    o_ref[0, p + t:, :] = jnp.zeros((o_ref.shape[1] - p - t, o_ref.shape[2]), o_ref.dtype)


def _pack_rows(caches, j, new, s_pad):
    _, b, p, nh, dh = caches.shape
    t, w = new.shape[1], new.shape[2]
    return pl.pallas_call(
        _pack_rows_kernel, out_shape=jax.ShapeDtypeStruct((b, s_pad, w), BF16), grid=(b,),
        in_specs=[pl.BlockSpec((None, 1, p, nh, dh), lambda i: (j, i, 0, 0, 0)),
                  pl.BlockSpec((1, t, w), lambda i: (i, 0, 0))],
        out_specs=pl.BlockSpec((1, s_pad, w), lambda i: (i, 0, 0)),
        compiler_params=_cparams(("parallel",)), name="pack_rows",
    )(caches, new)


def _pad_to(a, axis, n):
    extra = n - a.shape[axis]
    if extra == 0:
        return a
    widths = [(0, 0)] * a.ndim
    widths[axis] = (0, extra)
    return jnp.pad(a, widths)


def _round_up(n, m):
    return (n + m - 1) // m * m


class _Group:
    def __init__(self, x, past):
        self.b, self.t, self.d = x.shape
        self.past = past
        self.flat = past > 0
        self.s_valid = past + self.t
        if self.flat:
            self.tq = _round_up(self.t, LANES)
            self.s_pad = _round_up(self.s_valid, PLANE_ROWS)
            self.fox_bq = self.dsa_bq = self.tq
            self.bk = self.s_pad
        else:
            self.tq = self.s_pad = self.t
            self.fox_bq = _pick(self.t, 512)
            self.dsa_bq = _pick(self.t, 256)
            self.bk = _pick(self.t, 512)

    def proj_view(self, x):
        return x.reshape(1, self.b * self.t, self.d) if self.flat else x

    def proj_bm(self):
        return self.b * self.t if self.flat else _pick(self.t, 512)

    def rows(self, a):
        return a.reshape(self.b, self.t, a.shape[-1])

    def heads_t(self, a):
        if not self.flat:
            return a
        return a.reshape(a.shape[1], a.shape[2], self.b, self.t).transpose(2, 0, 1, 3)

    def pad_q(self, a):
        return _pad_to(a, a.ndim - 1, self.tq)

    def keys(self, new_rm, cache):
        return new_rm if cache is None else _pack_rows(cache[0], cache[1], new_rm, self.s_pad)

    def values(self, new, cache):
        return new if cache is None else self.keys(self.rows(new), cache)


def _qkvg_plan(width, dh, v_rows):
    plan = [(0, 0, width, dh ** -0.5 * LOG2E, [(0, "headT")]),
            (0, width, width, 1.0, [(1, "row"), (3, "stack")]),
            (0, 2 * width, width, 1.0, [(2, "row" if v_rows else "headT"), (4, "stack")]),
            (0, 3 * width, width, 1.0, [(5, "row")])]
    outs = [("headT", width, BF16), ("row", width, BF16), ("row" if v_rows else "headT", width, BF16),
            ("stack", width, F32), ("stack", width, F32), ("row", width, BF16)]
    return plan, outs


def _mixer_a(grp, x, w_main, w_f, b_f, cache, layer, stacked):
    nh = b_f.shape[0]
    da = w_main.shape[1] // 4
    plan, outs = _qkvg_plan(da, da // nh, grp.flat)
    plan = plan + [(1, 0, LANES, 1.0, [(6, "row")])]
    outs = outs + [("row", LANES, F32)]
    qt, k_rm, vt, k, v, g, fl = _inproj(grp.proj_view(x), [w_main, w_f], plan, outs, grp.proj_bm(), layer, stacked)
    qt, k_rm, g = grp.pad_q(grp.heads_t(qt)), grp.rows(k_rm), grp.rows(g)
    z = grp.rows(fl)[:, :, :nh].transpose(0, 2, 1)
    ck_, cv_ = (None, None) if cache is None else cache[:2]
    if cache is not None:
        z = _pad_to(jnp.concatenate([cache[2].astype(F32).transpose(0, 2, 1), z], axis=2), 2, grp.s_pad)
    lf_all, c_all = _logf_cumsum(z, b_f, grp.past, grp.s_valid)
    logf = lf_all[:, :, grp.past:grp.s_valid].transpose(0, 2, 1)
    cq = grp.pad_q(c_all[:, :, grp.past:grp.s_valid])
    o = _fox_attention(qt, grp.keys(k_rm, ck_), grp.values(vt, cv_), cq, c_all,
                       grp.past, grp.fox_bq, grp.bk)[:, :grp.t]
    return o, g, ([k, v], logf)


def _mixer_b(grp, x, w_in, conv_w, conv_b, w_rg, b_rg, w_ig, b_ig, lam, state):
    dr = w_in.shape[1] // 2
    plan = [(0, 0, dr, 1.0, [(0, "row")]), (0, dr, dr, 1.0, [(1, "row")])]
    u, g = _inproj(grp.proj_view(x), [w_in], plan, [("row", dr, F32), ("row", dr, BF16)], grp.proj_bm())
    u, g = grp.rows(u), grp.rows(g)
    cw = conv_w.shape[0]
    if state is None:
        buf0 = jnp.zeros((grp.b, SUBLANES, dr), F32)
        h0 = jnp.zeros((grp.b, 1, dr), F32)
    else:
        buf, h0 = state
        buf0 = jnp.pad(buf.astype(F32), ((0, 0), (SUBLANES - (cw - 1), 0), (0, 0)))
        h0 = h0.astype(F32).reshape(grp.b, 1, dr)
    h, tail, hl = _rglru(u, conv_w, conv_b, w_rg, b_rg, w_ig, b_ig, lam, buf0, h0, _pick(grp.t, 256))
    return h, g, (tail[:, SUBLANES - (cw - 1):], hl[:, 0])


def _mixer_c(grp, x, w_main, w_idx, nh, nh_idx, d_idx, cache, layer, stacked):
    dc = w_main.shape[1] // 4
    wq = nh_idx * d_idx
    plan, outs = _qkvg_plan(dc, dc // nh, grp.flat)
    plan = plan + [(1, 0, wq, 1.0, [(6, "row")]), (1, wq, LANES, 1.0, [(7, "row")])]
    outs = outs + [("row", wq, BF16), ("row", LANES, F32)]
    qt, k_rm, vt, k, v, g, qi, kw = _inproj(grp.proj_view(x), [w_main, w_idx], plan, outs, grp.proj_bm(), layer,
                                            stacked)
    qt, k_rm = grp.pad_q(grp.heads_t(qt)), grp.rows(k_rm)
    g, qi, kw = grp.rows(g), grp.rows(qi), grp.rows(kw)
    ki = kw[:, :, :d_idx]
    wit = grp.pad_q(kw[:, :, d_idx:d_idx + nh_idx].transpose(0, 2, 1))
    qit = grp.pad_q(qi.reshape(grp.b, grp.t, nh_idx, d_idx).transpose(0, 2, 3, 1))
    ck_, cv_ = (None, None) if cache is None else cache[:2]
    ki_all = ki if cache is None else _pad_to(jnp.concatenate([cache[2].astype(F32), ki], axis=1), 1, grp.s_pad)
    topk = min(TOPK_MAX, grp.s_valid // 4)
    mask = _dsa_select(qit, wit, ki_all.astype(BF16), grp.past, grp.s_valid, topk, grp.dsa_bq, grp.bk)
    o = _dsa_attention(qt, grp.keys(k_rm, ck_), grp.values(vt, cv_), mask, grp.past, grp.fox_bq, grp.bk)[:, :grp.t]
    return o, g, ([k, v], ki)


def _run_trunk(x, p, caches, past):
    depth = p["ln_g"].shape[0]
    alpha = (2 * depth) ** 0.25
    grp = _Group(x, past)
    n_a, n_c = (depth + 2) // 3, depth // 3
    kv_a, kv_c = None, None
    new_a, new_b, new_c = [], [], []
    for i in range(depth):
        j, kind = i // 3, i % 3
        if kind == 0:
            cache = None if caches is None else ((caches["a_k"], j), (caches["a_v"], j), caches["a_logf"][j])
            o, g, (kv_a, logf) = _mixer_a(grp, x, p["w_main_a"][j], p["w_f_a"][j], p["b_f_a"][j], cache,
                                          (j, n_a), kv_a)
            new_a.append(logf)
            w_out = p["w_out_a"][j]
        elif kind == 1:
            state = None if caches is None else (caches["b_conv"][j], caches["b_h"][j])
            o, g, st = _mixer_b(grp, x, p["w_in_b"][j], p["conv_w_b"][j], p["conv_b_b"][j], p["w_rg_b"][j],
                                p["b_rg_b"][j], p["w_ig_b"][j], p["b_ig_b"][j], p["lam_b"][j], state)
            new_b.append(st)
            w_out = p["w_out_b"][j]
        else:
            cache = None if caches is None else ((caches["c_k"], j), (caches["c_v"], j), caches["c_kidx"][j])
            o, g, (kv_c, ki) = _mixer_c(grp, x, p["w_main_c"][j], p["w_idx_c"][j], p["h_c"], p["h_idx"], p["d_idx"],
                                        cache, (j, n_c), kv_c)
            new_c.append(ki)
            w_out = p["w_out_c"][j]
        m = grp.b * grp.t
        x = _outproj_ln(o.reshape(m, -1), g.reshape(m, -1), x.reshape(m, grp.d), w_out,
                        p["ln_g"][i], p["ln_b"][i], alpha).reshape(grp.b, grp.t, grp.d)
    stack = lambda sts, n: jnp.stack([s[n] for s in sts])
    heads = lambda a, nh: a.reshape(a.shape[0], grp.b, grp.t, nh, a.shape[-1] * a.shape[-2] // nh)
    h_a = p["b_f_a"].shape[1]
    return (x, heads(kv_a[0], h_a), heads(kv_a[1], h_a), jnp.stack(new_a), stack(new_b, 0), stack(new_b, 1),
            heads(kv_c[0], p["h_c"]), heads(kv_c[1], p["h_c"]), jnp.stack(new_c))


def kernel(x_prompt, x_sample, cache_a_k, cache_a_v, cache_a_logf, state_b_conv, state_b_h, cache_c_k, cache_c_v, cache_c_kidx, w_in_a, b_f_a, w_out_a, w_in_b, conv_w_b, conv_b_b, w_rg_b, b_rg_b, w_ig_b, b_ig_b, lam_b, w_out_b, w_in_c, w_out_c, ln_g, ln_b):
    h_a = b_f_a.shape[1]
    d_a = w_out_a.shape[1]
    d_c = w_out_c.shape[1]
    h_c = cache_c_k.shape[3]
    d_idx = cache_c_kidx.shape[-1]
    h_idx = (w_in_c.shape[2] - 4 * d_c - d_idx) // (d_idx + 1)
    assert d_a // h_a == LANES and d_c // h_c == LANES, "head width must equal the lane count"
    assert w_in_a.shape[2] == 4 * d_a + h_a and d_idx + h_idx <= LANES
    past = cache_a_k.shape[2]
    assert past % CHUNK == 0 and past > 0

    w_idx = w_in_c[:, :, 4 * d_c:]
    w_idx = _pad_to(w_idx, 2, h_idx * d_idx + LANES)
    p = {"w_main_a": w_in_a[:, :, :4 * d_a].astype(BF16),
         "w_f_a": _pad_to(w_in_a[:, :, 4 * d_a:], 2, LANES).astype(BF16),
         "b_f_a": b_f_a, "w_out_a": w_out_a.astype(BF16),
         "w_in_b": w_in_b.astype(BF16), "conv_w_b": conv_w_b, "conv_b_b": conv_b_b, "w_rg_b": w_rg_b,
         "b_rg_b": b_rg_b, "w_ig_b": w_ig_b, "b_ig_b": b_ig_b, "lam_b": lam_b, "w_out_b": w_out_b.astype(BF16),
         "w_main_c": w_in_c[:, :, :4 * d_c].astype(BF16), "w_idx_c": w_idx.astype(BF16),
         "w_out_c": w_out_c.astype(BF16), "ln_g": ln_g, "ln_b": ln_b,
         "h_c": h_c, "h_idx": h_idx, "d_idx": d_idx}
    caches = {"a_k": cache_a_k, "a_v": cache_a_v, "a_logf": cache_a_logf, "b_conv": state_b_conv,
              "b_h": state_b_h, "c_k": cache_c_k, "c_v": cache_c_v, "c_kidx": cache_c_kidx}
    outs_p = _run_trunk(x_prompt, p, None, 0)
    outs_s = _run_trunk(x_sample, p, caches, past)
    return (outs_p[0], outs_s[0]) + outs_p[1:] + outs_s[1:]
```

```python
import functools
import math

import jax
import jax.numpy as jnp
import numpy as np
from jax import lax
from jax.experimental import pallas as pl
from jax.experimental.pallas import tpu as pltpu

NEG_INF = -1e30
LN_EPS = 1e-5
CHUNK = 64
CHUNK_SHIFT = 6
TOPK_MAX = 256
LRU_C = 8.0
LANES = 128
SUBLANES = 8
BF16_ROWS = 16
PLANE_ROWS = 32 * SUBLANES
INT_MIN = -(2 ** 31)
VMEM_LIMIT = 56 * 1024 * 1024
LOG2E = math.log2(math.e)
SKIP_BITS = 60.0

F32 = jnp.float32
BF16 = jnp.bfloat16


def _cparams(sem, flags=None):
    return pltpu.CompilerParams(dimension_semantics=sem, vmem_limit_bytes=VMEM_LIMIT, flags=flags)


def _pick(n, pref):
    if n <= pref:
        return n
    b = pref
    while n % b:
        b //= 2
    return b


def _inproj_kernel(x_ref, *refs, n_w, n_alias, layer, plan):
    w_refs, out_refs = refs[:n_w], refs[n_w + n_alias:]
    xb = x_ref[0].astype(BF16)
    for w_idx, c0, width, scale, outs in plan:
        r = jnp.dot(xb, w_refs[w_idx][:, c0:c0 + width], preferred_element_type=F32)
        if scale != 1.0:
            r = r * scale
        for o_idx, kind in outs:
            o = out_refs[o_idx]
            if kind == "row":
                o[0] = r.astype(o.dtype)
            elif kind == "stack":
                val = r.astype(o.dtype).reshape(o.shape[2:])
                if o.shape[0] == 1:
                    o[0, 0] = val
                else:
                    for l in range(o.shape[0]):
                        o[l, 0] = val if l == layer else jnp.zeros_like(val)
            else:
                for h in range(width // LANES):
                    o[0, h] = r[:, h * LANES:(h + 1) * LANES].T.astype(o.dtype)


def _inproj(x3, ws, plan, out_defs, bm, layer=(0, 1), stacked=None):
    bx, tx, d = x3.shape
    j, n_layers = layer
    grid = (bx, tx // bm)
    in_specs = [pl.BlockSpec((1, bm, d), lambda b, i: (b, i, 0))]
    for w in ws:
        in_specs.append(pl.BlockSpec(w.shape, lambda b, i: (0, 0)))
    stacked = list(stacked or [])
    in_specs += [pl.BlockSpec(memory_space=pl.ANY)] * len(stacked)
    out_shape, out_specs, aliases = [], [], {}
    for o_idx, (kind, width, dt) in enumerate(out_defs):
        if kind == "row":
            out_shape.append(jax.ShapeDtypeStruct((bx, tx, width), dt))
            out_specs.append(pl.BlockSpec((1, bm, width), lambda b, i: (b, i, 0)))
        elif kind == "stack":
            out_shape.append(jax.ShapeDtypeStruct((n_layers, bx, tx, width // LANES, LANES), dt))
            if stacked:
                out_specs.append(pl.BlockSpec((1, 1, bm, width // LANES, LANES), lambda b, i: (j, b, i, 0, 0)))
                aliases[1 + len(ws) + len(aliases)] = o_idx
            else:
                out_specs.append(pl.BlockSpec((n_layers, 1, bm, width // LANES, LANES), lambda b, i: (0, b, i, 0, 0)))
        else:
            nh = width // LANES
            out_shape.append(jax.ShapeDtypeStruct((bx, nh, LANES, tx), dt))
            out_specs.append(pl.BlockSpec((1, nh, LANES, bm), lambda b, i: (b, 0, 0, i)))
    assert len(aliases) == len(stacked)
    return pl.pallas_call(
        functools.partial(_inproj_kernel, n_w=len(ws), n_alias=len(stacked), layer=j, plan=tuple(plan)),
        out_shape=out_shape, grid=grid, in_specs=in_specs, out_specs=out_specs, input_output_aliases=aliases,
        compiler_params=_cparams(("parallel", "parallel")), name="inproj",
    )(x3, *ws, *stacked)


def _outproj_ln_kernel(o_ref, g_ref, x_ref, w_ref, lg_ref, lb_ref, y_ref, *, alpha):
    g = g_ref[...].astype(F32)
    og = (o_ref[...].astype(F32) * (g * jax.nn.sigmoid(g))).astype(BF16)
    y = jnp.dot(og, w_ref[...], preferred_element_type=F32)
    z = alpha * x_ref[...] + y
    mu = jnp.mean(z, axis=-1, keepdims=True)
    zc = z - mu
    var = jnp.mean(zc * zc, axis=-1, keepdims=True)
    y_ref[...] = zc * lax.rsqrt(var + LN_EPS) * lg_ref[...] + lb_ref[...]


def _outproj_ln(o2, g2, x2, w, ln_g, ln_b, alpha):
    m, d = x2.shape
    dk = o2.shape[1]
    bm = _pick(m, 512)
    row = lambda i: (i, 0)
    fixed = lambda i: (0, 0)
    return pl.pallas_call(
        functools.partial(_outproj_ln_kernel, alpha=alpha),
        out_shape=jax.ShapeDtypeStruct((m, d), F32), grid=(m // bm,),
        in_specs=[pl.BlockSpec((bm, dk), row), pl.BlockSpec((bm, dk), row), pl.BlockSpec((bm, d), row),
                  pl.BlockSpec((dk, d), fixed), pl.BlockSpec((1, d), fixed), pl.BlockSpec((1, d), fixed)],
        out_specs=pl.BlockSpec((bm, d), row),
        compiler_params=_cparams(("parallel",)), name="outproj_ln",
    )(o2, g2, x2, w, ln_g.reshape(1, d), ln_b.reshape(1, d))


def _log_sigmoid(x):
    return -(jnp.maximum(-x, 0.0) + jnp.log1p(jnp.exp(-jnp.abs(x))))


def _logf_cumsum_kernel(z_ref, bf_ref, lf_ref, c_ref, *, p0, p1):
    z = z_ref[0]
    pos = lax.broadcasted_iota(jnp.int32, z.shape, 1)
    is_new = jnp.logical_and(pos >= p0, pos < p1)
    lf = jnp.where(is_new, _log_sigmoid(z + bf_ref[...]), z)
    lf_ref[0] = lf
    c = lf
    s = 1
    while s < z.shape[1]:
        c = c + jnp.where(pos >= s, pltpu.roll(c, s, axis=1), 0.0)
        s *= 2
    c_ref[0] = c


def _logf_cumsum(z, b_f, p0, p1):
    b, h, l = z.shape
    blk = pl.BlockSpec((1, h, l), lambda i: (i, 0, 0))
    return pl.pallas_call(
        functools.partial(_logf_cumsum_kernel, p0=p0, p1=p1),
        out_shape=[jax.ShapeDtypeStruct(z.shape, F32)] * 2, grid=(b,),
        in_specs=[blk, pl.BlockSpec((h, 1), lambda i: (0, 0))], out_specs=[blk, blk],
        compiler_params=_cparams(("parallel",)), name="logf_cumsum",
    )(z, b_f.reshape(h, 1))


def _score_store(u, slot, s_sc, mc_sc):
    s_sc[slot] = u
    mc_sc[slot] = jnp.broadcast_to(jnp.max(u, axis=0, keepdims=True), mc_sc.shape[1:])


def _softmax_update(h, slot, vt, s_sc, mc_sc, m_sc, acc_sc, row_shift=None):
    u = s_sc[slot]
    m_prev = m_sc[h]
    m_cur = mc_sc[slot]
    if row_shift is not None:
        m_cur = m_cur + row_shift
    m_new = jnp.maximum(m_prev, m_cur)
    m_row = m_new[0:1]
    p = jnp.exp2(u - (m_row if row_shift is None else m_row - row_shift))
    alpha = jnp.exp2(m_prev - m_new)
    vt1 = jnp.concatenate([vt, jnp.ones((BF16_ROWS, vt.shape[1]), BF16)], axis=0)
    acc_sc[h] = acc_sc[h] * alpha[0:1] + jnp.dot(vt1, p.astype(BF16), preferred_element_type=F32)
    m_sc[h] = m_new


def _head(k_ref, h):
    dh = LANES
    return k_ref[0, :, h * dh:(h + 1) * dh]


def _head_t(v_ref, h, v_rows):
    if not v_rows:
        return v_ref[0, h]
    return _head(v_ref, h).astype(F32).T.astype(BF16)


def _scores(k, kx, qt, qx):
    return jnp.dot(jnp.concatenate([k, kx], axis=1), jnp.concatenate([qt, qx], axis=0),
                   preferred_element_type=F32)


def _pipelined_heads(nh, qk_phase, sm_phase):
    qk_phase(0, 0)
    for h in range(nh - 1):
        qk_phase(h + 1, (h + 1) % 2)
        sm_phase(h, h % 2)
    sm_phase(nh - 1, (nh - 1) % 2)


def _attn_init(m_sc, acc_sc):
    m_sc[...] = jnp.full(m_sc.shape, -jnp.inf, F32)
    acc_sc[...] = jnp.zeros(acc_sc.shape, F32)


def _attn_finish(o_ref, acc_sc, nh, dh):
    for h in range(nh):
        a = acc_sc[h]
        o_ref[0, :, h * dh:(h + 1) * dh] = (a[0:dh] / a[dh:dh + 1]).T.astype(o_ref.dtype)


def _attn_scratch(nh, bq, bk, dh):
    return [pltpu.VMEM((nh, SUBLANES, bq), F32), pltpu.VMEM((nh, dh + BF16_ROWS, bq), F32),
            pltpu.VMEM((2, bk, bq), F32), pltpu.VMEM((2, SUBLANES, bq), F32)]


def _fox_kernel(lo_ref, qt_ref, k_ref, v_ref, cq_ref, kx_ref, qx_ref, o_ref, m_sc, acc_sc, s_sc, mc_sc,
                *, bq, bk, q_off, nh, v_rows):
    qi, ki = pl.program_id(1), pl.program_id(2)
    q_first = q_off + qi * bq
    q_last = q_first + bq - 1
    k_first = ki * bk
    k_last = k_first + bk - 1

    @pl.when(ki == 0)
    def _():
        _attn_init(m_sc, acc_sc)

    def step(masked):
        kx = kx_ref[0]
        if masked:
            kpos = k_first + lax.broadcasted_iota(jnp.int32, (bk, bq), 0)
            qpos = q_first + lax.broadcasted_iota(jnp.int32, (bk, bq), 1)
            causal = kpos <= qpos

        def qk_phase(h, slot):
            u = _scores(_head(k_ref, h), kx, qt_ref[0, h], qx_ref[h])
            if masked:
                u = jnp.where(causal, u, NEG_INF)
            _score_store(u, slot, s_sc, mc_sc)

        def sm_phase(h, slot):
            cq2 = cq_ref[0, pl.ds(h, 1), :] * LOG2E
            _softmax_update(h, slot, _head_t(v_ref, h, v_rows), s_sc, mc_sc, m_sc, acc_sc, row_shift=cq2)

        _pipelined_heads(nh, qk_phase, sm_phase)

    needed = jnp.logical_and(k_first <= q_last, ki >= lo_ref[pl.program_id(0), qi])
    straddles = k_last > q_first

    @pl.when(jnp.logical_and(needed, straddles))
    def _():
        step(True)

    @pl.when(jnp.logical_and(needed, jnp.logical_not(straddles)))
    def _():
        step(False)

    @pl.when(ki == pl.num_programs(2) - 1)
    def _():
        _attn_finish(o_ref, acc_sc, nh, LANES)


def _split3(x):
    def top_half(v):
        bits = lax.bitcast_convert_type(v, jnp.uint32) & jnp.uint32(0xFFFF0000)
        return lax.bitcast_convert_type(bits, F32)

    hi = top_half(x)
    r1 = x - hi
    mid = top_half(r1)
    return hi.astype(BF16), mid.astype(BF16), (r1 - mid).astype(BF16)


def _v_spec(v, nh, dh, bk, kmap):
    if v.ndim == 3:
        return True, pl.BlockSpec((1, bk, nh * dh), lambda *ix: (ix[0], kmap(*ix), 0))
    return False, pl.BlockSpec((1, nh, dh, bk), lambda *ix: (ix[0], 0, 0, kmap(*ix)))


def _fox_first_block(qt_hm, k_rm, cq, ck, q_off, bq, bk):
    b, nh, dh, tq = qt_hm.shape
    s = k_rm.shape[1]
    nq, nk = tq // bq, s // bk
    qn = jnp.sqrt(jnp.sum(jnp.square(qt_hm.astype(F32)), axis=2)).reshape(b, nh, nq, bq).max(axis=-1)
    kn = jnp.sqrt(jnp.sum(jnp.square(k_rm.astype(F32).reshape(b, nk, bk, nh, dh)), axis=-1)).max(axis=2)
    kn = kn.transpose(0, 2, 1)
    first_self = [(q_off + qi * bq) // bk for qi in range(nq)]
    last_self = [min((q_off + (qi + 1) * bq - 1) // bk, nk - 1) for qi in range(nq)]
    kn_self = jnp.stack([kn[:, :, f:l + 1].max(axis=-1) for f, l in zip(first_self, last_self)], axis=-1)
    c_first = cq[:, :, ::bq]
    c_last = ck[:, :, bk - 1::bk]
    bound = (qn[..., :, None] * (kn[..., None, :] + kn_self[..., :, None])
             + (c_first[..., :, None] - c_last[..., None, :]) * LOG2E)
    skip = jnp.all(bound < -SKIP_BITS, axis=1)
    lo = jnp.sum(jnp.cumprod(skip.astype(jnp.int32), axis=-1), axis=-1)
    return jnp.minimum(lo, jnp.asarray(first_self, jnp.int32)[None, :]).astype(jnp.int32)


def _fox_attention(qt_hm, k_rm, v, cq, ck, q_off, bq, bk):
    b, nh, dh, tq = qt_hm.shape
    s = k_rm.shape[1]
    nq, nk = tq // bq, s // bk
    w = LANES // nh
    assert w >= 3
    kx = jnp.stack(_split3(ck * (-LOG2E)), axis=-1)
    kx = _pad_to(kx, 3, w).transpose(0, 2, 1, 3).reshape(b, s, LANES)
    rows = jnp.arange(LANES)[None, :, None]
    heads = jnp.arange(nh)[:, None, None]
    qx = jnp.broadcast_to(jnp.logical_and(rows >= heads * w, rows < heads * w + 3), (nh, LANES, bq)).astype(BF16)

    lo = _fox_first_block(qt_hm, k_rm, cq, ck, q_off, bq, bk)

    def kmap(bb, qi, ki, lo_ref):
        return jnp.clip(ki, lo_ref[bb, qi], (q_off + (qi + 1) * bq - 1) // bk)

    v_rows, v_spec = _v_spec(v, nh, dh, bk, kmap)
    grid_spec = pltpu.PrefetchScalarGridSpec(
        num_scalar_prefetch=1, grid=(b, nq, nk),
        in_specs=[pl.BlockSpec((1, nh, dh, bq), lambda bb, qi, ki, lo_ref: (bb, 0, 0, qi)),
                  pl.BlockSpec((1, bk, nh * dh), lambda *ix: (ix[0], kmap(*ix), 0)),
                  v_spec,
                  pl.BlockSpec((1, nh, bq), lambda bb, qi, ki, lo_ref: (bb, 0, qi)),
                  pl.BlockSpec((1, bk, LANES), lambda *ix: (ix[0], kmap(*ix), 0)),
                  pl.BlockSpec((nh, LANES, bq), lambda bb, qi, ki, lo_ref: (0, 0, 0))],
        out_specs=pl.BlockSpec((1, bq, nh * dh), lambda bb, qi, ki, lo_ref: (bb, qi, 0)),
        scratch_shapes=_attn_scratch(nh, bq, bk, dh))
    return pl.pallas_call(
        functools.partial(_fox_kernel, bq=bq, bk=bk, q_off=q_off, nh=nh, v_rows=v_rows),
        out_shape=jax.ShapeDtypeStruct((b, tq, nh * dh), BF16), grid_spec=grid_spec,
        compiler_params=_cparams(("parallel", "parallel", "arbitrary")), name="fox_attention",
    )(lo, qt_hm, k_rm, v, cq, kx, qx)


def _bit_planes(words):
    a = list(words)
    j, m = 16, 0x0000FFFF
    while j:
        k = 0
        while k < 32:
            t = (a[k] ^ lax.shift_right_logical(a[k + j], jnp.int32(j))) & jnp.int32(m)
            a[k] = a[k] ^ t
            a[k + j] = a[k + j] ^ lax.shift_left(t, jnp.int32(j))
            k = (k + j + 1) & ~j
        j >>= 1
        m = (m ^ (m << j)) & 0xFFFFFFFF
    return a


def _dsa_select_kernel(qit_ref, wit_ref, ki_ref, out_ref, key_sc, pl_sc, act_sc, j_sc,
                       *, bq, kb, q_off, s_valid, topk, nh_idx, idx_bits):
    qb = pl.program_id(1)
    q_first = q_off + qb * bq
    q_last = q_first + bq - 1
    n_adm = jnp.minimum(((q_last >> CHUNK_SHIFT) + 1) * CHUNK, s_valid)
    nkb = (jnp.maximum(n_adm, topk) + kb - 1) // kb
    nkb_total = key_sc.shape[0] // kb
    gpb = kb // PLANE_ROWS
    nkb4 = nkb // 4
    qchunk = (q_first + lax.broadcasted_iota(jnp.int32, (kb, bq), 1)) >> CHUNK_SHIFT
    row = lax.broadcasted_iota(jnp.int32, (kb, bq), 0)

    def admissible(kpos):
        return jnp.logical_and((kpos >> CHUNK_SHIFT) <= qchunk, kpos < s_valid)

    def rows(kblk):
        return pl.ds(pl.multiple_of(kblk * kb, kb), kb)

    def score_block(kblk, carry):
        kblock = ki_ref[0, rows(kblk), :]
        sc = jnp.zeros((kb, bq), F32)
        for h in range(nh_idx):
            d = jnp.dot(kblock, qit_ref[0, h], preferred_element_type=F32)
            sc = sc + wit_ref[0, h:h + 1, :] * jnp.maximum(d, 0.0)
        sc = jnp.where(admissible(kblk * kb + row), sc + 0.0, NEG_INF)
        bits = lax.bitcast_convert_type(sc, jnp.int32)
        key = bits ^ ((bits >> 31) & 0x7FFFFFFF)
        key_sc[rows(kblk), :] = key
        for g in range(gpb):
            base = g * PLANE_ROWS
            planes = _bit_planes([key[base + SUBLANES * j:base + SUBLANES * (j + 1), :] for j in range(32)])
            planes[0] = ~planes[0]
            for i in range(32):
                pl_sc[kblk * gpb + g, i] = planes[i]
            act_sc[kblk * gpb + g] = jnp.full((SUBLANES, bq), -1, jnp.int32)
        return carry

    lax.fori_loop(0, nkb, score_block, 0)

    def bit_step(it, carry):
        thr_u, above = carry

        def cnt_groups(g0, n, acc):
            for g in range(n):
                acc = acc + lax.population_count(act_sc[g0 + g] & pl_sc[g0 + g, it])
            return acc

        acc = lax.fori_loop(0, nkb4, lambda i, a: cnt_groups(i * gpb * 4, gpb * 4, a),
                            jnp.zeros((SUBLANES, bq), jnp.int32))
        acc = lax.fori_loop(nkb4 * 4, nkb, lambda i, a: cnt_groups(i * gpb, gpb, a), acc)
        c = jnp.sum(acc.astype(F32), axis=0, keepdims=True)
        take = (above + c) >= topk

        def upd_groups(g0, n, carry2):
            for g in range(n):
                a = act_sc[g0 + g]
                x = a & pl_sc[g0 + g, it]
                act_sc[g0 + g] = jnp.where(take, x, a ^ x)
            return carry2

        lax.fori_loop(0, nkb4, lambda i, c2: upd_groups(i * gpb * 4, gpb * 4, c2), 0)
        lax.fori_loop(nkb4 * 4, nkb, lambda i, c2: upd_groups(i * gpb, gpb, c2), 0)
        bit = lax.shift_left(jnp.int32(1), 31 - it)
        return thr_u | jnp.where(take, bit, 0), above + jnp.where(take, 0.0, c)

    thr_u, n_gt = lax.fori_loop(0, 32, bit_step, (jnp.zeros((1, bq), jnp.int32), jnp.zeros((1, bq), F32)))
    thr = thr_u ^ INT_MIN

    def eq_blk(kblk, acc):
        for g in range(gpb):
            acc = acc + lax.population_count(act_sc[kblk * gpb + g])
        return acc

    n_eq = jnp.sum(lax.fori_loop(0, nkb, eq_blk, jnp.zeros((SUBLANES, bq), jnp.int32)).astype(F32),
                   axis=0, keepdims=True)
    need = topk - n_gt
    has_excess = jnp.max(n_eq - need) > 0.0
    j_sc[...] = jnp.full(j_sc.shape, 2 ** 30, jnp.int32)

    @pl.when(has_excess)
    def _():
        def count_ties_below(cand):
            def blk(kblk, acc):
                hit = jnp.logical_and(key_sc[rows(kblk), :] == thr, kblk * kb + row < cand)
                ind = jnp.where(hit, 1.0, 0.0)
                for r in range(kb // SUBLANES):
                    acc = acc + ind[r * SUBLANES:(r + 1) * SUBLANES]
                return acc
            return jnp.sum(lax.fori_loop(0, nkb, blk, jnp.zeros((SUBLANES, bq), F32)), axis=0, keepdims=True)

        def idx_step(it, lo):
            cand = lo | lax.shift_left(jnp.int32(1), idx_bits - 1 - it)
            return jnp.where(count_ties_below(cand) < need, cand, lo)

        lo = lax.fori_loop(0, idx_bits, idx_step, jnp.zeros((1, bq), jnp.int32))
        j_sc[...] = jnp.broadcast_to(lo, j_sc.shape)

    j_cut = j_sc[0:1, :]

    def write_fast(kblk, carry):
        out_ref[0, 0, rows(kblk), :] = jnp.where(key_sc[rows(kblk), :] >= thr, 0.0, NEG_INF).astype(out_ref.dtype)
        return carry

    def write_block(kblk, carry):
        key = key_sc[rows(kblk), :]
        kpos = kblk * kb + row
        tie = jnp.logical_and(key == thr, kpos <= j_cut)
        sel = jnp.logical_and(jnp.logical_or(key > thr, tie), admissible(kpos))
        out_ref[0, 0, rows(kblk), :] = jnp.where(sel, 0.0, NEG_INF).astype(out_ref.dtype)
        return carry

    def fill_block(kblk, carry):
        out_ref[0, 0, rows(kblk), :] = jnp.full((kb, bq), NEG_INF, out_ref.dtype)
        return carry

    n_full = jnp.minimum(((q_first >> CHUNK_SHIFT) + 1) * CHUNK, s_valid) // kb
    n_fast = jnp.where(has_excess, 0, n_full)
    lax.fori_loop(0, n_fast, write_fast, 0)
    lax.fori_loop(n_fast, nkb, write_block, 0)
    lax.fori_loop(nkb, nkb_total, fill_block, 0)


def _dsa_select(qit_hm, wit, ki, q_off, s_valid, topk, bq, kb):
    b, nh_idx, d_idx, tq = qit_hm.shape
    s_pad = ki.shape[1]
    return pl.pallas_call(
        functools.partial(_dsa_select_kernel, bq=bq, kb=kb, q_off=q_off, s_valid=s_valid, topk=topk,
                          nh_idx=nh_idx, idx_bits=max(1, (s_pad - 1).bit_length())),
        out_shape=jax.ShapeDtypeStruct((b, tq // bq, s_pad, bq), BF16), grid=(b, tq // bq),
        in_specs=[pl.BlockSpec((1, nh_idx, d_idx, bq), lambda bb, qb: (bb, 0, 0, qb)),
                  pl.BlockSpec((1, nh_idx, bq), lambda bb, qb: (bb, 0, qb)),
                  pl.BlockSpec((1, s_pad, d_idx), lambda bb, qb: (bb, 0, 0))],
        out_specs=pl.BlockSpec((1, 1, s_pad, bq), lambda bb, qb: (bb, qb, 0, 0)),
        scratch_shapes=[pltpu.VMEM((s_pad, bq), jnp.int32),
                        pltpu.VMEM((s_pad // PLANE_ROWS, 32, SUBLANES, bq), jnp.int32),
                        pltpu.VMEM((s_pad // PLANE_ROWS, SUBLANES, bq), jnp.int32),
                        pltpu.VMEM((SUBLANES, bq), jnp.int32)],
        compiler_params=_cparams(("parallel", "arbitrary")), name="dsa_select",
    )(qit_hm, wit, ki)


def _dsa_attn_kernel(sl_ref, qt_ref, k_ref, v_ref, mb_ref, qx_ref, o_ref, m_sc, acc_sc, s_sc, mc_sc,
                     base_sc, t_sc, *, bq, bk, q_off, nh, v_rows):
    qi, ki = pl.program_id(1), pl.program_id(2)
    q_first = q_off + qi * bq
    q_last = q_first + bq - 1
    k_first = ki * bk
    k_last = k_first + bk - 1
    qposf = (q_first + lax.broadcasted_iota(jnp.int32, (1, bq), 1)).astype(F32)

    @pl.when(ki == 0)
    def _():
        _attn_init(m_sc, acc_sc)

    def step(past):
        base_sc[...] = jnp.concatenate([mb_ref[0, j] for j in range(mb_ref.shape[1])], axis=1).astype(F32)
        if past:
            kpos = k_first + lax.broadcasted_iota(jnp.int32, (bk, LANES), 0)
            lane = lax.broadcasted_iota(jnp.int32, (bk, LANES), 1)
            kx = jnp.where(lane < 3, kpos >> CHUNK_SHIFT, jnp.where(lane < 6, kpos & (CHUNK - 1), 0)).astype(BF16)
        else:
            kpos = k_first + lax.broadcasted_iota(jnp.int32, (bk, bq), 0)
            qpos = q_first + lax.broadcasted_iota(jnp.int32, (bk, bq), 1)
            t_sc[...] = jnp.minimum(kpos, 2 * qpos - kpos).astype(F32)

        def qk_phase(h, slot):
            if past:
                u = _scores(_head(k_ref, h), kx, qt_ref[0, h], qx_ref[h]) + base_sc[...]
            else:
                st = jnp.dot(_head(k_ref, h), qt_ref[0, h], preferred_element_type=F32)
                u = st + sl_ref[h] * t_sc[...] + base_sc[...]
            _score_store(u, slot, s_sc, mc_sc)

        def sm_phase(h, slot):
            _softmax_update(h, slot, _head_t(v_ref, h, v_rows), s_sc, mc_sc, m_sc, acc_sc,
                            row_shift=-sl_ref[h] * qposf)

        _pipelined_heads(nh, qk_phase, sm_phase)

    needed = k_first <= ((q_last >> CHUNK_SHIFT) << CHUNK_SHIFT) + CHUNK - 1
    past = k_last <= q_first

    @pl.when(past)
    def _():
        step(True)

    @pl.when(jnp.logical_and(needed, jnp.logical_not(past)))
    def _():
        step(False)

    @pl.when(ki == pl.num_programs(2) - 1)
    def _():
        _attn_finish(o_ref, acc_sc, nh, LANES)


def _dsa_attention(qt_hm, k_rm, v, mask, q_off, bq, bk):
    b, nh, dh, tq = qt_hm.shape
    s, mq = mask.shape[2], mask.shape[3]
    nq, nk, nsub = tq // bq, s // bk, bq // mq
    slopes2_np = (2.0 ** (-8.0 * np.arange(1, nh + 1) / nh) * LOG2E).astype(np.float32)
    slopes2 = jnp.asarray(slopes2_np)
    rest, pieces = slopes2_np, []
    for _ in range(3):
        piece = rest.astype(BF16).astype(np.float32)
        pieces.append(piece)
        rest = rest - piece
    col = np.zeros((nh, LANES), np.float32)
    col[:, 0:3] = np.stack(pieces, axis=1) * CHUNK
    col[:, 3:6] = np.stack(pieces, axis=1)
    qx = jnp.broadcast_to(jnp.asarray(col.astype(BF16))[:, :, None], (nh, LANES, bq))

    def kmap(bb, qi, ki):
        q_last = q_off + (qi + 1) * bq - 1
        return jnp.minimum(ki, (((q_last >> CHUNK_SHIFT) << CHUNK_SHIFT) + CHUNK - 1) // bk)

    v_rows, v_spec = _v_spec(v, nh, dh, bk, kmap)
    return pl.pallas_call(
        functools.partial(_dsa_attn_kernel, bq=bq, bk=bk, q_off=q_off, nh=nh, v_rows=v_rows),
        out_shape=jax.ShapeDtypeStruct((b, tq, nh * dh), BF16), grid=(b, nq, nk),
        in_specs=[pl.BlockSpec(memory_space=pltpu.SMEM),
                  pl.BlockSpec((1, nh, dh, bq), lambda bb, qi, ki: (bb, 0, 0, qi)),
                  pl.BlockSpec((1, bk, nh * dh), lambda bb, qi, ki: (bb, kmap(bb, qi, ki), 0)),
                  v_spec,
                  pl.BlockSpec((1, nsub, bk, mq), lambda bb, qi, ki: (bb, qi, kmap(bb, qi, ki), 0)),
                  pl.BlockSpec((nh, LANES, bq), lambda bb, qi, ki: (0, 0, 0))],
        out_specs=pl.BlockSpec((1, bq, nh * dh), lambda bb, qi, ki: (bb, qi, 0)),
        scratch_shapes=_attn_scratch(nh, bq, bk, dh) + [pltpu.VMEM((bk, bq), F32), pltpu.VMEM((bk, bq), F32)],
        compiler_params=_cparams(("parallel", "parallel", "arbitrary")), name="dsa_attention",
    )(slopes2, qt_hm, k_rm, v, mask, qx)


def _rglru_kernel(u_ref, cw_ref, cb_ref, wr_ref, br_ref, wig_ref, big_ref, lam_ref, buf0_ref, h0_ref,
                  h_ref, conv_ref, hl_ref, ubuf, a_sc, b_sc, hbuf, hcar, *, tt, conv_w, nblk):
    t = pl.program_id(1)
    pad = SUBLANES
    d = u_ref.shape[2]
    blk = d // nblk

    @pl.when(t == 0)
    def _():
        ubuf[0:pad, :] = buf0_ref[0]
        hcar[...] = jnp.broadcast_to(h0_ref[0], (SUBLANES, d))

    ubuf[pad:pad + tt, :] = u_ref[0]
    uc = cb_ref[...] + ubuf[pad:pad + tt, :] * cw_ref[conv_w - 1:conv_w, :]
    for j in range(conv_w - 1):
        off = pad - (conv_w - 1) + j
        uc = uc + ubuf[off:off + tt, :] * cw_ref[j:j + 1, :]

    lam = lam_ref[...]
    neg_sp = -LRU_C * (jnp.maximum(-lam, 0.0) + jnp.log1p(jnp.exp(-jnp.abs(lam))))
    for n in range(nblk):
        cs = slice(n * blk, (n + 1) * blk)
        ucn = uc[:, cs]
        ub = ucn.astype(BF16)
        r = jax.nn.sigmoid(jnp.dot(ub, wr_ref[n], preferred_element_type=F32) + br_ref[:, cs])
        i = jax.nn.sigmoid(jnp.dot(ub, wig_ref[n], preferred_element_type=F32) + big_ref[:, cs])
        log_a = r * neg_sp[:, cs]
        a = jnp.exp(log_a)
        a_sc[:, cs] = a
        b_sc[:, cs] = jnp.sqrt(-jnp.tanh(log_a) * (a * a + 1.0)) * (i * ucn)

    row = lax.broadcasted_iota(jnp.int32, (SUBLANES, d), 0)

    def group(gi, hprev):
        r0 = pl.multiple_of(gi * SUBLANES, SUBLANES)
        av = a_sc[pl.ds(r0, SUBLANES), :]
        bv = b_sc[pl.ds(r0, SUBLANES), :]
        s = 1
        while s < SUBLANES:
            a_sh = pltpu.roll(av, s, axis=0)
            b_sh = pltpu.roll(bv, s, axis=0)
            m = row >= s
            bv = jnp.where(m, av * b_sh + bv, bv)
            av = jnp.where(m, av * a_sh, av)
            s *= 2
        hrows = av * hprev + bv
        hbuf[pl.ds(r0, SUBLANES), :] = hrows
        return jnp.broadcast_to(hrows[SUBLANES - 1:SUBLANES, :], (SUBLANES, d))

    hlast = lax.fori_loop(0, tt // SUBLANES, group, hcar[...])
    h_ref[0] = hbuf[...].astype(h_ref.dtype)
    hcar[...] = hlast
    hl_ref[0] = hlast[0:1, :]
    tail = ubuf[tt:tt + pad, :]
    conv_ref[0] = tail
    ubuf[0:pad, :] = tail


def _rglru(u, conv_w, conv_b, w_rg, b_rg, w_ig, b_ig, lam, buf0, h0, tt):
    b, t, d = u.shape
    cw = conv_w.shape[0]
    nblk, blk = w_rg.shape[0], w_rg.shape[1]
    vec = lambda a: a.reshape(1, d)
    fixed2 = lambda bb, ti: (0, 0)
    fixed3 = lambda bb, ti: (0, 0, 0)
    perb = lambda bb, ti: (bb, 0, 0)
    return pl.pallas_call(
        functools.partial(_rglru_kernel, tt=tt, conv_w=cw, nblk=nblk),
        out_shape=[jax.ShapeDtypeStruct((b, t, d), BF16), jax.ShapeDtypeStruct((b, SUBLANES, d), F32),
                   jax.ShapeDtypeStruct((b, 1, d), F32)],
        grid=(b, t // tt),
        in_specs=[pl.BlockSpec((1, tt, d), lambda bb, ti: (bb, ti, 0)),
                  pl.BlockSpec((cw, d), fixed2), pl.BlockSpec((1, d), fixed2),
                  pl.BlockSpec((nblk, blk, blk), fixed3), pl.BlockSpec((1, d), fixed2),
                  pl.BlockSpec((nblk, blk, blk), fixed3), pl.BlockSpec((1, d), fixed2),
                  pl.BlockSpec((1, d), fixed2),
                  pl.BlockSpec((1, SUBLANES, d), perb), pl.BlockSpec((1, 1, d), perb)],
        out_specs=[pl.BlockSpec((1, tt, d), lambda bb, ti: (bb, ti, 0)),
                   pl.BlockSpec((1, SUBLANES, d), perb), pl.BlockSpec((1, 1, d), perb)],
        scratch_shapes=[pltpu.VMEM((SUBLANES + tt, d), F32), pltpu.VMEM((tt, d), F32),
                        pltpu.VMEM((tt, d), F32), pltpu.VMEM((tt, d), F32), pltpu.VMEM((SUBLANES, d), F32)],
        compiler_params=_cparams(("parallel", "arbitrary")), name="rglru",
    )(u, conv_w, vec(conv_b), w_rg.astype(BF16), vec(b_rg), w_ig.astype(BF16), vec(b_ig), vec(lam), buf0, h0)


def _pack_rows_kernel(c_ref, n_ref, o_ref):
    p, t = c_ref.shape[1], n_ref.shape[1]
    o_ref[0, 0:p, :] = c_ref[0].reshape(p, o_ref.shape[2]).astype(o_ref.dtype)
    o_ref[0, p:p + t, :] = n_ref[0]
    o_ref[0, p + t:, :] = jnp.zeros((o_ref.shape[1] - p - t, o_ref.shape[2]), o_ref.dtype)


def _pack_rows(caches, j, new, s_pad):
    _, b, p, nh, dh = caches.shape
    t, w = new.shape[1], new.shape[2]
    return pl.pallas_call(
        _pack_rows_kernel, out_shape=jax.ShapeDtypeStruct((b, s_pad, w), BF16), grid=(b,),
        in_specs=[pl.BlockSpec((None, 1, p, nh, dh), lambda i: (j, i, 0, 0, 0)),
                  pl.BlockSpec((1, t, w), lambda i: (i, 0, 0))],
        out_specs=pl.BlockSpec((1, s_pad, w), lambda i: (i, 0, 0)),
        compiler_params=_cparams(("parallel",)), name="pack_rows",
    )(caches, new)


def _pad_to(a, axis, n):
    extra = n - a.shape[axis]
    if extra == 0:
        return a
    widths = [(0, 0)] * a.ndim
    widths[axis] = (0, extra)
    return jnp.pad(a, widths)


def _round_up(n, m):
    return (n + m - 1) // m * m


class _Group:
    def __init__(self, x, past):
        self.b, self.t, self.d = x.shape
        self.past = past
        self.flat = past > 0
        self.s_valid = past + self.t
        if self.flat:
            self.tq = _round_up(self.t, LANES)
            self.s_pad = _round_up(self.s_valid, PLANE_ROWS)
            self.fox_bq = self.dsa_bq = self.tq
            self.bk = self.s_pad
        else:
            self.tq = self.s_pad = self.t
            self.fox_bq = _pick(self.t, 512)
            self.dsa_bq = _pick(self.t, 256)
            self.bk = _pick(self.t, 512)

    def proj_view(self, x):
        return x.reshape(1, self.b * self.t, self.d) if self.flat else x

    def proj_bm(self):
        return self.b * self.t if self.flat else _pick(self.t, 512)

    def rows(self, a):
        return a.reshape(self.b, self.t, a.shape[-1])

    def heads_t(self, a):
        if not self.flat:
            return a
        return a.reshape(a.shape[1], a.shape[2], self.b, self.t).transpose(2, 0, 1, 3)

    def pad_q(self, a):
        return _pad_to(a, a.ndim - 1, self.tq)

    def keys(self, new_rm, cache):
        return new_rm if cache is None else _pack_rows(cache[0], cache[1], new_rm, self.s_pad)

    def values(self, new, cache):
        return new if cache is None else self.keys(self.rows(new), cache)


def _qkvg_plan(width, dh, v_rows):
    plan = [(0, 0, width, dh ** -0.5 * LOG2E, [(0, "headT")]),
            (0, width, width, 1.0, [(1, "row"), (3, "stack")]),
            (0, 2 * width, width, 1.0, [(2, "row" if v_rows else "headT"), (4, "stack")]),
            (0, 3 * width, width, 1.0, [(5, "row")])]
    outs = [("headT", width, BF16), ("row", width, BF16), ("row" if v_rows else "headT", width, BF16),
            ("stack", width, F32), ("stack", width, F32), ("row", width, BF16)]
    return plan, outs


def _mixer_a(grp, x, w_main, w_f, b_f, cache, layer, stacked):
    nh = b_f.shape[0]
    da = w_main.shape[1] // 4
    plan, outs = _qkvg_plan(da, da // nh, grp.flat)
    plan = plan + [(1, 0, LANES, 1.0, [(6, "row")])]
    outs = outs + [("row", LANES, F32)]
    qt, k_rm, vt, k, v, g, fl = _inproj(grp.proj_view(x), [w_main, w_f], plan, outs, grp.proj_bm(), layer, stacked)
    qt, k_rm, g = grp.pad_q(grp.heads_t(qt)), grp.rows(k_rm), grp.rows(g)
    z = grp.rows(fl)[:, :, :nh].transpose(0, 2, 1)
    ck_, cv_ = (None, None) if cache is None else cache[:2]
    if cache is not None:
        z = _pad_to(jnp.concatenate([cache[2].astype(F32).transpose(0, 2, 1), z], axis=2), 2, grp.s_pad)
    lf_all, c_all = _logf_cumsum(z, b_f, grp.past, grp.s_valid)
    logf = lf_all[:, :, grp.past:grp.s_valid].transpose(0, 2, 1)
    cq = grp.pad_q(c_all[:, :, grp.past:grp.s_valid])
    o = _fox_attention(qt, grp.keys(k_rm, ck_), grp.values(vt, cv_), cq, c_all,
                       grp.past, grp.fox_bq, grp.bk)[:, :grp.t]
    return o, g, ([k, v], logf)


def _mixer_b(grp, x, w_in, conv_w, conv_b, w_rg, b_rg, w_ig, b_ig, lam, state):
    dr = w_in.shape[1] // 2
    plan = [(0, 0, dr, 1.0, [(0, "row")]), (0, dr, dr, 1.0, [(1, "row")])]
    u, g = _inproj(grp.proj_view(x), [w_in], plan, [("row", dr, F32), ("row", dr, BF16)], grp.proj_bm())
    u, g = grp.rows(u), grp.rows(g)
    cw = conv_w.shape[0]
    if state is None:
        buf0 = jnp.zeros((grp.b, SUBLANES, dr), F32)
        h0 = jnp.zeros((grp.b, 1, dr), F32)
    else:
        buf, h0 = state
        buf0 = jnp.pad(buf.astype(F32), ((0, 0), (SUBLANES - (cw - 1), 0), (0, 0)))
        h0 = h0.astype(F32).reshape(grp.b, 1, dr)
    h, tail, hl = _rglru(u, conv_w, conv_b, w_rg, b_rg, w_ig, b_ig, lam, buf0, h0, _pick(grp.t, 256))
    return h, g, (tail[:, SUBLANES - (cw - 1):], hl[:, 0])


def _mixer_c(grp, x, w_main, w_idx, nh, nh_idx, d_idx, cache, layer, stacked):
    dc = w_main.shape[1] // 4
    wq = nh_idx * d_idx
    plan, outs = _qkvg_plan(dc, dc // nh, grp.flat)
    plan = plan + [(1, 0, wq, 1.0, [(6, "row")]), (1, wq, LANES, 1.0, [(7, "row")])]
    outs = outs + [("row", wq, BF16), ("row", LANES, F32)]
    qt, k_rm, vt, k, v, g, qi, kw = _inproj(grp.proj_view(x), [w_main, w_idx], plan, outs, grp.proj_bm(), layer,
                                            stacked)
    qt, k_rm = grp.pad_q(grp.heads_t(qt)), grp.rows(k_rm)
    g, qi, kw = grp.rows(g), grp.rows(qi), grp.rows(kw)
    ki = kw[:, :, :d_idx]
    wit = grp.pad_q(kw[:, :, d_idx:d_idx + nh_idx].transpose(0, 2, 1))
    qit = grp.pad_q(qi.reshape(grp.b, grp.t, nh_idx, d_idx).transpose(0, 2, 3, 1))
    ck_, cv_ = (None, None) if cache is None else cache[:2]
    ki_all = ki if cache is None else _pad_to(jnp.concatenate([cache[2].astype(F32), ki], axis=1), 1, grp.s_pad)
    topk = min(TOPK_MAX, grp.s_valid // 4)
    mask = _dsa_select(qit, wit, ki_all.astype(BF16), grp.past, grp.s_valid, topk, grp.dsa_bq, grp.bk)
    o = _dsa_attention(qt, grp.keys(k_rm, ck_), grp.values(vt, cv_), mask, grp.past, grp.fox_bq, grp.bk)[:, :grp.t]
    return o, g, ([k, v], ki)


def _run_trunk(x, p, caches, past):
    depth = p["ln_g"].shape[0]
    alpha = (2 * depth) ** 0.25
    grp = _Group(x, past)
    n_a, n_c = (depth + 2) // 3, depth // 3
    kv_a, kv_c = None, None
    new_a, new_b, new_c = [], [], []
    for i in range(depth):
        j, kind = i // 3, i % 3
        if kind == 0:
            cache = None if caches is None else ((caches["a_k"], j), (caches["a_v"], j), caches["a_logf"][j])
            o, g, (kv_a, logf) = _mixer_a(grp, x, p["w_main_a"][j], p["w_f_a"][j], p["b_f_a"][j], cache,
                                          (j, n_a), kv_a)
            new_a.append(logf)
            w_out = p["w_out_a"][j]
        elif kind == 1:
            state = None if caches is None else (caches["b_conv"][j], caches["b_h"][j])
            o, g, st = _mixer_b(grp, x, p["w_in_b"][j], p["conv_w_b"][j], p["conv_b_b"][j], p["w_rg_b"][j],
                                p["b_rg_b"][j], p["w_ig_b"][j], p["b_ig_b"][j], p["lam_b"][j], state)
            new_b.append(st)
            w_out = p["w_out_b"][j]
        else:
            cache = None if caches is None else ((caches["c_k"], j), (caches["c_v"], j), caches["c_kidx"][j])
            o, g, (kv_c, ki) = _mixer_c(grp, x, p["w_main_c"][j], p["w_idx_c"][j], p["h_c"], p["h_idx"], p["d_idx"],
                                        cache, (j, n_c), kv_c)
            new_c.append(ki)
            w_out = p["w_out_c"][j]
        m = grp.b * grp.t
        x = _outproj_ln(o.reshape(m, -1), g.reshape(m, -1), x.reshape(m, grp.d), w_out,
                        p["ln_g"][i], p["ln_b"][i], alpha).reshape(grp.b, grp.t, grp.d)
    stack = lambda sts, n: jnp.stack([s[n] for s in sts])
    heads = lambda a, nh: a.reshape(a.shape[0], grp.b, grp.t, nh, a.shape[-1] * a.shape[-2] // nh)
    h_a = p["b_f_a"].shape[1]
    return (x, heads(kv_a[0], h_a), heads(kv_a[1], h_a), jnp.stack(new_a), stack(new_b, 0), stack(new_b, 1),
            heads(kv_c[0], p["h_c"]), heads(kv_c[1], p["h_c"]), jnp.stack(new_c))


def kernel(x_prompt, x_sample, cache_a_k, cache_a_v, cache_a_logf, state_b_conv, state_b_h, cache_c_k, cache_c_v, cache_c_kidx, w_in_a, b_f_a, w_out_a, w_in_b, conv_w_b, conv_b_b, w_rg_b, b_rg_b, w_ig_b, b_ig_b, lam_b, w_out_b, w_in_c, w_out_c, ln_g, ln_b):
    h_a = b_f_a.shape[1]
    d_a = w_out_a.shape[1]
    d_c = w_out_c.shape[1]
    h_c = cache_c_k.shape[3]
    d_idx = cache_c_kidx.shape[-1]
    h_idx = (w_in_c.shape[2] - 4 * d_c - d_idx) // (d_idx + 1)
    assert d_a // h_a == LANES and d_c // h_c == LANES, "head width must equal the lane count"
    assert w_in_a.shape[2] == 4 * d_a + h_a and d_idx + h_idx <= LANES
    past = cache_a_k.shape[2]
    assert past % CHUNK == 0 and past > 0

    w_idx = w_in_c[:, :, 4 * d_c:]
    w_idx = _pad_to(w_idx, 2, h_idx * d_idx + LANES)
    p = {"w_main_a": w_in_a[:, :, :4 * d_a].astype(BF16),
         "w_f_a": _pad_to(w_in_a[:, :, 4 * d_a:], 2, LANES).astype(BF16),
         "b_f_a": b_f_a, "w_out_a": w_out_a.astype(BF16),
         "w_in_b": w_in_b.astype(BF16), "conv_w_b": conv_w_b, "conv_b_b": conv_b_b, "w_rg_b": w_rg_b,
         "b_rg_b": b_rg_b, "w_ig_b": w_ig_b, "b_ig_b": b_ig_b, "lam_b": lam_b, "w_out_b": w_out_b.astype(BF16),
         "w_main_c": w_in_c[:, :, :4 * d_c].astype(BF16), "w_idx_c": w_idx.astype(BF16),
         "w_out_c": w_out_c.astype(BF16), "ln_g": ln_g, "ln_b": ln_b,
         "h_c": h_c, "h_idx": h_idx, "d_idx": d_idx}
    caches = {"a_k": cache_a_k, "a_v": cache_a_v, "a_logf": cache_a_logf, "b_conv": state_b_conv,
              "b_h": state_b_h, "c_k": cache_c_k, "c_v": cache_c_v, "c_kidx": cache_c_kidx}
    outs_p = _run_trunk(x_prompt, p, None, 0)
    outs_s = _run_trunk(x_sample, p, caches, past)
    return (outs_p[0], outs_s[0]) + outs_p[1:] + outs_s[1:]
```

```python
import functools
import math

import jax
import jax.numpy as jnp
import numpy as np
from jax import lax
from jax.experimental import pallas as pl
from jax.experimental.pallas import tpu as pltpu

NEG_INF = -1e30
LN_EPS = 1e-5
CHUNK = 64
CHUNK_SHIFT = 6
TOPK_MAX = 256
LRU_C = 8.0
LANES = 128
SUBLANES = 8
BF16_ROWS = 16
PLANE_ROWS = 32 * SUBLANES
INT_MIN = -(2 ** 31)
VMEM_LIMIT = 56 * 1024 * 1024
LOG2E = math.log2(math.e)
SKIP_BITS = 60.0

F32 = jnp.float32
BF16 = jnp.bfloat16


def _cparams(sem, flags=None):
    return pltpu.CompilerParams(dimension_semantics=sem, vmem_limit_bytes=VMEM_LIMIT, flags=flags)


def _pick(n, pref):
    if n <= pref:
        return n
    b = pref
    while n % b:
        b //= 2
    return b


def _inproj_kernel(x_ref, *refs, n_w, n_alias, layer, plan):
    w_refs, out_refs = refs[:n_w], refs[n_w + n_alias:]
    xb = x_ref[0].astype(BF16)
    for w_idx, c0, width, scale, outs in plan:
        r = jnp.dot(xb, w_refs[w_idx][:, c0:c0 + width], preferred_element_type=F32)
        if scale != 1.0:
            r = r * scale
        for o_idx, kind in outs:
            o = out_refs[o_idx]
            if kind == "row":
                o[0] = r.astype(o.dtype)
            elif kind == "stack":
                val = r.astype(o.dtype).reshape(o.shape[2:])
                if o.shape[0] == 1:
                    o[0, 0] = val
                else:
                    for l in range(o.shape[0]):
                        o[l, 0] = val if l == layer else jnp.zeros_like(val)
            else:
                for h in range(width // LANES):
                    o[0, h] = r[:, h * LANES:(h + 1) * LANES].T.astype(o.dtype)


def _inproj(x3, ws, plan, out_defs, bm, layer=(0, 1), stacked=None):
    bx, tx, d = x3.shape
    j, n_layers = layer
    grid = (bx, tx // bm)
    in_specs = [pl.BlockSpec((1, bm, d), lambda b, i: (b, i, 0))]
    for w in ws:
        in_specs.append(pl.BlockSpec(w.shape, lambda b, i: (0, 0)))
    stacked = list(stacked or [])
    in_specs += [pl.BlockSpec(memory_space=pl.ANY)] * len(stacked)
    out_shape, out_specs, aliases = [], [], {}
    for o_idx, (kind, width, dt) in enumerate(out_defs):
        if kind == "row":
            out_shape.append(jax.ShapeDtypeStruct((bx, tx, width), dt))
            out_specs.append(pl.BlockSpec((1, bm, width), lambda b, i: (b, i, 0)))
        elif kind == "stack":
            out_shape.append(jax.ShapeDtypeStruct((n_layers, bx, tx, width // LANES, LANES), dt))
            if stacked:
                out_specs.append(pl.BlockSpec((1, 1, bm, width // LANES, LANES), lambda b, i: (j, b, i, 0, 0)))
                aliases[1 + len(ws) + len(aliases)] = o_idx
            else:
                out_specs.append(pl.BlockSpec((n_layers, 1, bm, width // LANES, LANES), lambda b, i: (0, b, i, 0, 0)))
        else:
            nh = width // LANES
            out_shape.append(jax.ShapeDtypeStruct((bx, nh, LANES, tx), dt))
            out_specs.append(pl.BlockSpec((1, nh, LANES, bm), lambda b, i: (b, 0, 0, i)))
    assert len(aliases) == len(stacked)
    return pl.pallas_call(
        functools.partial(_inproj_kernel, n_w=len(ws), n_alias=len(stacked), layer=j, plan=tuple(plan)),
        out_shape=out_shape, grid=grid, in_specs=in_specs, out_specs=out_specs, input_output_aliases=aliases,
        compiler_params=_cparams(("parallel", "parallel")), name="inproj",
    )(x3, *ws, *stacked)


def _outproj_ln_kernel(o_ref, g_ref, x_ref, w_ref, lg_ref, lb_ref, y_ref, *, alpha):
    g = g_ref[...].astype(F32)
    og = (o_ref[...].astype(F32) * (g * jax.nn.sigmoid(g))).astype(BF16)
    y = jnp.dot(og, w_ref[...], preferred_element_type=F32)
    z = alpha * x_ref[...] + y
    mu = jnp.mean(z, axis=-1, keepdims=True)
    zc = z - mu
    var = jnp.mean(zc * zc, axis=-1, keepdims=True)
    y_ref[...] = zc * lax.rsqrt(var + LN_EPS) * lg_ref[...] + lb_ref[...]


def _outproj_ln(o2, g2, x2, w, ln_g, ln_b, alpha):
    m, d = x2.shape
    dk = o2.shape[1]
    bm = _pick(m, 512)
    row = lambda i: (i, 0)
    fixed = lambda i: (0, 0)
    return pl.pallas_call(
        functools.partial(_outproj_ln_kernel, alpha=alpha),
        out_shape=jax.ShapeDtypeStruct((m, d), F32), grid=(m // bm,),
        in_specs=[pl.BlockSpec((bm, dk), row), pl.BlockSpec((bm, dk), row), pl.BlockSpec((bm, d), row),
                  pl.BlockSpec((dk, d), fixed), pl.BlockSpec((1, d), fixed), pl.BlockSpec((1, d), fixed)],
        out_specs=pl.BlockSpec((bm, d), row),
        compiler_params=_cparams(("parallel",)), name="outproj_ln",
    )(o2, g2, x2, w, ln_g.reshape(1, d), ln_b.reshape(1, d))


def _log_sigmoid(x):
    return -(jnp.maximum(-x, 0.0) + jnp.log1p(jnp.exp(-jnp.abs(x))))


def _logf_cumsum_kernel(z_ref, bf_ref, lf_ref, c_ref, *, p0, p1):
    z = z_ref[0]
    pos = lax.broadcasted_iota(jnp.int32, z.shape, 1)
    is_new = jnp.logical_and(pos >= p0, pos < p1)
    lf = jnp.where(is_new, _log_sigmoid(z + bf_ref[...]), z)
    lf_ref[0] = lf
    c = lf
    s = 1
    while s < z.shape[1]:
        c = c + jnp.where(pos >= s, pltpu.roll(c, s, axis=1), 0.0)
        s *= 2
    c_ref[0] = c


def _logf_cumsum(z, b_f, p0, p1):
    b, h, l = z.shape
    blk = pl.BlockSpec((1, h, l), lambda i: (i, 0, 0))
    return pl.pallas_call(
        functools.partial(_logf_cumsum_kernel, p0=p0, p1=p1),
        out_shape=[jax.ShapeDtypeStruct(z.shape, F32)] * 2, grid=(b,),
        in_specs=[blk, pl.BlockSpec((h, 1), lambda i: (0, 0))], out_specs=[blk, blk],
        compiler_params=_cparams(("parallel",)), name="logf_cumsum",
    )(z, b_f.reshape(h, 1))


def _score_store(u, slot, s_sc, mc_sc):
    s_sc[slot] = u
    mc_sc[slot] = jnp.broadcast_to(jnp.max(u, axis=0, keepdims=True), mc_sc.shape[1:])


def _softmax_update(h, slot, vt, s_sc, mc_sc, m_sc, acc_sc, row_shift=None):
    u = s_sc[slot]
    m_prev = m_sc[h]
    m_cur = mc_sc[slot]
    if row_shift is not None:
        m_cur = m_cur + row_shift
    m_new = jnp.maximum(m_prev, m_cur)
    m_row = m_new[0:1]
    p = jnp.exp2(u - (m_row if row_shift is None else m_row - row_shift))
    alpha = jnp.exp2(m_prev - m_new)
    vt1 = jnp.concatenate([vt, jnp.ones((BF16_ROWS, vt.shape[1]), BF16)], axis=0)
    acc_sc[h] = acc_sc[h] * alpha[0:1] + jnp.dot(vt1, p.astype(BF16), preferred_element_type=F32)
    m_sc[h] = m_new


def _head(k_ref, h):
    dh = LANES
    return k_ref[0, :, h * dh:(h + 1) * dh]


def _head_t(v_ref, h, v_rows):
    if not v_rows:
        return v_ref[0, h]
    return _head(v_ref, h).astype(F32).T.astype(BF16)


def _scores(k, kx, qt, qx):
    return jnp.dot(jnp.concatenate([k, kx], axis=1), jnp.concatenate([qt, qx], axis=0),
                   preferred_element_type=F32)


def _pipelined_heads(nh, qk_phase, sm_phase):
    qk_phase(0, 0)
    for h in range(nh - 1):
        qk_phase(h + 1, (h + 1) % 2)
        sm_phase(h, h % 2)
    sm_phase(nh - 1, (nh - 1) % 2)


def _attn_init(m_sc, acc_sc):
    m_sc[...] = jnp.full(m_sc.shape, -jnp.inf, F32)
    acc_sc[...] = jnp.zeros(acc_sc.shape, F32)


def _attn_finish(o_ref, acc_sc, nh, dh):
    for h in range(nh):
        a = acc_sc[h]
        o_ref[0, :, h * dh:(h + 1) * dh] = (a[0:dh] / a[dh:dh + 1]).T.astype(o_ref.dtype)


def _attn_scratch(nh, bq, bk, dh):
    return [pltpu.VMEM((nh, SUBLANES, bq), F32), pltpu.VMEM((nh, dh + BF16_ROWS, bq), F32),
            pltpu.VMEM((2, bk, bq), F32), pltpu.VMEM((2, SUBLANES, bq), F32)]


def _fox_kernel(lo_ref, qt_ref, k_ref, v_ref, cq_ref, kx_ref, qx_ref, o_ref, m_sc, acc_sc, s_sc, mc_sc,
                *, bq, bk, q_off, nh, v_rows):
    qi, ki = pl.program_id(1), pl.program_id(2)
    q_first = q_off + qi * bq
    q_last = q_first + bq - 1
    k_first = ki * bk
    k_last = k_first + bk - 1

    @pl.when(ki == 0)
    def _():
        _attn_init(m_sc, acc_sc)

    def step(masked):
        kx = kx_ref[0]
        if masked:
            kpos = k_first + lax.broadcasted_iota(jnp.int32, (bk, bq), 0)
            qpos = q_first + lax.broadcasted_iota(jnp.int32, (bk, bq), 1)
            causal = kpos <= qpos

        def qk_phase(h, slot):
            u = _scores(_head(k_ref, h), kx, qt_ref[0, h], qx_ref[h])
            if masked:
                u = jnp.where(causal, u, NEG_INF)
            _score_store(u, slot, s_sc, mc_sc)

        def sm_phase(h, slot):
            cq2 = cq_ref[0, pl.ds(h, 1), :] * LOG2E
            _softmax_update(h, slot, _head_t(v_ref, h, v_rows), s_sc, mc_sc, m_sc, acc_sc, row_shift=cq2)

        _pipelined_heads(nh, qk_phase, sm_phase)

    needed = jnp.logical_and(k_first <= q_last, ki >= lo_ref[pl.program_id(0), qi])
    straddles = k_last > q_first

    @pl.when(jnp.logical_and(needed, straddles))
    def _():
        step(True)

    @pl.when(jnp.logical_and(needed, jnp.logical_not(straddles)))
    def _():
        step(False)

    @pl.when(ki == pl.num_programs(2) - 1)
    def _():
        _attn_finish(o_ref, acc_sc, nh, LANES)


def _split3(x):
    def top_half(v):
        bits = lax.bitcast_convert_type(v, jnp.uint32) & jnp.uint32(0xFFFF0000)
        return lax.bitcast_convert_type(bits, F32)

    hi = top_half(x)
    r1 = x - hi
    mid = top_half(r1)
    return hi.astype(BF16), mid.astype(BF16), (r1 - mid).astype(BF16)


def _v_spec(v, nh, dh, bk, kmap):
    if v.ndim == 3:
        return True, pl.BlockSpec((1, bk, nh * dh), lambda *ix: (ix[0], kmap(*ix), 0))
    return False, pl.BlockSpec((1, nh, dh, bk), lambda *ix: (ix[0], 0, 0, kmap(*ix)))


def _fox_first_block(qt_hm, k_rm, cq, ck, q_off, bq, bk):
    b, nh, dh, tq = qt_hm.shape
    s = k_rm.shape[1]
    nq, nk = tq // bq, s // bk
    qn = jnp.sqrt(jnp.sum(jnp.square(qt_hm.astype(F32)), axis=2)).reshape(b, nh, nq, bq).max(axis=-1)
    kn = jnp.sqrt(jnp.sum(jnp.square(k_rm.astype(F32).reshape(b, nk, bk, nh, dh)), axis=-1)).max(axis=2)
    kn = kn.transpose(0, 2, 1)
    first_self = [(q_off + qi * bq) // bk for qi in range(nq)]
    last_self = [min((q_off + (qi + 1) * bq - 1) // bk, nk - 1) for qi in range(nq)]
    kn_self = jnp.stack([kn[:, :, f:l + 1].max(axis=-1) for f, l in zip(first_self, last_self)], axis=-1)
    c_first = cq[:, :, ::bq]
    c_last = ck[:, :, bk - 1::bk]
    bound = (qn[..., :, None] * (kn[..., None, :] + kn_self[..., :, None])
             + (c_first[..., :, None] - c_last[..., None, :]) * LOG2E)
    skip = jnp.all(bound < -SKIP_BITS, axis=1)
    lo = jnp.sum(jnp.cumprod(skip.astype(jnp.int32), axis=-1), axis=-1)
    return jnp.minimum(lo, jnp.asarray(first_self, jnp.int32)[None, :]).astype(jnp.int32)


def _fox_attention(qt_hm, k_rm, v, cq, ck, q_off, bq, bk):
    b, nh, dh, tq = qt_hm.shape
    s = k_rm.shape[1]
    nq, nk = tq // bq, s // bk
    w = LANES // nh
    assert w >= 3
    kx = jnp.stack(_split3(ck * (-LOG2E)), axis=-1)
    kx = _pad_to(kx, 3, w).transpose(0, 2, 1, 3).reshape(b, s, LANES)
    rows = jnp.arange(LANES)[None, :, None]
    heads = jnp.arange(nh)[:, None, None]
    qx = jnp.broadcast_to(jnp.logical_and(rows >= heads * w, rows < heads * w + 3), (nh, LANES, bq)).astype(BF16)

    lo = _fox_first_block(qt_hm, k_rm, cq, ck, q_off, bq, bk)

    def kmap(bb, qi, ki, lo_ref):
        return jnp.clip(ki, lo_ref[bb, qi], (q_off + (qi + 1) * bq - 1) // bk)

    v_rows, v_spec = _v_spec(v, nh, dh, bk, kmap)
    grid_spec = pltpu.PrefetchScalarGridSpec(
        num_scalar_prefetch=1, grid=(b, nq, nk),
        in_specs=[pl.BlockSpec((1, nh, dh, bq), lambda bb, qi, ki, lo_ref: (bb, 0, 0, qi)),
                  pl.BlockSpec((1, bk, nh * dh), lambda *ix: (ix[0], kmap(*ix), 0)),
                  v_spec,
                  pl.BlockSpec((1, nh, bq), lambda bb, qi, ki, lo_ref: (bb, 0, qi)),
                  pl.BlockSpec((1, bk, LANES), lambda *ix: (ix[0], kmap(*ix), 0)),
                  pl.BlockSpec((nh, LANES, bq), lambda bb, qi, ki, lo_ref: (0, 0, 0))],
        out_specs=pl.BlockSpec((1, bq, nh * dh), lambda bb, qi, ki, lo_ref: (bb, qi, 0)),
        scratch_shapes=_attn_scratch(nh, bq, bk, dh))
    return pl.pallas_call(
        functools.partial(_fox_kernel, bq=bq, bk=bk, q_off=q_off, nh=nh, v_rows=v_rows),
        out_shape=jax.ShapeDtypeStruct((b, tq, nh * dh), BF16), grid_spec=grid_spec,
        compiler_params=_cparams(("parallel", "parallel", "arbitrary")), name="fox_attention",
    )(lo, qt_hm, k_rm, v, cq, kx, qx)


def _bit_planes(words):
    a = list(words)
    j, m = 16, 0x0000FFFF
    while j:
        k = 0
        while k < 32:
            t = (a[k] ^ lax.shift_right_logical(a[k + j], jnp.int32(j))) & jnp.int32(m)
            a[k] = a[k] ^ t
            a[k + j] = a[k + j] ^ lax.shift_left(t, jnp.int32(j))
            k = (k + j + 1) & ~j
        j >>= 1
        m = (m ^ (m << j)) & 0xFFFFFFFF
    return a


def _dsa_select_kernel(qit_ref, wit_ref, ki_ref, out_ref, key_sc, pl_sc, act_sc, j_sc,
                       *, bq, kb, q_off, s_valid, topk, nh_idx, idx_bits):
    qb = pl.program_id(1)
    q_first = q_off + qb * bq
    q_last = q_first + bq - 1
    n_adm = jnp.minimum(((q_last >> CHUNK_SHIFT) + 1) * CHUNK, s_valid)
    nkb = (jnp.maximum(n_adm, topk) + kb - 1) // kb
    nkb_total = key_sc.shape[0] // kb
    gpb = kb // PLANE_ROWS
    nkb4 = nkb // 4
    qchunk = (q_first + lax.broadcasted_iota(jnp.int32, (kb, bq), 1)) >> CHUNK_SHIFT
    row = lax.broadcasted_iota(jnp.int32, (kb, bq), 0)

    def admissible(kpos):
        return jnp.logical_and((kpos >> CHUNK_SHIFT) <= qchunk, kpos < s_valid)

    def rows(kblk):
        return pl.ds(pl.multiple_of(kblk * kb, kb), kb)

    def score_block(kblk, carry, all_admissible):
        kblock = ki_ref[0, rows(kblk), :]
        sc = jnp.zeros((kb, bq), F32)
        for h in range(nh_idx):
            d = jnp.dot(kblock, qit_ref[0, h], preferred_element_type=F32)
            sc = sc + wit_ref[0, h:h + 1, :] * jnp.maximum(d, 0.0)
        sc = sc + 0.0
        if not all_admissible:
            sc = jnp.where(admissible(kblk * kb + row), sc, NEG_INF)
        bits = lax.bitcast_convert_type(sc, jnp.int32)
        key = bits ^ ((bits >> 31) & 0x7FFFFFFF)
        key_sc[rows(kblk), :] = key
        for g in range(gpb):
            base = g * PLANE_ROWS
            planes = _bit_planes([key[base + SUBLANES * j:base + SUBLANES * (j + 1), :] for j in range(32)])
            planes[0] = ~planes[0]
            for i in range(32):
                pl_sc[kblk * gpb + g, i] = planes[i]
            act_sc[kblk * gpb + g] = jnp.full((SUBLANES, bq), -1, jnp.int32)
        return carry

    n_full = jnp.minimum(((q_first >> CHUNK_SHIFT) + 1) * CHUNK, s_valid) // kb
    lax.fori_loop(0, n_full, functools.partial(score_block, all_admissible=True), 0)
    lax.fori_loop(n_full, nkb, functools.partial(score_block, all_admissible=False), 0)

    def bit_step(it, carry):
        thr_u, above = carry

        def cnt_groups(g0, n, acc):
            for g in range(n):
                acc = acc + lax.population_count(act_sc[g0 + g] & pl_sc[g0 + g, it])
            return acc

        acc = lax.fori_loop(0, nkb4, lambda i, a: cnt_groups(i * gpb * 4, gpb * 4, a),
                            jnp.zeros((SUBLANES, bq), jnp.int32))
        acc = lax.fori_loop(nkb4 * 4, nkb, lambda i, a: cnt_groups(i * gpb, gpb, a), acc)
        c = jnp.sum(acc.astype(F32), axis=0, keepdims=True)
        take = (above + c) >= topk

        def upd_groups(g0, n, carry2):
            for g in range(n):
                a = act_sc[g0 + g]
                x = a & pl_sc[g0 + g, it]
                act_sc[g0 + g] = jnp.where(take, x, a ^ x)
            return carry2

        lax.fori_loop(0, nkb4, lambda i, c2: upd_groups(i * gpb * 4, gpb * 4, c2), 0)
        lax.fori_loop(nkb4 * 4, nkb, lambda i, c2: upd_groups(i * gpb, gpb, c2), 0)
        bit = lax.shift_left(jnp.int32(1), 31 - it)
        return thr_u | jnp.where(take, bit, 0), above + jnp.where(take, 0.0, c)

    thr_u, n_gt = lax.fori_loop(0, 32, bit_step, (jnp.zeros((1, bq), jnp.int32), jnp.zeros((1, bq), F32)))
    thr = thr_u ^ INT_MIN

    def eq_blk(kblk, acc):
        for g in range(gpb):
            acc = acc + lax.population_count(act_sc[kblk * gpb + g])
        return acc

    n_eq = jnp.sum(lax.fori_loop(0, nkb, eq_blk, jnp.zeros((SUBLANES, bq), jnp.int32)).astype(F32),
                   axis=0, keepdims=True)
    need = topk - n_gt
    has_excess = jnp.max(n_eq - need) > 0.0
    j_sc[...] = jnp.full(j_sc.shape, 2 ** 30, jnp.int32)

    @pl.when(has_excess)
    def _():
        def count_ties_below(cand):
            def blk(kblk, acc):
                hit = jnp.logical_and(key_sc[rows(kblk), :] == thr, kblk * kb + row < cand)
                ind = jnp.where(hit, 1.0, 0.0)
                for r in range(kb // SUBLANES):
                    acc = acc + ind[r * SUBLANES:(r + 1) * SUBLANES]
                return acc
            return jnp.sum(lax.fori_loop(0, nkb, blk, jnp.zeros((SUBLANES, bq), F32)), axis=0, keepdims=True)

        def idx_step(it, lo):
            cand = lo | lax.shift_left(jnp.int32(1), idx_bits - 1 - it)
            return jnp.where(count_ties_below(cand) < need, cand, lo)

        lo = lax.fori_loop(0, idx_bits, idx_step, jnp.zeros((1, bq), jnp.int32))
        j_sc[...] = jnp.broadcast_to(lo, j_sc.shape)

    j_cut = j_sc[0:1, :]

    def write_fast(kblk, carry):
        out_ref[0, 0, rows(kblk), :] = jnp.where(key_sc[rows(kblk), :] >= thr, 0.0, NEG_INF).astype(out_ref.dtype)
        return carry

    def write_block(kblk, carry):
        key = key_sc[rows(kblk), :]
        kpos = kblk * kb + row
        tie = jnp.logical_and(key == thr, kpos <= j_cut)
        sel = jnp.logical_and(jnp.logical_or(key > thr, tie), admissible(kpos))
        out_ref[0, 0, rows(kblk), :] = jnp.where(sel, 0.0, NEG_INF).astype(out_ref.dtype)
        return carry

    def fill_block(kblk, carry):
        out_ref[0, 0, rows(kblk), :] = jnp.full((kb, bq), NEG_INF, out_ref.dtype)
        return carry

    n_fast = jnp.where(has_excess, 0, n_full)
    lax.fori_loop(0, n_fast, write_fast, 0)
    lax.fori_loop(n_fast, nkb, write_block, 0)
    lax.fori_loop(nkb, nkb_total, fill_block, 0)


def _dsa_select(qit_hm, wit, ki, q_off, s_valid, topk, bq, kb):
    b, nh_idx, d_idx, tq = qit_hm.shape
    s_pad = ki.shape[1]
    return pl.pallas_call(
        functools.partial(_dsa_select_kernel, bq=bq, kb=kb, q_off=q_off, s_valid=s_valid, topk=topk,
                          nh_idx=nh_idx, idx_bits=max(1, (s_pad - 1).bit_length())),
        out_shape=jax.ShapeDtypeStruct((b, tq // bq, s_pad, bq), BF16), grid=(b, tq // bq),
        in_specs=[pl.BlockSpec((1, nh_idx, d_idx, bq), lambda bb, qb: (bb, 0, 0, qb)),
                  pl.BlockSpec((1, nh_idx, bq), lambda bb, qb: (bb, 0, qb)),
                  pl.BlockSpec((1, s_pad, d_idx), lambda bb, qb: (bb, 0, 0))],
        out_specs=pl.BlockSpec((1, 1, s_pad, bq), lambda bb, qb: (bb, qb, 0, 0)),
        scratch_shapes=[pltpu.VMEM((s_pad, bq), jnp.int32),
                        pltpu.VMEM((s_pad // PLANE_ROWS, 32, SUBLANES, bq), jnp.int32),
                        pltpu.VMEM((s_pad // PLANE_ROWS, SUBLANES, bq), jnp.int32),
                        pltpu.VMEM((SUBLANES, bq), jnp.int32)],
        compiler_params=_cparams(("parallel", "arbitrary")), name="dsa_select",
    )(qit_hm, wit, ki)


def _dsa_attn_kernel(sl_ref, qt_ref, k_ref, v_ref, mb_ref, qx_ref, o_ref, m_sc, acc_sc, s_sc, mc_sc,
                     base_sc, t_sc, *, bq, bk, q_off, nh, v_rows):
    qi, ki = pl.program_id(1), pl.program_id(2)
    q_first = q_off + qi * bq
    q_last = q_first + bq - 1
    k_first = ki * bk
    k_last = k_first + bk - 1
    qposf = (q_first + lax.broadcasted_iota(jnp.int32, (1, bq), 1)).astype(F32)

    @pl.when(ki == 0)
    def _():
        _attn_init(m_sc, acc_sc)

    def step(past):
        base_sc[...] = jnp.concatenate([mb_ref[0, j] for j in range(mb_ref.shape[1])], axis=1).astype(F32)
        if past:
            kpos = k_first + lax.broadcasted_iota(jnp.int32, (bk, LANES), 0)
            lane = lax.broadcasted_iota(jnp.int32, (bk, LANES), 1)
            kx = jnp.where(lane < 3, kpos >> CHUNK_SHIFT, jnp.where(lane < 6, kpos & (CHUNK - 1), 0)).astype(BF16)
        else:
            kpos = k_first + lax.broadcasted_iota(jnp.int32, (bk, bq), 0)
            qpos = q_first + lax.broadcasted_iota(jnp.int32, (bk, bq), 1)
            t_sc[...] = jnp.minimum(kpos, 2 * qpos - kpos).astype(F32)

        def qk_phase(h, slot):
            if past:
                u = _scores(_head(k_ref, h), kx, qt_ref[0, h], qx_ref[h]) + base_sc[...]
            else:
                st = jnp.dot(_head(k_ref, h), qt_ref[0, h], preferred_element_type=F32)
                u = st + sl_ref[h] * t_sc[...] + base_sc[...]
            _score_store(u, slot, s_sc, mc_sc)

        def sm_phase(h, slot):
            _softmax_update(h, slot, _head_t(v_ref, h, v_rows), s_sc, mc_sc, m_sc, acc_sc,
                            row_shift=-sl_ref[h] * qposf)

        _pipelined_heads(nh, qk_phase, sm_phase)

    needed = k_first <= ((q_last >> CHUNK_SHIFT) << CHUNK_SHIFT) + CHUNK - 1
    past = k_last <= q_first

    @pl.when(past)
    def _():
        step(True)

    @pl.when(jnp.logical_and(needed, jnp.logical_not(past)))
    def _():
        step(False)

    @pl.when(ki == pl.num_programs(2) - 1)
    def _():
        _attn_finish(o_ref, acc_sc, nh, LANES)


def _dsa_attention(qt_hm, k_rm, v, mask, q_off, bq, bk):
    b, nh, dh, tq = qt_hm.shape
    s, mq = mask.shape[2], mask.shape[3]
    nq, nk, nsub = tq // bq, s // bk, bq // mq
    slopes2_np = (2.0 ** (-8.0 * np.arange(1, nh + 1) / nh) * LOG2E).astype(np.float32)
    slopes2 = jnp.asarray(slopes2_np)
    rest, pieces = slopes2_np, []
    for _ in range(3):
        piece = rest.astype(BF16).astype(np.float32)
        pieces.append(piece)
        rest = rest - piece
    col = np.zeros((nh, LANES), np.float32)
    col[:, 0:3] = np.stack(pieces, axis=1) * CHUNK
    col[:, 3:6] = np.stack(pieces, axis=1)
    qx = jnp.broadcast_to(jnp.asarray(col.astype(BF16))[:, :, None], (nh, LANES, bq))

    def kmap(bb, qi, ki):
        q_last = q_off + (qi + 1) * bq - 1
        return jnp.minimum(ki, (((q_last >> CHUNK_SHIFT) << CHUNK_SHIFT) + CHUNK - 1) // bk)

    v_rows, v_spec = _v_spec(v, nh, dh, bk, kmap)
    return pl.pallas_call(
        functools.partial(_dsa_attn_kernel, bq=bq, bk=bk, q_off=q_off, nh=nh, v_rows=v_rows),
        out_shape=jax.ShapeDtypeStruct((b, tq, nh * dh), BF16), grid=(b, nq, nk),
        in_specs=[pl.BlockSpec(memory_space=pltpu.SMEM),
                  pl.BlockSpec((1, nh, dh, bq), lambda bb, qi, ki: (bb, 0, 0, qi)),
                  pl.BlockSpec((1, bk, nh * dh), lambda bb, qi, ki: (bb, kmap(bb, qi, ki), 0)),
                  v_spec,
                  pl.BlockSpec((1, nsub, bk, mq), lambda bb, qi, ki: (bb, qi, kmap(bb, qi, ki), 0)),
                  pl.BlockSpec((nh, LANES, bq), lambda bb, qi, ki: (0, 0, 0))],
        out_specs=pl.BlockSpec((1, bq, nh * dh), lambda bb, qi, ki: (bb, qi, 0)),
        scratch_shapes=_attn_scratch(nh, bq, bk, dh) + [pltpu.VMEM((bk, bq), F32), pltpu.VMEM((bk, bq), F32)],
        compiler_params=_cparams(("parallel", "parallel", "arbitrary")), name="dsa_attention",
    )(slopes2, qt_hm, k_rm, v, mask, qx)


def _rglru_kernel(u_ref, cw_ref, cb_ref, wr_ref, br_ref, wig_ref, big_ref, lam_ref, buf0_ref, h0_ref,
                  h_ref, conv_ref, hl_ref, ubuf, a_sc, b_sc, hbuf, hcar, *, tt, conv_w, nblk):
    t = pl.program_id(1)
    pad = SUBLANES
    d = u_ref.shape[2]
    blk = d // nblk

    @pl.when(t == 0)
    def _():
        ubuf[0:pad, :] = buf0_ref[0]
        hcar[...] = jnp.broadcast_to(h0_ref[0], (SUBLANES, d))

    ubuf[pad:pad + tt, :] = u_ref[0]
    uc = cb_ref[...] + ubuf[pad:pad + tt, :] * cw_ref[conv_w - 1:conv_w, :]
    for j in range(conv_w - 1):
        off = pad - (conv_w - 1) + j
        uc = uc + ubuf[off:off + tt, :] * cw_ref[j:j + 1, :]

    lam = lam_ref[...]
    neg_sp = -LRU_C * (jnp.maximum(-lam, 0.0) + jnp.log1p(jnp.exp(-jnp.abs(lam))))
    for n in range(nblk):
        cs = slice(n * blk, (n + 1) * blk)
        ucn = uc[:, cs]
        ub = ucn.astype(BF16)
        r = jax.nn.sigmoid(jnp.dot(ub, wr_ref[n], preferred_element_type=F32) + br_ref[:, cs])
        i = jax.nn.sigmoid(jnp.dot(ub, wig_ref[n], preferred_element_type=F32) + big_ref[:, cs])
        log_a = r * neg_sp[:, cs]
        a = jnp.exp(log_a)
        a_sc[:, cs] = a
        b_sc[:, cs] = jnp.sqrt(-jnp.tanh(log_a) * (a * a + 1.0)) * (i * ucn)

    row = lax.broadcasted_iota(jnp.int32, (SUBLANES, d), 0)

    def group(gi, hprev):
        r0 = pl.multiple_of(gi * SUBLANES, SUBLANES)
        av = a_sc[pl.ds(r0, SUBLANES), :]
        bv = b_sc[pl.ds(r0, SUBLANES), :]
        s = 1
        while s < SUBLANES:
            a_sh = pltpu.roll(av, s, axis=0)
            b_sh = pltpu.roll(bv, s, axis=0)
            m = row >= s
            bv = jnp.where(m, av * b_sh + bv, bv)
            av = jnp.where(m, av * a_sh, av)
            s *= 2
        hrows = av * hprev + bv
        hbuf[pl.ds(r0, SUBLANES), :] = hrows
        return jnp.broadcast_to(hrows[SUBLANES - 1:SUBLANES, :], (SUBLANES, d))

    hlast = lax.fori_loop(0, tt // SUBLANES, group, hcar[...])
    h_ref[0] = hbuf[...].astype(h_ref.dtype)
    hcar[...] = hlast
    hl_ref[0] = hlast[0:1, :]
    tail = ubuf[tt:tt + pad, :]
    conv_ref[0] = tail
    ubuf[0:pad, :] = tail


def _rglru(u, conv_w, conv_b, w_rg, b_rg, w_ig, b_ig, lam, buf0, h0, tt):
    b, t, d = u.shape
    cw = conv_w.shape[0]
    nblk, blk = w_rg.shape[0], w_rg.shape[1]
    vec = lambda a: a.reshape(1, d)
    fixed2 = lambda bb, ti: (0, 0)
    fixed3 = lambda bb, ti: (0, 0, 0)
    perb = lambda bb, ti: (bb, 0, 0)
    return pl.pallas_call(
        functools.partial(_rglru_kernel, tt=tt, conv_w=cw, nblk=nblk),
        out_shape=[jax.ShapeDtypeStruct((b, t, d), BF16), jax.ShapeDtypeStruct((b, SUBLANES, d), F32),
                   jax.ShapeDtypeStruct((b, 1, d), F32)],
        grid=(b, t // tt),
        in_specs=[pl.BlockSpec((1, tt, d), lambda bb, ti: (bb, ti, 0)),
                  pl.BlockSpec((cw, d), fixed2), pl.BlockSpec((1, d), fixed2),
                  pl.BlockSpec((nblk, blk, blk), fixed3), pl.BlockSpec((1, d), fixed2),
                  pl.BlockSpec((nblk, blk, blk), fixed3), pl.BlockSpec((1, d), fixed2),
                  pl.BlockSpec((1, d), fixed2),
                  pl.BlockSpec((1, SUBLANES, d), perb), pl.BlockSpec((1, 1, d), perb)],
        out_specs=[pl.BlockSpec((1, tt, d), lambda bb, ti: (bb, ti, 0)),
                   pl.BlockSpec((1, SUBLANES, d), perb), pl.BlockSpec((1, 1, d), perb)],
        scratch_shapes=[pltpu.VMEM((SUBLANES + tt, d), F32), pltpu.VMEM((tt, d), F32),
                        pltpu.VMEM((tt, d), F32), pltpu.VMEM((tt, d), F32), pltpu.VMEM((SUBLANES, d), F32)],
        compiler_params=_cparams(("parallel", "arbitrary")), name="rglru",
    )(u, conv_w, vec(conv_b), w_rg.astype(BF16), vec(b_rg), w_ig.astype(BF16), vec(b_ig), vec(lam), buf0, h0)


def _pack_rows_kernel(c_ref, n_ref, o_ref):
    p, t = c_ref.shape[1], n_ref.shape[1]
    o_ref[0, 0:p, :] = c_ref[0].reshape(p, o_ref.shape[2]).astype(o_ref.dtype)
    o_ref[0, p:p + t, :] = n_ref[0]
    o_ref[0, p + t:, :] = jnp.zeros((o_ref.shape[1] - p - t, o_ref.shape[2]), o_ref.dtype)


def _pack_rows(caches, j, new, s_pad):
    _, b, p, nh, dh = caches.shape
    t, w = new.shape[1], new.shape[2]
    return pl.pallas_call(
        _pack_rows_kernel, out_shape=jax.ShapeDtypeStruct((b, s_pad, w), BF16), grid=(b,),
        in_specs=[pl.BlockSpec((None, 1, p, nh, dh), lambda i: (j, i, 0, 0, 0)),
                  pl.BlockSpec((1, t, w), lambda i: (i, 0, 0))],
        out_specs=pl.BlockSpec((1, s_pad, w), lambda i: (i, 0, 0)),
        compiler_params=_cparams(("parallel",)), name="pack_rows",
    )(caches, new)


def _pad_to(a, axis, n):
    extra = n - a.shape[axis]
    if extra == 0:
        return a
    widths = [(0, 0)] * a.ndim
    widths[axis] = (0, extra)
    return jnp.pad(a, widths)


def _round_up(n, m):
    return (n + m - 1) // m * m


class _Group:
    def __init__(self, x, past):
        self.b, self.t, self.d = x.shape
        self.past = past
        self.flat = past > 0
        self.s_valid = past + self.t
        if self.flat:
            self.tq = _round_up(self.t, LANES)
            self.s_pad = _round_up(self.s_valid, PLANE_ROWS)
            self.fox_bq = self.dsa_bq = self.tq
            self.bk = self.s_pad
        else:
            self.tq = self.s_pad = self.t
            self.fox_bq = _pick(self.t, 512)
            self.dsa_bq = _pick(self.t, 256)
            self.bk = _pick(self.t, 512)

    def proj_view(self, x):
        return x.reshape(1, self.b * self.t, self.d) if self.flat else x

    def proj_bm(self):
        return self.b * self.t if self.flat else _pick(self.t, 512)

    def rows(self, a):
        return a.reshape(self.b, self.t, a.shape[-1])

    def heads_t(self, a):
        if not self.flat:
            return a
        return a.reshape(a.shape[1], a.shape[2], self.b, self.t).transpose(2, 0, 1, 3)

    def pad_q(self, a):
        return _pad_to(a, a.ndim - 1, self.tq)

    def keys(self, new_rm, cache):
        return new_rm if cache is None else _pack_rows(cache[0], cache[1], new_rm, self.s_pad)

    def values(self, new, cache):
        return new if cache is None else self.keys(self.rows(new), cache)


def _qkvg_plan(width, dh, v_rows):
    plan = [(0, 0, width, dh ** -0.5 * LOG2E, [(0, "headT")]),
            (0, width, width, 1.0, [(1, "row"), (3, "stack")]),
            (0, 2 * width, width, 1.0, [(2, "row" if v_rows else "headT"), (4, "stack")]),
            (0, 3 * width, width, 1.0, [(5, "row")])]
    outs = [("headT", width, BF16), ("row", width, BF16), ("row" if v_rows else "headT", width, BF16),
            ("stack", width, F32), ("stack", width, F32), ("row", width, BF16)]
    return plan, outs


def _mixer_a(grp, x, w_main, w_f, b_f, cache, layer, stacked):
    nh = b_f.shape[0]
    da = w_main.shape[1] // 4
    plan, outs = _qkvg_plan(da, da // nh, grp.flat)
    plan = plan + [(1, 0, LANES, 1.0, [(6, "row")])]
    outs = outs + [("row", LANES, F32)]
    qt, k_rm, vt, k, v, g, fl = _inproj(grp.proj_view(x), [w_main, w_f], plan, outs, grp.proj_bm(), layer, stacked)
    qt, k_rm, g = grp.pad_q(grp.heads_t(qt)), grp.rows(k_rm), grp.rows(g)
    z = grp.rows(fl)[:, :, :nh].transpose(0, 2, 1)
    ck_, cv_ = (None, None) if cache is None else cache[:2]
    if cache is not None:
        z = _pad_to(jnp.concatenate([cache[2].astype(F32).transpose(0, 2, 1), z], axis=2), 2, grp.s_pad)
    lf_all, c_all = _logf_cumsum(z, b_f, grp.past, grp.s_valid)
    logf = lf_all[:, :, grp.past:grp.s_valid].transpose(0, 2, 1)
    cq = grp.pad_q(c_all[:, :, grp.past:grp.s_valid])
    o = _fox_attention(qt, grp.keys(k_rm, ck_), grp.values(vt, cv_), cq, c_all,
                       grp.past, grp.fox_bq, grp.bk)[:, :grp.t]
    return o, g, ([k, v], logf)


def _mixer_b(grp, x, w_in, conv_w, conv_b, w_rg, b_rg, w_ig, b_ig, lam, state):
    dr = w_in.shape[1] // 2
    plan = [(0, 0, dr, 1.0, [(0, "row")]), (0, dr, dr, 1.0, [(1, "row")])]
    u, g = _inproj(grp.proj_view(x), [w_in], plan, [("row", dr, F32), ("row", dr, BF16)], grp.proj_bm())
    u, g = grp.rows(u), grp.rows(g)
    cw = conv_w.shape[0]
    if state is None:
        buf0 = jnp.zeros((grp.b, SUBLANES, dr), F32)
        h0 = jnp.zeros((grp.b, 1, dr), F32)
    else:
        buf, h0 = state
        buf0 = jnp.pad(buf.astype(F32), ((0, 0), (SUBLANES - (cw - 1), 0), (0, 0)))
        h0 = h0.astype(F32).reshape(grp.b, 1, dr)
    h, tail, hl = _rglru(u, conv_w, conv_b, w_rg, b_rg, w_ig, b_ig, lam, buf0, h0, _pick(grp.t, 256))
    return h, g, (tail[:, SUBLANES - (cw - 1):], hl[:, 0])


def _mixer_c(grp, x, w_main, w_idx, nh, nh_idx, d_idx, cache, layer, stacked):
    dc = w_main.shape[1] // 4
    wq = nh_idx * d_idx
    plan, outs = _qkvg_plan(dc, dc // nh, grp.flat)
    plan = plan + [(1, 0, wq, 1.0, [(6, "row")]), (1, wq, LANES, 1.0, [(7, "row")])]
    outs = outs + [("row", wq, BF16), ("row", LANES, F32)]
    qt, k_rm, vt, k, v, g, qi, kw = _inproj(grp.proj_view(x), [w_main, w_idx], plan, outs, grp.proj_bm(), layer,
                                            stacked)
    qt, k_rm = grp.pad_q(grp.heads_t(qt)), grp.rows(k_rm)
    g, qi, kw = grp.rows(g), grp.rows(qi), grp.rows(kw)
    ki = kw[:, :, :d_idx]
    wit = grp.pad_q(kw[:, :, d_idx:d_idx + nh_idx].transpose(0, 2, 1))
    qit = grp.pad_q(qi.reshape(grp.b, grp.t, nh_idx, d_idx).transpose(0, 2, 3, 1))
    ck_, cv_ = (None, None) if cache is None else cache[:2]
    ki_all = ki if cache is None else _pad_to(jnp.concatenate([cache[2].astype(F32), ki], axis=1), 1, grp.s_pad)
    topk = min(TOPK_MAX, grp.s_valid // 4)
    mask = _dsa_select(qit, wit, ki_all.astype(BF16), grp.past, grp.s_valid, topk, grp.dsa_bq, grp.bk)
    o = _dsa_attention(qt, grp.keys(k_rm, ck_), grp.values(vt, cv_), mask, grp.past, grp.fox_bq, grp.bk)[:, :grp.t]
    return o, g, ([k, v], ki)


def _run_trunk(x, p, caches, past):
    depth = p["ln_g"].shape[0]
    alpha = (2 * depth) ** 0.25
    grp = _Group(x, past)
    n_a, n_c = (depth + 2) // 3, depth // 3
    kv_a, kv_c = None, None
    new_a, new_b, new_c = [], [], []
    for i in range(depth):
        j, kind = i // 3, i % 3
        if kind == 0:
            cache = None if caches is None else ((caches["a_k"], j), (caches["a_v"], j), caches["a_logf"][j])
            o, g, (kv_a, logf) = _mixer_a(grp, x, p["w_main_a"][j], p["w_f_a"][j], p["b_f_a"][j], cache,
                                          (j, n_a), kv_a)
            new_a.append(logf)
            w_out = p["w_out_a"][j]
        elif kind == 1:
            state = None if caches is None else (caches["b_conv"][j], caches["b_h"][j])
            o, g, st = _mixer_b(grp, x, p["w_in_b"][j], p["conv_w_b"][j], p["conv_b_b"][j], p["w_rg_b"][j],
                                p["b_rg_b"][j], p["w_ig_b"][j], p["b_ig_b"][j], p["lam_b"][j], state)
            new_b.append(st)
            w_out = p["w_out_b"][j]
        else:
            cache = None if caches is None else ((caches["c_k"], j), (caches["c_v"], j), caches["c_kidx"][j])
            o, g, (kv_c, ki) = _mixer_c(grp, x, p["w_main_c"][j], p["w_idx_c"][j], p["h_c"], p["h_idx"], p["d_idx"],
                                        cache, (j, n_c), kv_c)
            new_c.append(ki)
            w_out = p["w_out_c"][j]
        m = grp.b * grp.t
        x = _outproj_ln(o.reshape(m, -1), g.reshape(m, -1), x.reshape(m, grp.d), w_out,
                        p["ln_g"][i], p["ln_b"][i], alpha).reshape(grp.b, grp.t, grp.d)
    stack = lambda sts, n: jnp.stack([s[n] for s in sts])
    heads = lambda a, nh: a.reshape(a.shape[0], grp.b, grp.t, nh, a.shape[-1] * a.shape[-2] // nh)
    h_a = p["b_f_a"].shape[1]
    return (x, heads(kv_a[0], h_a), heads(kv_a[1], h_a), jnp.stack(new_a), stack(new_b, 0), stack(new_b, 1),
            heads(kv_c[0], p["h_c"]), heads(kv_c[1], p["h_c"]), jnp.stack(new_c))


def kernel(x_prompt, x_sample, cache_a_k, cache_a_v, cache_a_logf, state_b_conv, state_b_h, cache_c_k, cache_c_v, cache_c_kidx, w_in_a, b_f_a, w_out_a, w_in_b, conv_w_b, conv_b_b, w_rg_b, b_rg_b, w_ig_b, b_ig_b, lam_b, w_out_b, w_in_c, w_out_c, ln_g, ln_b):
    h_a = b_f_a.shape[1]
    d_a = w_out_a.shape[1]
    d_c = w_out_c.shape[1]
    h_c = cache_c_k.shape[3]
    d_idx = cache_c_kidx.shape[-1]
    h_idx = (w_in_c.shape[2] - 4 * d_c - d_idx) // (d_idx + 1)
    assert d_a // h_a == LANES and d_c // h_c == LANES, "head width must equal the lane count"
    assert w_in_a.shape[2] == 4 * d_a + h_a and d_idx + h_idx <= LANES
    past = cache_a_k.shape[2]
    assert past % CHUNK == 0 and past > 0

    w_idx = w_in_c[:, :, 4 * d_c:]
    w_idx = _pad_to(w_idx, 2, h_idx * d_idx + LANES)
    p = {"w_main_a": w_in_a[:, :, :4 * d_a].astype(BF16),
         "w_f_a": _pad_to(w_in_a[:, :, 4 * d_a:], 2, LANES).astype(BF16),
         "b_f_a": b_f_a, "w_out_a": w_out_a.astype(BF16),
         "w_in_b": w_in_b.astype(BF16), "conv_w_b": conv_w_b, "conv_b_b": conv_b_b, "w_rg_b": w_rg_b,
         "b_rg_b": b_rg_b, "w_ig_b": w_ig_b, "b_ig_b": b_ig_b, "lam_b": lam_b, "w_out_b": w_out_b.astype(BF16),
         "w_main_c": w_in_c[:, :, :4 * d_c].astype(BF16), "w_idx_c": w_idx.astype(BF16),
         "w_out_c": w_out_c.astype(BF16), "ln_g": ln_g, "ln_b": ln_b,
         "h_c": h_c, "h_idx": h_idx, "d_idx": d_idx}
    caches = {"a_k": cache_a_k, "a_v": cache_a_v, "a_logf": cache_a_logf, "b_conv": state_b_conv,
              "b_h": state_b_h, "c_k": cache_c_k, "c_v": cache_c_v, "c_kidx": cache_c_kidx}
    outs_p = _run_trunk(x_prompt, p, None, 0)
    outs_s = _run_trunk(x_sample, p, caches, past)
    return (outs_p[0], outs_s[0]) + outs_p[1:] + outs_s[1:]
```

```python
import functools
import math

import jax
import jax.numpy as jnp
import numpy as np
from jax import lax
from jax.experimental import pallas as pl
from jax.experimental.pallas import tpu as pltpu

NEG_INF = -1e30
LN_EPS = 1e-5
CHUNK = 64
CHUNK_SHIFT = 6
TOPK_MAX = 256
LRU_C = 8.0
LANES = 128
SUBLANES = 8
BF16_ROWS = 16
PLANE_ROWS = 32 * SUBLANES
INT_MIN = -(2 ** 31)
VMEM_LIMIT = 56 * 1024 * 1024
LOG2E = math.log2(math.e)
SKIP_BITS = 60.0

F32 = jnp.float32
BF16 = jnp.bfloat16


def _cparams(sem, flags=None):
    return pltpu.CompilerParams(dimension_semantics=sem, vmem_limit_bytes=VMEM_LIMIT, flags=flags)


def _pick(n, pref):
    if n <= pref:
        return n
    b = pref
    while n % b:
        b //= 2
    return b


def _inproj_kernel(x_ref, *refs, n_w, n_alias, layer, plan):
    w_refs, out_refs = refs[:n_w], refs[n_w + n_alias:]
    xb = x_ref[0].astype(BF16)
    for w_idx, c0, width, scale, outs in plan:
        r = jnp.dot(xb, w_refs[w_idx][:, c0:c0 + width], preferred_element_type=F32)
        if scale != 1.0:
            r = r * scale
        for o_idx, kind in outs:
            o = out_refs[o_idx]
            if kind == "row":
                o[0] = r.astype(o.dtype)
            elif kind == "stack":
                val = r.astype(o.dtype).reshape(o.shape[2:])
                if o.shape[0] == 1:
                    o[0, 0] = val
                else:
                    for l in range(o.shape[0]):
                        o[l, 0] = val if l == layer else jnp.zeros_like(val)
            else:
                for h in range(width // LANES):
                    o[0, h] = r[:, h * LANES:(h + 1) * LANES].T.astype(o.dtype)


def _inproj(x3, ws, plan, out_defs, bm, layer=(0, 1), stacked=None):
    bx, tx, d = x3.shape
    j, n_layers = layer
    grid = (bx, tx // bm)
    in_specs = [pl.BlockSpec((1, bm, d), lambda b, i: (b, i, 0))]
    for w in ws:
        in_specs.append(pl.BlockSpec(w.shape, lambda b, i: (0, 0)))
    stacked = list(stacked or [])
    in_specs += [pl.BlockSpec(memory_space=pl.ANY)] * len(stacked)
    out_shape, out_specs, aliases = [], [], {}
    for o_idx, (kind, width, dt) in enumerate(out_defs):
        if kind == "row":
            out_shape.append(jax.ShapeDtypeStruct((bx, tx, width), dt))
            out_specs.append(pl.BlockSpec((1, bm, width), lambda b, i: (b, i, 0)))
        elif kind == "stack":
            out_shape.append(jax.ShapeDtypeStruct((n_layers, bx, tx, width // LANES, LANES), dt))
            if stacked:
                out_specs.append(pl.BlockSpec((1, 1, bm, width // LANES, LANES), lambda b, i: (j, b, i, 0, 0)))
                aliases[1 + len(ws) + len(aliases)] = o_idx
            else:
                out_specs.append(pl.BlockSpec((n_layers, 1, bm, width // LANES, LANES), lambda b, i: (0, b, i, 0, 0)))
        else:
            nh = width // LANES
            out_shape.append(jax.ShapeDtypeStruct((bx, nh, LANES, tx), dt))
            out_specs.append(pl.BlockSpec((1, nh, LANES, bm), lambda b, i: (b, 0, 0, i)))
    assert len(aliases) == len(stacked)
    return pl.pallas_call(
        functools.partial(_inproj_kernel, n_w=len(ws), n_alias=len(stacked), layer=j, plan=tuple(plan)),
        out_shape=out_shape, grid=grid, in_specs=in_specs, out_specs=out_specs, input_output_aliases=aliases,
        compiler_params=_cparams(("parallel", "parallel")), name="inproj",
    )(x3, *ws, *stacked)


def _outproj_ln_kernel(o_ref, g_ref, x_ref, w_ref, lg_ref, lb_ref, y_ref, *, alpha):
    g = g_ref[...].astype(F32)
    og = (o_ref[...].astype(F32) * (g * jax.nn.sigmoid(g))).astype(BF16)
    y = jnp.dot(og, w_ref[...], preferred_element_type=F32)
    z = alpha * x_ref[...] + y
    mu = jnp.mean(z, axis=-1, keepdims=True)
    zc = z - mu
    var = jnp.mean(zc * zc, axis=-1, keepdims=True)
    y_ref[...] = zc * lax.rsqrt(var + LN_EPS) * lg_ref[...] + lb_ref[...]


def _outproj_ln(o2, g2, x2, w, ln_g, ln_b, alpha):
    m, d = x2.shape
    dk = o2.shape[1]
    bm = _pick(m, 512)
    row = lambda i: (i, 0)
    fixed = lambda i: (0, 0)
    return pl.pallas_call(
        functools.partial(_outproj_ln_kernel, alpha=alpha),
        out_shape=jax.ShapeDtypeStruct((m, d), F32), grid=(m // bm,),
        in_specs=[pl.BlockSpec((bm, dk), row), pl.BlockSpec((bm, dk), row), pl.BlockSpec((bm, d), row),
                  pl.BlockSpec((dk, d), fixed), pl.BlockSpec((1, d), fixed), pl.BlockSpec((1, d), fixed)],
        out_specs=pl.BlockSpec((bm, d), row),
        compiler_params=_cparams(("parallel",)), name="outproj_ln",
    )(o2, g2, x2, w, ln_g.reshape(1, d), ln_b.reshape(1, d))


def _log_sigmoid(x):
    return -(jnp.maximum(-x, 0.0) + jnp.log1p(jnp.exp(-jnp.abs(x))))


def _logf_cumsum_kernel(z_ref, bf_ref, lf_ref, c_ref, *, p0, p1):
    z = z_ref[0]
    pos = lax.broadcasted_iota(jnp.int32, z.shape, 1)
    is_new = jnp.logical_and(pos >= p0, pos < p1)
    lf = jnp.where(is_new, _log_sigmoid(z + bf_ref[...]), z)
    lf_ref[0] = lf
    c = lf
    s = 1
    while s < z.shape[1]:
        c = c + jnp.where(pos >= s, pltpu.roll(c, s, axis=1), 0.0)
        s *= 2
    c_ref[0] = c


def _logf_cumsum(z, b_f, p0, p1):
    b, h, l = z.shape
    blk = pl.BlockSpec((1, h, l), lambda i: (i, 0, 0))
    return pl.pallas_call(
        functools.partial(_logf_cumsum_kernel, p0=p0, p1=p1),
        out_shape=[jax.ShapeDtypeStruct(z.shape, F32)] * 2, grid=(b,),
        in_specs=[blk, pl.BlockSpec((h, 1), lambda i: (0, 0))], out_specs=[blk, blk],
        compiler_params=_cparams(("parallel",)), name="logf_cumsum",
    )(z, b_f.reshape(h, 1))


def _score_store(u, slot, s_sc, mc_sc):
    s_sc[slot] = u
    mc_sc[slot] = jnp.broadcast_to(jnp.max(u, axis=0, keepdims=True), mc_sc.shape[1:])


def _softmax_update(h, slot, vt, s_sc, mc_sc, m_sc, acc_sc, row_shift=None):
    u = s_sc[slot]
    m_prev = m_sc[h]
    m_cur = mc_sc[slot]
    if row_shift is not None:
        m_cur = m_cur + row_shift
    m_new = jnp.maximum(m_prev, m_cur)
    m_row = m_new[0:1]
    p = jnp.exp2(u - (m_row if row_shift is None else m_row - row_shift))
    alpha = jnp.exp2(m_prev - m_new)
    vt1 = jnp.concatenate([vt, jnp.ones((BF16_ROWS, vt.shape[1]), BF16)], axis=0)
    acc_sc[h] = acc_sc[h] * alpha[0:1] + jnp.dot(vt1, p.astype(BF16), preferred_element_type=F32)
    m_sc[h] = m_new


def _head(k_ref, h):
    dh = LANES
    return k_ref[0, :, h * dh:(h + 1) * dh]


def _head_t(v_ref, h, v_rows):
    if not v_rows:
        return v_ref[0, h]
    return _head(v_ref, h).astype(F32).T.astype(BF16)


def _scores(k, kx, qt, qx):
    return jnp.dot(jnp.concatenate([k, kx], axis=1), jnp.concatenate([qt, qx], axis=0),
                   preferred_element_type=F32)


def _pipelined_heads(nh, qk_phase, sm_phase):
    qk_phase(0, 0)
    for h in range(nh - 1):
        qk_phase(h + 1, (h + 1) % 2)
        sm_phase(h, h % 2)
    sm_phase(nh - 1, (nh - 1) % 2)


def _attn_init(m_sc, acc_sc):
    m_sc[...] = jnp.full(m_sc.shape, -jnp.inf, F32)
    acc_sc[...] = jnp.zeros(acc_sc.shape, F32)


def _attn_finish(o_ref, acc_sc, nh, dh):
    for h in range(nh):
        a = acc_sc[h]
        o_ref[0, :, h * dh:(h + 1) * dh] = (a[0:dh] / a[dh:dh + 1]).T.astype(o_ref.dtype)


def _attn_scratch(nh, bq, bk, dh):
    return [pltpu.VMEM((nh, SUBLANES, bq), F32), pltpu.VMEM((nh, dh + BF16_ROWS, bq), F32),
            pltpu.VMEM((2, bk, bq), F32), pltpu.VMEM((2, SUBLANES, bq), F32)]


def _fox_kernel(lo_ref, qt_ref, k_ref, v_ref, cq_ref, kx_ref, qx_ref, o_ref, m_sc, acc_sc, s_sc, mc_sc,
                *, bq, bk, q_off, nh, v_rows):
    qi, ki = pl.program_id(1), pl.program_id(2)
    q_first = q_off + qi * bq
    q_last = q_first + bq - 1
    k_first = ki * bk
    k_last = k_first + bk - 1

    @pl.when(ki == 0)
    def _():
        _attn_init(m_sc, acc_sc)

    def step(masked):
        kx = kx_ref[0]
        if masked:
            kpos = k_first + lax.broadcasted_iota(jnp.int32, (bk, bq), 0)
            qpos = q_first + lax.broadcasted_iota(jnp.int32, (bk, bq), 1)
            causal = kpos <= qpos

        def qk_phase(h, slot):
            u = _scores(_head(k_ref, h), kx, qt_ref[0, h], qx_ref[h])
            if masked:
                u = jnp.where(causal, u, NEG_INF)
            _score_store(u, slot, s_sc, mc_sc)

        def sm_phase(h, slot):
            cq2 = cq_ref[0, pl.ds(h, 1), :] * LOG2E
            _softmax_update(h, slot, _head_t(v_ref, h, v_rows), s_sc, mc_sc, m_sc, acc_sc, row_shift=cq2)

        _pipelined_heads(nh, qk_phase, sm_phase)

    needed = jnp.logical_and(k_first <= q_last, ki >= lo_ref[pl.program_id(0), qi])
    straddles = k_last > q_first

    @pl.when(jnp.logical_and(needed, straddles))
    def _():
        step(True)

    @pl.when(jnp.logical_and(needed, jnp.logical_not(straddles)))
    def _():
        step(False)

    @pl.when(ki == pl.num_programs(2) - 1)
    def _():
        _attn_finish(o_ref, acc_sc, nh, LANES)


def _split3(x):
    def top_half(v):
        bits = lax.bitcast_convert_type(v, jnp.uint32) & jnp.uint32(0xFFFF0000)
        return lax.bitcast_convert_type(bits, F32)

    hi = top_half(x)
    r1 = x - hi
    mid = top_half(r1)
    return hi.astype(BF16), mid.astype(BF16), (r1 - mid).astype(BF16)


def _v_spec(v, nh, dh, bk, kmap):
    if v.ndim == 3:
        return True, pl.BlockSpec((1, bk, nh * dh), lambda *ix: (ix[0], kmap(*ix), 0))
    return False, pl.BlockSpec((1, nh, dh, bk), lambda *ix: (ix[0], 0, 0, kmap(*ix)))


def _fox_first_block(qt_hm, k_rm, cq, ck, q_off, bq, bk):
    b, nh, dh, tq = qt_hm.shape
    s = k_rm.shape[1]
    nq, nk = tq // bq, s // bk
    qn = jnp.sqrt(jnp.sum(jnp.square(qt_hm.astype(F32)), axis=2)).reshape(b, nh, nq, bq).max(axis=-1)
    kn = jnp.sqrt(jnp.sum(jnp.square(k_rm.astype(F32).reshape(b, nk, bk, nh, dh)), axis=-1)).max(axis=2)
    kn = kn.transpose(0, 2, 1)
    first_self = [(q_off + qi * bq) // bk for qi in range(nq)]
    last_self = [min((q_off + (qi + 1) * bq - 1) // bk, nk - 1) for qi in range(nq)]
    kn_self = jnp.stack([kn[:, :, f:l + 1].max(axis=-1) for f, l in zip(first_self, last_self)], axis=-1)
    c_first = cq[:, :, ::bq]
    c_last = ck[:, :, bk - 1::bk]
    bound = (qn[..., :, None] * (kn[..., None, :] + kn_self[..., :, None])
             + (c_first[..., :, None] - c_last[..., None, :]) * LOG2E)
    skip = jnp.all(bound < -SKIP_BITS, axis=1)
    lo = jnp.sum(jnp.cumprod(skip.astype(jnp.int32), axis=-1), axis=-1)
    return jnp.minimum(lo, jnp.asarray(first_self, jnp.int32)[None, :]).astype(jnp.int32)


def _fox_attention(qt_hm, k_rm, v, cq, ck, q_off, bq, bk):
    b, nh, dh, tq = qt_hm.shape
    s = k_rm.shape[1]
    nq, nk = tq // bq, s // bk
    w = LANES // nh
    assert w >= 3
    kx = jnp.stack(_split3(ck * (-LOG2E)), axis=-1)
    kx = _pad_to(kx, 3, w).transpose(0, 2, 1, 3).reshape(b, s, LANES)
    rows = jnp.arange(LANES)[None, :, None]
    heads = jnp.arange(nh)[:, None, None]
    qx = jnp.broadcast_to(jnp.logical_and(rows >= heads * w, rows < heads * w + 3), (nh, LANES, bq)).astype(BF16)

    lo = _fox_first_block(qt_hm, k_rm, cq, ck, q_off, bq, bk)

    def kmap(bb, qi, ki, lo_ref):
        return jnp.clip(ki, lo_ref[bb, qi], (q_off + (qi + 1) * bq - 1) // bk)

    v_rows, v_spec = _v_spec(v, nh, dh, bk, kmap)
    grid_spec = pltpu.PrefetchScalarGridSpec(
        num_scalar_prefetch=1, grid=(b, nq, nk),
        in_specs=[pl.BlockSpec((1, nh, dh, bq), lambda bb, qi, ki, lo_ref: (bb, 0, 0, qi)),
                  pl.BlockSpec((1, bk, nh * dh), lambda *ix: (ix[0], kmap(*ix), 0)),
                  v_spec,
                  pl.BlockSpec((1, nh, bq), lambda bb, qi, ki, lo_ref: (bb, 0, qi)),
                  pl.BlockSpec((1, bk, LANES), lambda *ix: (ix[0], kmap(*ix), 0)),
                  pl.BlockSpec((nh, LANES, bq), lambda bb, qi, ki, lo_ref: (0, 0, 0))],
        out_specs=pl.BlockSpec((1, bq, nh * dh), lambda bb, qi, ki, lo_ref: (bb, qi, 0)),
        scratch_shapes=_attn_scratch(nh, bq, bk, dh))
    return pl.pallas_call(
        functools.partial(_fox_kernel, bq=bq, bk=bk, q_off=q_off, nh=nh, v_rows=v_rows),
        out_shape=jax.ShapeDtypeStruct((b, tq, nh * dh), BF16), grid_spec=grid_spec,
        compiler_params=_cparams(("parallel", "parallel", "arbitrary")), name="fox_attention",
    )(lo, qt_hm, k_rm, v, cq, kx, qx)


def _bit_planes(words):
    a = list(words)
    j, m = 16, 0x0000FFFF
    while j:
        k = 0
        while k < 32:
            t = (a[k] ^ lax.shift_right_logical(a[k + j], jnp.int32(j))) & jnp.int32(m)
            a[k] = a[k] ^ t
            a[k + j] = a[k + j] ^ lax.shift_left(t, jnp.int32(j))
            k = (k + j + 1) & ~j
        j >>= 1
        m = (m ^ (m << j)) & 0xFFFFFFFF
    return a


def _dsa_select_kernel(qit_ref, wit_ref, ki_ref, out_ref, key_sc, pl_sc, act_sc, j_sc,
                       *, bq, kb, q_off, s_valid, topk, nh_idx, idx_bits):
    qb = pl.program_id(1)
    q_first = q_off + qb * bq
    q_last = q_first + bq - 1
    n_adm = jnp.minimum(((q_last >> CHUNK_SHIFT) + 1) * CHUNK, s_valid)
    nkb = (jnp.maximum(n_adm, topk) + kb - 1) // kb
    nkb_total = key_sc.shape[0] // kb
    gpb = kb // PLANE_ROWS
    nkb4 = nkb // 4
    qchunk = (q_first + lax.broadcasted_iota(jnp.int32, (kb, bq), 1)) >> CHUNK_SHIFT
    row = lax.broadcasted_iota(jnp.int32, (kb, bq), 0)

    def admissible(kpos):
        return jnp.logical_and((kpos >> CHUNK_SHIFT) <= qchunk, kpos < s_valid)

    def rows(kblk):
        return pl.ds(pl.multiple_of(kblk * kb, kb), kb)

    def score_block(kblk, carry, all_admissible):
        kblock = ki_ref[0, rows(kblk), :]
        sc = jnp.zeros((kb, bq), F32)
        for h in range(nh_idx):
            d = jnp.dot(kblock, qit_ref[0, h], preferred_element_type=F32)
            sc = sc + wit_ref[0, h:h + 1, :] * jnp.maximum(d, 0.0)
        sc = sc + 0.0
        if not all_admissible:
            sc = jnp.where(admissible(kblk * kb + row), sc, NEG_INF)
        bits = lax.bitcast_convert_type(sc, jnp.int32)
        key = bits ^ ((bits >> 31) & 0x7FFFFFFF)
        key_sc[rows(kblk), :] = key
        for g in range(gpb):
            base = g * PLANE_ROWS
            planes = _bit_planes([key[base + SUBLANES * j:base + SUBLANES * (j + 1), :] for j in range(32)])
            planes[0] = ~planes[0]
            for i in range(32):
                pl_sc[kblk * gpb + g, i] = planes[i]
            act_sc[kblk * gpb + g] = jnp.full((SUBLANES, bq), -1, jnp.int32)
        return carry

    n_full = jnp.minimum(((q_first >> CHUNK_SHIFT) + 1) * CHUNK, s_valid) // kb
    lax.fori_loop(0, n_full, functools.partial(score_block, all_admissible=True), 0)
    lax.fori_loop(n_full, nkb, functools.partial(score_block, all_admissible=False), 0)

    def two_bit_step(it, carry):
        thr_u, above = carry
        i1, i0 = 2 * it, 2 * it + 1

        def split(g):
            a = act_sc[g]
            a1 = a & pl_sc[g, i1]
            x11 = a1 & pl_sc[g, i0]
            return a, x11, a1 ^ x11, (a & pl_sc[g, i0]) ^ x11

        def cnt_groups(g0, n, accs):
            c11, c10, c01 = accs
            for g in range(n):
                _, x11, x10, x01 = split(g0 + g)
                c11 = c11 + lax.population_count(x11)
                c10 = c10 + lax.population_count(x10)
                c01 = c01 + lax.population_count(x01)
            return c11, c10, c01

        zero = jnp.zeros((SUBLANES, bq), jnp.int32)
        accs = lax.fori_loop(0, nkb4, lambda i, c: cnt_groups(i * gpb * 4, gpb * 4, c), (zero, zero, zero))
        accs = lax.fori_loop(nkb4 * 4, nkb, lambda i, c: cnt_groups(i * gpb, gpb, c), accs)
        c11, c10, c01 = [jnp.sum(c.astype(F32), axis=0, keepdims=True) for c in accs]
        s1 = above + c11
        s2 = s1 + c10
        s3 = s2 + c01
        t11, t10, t01 = s1 >= topk, s2 >= topk, s3 >= topk

        def upd_groups(g0, n, carry2):
            for g in range(n):
                a, x11, x10, x01 = split(g0 + g)
                x00 = a ^ x11 ^ x10 ^ x01
                act_sc[g0 + g] = jnp.where(t11, x11, jnp.where(t10, x10, jnp.where(t01, x01, x00)))
            return carry2

        lax.fori_loop(0, nkb4, lambda i, c2: upd_groups(i * gpb * 4, gpb * 4, c2), 0)
        lax.fori_loop(nkb4 * 4, nkb, lambda i, c2: upd_groups(i * gpb, gpb, c2), 0)
        hi = lax.shift_left(jnp.int32(1), 31 - i1)
        lo = lax.shift_left(jnp.int32(1), 31 - i0)
        low_bit = jnp.where(t11, lo, jnp.where(t10, 0, jnp.where(t01, lo, 0)))
        return (thr_u | jnp.where(t10, hi, 0) | low_bit,
                jnp.where(t11, above, jnp.where(t10, s1, jnp.where(t01, s2, s3))))

    thr_u, n_gt = lax.fori_loop(0, 16, two_bit_step, (jnp.zeros((1, bq), jnp.int32), jnp.zeros((1, bq), F32)))
    thr = thr_u ^ INT_MIN

    def eq_blk(kblk, acc):
        for g in range(gpb):
            acc = acc + lax.population_count(act_sc[kblk * gpb + g])
        return acc

    n_eq = jnp.sum(lax.fori_loop(0, nkb, eq_blk, jnp.zeros((SUBLANES, bq), jnp.int32)).astype(F32),
                   axis=0, keepdims=True)
    need = topk - n_gt
    has_excess = jnp.max(n_eq - need) > 0.0
    j_sc[...] = jnp.full(j_sc.shape, 2 ** 30, jnp.int32)

    @pl.when(has_excess)
    def _():
        def count_ties_below(cand):
            def blk(kblk, acc):
                hit = jnp.logical_and(key_sc[rows(kblk), :] == thr, kblk * kb + row < cand)
                ind = jnp.where(hit, 1.0, 0.0)
                for r in range(kb // SUBLANES):
                    acc = acc + ind[r * SUBLANES:(r + 1) * SUBLANES]
                return acc
            return jnp.sum(lax.fori_loop(0, nkb, blk, jnp.zeros((SUBLANES, bq), F32)), axis=0, keepdims=True)

        def idx_step(it, lo):
            cand = lo | lax.shift_left(jnp.int32(1), idx_bits - 1 - it)
            return jnp.where(count_ties_below(cand) < need, cand, lo)

        lo = lax.fori_loop(0, idx_bits, idx_step, jnp.zeros((1, bq), jnp.int32))
        j_sc[...] = jnp.broadcast_to(lo, j_sc.shape)

    j_cut = j_sc[0:1, :]

    def write_fast(kblk, carry):
        out_ref[0, 0, rows(kblk), :] = jnp.where(key_sc[rows(kblk), :] >= thr, 0.0, NEG_INF).astype(out_ref.dtype)
        return carry

    def write_block(kblk, carry):
        key = key_sc[rows(kblk), :]
        kpos = kblk * kb + row
        tie = jnp.logical_and(key == thr, kpos <= j_cut)
        sel = jnp.logical_and(jnp.logical_or(key > thr, tie), admissible(kpos))
        out_ref[0, 0, rows(kblk), :] = jnp.where(sel, 0.0, NEG_INF).astype(out_ref.dtype)
        return carry

    def fill_block(kblk, carry):
        out_ref[0, 0, rows(kblk), :] = jnp.full((kb, bq), NEG_INF, out_ref.dtype)
        return carry

    n_fast = jnp.where(has_excess, 0, n_full)
    lax.fori_loop(0, n_fast, write_fast, 0)
    lax.fori_loop(n_fast, nkb, write_block, 0)
    lax.fori_loop(nkb, nkb_total, fill_block, 0)


def _dsa_select(qit_hm, wit, ki, q_off, s_valid, topk, bq, kb):
    b, nh_idx, d_idx, tq = qit_hm.shape
    s_pad = ki.shape[1]
    return pl.pallas_call(
        functools.partial(_dsa_select_kernel, bq=bq, kb=kb, q_off=q_off, s_valid=s_valid, topk=topk,
                          nh_idx=nh_idx, idx_bits=max(1, (s_pad - 1).bit_length())),
        out_shape=jax.ShapeDtypeStruct((b, tq // bq, s_pad, bq), BF16), grid=(b, tq // bq),
        in_specs=[pl.BlockSpec((1, nh_idx, d_idx, bq), lambda bb, qb: (bb, 0, 0, qb)),
                  pl.BlockSpec((1, nh_idx, bq), lambda bb, qb: (bb, 0, qb)),
                  pl.BlockSpec((1, s_pad, d_idx), lambda bb, qb: (bb, 0, 0))],
        out_specs=pl.BlockSpec((1, 1, s_pad, bq), lambda bb, qb: (bb, qb, 0, 0)),
        scratch_shapes=[pltpu.VMEM((s_pad, bq), jnp.int32),
                        pltpu.VMEM((s_pad // PLANE_ROWS, 32, SUBLANES, bq), jnp.int32),
                        pltpu.VMEM((s_pad // PLANE_ROWS, SUBLANES, bq), jnp.int32),
                        pltpu.VMEM((SUBLANES, bq), jnp.int32)],
        compiler_params=_cparams(("parallel", "arbitrary")), name="dsa_select",
    )(qit_hm, wit, ki)


def _dsa_attn_kernel(sl_ref, qt_ref, k_ref, v_ref, mb_ref, qx_ref, o_ref, m_sc, acc_sc, s_sc, mc_sc,
                     base_sc, t_sc, *, bq, bk, q_off, nh, v_rows):
    qi, ki = pl.program_id(1), pl.program_id(2)
    q_first = q_off + qi * bq
    q_last = q_first + bq - 1
    k_first = ki * bk
    k_last = k_first + bk - 1
    qposf = (q_first + lax.broadcasted_iota(jnp.int32, (1, bq), 1)).astype(F32)

    @pl.when(ki == 0)
    def _():
        _attn_init(m_sc, acc_sc)

    def step(past):
        base_sc[...] = jnp.concatenate([mb_ref[0, j] for j in range(mb_ref.shape[1])], axis=1).astype(F32)
        if past:
            kpos = k_first + lax.broadcasted_iota(jnp.int32, (bk, LANES), 0)
            lane = lax.broadcasted_iota(jnp.int32, (bk, LANES), 1)
            kx = jnp.where(lane < 3, kpos >> CHUNK_SHIFT, jnp.where(lane < 6, kpos & (CHUNK - 1), 0)).astype(BF16)
        else:
            kpos = k_first + lax.broadcasted_iota(jnp.int32, (bk, bq), 0)
            qpos = q_first + lax.broadcasted_iota(jnp.int32, (bk, bq), 1)
            t_sc[...] = jnp.minimum(kpos, 2 * qpos - kpos).astype(F32)

        def qk_phase(h, slot):
            if past:
                u = _scores(_head(k_ref, h), kx, qt_ref[0, h], qx_ref[h]) + base_sc[...]
            else:
                st = jnp.dot(_head(k_ref, h), qt_ref[0, h], preferred_element_type=F32)
                u = st + sl_ref[h] * t_sc[...] + base_sc[...]
            _score_store(u, slot, s_sc, mc_sc)

        def sm_phase(h, slot):
            _softmax_update(h, slot, _head_t(v_ref, h, v_rows), s_sc, mc_sc, m_sc, acc_sc,
                            row_shift=-sl_ref[h] * qposf)

        _pipelined_heads(nh, qk_phase, sm_phase)

    needed = k_first <= ((q_last >> CHUNK_SHIFT) << CHUNK_SHIFT) + CHUNK - 1
    past = k_last <= q_first

    @pl.when(past)
    def _():
        step(True)

    @pl.when(jnp.logical_and(needed, jnp.logical_not(past)))
    def _():
        step(False)

    @pl.when(ki == pl.num_programs(2) - 1)
    def _():
        _attn_finish(o_ref, acc_sc, nh, LANES)


def _dsa_attention(qt_hm, k_rm, v, mask, q_off, bq, bk):
    b, nh, dh, tq = qt_hm.shape
    s, mq = mask.shape[2], mask.shape[3]
    nq, nk, nsub = tq // bq, s // bk, bq // mq
    slopes2_np = (2.0 ** (-8.0 * np.arange(1, nh + 1) / nh) * LOG2E).astype(np.float32)
    slopes2 = jnp.asarray(slopes2_np)
    rest, pieces = slopes2_np, []
    for _ in range(3):
        piece = rest.astype(BF16).astype(np.float32)
        pieces.append(piece)
        rest = rest - piece
    col = np.zeros((nh, LANES), np.float32)
    col[:, 0:3] = np.stack(pieces, axis=1) * CHUNK
    col[:, 3:6] = np.stack(pieces, axis=1)
    qx = jnp.broadcast_to(jnp.asarray(col.astype(BF16))[:, :, None], (nh, LANES, bq))

    def kmap(bb, qi, ki):
        q_last = q_off + (qi + 1) * bq - 1
        return jnp.minimum(ki, (((q_last >> CHUNK_SHIFT) << CHUNK_SHIFT) + CHUNK - 1) // bk)

    v_rows, v_spec = _v_spec(v, nh, dh, bk, kmap)
    return pl.pallas_call(
        functools.partial(_dsa_attn_kernel, bq=bq, bk=bk, q_off=q_off, nh=nh, v_rows=v_rows),
        out_shape=jax.ShapeDtypeStruct((b, tq, nh * dh), BF16), grid=(b, nq, nk),
        in_specs=[pl.BlockSpec(memory_space=pltpu.SMEM),
                  pl.BlockSpec((1, nh, dh, bq), lambda bb, qi, ki: (bb, 0, 0, qi)),
                  pl.BlockSpec((1, bk, nh * dh), lambda bb, qi, ki: (bb, kmap(bb, qi, ki), 0)),
                  v_spec,
                  pl.BlockSpec((1, nsub, bk, mq), lambda bb, qi, ki: (bb, qi, kmap(bb, qi, ki), 0)),
                  pl.BlockSpec((nh, LANES, bq), lambda bb, qi, ki: (0, 0, 0))],
        out_specs=pl.BlockSpec((1, bq, nh * dh), lambda bb, qi, ki: (bb, qi, 0)),
        scratch_shapes=_attn_scratch(nh, bq, bk, dh) + [pltpu.VMEM((bk, bq), F32), pltpu.VMEM((bk, bq), F32)],
        compiler_params=_cparams(("parallel", "parallel", "arbitrary")), name="dsa_attention",
    )(slopes2, qt_hm, k_rm, v, mask, qx)


def _rglru_kernel(u_ref, cw_ref, cb_ref, wr_ref, br_ref, wig_ref, big_ref, lam_ref, buf0_ref, h0_ref,
                  h_ref, conv_ref, hl_ref, ubuf, a_sc, b_sc, hbuf, hcar, *, tt, conv_w, nblk):
    t = pl.program_id(1)
    pad = SUBLANES
    d = u_ref.shape[2]
    blk = d // nblk

    @pl.when(t == 0)
    def _():
        ubuf[0:pad, :] = buf0_ref[0]
        hcar[...] = jnp.broadcast_to(h0_ref[0], (SUBLANES, d))

    ubuf[pad:pad + tt, :] = u_ref[0]
    uc = cb_ref[...] + ubuf[pad:pad + tt, :] * cw_ref[conv_w - 1:conv_w, :]
    for j in range(conv_w - 1):
        off = pad - (conv_w - 1) + j
        uc = uc + ubuf[off:off + tt, :] * cw_ref[j:j + 1, :]

    lam = lam_ref[...]
    neg_sp = -LRU_C * (jnp.maximum(-lam, 0.0) + jnp.log1p(jnp.exp(-jnp.abs(lam))))
    for n in range(nblk):
        cs = slice(n * blk, (n + 1) * blk)
        ucn = uc[:, cs]
        ub = ucn.astype(BF16)
        r = jax.nn.sigmoid(jnp.dot(ub, wr_ref[n], preferred_element_type=F32) + br_ref[:, cs])
        i = jax.nn.sigmoid(jnp.dot(ub, wig_ref[n], preferred_element_type=F32) + big_ref[:, cs])
        log_a = r * neg_sp[:, cs]
        a = jnp.exp(log_a)
        a_sc[:, cs] = a
        b_sc[:, cs] = jnp.sqrt(-jnp.tanh(log_a) * (a * a + 1.0)) * (i * ucn)

    row = lax.broadcasted_iota(jnp.int32, (SUBLANES, d), 0)

    def group(gi, hprev):
        r0 = pl.multiple_of(gi * SUBLANES, SUBLANES)
        av = a_sc[pl.ds(r0, SUBLANES), :]
        bv = b_sc[pl.ds(r0, SUBLANES), :]
        s = 1
        while s < SUBLANES:
            a_sh = pltpu.roll(av, s, axis=0)
            b_sh = pltpu.roll(bv, s, axis=0)
            m = row >= s
            bv = jnp.where(m, av * b_sh + bv, bv)
            av = jnp.where(m, av * a_sh, av)
            s *= 2
        hrows = av * hprev + bv
        hbuf[pl.ds(r0, SUBLANES), :] = hrows
        return jnp.broadcast_to(hrows[SUBLANES - 1:SUBLANES, :], (SUBLANES, d))

    hlast = lax.fori_loop(0, tt // SUBLANES, group, hcar[...])
    h_ref[0] = hbuf[...].astype(h_ref.dtype)
    hcar[...] = hlast
    hl_ref[0] = hlast[0:1, :]
    tail = ubuf[tt:tt + pad, :]
    conv_ref[0] = tail
    ubuf[0:pad, :] = tail


def _rglru(u, conv_w, conv_b, w_rg, b_rg, w_ig, b_ig, lam, buf0, h0, tt):
    b, t, d = u.shape
    cw = conv_w.shape[0]
    nblk, blk = w_rg.shape[0], w_rg.shape[1]
    vec = lambda a: a.reshape(1, d)
    fixed2 = lambda bb, ti: (0, 0)
    fixed3 = lambda bb, ti: (0, 0, 0)
    perb = lambda bb, ti: (bb, 0, 0)
    return pl.pallas_call(
        functools.partial(_rglru_kernel, tt=tt, conv_w=cw, nblk=nblk),
        out_shape=[jax.ShapeDtypeStruct((b, t, d), BF16), jax.ShapeDtypeStruct((b, SUBLANES, d), F32),
                   jax.ShapeDtypeStruct((b, 1, d), F32)],
        grid=(b, t // tt),
        in_specs=[pl.BlockSpec((1, tt, d), lambda bb, ti: (bb, ti, 0)),
                  pl.BlockSpec((cw, d), fixed2), pl.BlockSpec((1, d), fixed2),
                  pl.BlockSpec((nblk, blk, blk), fixed3), pl.BlockSpec((1, d), fixed2),
                  pl.BlockSpec((nblk, blk, blk), fixed3), pl.BlockSpec((1, d), fixed2),
                  pl.BlockSpec((1, d), fixed2),
                  pl.BlockSpec((1, SUBLANES, d), perb), pl.BlockSpec((1, 1, d), perb)],
        out_specs=[pl.BlockSpec((1, tt, d), lambda bb, ti: (bb, ti, 0)),
                   pl.BlockSpec((1, SUBLANES, d), perb), pl.BlockSpec((1, 1, d), perb)],
        scratch_shapes=[pltpu.VMEM((SUBLANES + tt, d), F32), pltpu.VMEM((tt, d), F32),
                        pltpu.VMEM((tt, d), F32), pltpu.VMEM((tt, d), F32), pltpu.VMEM((SUBLANES, d), F32)],
        compiler_params=_cparams(("parallel", "arbitrary")), name="rglru",
    )(u, conv_w, vec(conv_b), w_rg.astype(BF16), vec(b_rg), w_ig.astype(BF16), vec(b_ig), vec(lam), buf0, h0)


def _pack_rows_kernel(c_ref, n_ref, o_ref):
    p, t = c_ref.shape[1], n_ref.shape[1]
    o_ref[0, 0:p, :] = c_ref[0].reshape(p, o_ref.shape[2]).astype(o_ref.dtype)
    o_ref[0, p:p + t, :] = n_ref[0]
    o_ref[0, p + t:, :] = jnp.zeros((o_ref.shape[1] - p - t, o_ref.shape[2]), o_ref.dtype)


def _pack_rows(caches, j, new, s_pad):
    _, b, p, nh, dh = caches.shape
    t, w = new.shape[1], new.shape[2]
    return pl.pallas_call(
        _pack_rows_kernel, out_shape=jax.ShapeDtypeStruct((b, s_pad, w), BF16), grid=(b,),
        in_specs=[pl.BlockSpec((None, 1, p, nh, dh), lambda i: (j, i, 0, 0, 0)),
                  pl.BlockSpec((1, t, w), lambda i: (i, 0, 0))],
        out_specs=pl.BlockSpec((1, s_pad, w), lambda i: (i, 0, 0)),
        compiler_params=_cparams(("parallel",)), name="pack_rows",
    )(caches, new)


def _pad_to(a, axis, n):
    extra = n - a.shape[axis]
    if extra == 0:
        return a
    widths = [(0, 0)] * a.ndim
    widths[axis] = (0, extra)
    return jnp.pad(a, widths)


def _round_up(n, m):
    return (n + m - 1) // m * m


class _Group:
    def __init__(self, x, past):
        self.b, self.t, self.d = x.shape
        self.past = past
        self.flat = past > 0
        self.s_valid = past + self.t
        if self.flat:
            self.tq = _round_up(self.t, LANES)
            self.s_pad = _round_up(self.s_valid, PLANE_ROWS)
            self.fox_bq = self.dsa_bq = self.tq
            self.bk = self.s_pad
        else:
            self.tq = self.s_pad = self.t
            self.fox_bq = _pick(self.t, 512)
            self.dsa_bq = _pick(self.t, 256)
            self.bk = _pick(self.t, 512)

    def proj_view(self, x):
        return x.reshape(1, self.b * self.t, self.d) if self.flat else x

    def proj_bm(self):
        return self.b * self.t if self.flat else _pick(self.t, 512)

    def rows(self, a):
        return a.reshape(self.b, self.t, a.shape[-1])

    def heads_t(self, a):
        if not self.flat:
            return a
        return a.reshape(a.shape[1], a.shape[2], self.b, self.t).transpose(2, 0, 1, 3)

    def pad_q(self, a):
        return _pad_to(a, a.ndim - 1, self.tq)

    def keys(self, new_rm, cache):
        return new_rm if cache is None else _pack_rows(cache[0], cache[1], new_rm, self.s_pad)

    def values(self, new, cache):
        return new if cache is None else self.keys(self.rows(new), cache)


def _qkvg_plan(width, dh, v_rows):
    plan = [(0, 0, width, dh ** -0.5 * LOG2E, [(0, "headT")]),
            (0, width, width, 1.0, [(1, "row"), (3, "stack")]),
            (0, 2 * width, width, 1.0, [(2, "row" if v_rows else "headT"), (4, "stack")]),
            (0, 3 * width, width, 1.0, [(5, "row")])]
    outs = [("headT", width, BF16), ("row", width, BF16), ("row" if v_rows else "headT", width, BF16),
            ("stack", width, F32), ("stack", width, F32), ("row", width, BF16)]
    return plan, outs


def _mixer_a(grp, x, w_main, w_f, b_f, cache, layer, stacked):
    nh = b_f.shape[0]
    da = w_main.shape[1] // 4
    plan, outs = _qkvg_plan(da, da // nh, grp.flat)
    plan = plan + [(1, 0, LANES, 1.0, [(6, "row")])]
    outs = outs + [("row", LANES, F32)]
    qt, k_rm, vt, k, v, g, fl = _inproj(grp.proj_view(x), [w_main, w_f], plan, outs, grp.proj_bm(), layer, stacked)
    qt, k_rm, g = grp.pad_q(grp.heads_t(qt)), grp.rows(k_rm), grp.rows(g)
    z = grp.rows(fl)[:, :, :nh].transpose(0, 2, 1)
    ck_, cv_ = (None, None) if cache is None else cache[:2]
    if cache is not None:
        z = _pad_to(jnp.concatenate([cache[2].astype(F32).transpose(0, 2, 1), z], axis=2), 2, grp.s_pad)
    lf_all, c_all = _logf_cumsum(z, b_f, grp.past, grp.s_valid)
    logf = lf_all[:, :, grp.past:grp.s_valid].transpose(0, 2, 1)
    cq = grp.pad_q(c_all[:, :, grp.past:grp.s_valid])
    o = _fox_attention(qt, grp.keys(k_rm, ck_), grp.values(vt, cv_), cq, c_all,
                       grp.past, grp.fox_bq, grp.bk)[:, :grp.t]
    return o, g, ([k, v], logf)


def _mixer_b(grp, x, w_in, conv_w, conv_b, w_rg, b_rg, w_ig, b_ig, lam, state):
    dr = w_in.shape[1] // 2
    plan = [(0, 0, dr, 1.0, [(0, "row")]), (0, dr, dr, 1.0, [(1, "row")])]
    u, g = _inproj(grp.proj_view(x), [w_in], plan, [("row", dr, F32), ("row", dr, BF16)], grp.proj_bm())
    u, g = grp.rows(u), grp.rows(g)
    cw = conv_w.shape[0]
    if state is None:
        buf0 = jnp.zeros((grp.b, SUBLANES, dr), F32)
        h0 = jnp.zeros((grp.b, 1, dr), F32)
    else:
        buf, h0 = state
        buf0 = jnp.pad(buf.astype(F32), ((0, 0), (SUBLANES - (cw - 1), 0), (0, 0)))
        h0 = h0.astype(F32).reshape(grp.b, 1, dr)
    h, tail, hl = _rglru(u, conv_w, conv_b, w_rg, b_rg, w_ig, b_ig, lam, buf0, h0, _pick(grp.t, 256))
    return h, g, (tail[:, SUBLANES - (cw - 1):], hl[:, 0])


def _mixer_c(grp, x, w_main, w_idx, nh, nh_idx, d_idx, cache, layer, stacked):
    dc = w_main.shape[1] // 4
    wq = nh_idx * d_idx
    plan, outs = _qkvg_plan(dc, dc // nh, grp.flat)
    plan = plan + [(1, 0, wq, 1.0, [(6, "row")]), (1, wq, LANES, 1.0, [(7, "row")])]
    outs = outs + [("row", wq, BF16), ("row", LANES, F32)]
    qt, k_rm, vt, k, v, g, qi, kw = _inproj(grp.proj_view(x), [w_main, w_idx], plan, outs, grp.proj_bm(), layer,
                                            stacked)
    qt, k_rm = grp.pad_q(grp.heads_t(qt)), grp.rows(k_rm)
    g, qi, kw = grp.rows(g), grp.rows(qi), grp.rows(kw)
    ki = kw[:, :, :d_idx]
    wit = grp.pad_q(kw[:, :, d_idx:d_idx + nh_idx].transpose(0, 2, 1))
    qit = grp.pad_q(qi.reshape(grp.b, grp.t, nh_idx, d_idx).transpose(0, 2, 3, 1))
    ck_, cv_ = (None, None) if cache is None else cache[:2]
    ki_all = ki if cache is None else _pad_to(jnp.concatenate([cache[2].astype(F32), ki], axis=1), 1, grp.s_pad)
    topk = min(TOPK_MAX, grp.s_valid // 4)
    mask = _dsa_select(qit, wit, ki_all.astype(BF16), grp.past, grp.s_valid, topk, grp.dsa_bq, grp.bk)
    o = _dsa_attention(qt, grp.keys(k_rm, ck_), grp.values(vt, cv_), mask, grp.past, grp.fox_bq, grp.bk)[:, :grp.t]
    return o, g, ([k, v], ki)


def _run_trunk(x, p, caches, past):
    depth = p["ln_g"].shape[0]
    alpha = (2 * depth) ** 0.25
    grp = _Group(x, past)
    n_a, n_c = (depth + 2) // 3, depth // 3
    kv_a, kv_c = None, None
    new_a, new_b, new_c = [], [], []
    for i in range(depth):
        j, kind = i // 3, i % 3
        if kind == 0:
            cache = None if caches is None else ((caches["a_k"], j), (caches["a_v"], j), caches["a_logf"][j])
            o, g, (kv_a, logf) = _mixer_a(grp, x, p["w_main_a"][j], p["w_f_a"][j], p["b_f_a"][j], cache,
                                          (j, n_a), kv_a)
            new_a.append(logf)
            w_out = p["w_out_a"][j]
        elif kind == 1:
            state = None if caches is None else (caches["b_conv"][j], caches["b_h"][j])
            o, g, st = _mixer_b(grp, x, p["w_in_b"][j], p["conv_w_b"][j], p["conv_b_b"][j], p["w_rg_b"][j],
                                p["b_rg_b"][j], p["w_ig_b"][j], p["b_ig_b"][j], p["lam_b"][j], state)
            new_b.append(st)
            w_out = p["w_out_b"][j]
        else:
            cache = None if caches is None else ((caches["c_k"], j), (caches["c_v"], j), caches["c_kidx"][j])
            o, g, (kv_c, ki) = _mixer_c(grp, x, p["w_main_c"][j], p["w_idx_c"][j], p["h_c"], p["h_idx"], p["d_idx"],
                                        cache, (j, n_c), kv_c)
            new_c.append(ki)
            w_out = p["w_out_c"][j]
        m = grp.b * grp.t
        x = _outproj_ln(o.reshape(m, -1), g.reshape(m, -1), x.reshape(m, grp.d), w_out,
                        p["ln_g"][i], p["ln_b"][i], alpha).reshape(grp.b, grp.t, grp.d)
    stack = lambda sts, n: jnp.stack([s[n] for s in sts])
    heads = lambda a, nh: a.reshape(a.shape[0], grp.b, grp.t, nh, a.shape[-1] * a.shape[-2] // nh)
    h_a = p["b_f_a"].shape[1]
    return (x, heads(kv_a[0], h_a), heads(kv_a[1], h_a), jnp.stack(new_a), stack(new_b, 0), stack(new_b, 1),
            heads(kv_c[0], p["h_c"]), heads(kv_c[1], p["h_c"]), jnp.stack(new_c))


def kernel(x_prompt, x_sample, cache_a_k, cache_a_v, cache_a_logf, state_b_conv, state_b_h, cache_c_k, cache_c_v, cache_c_kidx, w_in_a, b_f_a, w_out_a, w_in_b, conv_w_b, conv_b_b, w_rg_b, b_rg_b, w_ig_b, b_ig_b, lam_b, w_out_b, w_in_c, w_out_c, ln_g, ln_b):
    h_a = b_f_a.shape[1]
    d_a = w_out_a.shape[1]
    d_c = w_out_c.shape[1]
    h_c = cache_c_k.shape[3]
    d_idx = cache_c_kidx.shape[-1]
    h_idx = (w_in_c.shape[2] - 4 * d_c - d_idx) // (d_idx + 1)
    assert d_a // h_a == LANES and d_c // h_c == LANES, "head width must equal the lane count"
    assert w_in_a.shape[2] == 4 * d_a + h_a and d_idx + h_idx <= LANES
    past = cache_a_k.shape[2]
    assert past % CHUNK == 0 and past > 0

    w_idx = w_in_c[:, :, 4 * d_c:]
    w_idx = _pad_to(w_idx, 2, h_idx * d_idx + LANES)
    p = {"w_main_a": w_in_a[:, :, :4 * d_a].astype(BF16),
         "w_f_a": _pad_to(w_in_a[:, :, 4 * d_a:], 2, LANES).astype(BF16),
         "b_f_a": b_f_a, "w_out_a": w_out_a.astype(BF16),
         "w_in_b": w_in_b.astype(BF16), "conv_w_b": conv_w_b, "conv_b_b": conv_b_b, "w_rg_b": w_rg_b,
         "b_rg_b": b_rg_b, "w_ig_b": w_ig_b, "b_ig_b": b_ig_b, "lam_b": lam_b, "w_out_b": w_out_b.astype(BF16),
         "w_main_c": w_in_c[:, :, :4 * d_c].astype(BF16), "w_idx_c": w_idx.astype(BF16),
         "w_out_c": w_out_c.astype(BF16), "ln_g": ln_g, "ln_b": ln_b,
         "h_c": h_c, "h_idx": h_idx, "d_idx": d_idx}
    caches = {"a_k": cache_a_k, "a_v": cache_a_v, "a_logf": cache_a_logf, "b_conv": state_b_conv,
              "b_h": state_b_h, "c_k": cache_c_k, "c_v": cache_c_v, "c_kidx": cache_c_kidx}
    outs_p = _run_trunk(x_prompt, p, None, 0)
    outs_s = _run_trunk(x_sample, p, caches, past)
    return (outs_p[0], outs_s[0]) + outs_p[1:] + outs_s[1:]
```

```python
import functools
import math

import jax
import jax.numpy as jnp
import numpy as np
from jax import lax
from jax.experimental import pallas as pl
from jax.experimental.pallas import tpu as pltpu

NEG_INF = -1e30
LN_EPS = 1e-5
CHUNK = 64
CHUNK_SHIFT = 6
TOPK_MAX = 256
LRU_C = 8.0
LANES = 128
SUBLANES = 8
BF16_ROWS = 16
PLANE_ROWS = 32 * SUBLANES
INT_MIN = -(2 ** 31)
VMEM_LIMIT = 56 * 1024 * 1024
LOG2E = math.log2(math.e)
SKIP_BITS = 60.0

F32 = jnp.float32
BF16 = jnp.bfloat16


def _cparams(sem, flags=None):
    return pltpu.CompilerParams(dimension_semantics=sem, vmem_limit_bytes=VMEM_LIMIT, flags=flags)


def _pick(n, pref):
    if n <= pref:
        return n
    b = pref
    while n % b:
        b //= 2
    return b


def _inproj_kernel(x_ref, *refs, n_w, n_alias, layer, plan):
    w_refs, out_refs = refs[:n_w], refs[n_w + n_alias:]
    xb = x_ref[0].astype(BF16)
    for w_idx, c0, width, scale, outs in plan:
        r = jnp.dot(xb, w_refs[w_idx][:, c0:c0 + width], preferred_element_type=F32)
        if scale != 1.0:
            r = r * scale
        for o_idx, kind in outs:
            o = out_refs[o_idx]
            if kind == "row":
                o[0] = r.astype(o.dtype)
            elif kind == "norm":
                rr = jnp.square(r.astype(BF16).astype(F32))
                o[0] = jnp.sqrt(jnp.concatenate(
                    [jnp.sum(rr[:, h * LANES:(h + 1) * LANES], axis=1, keepdims=True) for h in range(width // LANES)],
                    axis=1))
            elif kind == "stack":
                val = r.astype(o.dtype).reshape(o.shape[2:])
                if o.shape[0] == 1:
                    o[0, 0] = val
                else:
                    for l in range(o.shape[0]):
                        o[l, 0] = val if l == layer else jnp.zeros_like(val)
            else:
                for h in range(width // LANES):
                    o[0, h] = r[:, h * LANES:(h + 1) * LANES].T.astype(o.dtype)


def _inproj(x3, ws, plan, out_defs, bm, layer=(0, 1), stacked=None):
    bx, tx, d = x3.shape
    j, n_layers = layer
    grid = (bx, tx // bm)
    in_specs = [pl.BlockSpec((1, bm, d), lambda b, i: (b, i, 0))]
    for w in ws:
        in_specs.append(pl.BlockSpec(w.shape, lambda b, i: (0, 0)))
    stacked = list(stacked or [])
    in_specs += [pl.BlockSpec(memory_space=pl.ANY)] * len(stacked)
    out_shape, out_specs, aliases = [], [], {}
    for o_idx, (kind, width, dt) in enumerate(out_defs):
        if kind == "row":
            out_shape.append(jax.ShapeDtypeStruct((bx, tx, width), dt))
            out_specs.append(pl.BlockSpec((1, bm, width), lambda b, i: (b, i, 0)))
        elif kind == "norm":
            out_shape.append(jax.ShapeDtypeStruct((bx, tx, width // LANES), dt))
            out_specs.append(pl.BlockSpec((1, bm, width // LANES), lambda b, i: (b, i, 0)))
        elif kind == "stack":
            out_shape.append(jax.ShapeDtypeStruct((n_layers, bx, tx, width // LANES, LANES), dt))
            if stacked:
                out_specs.append(pl.BlockSpec((1, 1, bm, width // LANES, LANES), lambda b, i: (j, b, i, 0, 0)))
                aliases[1 + len(ws) + len(aliases)] = o_idx
            else:
                out_specs.append(pl.BlockSpec((n_layers, 1, bm, width // LANES, LANES), lambda b, i: (0, b, i, 0, 0)))
        else:
            nh = width // LANES
            out_shape.append(jax.ShapeDtypeStruct((bx, nh, LANES, tx), dt))
            out_specs.append(pl.BlockSpec((1, nh, LANES, bm), lambda b, i: (b, 0, 0, i)))
    assert len(aliases) == len(stacked)
    return pl.pallas_call(
        functools.partial(_inproj_kernel, n_w=len(ws), n_alias=len(stacked), layer=j, plan=tuple(plan)),
        out_shape=out_shape, grid=grid, in_specs=in_specs, out_specs=out_specs, input_output_aliases=aliases,
        compiler_params=_cparams(("parallel", "parallel")), name="inproj",
    )(x3, *ws, *stacked)


def _outproj_ln_kernel(o_ref, g_ref, x_ref, w_ref, lg_ref, lb_ref, y_ref, *, alpha):
    g = g_ref[...].astype(F32)
    og = (o_ref[...].astype(F32) * (g * jax.nn.sigmoid(g))).astype(BF16)
    y = jnp.dot(og, w_ref[...], preferred_element_type=F32)
    z = alpha * x_ref[...] + y
    mu = jnp.mean(z, axis=-1, keepdims=True)
    zc = z - mu
    var = jnp.mean(zc * zc, axis=-1, keepdims=True)
    y_ref[...] = zc * lax.rsqrt(var + LN_EPS) * lg_ref[...] + lb_ref[...]


def _outproj_ln(o2, g2, x2, w, ln_g, ln_b, alpha):
    m, d = x2.shape
    dk = o2.shape[1]
    bm = _pick(m, 512)
    row = lambda i: (i, 0)
    fixed = lambda i: (0, 0)
    return pl.pallas_call(
        functools.partial(_outproj_ln_kernel, alpha=alpha),
        out_shape=jax.ShapeDtypeStruct((m, d), F32), grid=(m // bm,),
        in_specs=[pl.BlockSpec((bm, dk), row), pl.BlockSpec((bm, dk), row), pl.BlockSpec((bm, d), row),
                  pl.BlockSpec((dk, d), fixed), pl.BlockSpec((1, d), fixed), pl.BlockSpec((1, d), fixed)],
        out_specs=pl.BlockSpec((bm, d), row),
        compiler_params=_cparams(("parallel",)), name="outproj_ln",
    )(o2, g2, x2, w, ln_g.reshape(1, d), ln_b.reshape(1, d))


def _log_sigmoid(x):
    return -(jnp.maximum(-x, 0.0) + jnp.log1p(jnp.exp(-jnp.abs(x))))


def _logf_cumsum_kernel(z_ref, bf_ref, lf_ref, c_ref, *, p0, p1):
    z = z_ref[0]
    pos = lax.broadcasted_iota(jnp.int32, z.shape, 1)
    is_new = jnp.logical_and(pos >= p0, pos < p1)
    lf = jnp.where(is_new, _log_sigmoid(z + bf_ref[...]), z)
    lf_ref[0] = lf
    c = lf
    s = 1
    while s < z.shape[1]:
        c = c + jnp.where(pos >= s, pltpu.roll(c, s, axis=1), 0.0)
        s *= 2
    c_ref[0] = c


def _logf_cumsum(z, b_f, p0, p1):
    b, h, l = z.shape
    blk = pl.BlockSpec((1, h, l), lambda i: (i, 0, 0))
    return pl.pallas_call(
        functools.partial(_logf_cumsum_kernel, p0=p0, p1=p1),
        out_shape=[jax.ShapeDtypeStruct(z.shape, F32)] * 2, grid=(b,),
        in_specs=[blk, pl.BlockSpec((h, 1), lambda i: (0, 0))], out_specs=[blk, blk],
        compiler_params=_cparams(("parallel",)), name="logf_cumsum",
    )(z, b_f.reshape(h, 1))


def _score_store(u, slot, s_sc, mc_sc):
    s_sc[slot] = u
    mc_sc[slot] = jnp.broadcast_to(jnp.max(u, axis=0, keepdims=True), mc_sc.shape[1:])


def _softmax_update(h, slot, vt, s_sc, mc_sc, m_sc, acc_sc, row_shift=None):
    u = s_sc[slot]
    m_prev = m_sc[h]
    m_cur = mc_sc[slot]
    if row_shift is not None:
        m_cur = m_cur + row_shift
    m_new = jnp.maximum(m_prev, m_cur)
    m_row = m_new[0:1]
    p = jnp.exp2(u - (m_row if row_shift is None else m_row - row_shift))
    alpha = jnp.exp2(m_prev - m_new)
    vt1 = jnp.concatenate([vt, jnp.ones((BF16_ROWS, vt.shape[1]), BF16)], axis=0)
    acc_sc[h] = acc_sc[h] * alpha[0:1] + jnp.dot(vt1, p.astype(BF16), preferred_element_type=F32)
    m_sc[h] = m_new


def _head(k_ref, h):
    dh = LANES
    return k_ref[0, :, h * dh:(h + 1) * dh]


def _head_t(v_ref, h, v_rows):
    if not v_rows:
        return v_ref[0, h]
    return _head(v_ref, h).astype(F32).T.astype(BF16)


def _scores(k, kx, qt, qx):
    return jnp.dot(jnp.concatenate([k, kx], axis=1), jnp.concatenate([qt, qx], axis=0),
                   preferred_element_type=F32)


def _pipelined_heads(nh, qk_phase, sm_phase):
    qk_phase(0, 0)
    for h in range(nh - 1):
        qk_phase(h + 1, (h + 1) % 2)
        sm_phase(h, h % 2)
    sm_phase(nh - 1, (nh - 1) % 2)


def _attn_init(m_sc, acc_sc):
    m_sc[...] = jnp.full(m_sc.shape, -jnp.inf, F32)
    acc_sc[...] = jnp.zeros(acc_sc.shape, F32)


def _attn_finish(o_ref, acc_sc, nh, dh):
    for h in range(nh):
        a = acc_sc[h]
        o_ref[0, :, h * dh:(h + 1) * dh] = (a[0:dh] / a[dh:dh + 1]).T.astype(o_ref.dtype)


def _attn_scratch(nh, bq, bk, dh):
    return [pltpu.VMEM((nh, SUBLANES, bq), F32), pltpu.VMEM((nh, dh + BF16_ROWS, bq), F32),
            pltpu.VMEM((2, bk, bq), F32), pltpu.VMEM((2, SUBLANES, bq), F32)]


def _fox_kernel(lo_ref, qt_ref, k_ref, v_ref, cq_ref, kx_ref, qx_ref, o_ref, m_sc, acc_sc, s_sc, mc_sc,
                *, bq, bk, q_off, nh, v_rows):
    qi, ki = pl.program_id(1), pl.program_id(2)
    q_first = q_off + qi * bq
    q_last = q_first + bq - 1
    k_first = ki * bk
    k_last = k_first + bk - 1

    @pl.when(ki == 0)
    def _():
        _attn_init(m_sc, acc_sc)

    def step(masked):
        kx = kx_ref[0]
        if masked:
            kpos = k_first + lax.broadcasted_iota(jnp.int32, (bk, bq), 0)
            qpos = q_first + lax.broadcasted_iota(jnp.int32, (bk, bq), 1)
            causal = kpos <= qpos

        def qk_phase(h, slot):
            u = _scores(_head(k_ref, h), kx, qt_ref[0, h], qx_ref[h])
            if masked:
                u = jnp.where(causal, u, NEG_INF)
            _score_store(u, slot, s_sc, mc_sc)

        def sm_phase(h, slot):
            cq2 = cq_ref[0, pl.ds(h, 1), :] * LOG2E
            _softmax_update(h, slot, _head_t(v_ref, h, v_rows), s_sc, mc_sc, m_sc, acc_sc, row_shift=cq2)

        _pipelined_heads(nh, qk_phase, sm_phase)

    needed = jnp.logical_and(k_first <= q_last, ki >= lo_ref[pl.program_id(0), qi])
    straddles = k_last > q_first

    @pl.when(jnp.logical_and(needed, straddles))
    def _():
        step(True)

    @pl.when(jnp.logical_and(needed, jnp.logical_not(straddles)))
    def _():
        step(False)

    @pl.when(ki == pl.num_programs(2) - 1)
    def _():
        _attn_finish(o_ref, acc_sc, nh, LANES)


def _split3(x):
    def top_half(v):
        bits = lax.bitcast_convert_type(v, jnp.uint32) & jnp.uint32(0xFFFF0000)
        return lax.bitcast_convert_type(bits, F32)

    hi = top_half(x)
    r1 = x - hi
    mid = top_half(r1)
    return hi.astype(BF16), mid.astype(BF16), (r1 - mid).astype(BF16)


def _v_spec(v, nh, dh, bk, kmap):
    if v.ndim == 3:
        return True, pl.BlockSpec((1, bk, nh * dh), lambda *ix: (ix[0], kmap(*ix), 0))
    return False, pl.BlockSpec((1, nh, dh, bk), lambda *ix: (ix[0], 0, 0, kmap(*ix)))


def _fox_first_block(qn_rows, kn_rows, cq, ck, q_off, bq, bk):
    b, nh, tq = cq.shape
    nq, nk = tq // bq, ck.shape[2] // bk
    if nk == 1 or qn_rows is None:
        return jnp.zeros((b, nq), jnp.int32)
    qn = qn_rows.reshape(b, nq, bq, nh).max(axis=2).transpose(0, 2, 1)
    kn = kn_rows.reshape(b, nk, bk, nh).max(axis=2).transpose(0, 2, 1)
    first_self = [(q_off + qi * bq) // bk for qi in range(nq)]
    last_self = [min((q_off + (qi + 1) * bq - 1) // bk, nk - 1) for qi in range(nq)]
    kn_self = jnp.stack([kn[:, :, f:l + 1].max(axis=-1) for f, l in zip(first_self, last_self)], axis=-1)
    c_first = cq[:, :, ::bq]
    c_last = ck[:, :, bk - 1::bk]
    bound = (qn[..., :, None] * (kn[..., None, :] + kn_self[..., :, None])
             + (c_first[..., :, None] - c_last[..., None, :]) * LOG2E)
    skip = jnp.all(bound < -SKIP_BITS, axis=1)
    lo = jnp.sum(jnp.cumprod(skip.astype(jnp.int32), axis=-1), axis=-1)
    return jnp.minimum(lo, jnp.asarray(first_self, jnp.int32)[None, :]).astype(jnp.int32)


def _fox_attention(qt_hm, k_rm, v, cq, ck, q_off, bq, bk, qn_rows=None, kn_rows=None):
    b, nh, dh, tq = qt_hm.shape
    s = k_rm.shape[1]
    nq, nk = tq // bq, s // bk
    w = LANES // nh
    assert w >= 3
    kx = jnp.stack(_split3(ck * (-LOG2E)), axis=-1)
    kx = _pad_to(kx, 3, w).transpose(0, 2, 1, 3).reshape(b, s, LANES)
    rows = jnp.arange(LANES)[None, :, None]
    heads = jnp.arange(nh)[:, None, None]
    qx = jnp.broadcast_to(jnp.logical_and(rows >= heads * w, rows < heads * w + 3), (nh, LANES, bq)).astype(BF16)

    lo = _fox_first_block(qn_rows, kn_rows, cq, ck, q_off, bq, bk)

    def kmap(bb, qi, ki, lo_ref):
        return jnp.clip(ki, lo_ref[bb, qi], (q_off + (qi + 1) * bq - 1) // bk)

    v_rows, v_spec = _v_spec(v, nh, dh, bk, kmap)
    grid_spec = pltpu.PrefetchScalarGridSpec(
        num_scalar_prefetch=1, grid=(b, nq, nk),
        in_specs=[pl.BlockSpec((1, nh, dh, bq), lambda bb, qi, ki, lo_ref: (bb, 0, 0, qi)),
                  pl.BlockSpec((1, bk, nh * dh), lambda *ix: (ix[0], kmap(*ix), 0)),
                  v_spec,
                  pl.BlockSpec((1, nh, bq), lambda bb, qi, ki, lo_ref: (bb, 0, qi)),
                  pl.BlockSpec((1, bk, LANES), lambda *ix: (ix[0], kmap(*ix), 0)),
                  pl.BlockSpec((nh, LANES, bq), lambda bb, qi, ki, lo_ref: (0, 0, 0))],
        out_specs=pl.BlockSpec((1, bq, nh * dh), lambda bb, qi, ki, lo_ref: (bb, qi, 0)),
        scratch_shapes=_attn_scratch(nh, bq, bk, dh))
    return pl.pallas_call(
        functools.partial(_fox_kernel, bq=bq, bk=bk, q_off=q_off, nh=nh, v_rows=v_rows),
        out_shape=jax.ShapeDtypeStruct((b, tq, nh * dh), BF16), grid_spec=grid_spec,
        compiler_params=_cparams(("parallel", "parallel", "arbitrary")), name="fox_attention",
    )(lo, qt_hm, k_rm, v, cq, kx, qx)


def _bit_planes(words):
    a = list(words)
    j, m = 16, 0x0000FFFF
    while j:
        k = 0
        while k < 32:
            t = (a[k] ^ lax.shift_right_logical(a[k + j], jnp.int32(j))) & jnp.int32(m)
            a[k] = a[k] ^ t
            a[k + j] = a[k + j] ^ lax.shift_left(t, jnp.int32(j))
            k = (k + j + 1) & ~j
        j >>= 1
        m = (m ^ (m << j)) & 0xFFFFFFFF
    return a


def _dsa_select_kernel(qit_ref, wit_ref, ki_ref, out_ref, key_sc, pl_sc, act_sc, j_sc,
                       *, bq, kb, q_off, s_valid, topk, nh_idx, idx_bits):
    qb = pl.program_id(1)
    q_first = q_off + qb * bq
    q_last = q_first + bq - 1
    n_adm = jnp.minimum(((q_last >> CHUNK_SHIFT) + 1) * CHUNK, s_valid)
    nkb = (jnp.maximum(n_adm, topk) + kb - 1) // kb
    nkb_total = key_sc.shape[0] // kb
    gpb = kb // PLANE_ROWS
    nkb4 = nkb // 4
    qchunk = (q_first + lax.broadcasted_iota(jnp.int32, (kb, bq), 1)) >> CHUNK_SHIFT
    row = lax.broadcasted_iota(jnp.int32, (kb, bq), 0)

    def admissible(kpos):
        return jnp.logical_and((kpos >> CHUNK_SHIFT) <= qchunk, kpos < s_valid)

    def rows(kblk):
        return pl.ds(pl.multiple_of(kblk * kb, kb), kb)

    def score_block(kblk, carry, all_admissible):
        kblock = ki_ref[0, rows(kblk), :]
        sc = jnp.zeros((kb, bq), F32)
        for h in range(nh_idx):
            d = jnp.dot(kblock, qit_ref[0, h], preferred_element_type=F32)
            sc = sc + wit_ref[0, h:h + 1, :] * jnp.maximum(d, 0.0)
        sc = sc + 0.0
        if not all_admissible:
            sc = jnp.where(admissible(kblk * kb + row), sc, NEG_INF)
        bits = lax.bitcast_convert_type(sc, jnp.int32)
        key = bits ^ ((bits >> 31) & 0x7FFFFFFF)
        key_sc[rows(kblk), :] = key
        for g in range(gpb):
            base = g * PLANE_ROWS
            planes = _bit_planes([key[base + SUBLANES * j:base + SUBLANES * (j + 1), :] for j in range(32)])
            planes[0] = ~planes[0]
            for i in range(32):
                pl_sc[kblk * gpb + g, i] = planes[i]
            act_sc[kblk * gpb + g] = jnp.full((SUBLANES, bq), -1, jnp.int32)
        return carry

    n_full = jnp.minimum(((q_first >> CHUNK_SHIFT) + 1) * CHUNK, s_valid) // kb
    lax.fori_loop(0, n_full, functools.partial(score_block, all_admissible=True), 0)
    lax.fori_loop(n_full, nkb, functools.partial(score_block, all_admissible=False), 0)

    def two_bit_step(it, carry):
        thr_u, above = carry
        i1, i0 = 2 * it, 2 * it + 1

        def split(g):
            a = act_sc[g]
            a1 = a & pl_sc[g, i1]
            x11 = a1 & pl_sc[g, i0]
            return a, x11, a1 ^ x11, (a & pl_sc[g, i0]) ^ x11

        def cnt_groups(g0, n, accs):
            c11, c10, c01 = accs
            for g in range(n):
                _, x11, x10, x01 = split(g0 + g)
                c11 = c11 + lax.population_count(x11)
                c10 = c10 + lax.population_count(x10)
                c01 = c01 + lax.population_count(x01)
            return c11, c10, c01

        zero = jnp.zeros((SUBLANES, bq), jnp.int32)
        accs = lax.fori_loop(0, nkb4, lambda i, c: cnt_groups(i * gpb * 4, gpb * 4, c), (zero, zero, zero))
        accs = lax.fori_loop(nkb4 * 4, nkb, lambda i, c: cnt_groups(i * gpb, gpb, c), accs)
        c11, c10, c01 = [jnp.sum(c.astype(F32), axis=0, keepdims=True) for c in accs]
        s1 = above + c11
        s2 = s1 + c10
        s3 = s2 + c01
        t11, t10, t01 = s1 >= topk, s2 >= topk, s3 >= topk

        def upd_groups(g0, n, carry2):
            for g in range(n):
                a, x11, x10, x01 = split(g0 + g)
                x00 = a ^ x11 ^ x10 ^ x01
                act_sc[g0 + g] = jnp.where(t11, x11, jnp.where(t10, x10, jnp.where(t01, x01, x00)))
            return carry2

        lax.fori_loop(0, nkb4, lambda i, c2: upd_groups(i * gpb * 4, gpb * 4, c2), 0)
        lax.fori_loop(nkb4 * 4, nkb, lambda i, c2: upd_groups(i * gpb, gpb, c2), 0)
        hi = lax.shift_left(jnp.int32(1), 31 - i1)
        lo = lax.shift_left(jnp.int32(1), 31 - i0)
        low_bit = jnp.where(t11, lo, jnp.where(t10, 0, jnp.where(t01, lo, 0)))
        return (thr_u | jnp.where(t10, hi, 0) | low_bit,
                jnp.where(t11, above, jnp.where(t10, s1, jnp.where(t01, s2, s3))))

    thr_u, n_gt = lax.fori_loop(0, 16, two_bit_step, (jnp.zeros((1, bq), jnp.int32), jnp.zeros((1, bq), F32)))
    thr = thr_u ^ INT_MIN

    def eq_blk(kblk, acc):
        for g in range(gpb):
            acc = acc + lax.population_count(act_sc[kblk * gpb + g])
        return acc

    n_eq = jnp.sum(lax.fori_loop(0, nkb, eq_blk, jnp.zeros((SUBLANES, bq), jnp.int32)).astype(F32),
                   axis=0, keepdims=True)
    need = topk - n_gt
    has_excess = jnp.max(n_eq - need) > 0.0
    j_sc[...] = jnp.full(j_sc.shape, 2 ** 30, jnp.int32)

    @pl.when(has_excess)
    def _():
        def count_ties_below(cand):
            def blk(kblk, acc):
                hit = jnp.logical_and(key_sc[rows(kblk), :] == thr, kblk * kb + row < cand)
                ind = jnp.where(hit, 1.0, 0.0)
                for r in range(kb // SUBLANES):
                    acc = acc + ind[r * SUBLANES:(r + 1) * SUBLANES]
                return acc
            return jnp.sum(lax.fori_loop(0, nkb, blk, jnp.zeros((SUBLANES, bq), F32)), axis=0, keepdims=True)

        def idx_step(it, lo):
            cand = lo | lax.shift_left(jnp.int32(1), idx_bits - 1 - it)
            return jnp.where(count_ties_below(cand) < need, cand, lo)

        lo = lax.fori_loop(0, idx_bits, idx_step, jnp.zeros((1, bq), jnp.int32))
        j_sc[...] = jnp.broadcast_to(lo, j_sc.shape)

    j_cut = j_sc[0:1, :]

    def write_fast(kblk, carry):
        out_ref[0, 0, rows(kblk), :] = jnp.where(key_sc[rows(kblk), :] >= thr, 0.0, NEG_INF).astype(out_ref.dtype)
        return carry

    def write_block(kblk, carry):
        key = key_sc[rows(kblk), :]
        kpos = kblk * kb + row
        tie = jnp.logical_and(key == thr, kpos <= j_cut)
        sel = jnp.logical_and(jnp.logical_or(key > thr, tie), admissible(kpos))
        out_ref[0, 0, rows(kblk), :] = jnp.where(sel, 0.0, NEG_INF).astype(out_ref.dtype)
        return carry

    def fill_block(kblk, carry):
        out_ref[0, 0, rows(kblk), :] = jnp.full((kb, bq), NEG_INF, out_ref.dtype)
        return carry

    n_fast = jnp.where(has_excess, 0, n_full)
    lax.fori_loop(0, n_fast, write_fast, 0)
    lax.fori_loop(n_fast, nkb, write_block, 0)
    lax.fori_loop(nkb, nkb_total, fill_block, 0)


def _dsa_select(qit_hm, wit, ki, q_off, s_valid, topk, bq, kb):
    b, nh_idx, d_idx, tq = qit_hm.shape
    s_pad = ki.shape[1]
    return pl.pallas_call(
        functools.partial(_dsa_select_kernel, bq=bq, kb=kb, q_off=q_off, s_valid=s_valid, topk=topk,
                          nh_idx=nh_idx, idx_bits=max(1, (s_pad - 1).bit_length())),
        out_shape=jax.ShapeDtypeStruct((b, tq // bq, s_pad, bq), BF16), grid=(b, tq // bq),
        in_specs=[pl.BlockSpec((1, nh_idx, d_idx, bq), lambda bb, qb: (bb, 0, 0, qb)),
                  pl.BlockSpec((1, nh_idx, bq), lambda bb, qb: (bb, 0, qb)),
                  pl.BlockSpec((1, s_pad, d_idx), lambda bb, qb: (bb, 0, 0))],
        out_specs=pl.BlockSpec((1, 1, s_pad, bq), lambda bb, qb: (bb, qb, 0, 0)),
        scratch_shapes=[pltpu.VMEM((s_pad, bq), jnp.int32),
                        pltpu.VMEM((s_pad // PLANE_ROWS, 32, SUBLANES, bq), jnp.int32),
                        pltpu.VMEM((s_pad // PLANE_ROWS, SUBLANES, bq), jnp.int32),
                        pltpu.VMEM((SUBLANES, bq), jnp.int32)],
        compiler_params=_cparams(("parallel", "arbitrary")), name="dsa_select",
    )(qit_hm, wit, ki)


def _dsa_attn_kernel(sl_ref, qt_ref, k_ref, v_ref, mb_ref, qx_ref, o_ref, m_sc, acc_sc, s_sc, mc_sc,
                     base_sc, t_sc, *, bq, bk, q_off, nh, v_rows):
    qi, ki = pl.program_id(1), pl.program_id(2)
    q_first = q_off + qi * bq
    q_last = q_first + bq - 1
    k_first = ki * bk
    k_last = k_first + bk - 1
    qposf = (q_first + lax.broadcasted_iota(jnp.int32, (1, bq), 1)).astype(F32)

    @pl.when(ki == 0)
    def _():
        _attn_init(m_sc, acc_sc)

    def step(past):
        base_sc[...] = jnp.concatenate([mb_ref[0, j] for j in range(mb_ref.shape[1])], axis=1).astype(F32)
        if past:
            kpos = k_first + lax.broadcasted_iota(jnp.int32, (bk, LANES), 0)
            lane = lax.broadcasted_iota(jnp.int32, (bk, LANES), 1)
            kx = jnp.where(lane < 3, kpos >> CHUNK_SHIFT, jnp.where(lane < 6, kpos & (CHUNK - 1), 0)).astype(BF16)
        else:
            kpos = k_first + lax.broadcasted_iota(jnp.int32, (bk, bq), 0)
            qpos = q_first + lax.broadcasted_iota(jnp.int32, (bk, bq), 1)
            t_sc[...] = jnp.minimum(kpos, 2 * qpos - kpos).astype(F32)

        def qk_phase(h, slot):
            if past:
                u = _scores(_head(k_ref, h), kx, qt_ref[0, h], qx_ref[h]) + base_sc[...]
            else:
                st = jnp.dot(_head(k_ref, h), qt_ref[0, h], preferred_element_type=F32)
                u = st + sl_ref[h] * t_sc[...] + base_sc[...]
            _score_store(u, slot, s_sc, mc_sc)

        def sm_phase(h, slot):
            _softmax_update(h, slot, _head_t(v_ref, h, v_rows), s_sc, mc_sc, m_sc, acc_sc,
                            row_shift=-sl_ref[h] * qposf)

        _pipelined_heads(nh, qk_phase, sm_phase)

    needed = k_first <= ((q_last >> CHUNK_SHIFT) << CHUNK_SHIFT) + CHUNK - 1
    past = k_last <= q_first

    @pl.when(past)
    def _():
        step(True)

    @pl.when(jnp.logical_and(needed, jnp.logical_not(past)))
    def _():
        step(False)

    @pl.when(ki == pl.num_programs(2) - 1)
    def _():
        _attn_finish(o_ref, acc_sc, nh, LANES)


def _dsa_attention(qt_hm, k_rm, v, mask, q_off, bq, bk):
    b, nh, dh, tq = qt_hm.shape
    s, mq = mask.shape[2], mask.shape[3]
    nq, nk, nsub = tq // bq, s // bk, bq // mq
    slopes2_np = (2.0 ** (-8.0 * np.arange(1, nh + 1) / nh) * LOG2E).astype(np.float32)
    slopes2 = jnp.asarray(slopes2_np)
    rest, pieces = slopes2_np, []
    for _ in range(3):
        piece = rest.astype(BF16).astype(np.float32)
        pieces.append(piece)
        rest = rest - piece
    col = np.zeros((nh, LANES), np.float32)
    col[:, 0:3] = np.stack(pieces, axis=1) * CHUNK
    col[:, 3:6] = np.stack(pieces, axis=1)
    qx = jnp.broadcast_to(jnp.asarray(col.astype(BF16))[:, :, None], (nh, LANES, bq))

    def kmap(bb, qi, ki):
        q_last = q_off + (qi + 1) * bq - 1
        return jnp.minimum(ki, (((q_last >> CHUNK_SHIFT) << CHUNK_SHIFT) + CHUNK - 1) // bk)

    v_rows, v_spec = _v_spec(v, nh, dh, bk, kmap)
    return pl.pallas_call(
        functools.partial(_dsa_attn_kernel, bq=bq, bk=bk, q_off=q_off, nh=nh, v_rows=v_rows),
        out_shape=jax.ShapeDtypeStruct((b, tq, nh * dh), BF16), grid=(b, nq, nk),
        in_specs=[pl.BlockSpec(memory_space=pltpu.SMEM),
                  pl.BlockSpec((1, nh, dh, bq), lambda bb, qi, ki: (bb, 0, 0, qi)),
                  pl.BlockSpec((1, bk, nh * dh), lambda bb, qi, ki: (bb, kmap(bb, qi, ki), 0)),
                  v_spec,
                  pl.BlockSpec((1, nsub, bk, mq), lambda bb, qi, ki: (bb, qi, kmap(bb, qi, ki), 0)),
                  pl.BlockSpec((nh, LANES, bq), lambda bb, qi, ki: (0, 0, 0))],
        out_specs=pl.BlockSpec((1, bq, nh * dh), lambda bb, qi, ki: (bb, qi, 0)),
        scratch_shapes=_attn_scratch(nh, bq, bk, dh) + [pltpu.VMEM((bk, bq), F32), pltpu.VMEM((bk, bq), F32)],
        compiler_params=_cparams(("parallel", "parallel", "arbitrary")), name="dsa_attention",
    )(slopes2, qt_hm, k_rm, v, mask, qx)


def _rglru_kernel(u_ref, cw_ref, cb_ref, wr_ref, br_ref, wig_ref, big_ref, lam_ref, buf0_ref, h0_ref,
                  h_ref, conv_ref, hl_ref, ubuf, a_sc, b_sc, hbuf, hcar, *, tt, conv_w, nblk):
    t = pl.program_id(1)
    pad = SUBLANES
    d = u_ref.shape[2]
    blk = d // nblk

    @pl.when(t == 0)
    def _():
        ubuf[0:pad, :] = buf0_ref[0]
        hcar[...] = jnp.broadcast_to(h0_ref[0], (SUBLANES, d))

    ubuf[pad:pad + tt, :] = u_ref[0]
    uc = cb_ref[...] + ubuf[pad:pad + tt, :] * cw_ref[conv_w - 1:conv_w, :]
    for j in range(conv_w - 1):
        off = pad - (conv_w - 1) + j
        uc = uc + ubuf[off:off + tt, :] * cw_ref[j:j + 1, :]

    lam = lam_ref[...]
    neg_sp = -LRU_C * (jnp.maximum(-lam, 0.0) + jnp.log1p(jnp.exp(-jnp.abs(lam))))
    for n in range(nblk):
        cs = slice(n * blk, (n + 1) * blk)
        ucn = uc[:, cs]
        ub = ucn.astype(BF16)
        r = jax.nn.sigmoid(jnp.dot(ub, wr_ref[n], preferred_element_type=F32) + br_ref[:, cs])
        i = jax.nn.sigmoid(jnp.dot(ub, wig_ref[n], preferred_element_type=F32) + big_ref[:, cs])
        log_a = r * neg_sp[:, cs]
        a = jnp.exp(log_a)
        a_sc[:, cs] = a
        b_sc[:, cs] = jnp.sqrt(-jnp.tanh(log_a) * (a * a + 1.0)) * (i * ucn)

    row = lax.broadcasted_iota(jnp.int32, (SUBLANES, d), 0)

    def group(gi, hprev):
        r0 = pl.multiple_of(gi * SUBLANES, SUBLANES)
        av = a_sc[pl.ds(r0, SUBLANES), :]
        bv = b_sc[pl.ds(r0, SUBLANES), :]
        s = 1
        while s < SUBLANES:
            a_sh = pltpu.roll(av, s, axis=0)
            b_sh = pltpu.roll(bv, s, axis=0)
            m = row >= s
            bv = jnp.where(m, av * b_sh + bv, bv)
            av = jnp.where(m, av * a_sh, av)
            s *= 2
        hrows = av * hprev + bv
        hbuf[pl.ds(r0, SUBLANES), :] = hrows
        return jnp.broadcast_to(hrows[SUBLANES - 1:SUBLANES, :], (SUBLANES, d))

    hlast = lax.fori_loop(0, tt // SUBLANES, group, hcar[...])
    h_ref[0] = hbuf[...].astype(h_ref.dtype)
    hcar[...] = hlast
    hl_ref[0] = hlast[0:1, :]
    tail = ubuf[tt:tt + pad, :]
    conv_ref[0] = tail
    ubuf[0:pad, :] = tail


def _rglru(u, conv_w, conv_b, w_rg, b_rg, w_ig, b_ig, lam, buf0, h0, tt):
    b, t, d = u.shape
    cw = conv_w.shape[0]
    nblk, blk = w_rg.shape[0], w_rg.shape[1]
    vec = lambda a: a.reshape(1, d)
    fixed2 = lambda bb, ti: (0, 0)
    fixed3 = lambda bb, ti: (0, 0, 0)
    perb = lambda bb, ti: (bb, 0, 0)
    return pl.pallas_call(
        functools.partial(_rglru_kernel, tt=tt, conv_w=cw, nblk=nblk),
        out_shape=[jax.ShapeDtypeStruct((b, t, d), BF16), jax.ShapeDtypeStruct((b, SUBLANES, d), F32),
                   jax.ShapeDtypeStruct((b, 1, d), F32)],
        grid=(b, t // tt),
        in_specs=[pl.BlockSpec((1, tt, d), lambda bb, ti: (bb, ti, 0)),
                  pl.BlockSpec((cw, d), fixed2), pl.BlockSpec((1, d), fixed2),
                  pl.BlockSpec((nblk, blk, blk), fixed3), pl.BlockSpec((1, d), fixed2),
                  pl.BlockSpec((nblk, blk, blk), fixed3), pl.BlockSpec((1, d), fixed2),
                  pl.BlockSpec((1, d), fixed2),
                  pl.BlockSpec((1, SUBLANES, d), perb), pl.BlockSpec((1, 1, d), perb)],
        out_specs=[pl.BlockSpec((1, tt, d), lambda bb, ti: (bb, ti, 0)),
                   pl.BlockSpec((1, SUBLANES, d), perb), pl.BlockSpec((1, 1, d), perb)],
        scratch_shapes=[pltpu.VMEM((SUBLANES + tt, d), F32), pltpu.VMEM((tt, d), F32),
                        pltpu.VMEM((tt, d), F32), pltpu.VMEM((tt, d), F32), pltpu.VMEM((SUBLANES, d), F32)],
        compiler_params=_cparams(("parallel", "arbitrary")), name="rglru",
    )(u, conv_w, vec(conv_b), w_rg.astype(BF16), vec(b_rg), w_ig.astype(BF16), vec(b_ig), vec(lam), buf0, h0)


def _pack_rows_kernel(c_ref, n_ref, o_ref):
    p, t = c_ref.shape[1], n_ref.shape[1]
    o_ref[0, 0:p, :] = c_ref[0].reshape(p, o_ref.shape[2]).astype(o_ref.dtype)
    o_ref[0, p:p + t, :] = n_ref[0]
    o_ref[0, p + t:, :] = jnp.zeros((o_ref.shape[1] - p - t, o_ref.shape[2]), o_ref.dtype)


def _pack_rows(caches, j, new, s_pad):
    _, b, p, nh, dh = caches.shape
    t, w = new.shape[1], new.shape[2]
    return pl.pallas_call(
        _pack_rows_kernel, out_shape=jax.ShapeDtypeStruct((b, s_pad, w), BF16), grid=(b,),
        in_specs=[pl.BlockSpec((None, 1, p, nh, dh), lambda i: (j, i, 0, 0, 0)),
                  pl.BlockSpec((1, t, w), lambda i: (i, 0, 0))],
        out_specs=pl.BlockSpec((1, s_pad, w), lambda i: (i, 0, 0)),
        compiler_params=_cparams(("parallel",)), name="pack_rows",
    )(caches, new)


def _pad_to(a, axis, n):
    extra = n - a.shape[axis]
    if extra == 0:
        return a
    widths = [(0, 0)] * a.ndim
    widths[axis] = (0, extra)
    return jnp.pad(a, widths)


def _round_up(n, m):
    return (n + m - 1) // m * m


class _Group:
    def __init__(self, x, past):
        self.b, self.t, self.d = x.shape
        self.past = past
        self.flat = past > 0
        self.s_valid = past + self.t
        if self.flat:
            self.tq = _round_up(self.t, LANES)
            self.s_pad = _round_up(self.s_valid, PLANE_ROWS)
            self.fox_bq = self.dsa_bq = self.tq
            self.bk = self.s_pad
        else:
            self.tq = self.s_pad = self.t
            self.fox_bq = _pick(self.t, 512)
            self.dsa_bq = _pick(self.t, 256)
            self.bk = _pick(self.t, 512)

    def proj_view(self, x):
        return x.reshape(1, self.b * self.t, self.d) if self.flat else x

    def proj_bm(self):
        return self.b * self.t if self.flat else _pick(self.t, 512)

    def rows(self, a):
        return a.reshape(self.b, self.t, a.shape[-1])

    def heads_t(self, a):
        if not self.flat:
            return a
        return a.reshape(a.shape[1], a.shape[2], self.b, self.t).transpose(2, 0, 1, 3)

    def pad_q(self, a):
        return _pad_to(a, a.ndim - 1, self.tq)

    def keys(self, new_rm, cache):
        return new_rm if cache is None else _pack_rows(cache[0], cache[1], new_rm, self.s_pad)

    def values(self, new, cache):
        return new if cache is None else self.keys(self.rows(new), cache)


def _qkvg_plan(width, dh, v_rows):
    plan = [(0, 0, width, dh ** -0.5 * LOG2E, [(0, "headT")]),
            (0, width, width, 1.0, [(1, "row"), (3, "stack")]),
            (0, 2 * width, width, 1.0, [(2, "row" if v_rows else "headT"), (4, "stack")]),
            (0, 3 * width, width, 1.0, [(5, "row")])]
    outs = [("headT", width, BF16), ("row", width, BF16), ("row" if v_rows else "headT", width, BF16),
            ("stack", width, F32), ("stack", width, F32), ("row", width, BF16)]
    return plan, outs


def _mixer_a(grp, x, w_main, w_f, b_f, cache, layer, stacked):
    nh = b_f.shape[0]
    da = w_main.shape[1] // 4
    plan, outs = _qkvg_plan(da, da // nh, grp.flat)
    plan = plan + [(1, 0, LANES, 1.0, [(6, "row")])]
    outs = outs + [("row", LANES, F32)]
    qn = kn = None
    if not grp.flat:
        plan[0][4].append((7, "norm"))
        plan[1][4].append((8, "norm"))
        outs = outs + [("norm", da, F32)] * 2
        qt, k_rm, vt, k, v, g, fl, qn, kn = _inproj(grp.proj_view(x), [w_main, w_f], plan, outs, grp.proj_bm(), layer,
                                                    stacked)
    else:
        qt, k_rm, vt, k, v, g, fl = _inproj(grp.proj_view(x), [w_main, w_f], plan, outs, grp.proj_bm(), layer,
                                            stacked)
    qt, k_rm, g = grp.pad_q(grp.heads_t(qt)), grp.rows(k_rm), grp.rows(g)
    z = grp.rows(fl)[:, :, :nh].transpose(0, 2, 1)
    ck_, cv_ = (None, None) if cache is None else cache[:2]
    if cache is not None:
        z = _pad_to(jnp.concatenate([cache[2].astype(F32).transpose(0, 2, 1), z], axis=2), 2, grp.s_pad)
    lf_all, c_all = _logf_cumsum(z, b_f, grp.past, grp.s_valid)
    logf = lf_all[:, :, grp.past:grp.s_valid].transpose(0, 2, 1)
    cq = grp.pad_q(c_all[:, :, grp.past:grp.s_valid])
    o = _fox_attention(qt, grp.keys(k_rm, ck_), grp.values(vt, cv_), cq, c_all,
                       grp.past, grp.fox_bq, grp.bk, qn, kn)[:, :grp.t]
    return o, g, ([k, v], logf)


def _mixer_b(grp, x, w_in, conv_w, conv_b, w_rg, b_rg, w_ig, b_ig, lam, state):
    dr = w_in.shape[1] // 2
    plan = [(0, 0, dr, 1.0, [(0, "row")]), (0, dr, dr, 1.0, [(1, "row")])]
    u, g = _inproj(grp.proj_view(x), [w_in], plan, [("row", dr, F32), ("row", dr, BF16)], grp.proj_bm())
    u, g = grp.rows(u), grp.rows(g)
    cw = conv_w.shape[0]
    if state is None:
        buf0 = jnp.zeros((grp.b, SUBLANES, dr), F32)
        h0 = jnp.zeros((grp.b, 1, dr), F32)
    else:
        buf, h0 = state
        buf0 = jnp.pad(buf.astype(F32), ((0, 0), (SUBLANES - (cw - 1), 0), (0, 0)))
        h0 = h0.astype(F32).reshape(grp.b, 1, dr)
    h, tail, hl = _rglru(u, conv_w, conv_b, w_rg, b_rg, w_ig, b_ig, lam, buf0, h0, _pick(grp.t, 256))
    return h, g, (tail[:, SUBLANES - (cw - 1):], hl[:, 0])


def _mixer_c(grp, x, w_main, w_idx, nh, nh_idx, d_idx, cache, layer, stacked):
    dc = w_main.shape[1] // 4
    wq = nh_idx * d_idx
    plan, outs = _qkvg_plan(dc, dc // nh, grp.flat)
    plan = plan + [(1, 0, wq, 1.0, [(6, "row")]), (1, wq, LANES, 1.0, [(7, "row")])]
    outs = outs + [("row", wq, BF16), ("row", LANES, F32)]
    qt, k_rm, vt, k, v, g, qi, kw = _inproj(grp.proj_view(x), [w_main, w_idx], plan, outs, grp.proj_bm(), layer,
                                            stacked)
    qt, k_rm = grp.pad_q(grp.heads_t(qt)), grp.rows(k_rm)
    g, qi, kw = grp.rows(g), grp.rows(qi), grp.rows(kw)
    ki = kw[:, :, :d_idx]
    wit = grp.pad_q(kw[:, :, d_idx:d_idx + nh_idx].transpose(0, 2, 1))
    qit = grp.pad_q(qi.reshape(grp.b, grp.t, nh_idx, d_idx).transpose(0, 2, 3, 1))
    ck_, cv_ = (None, None) if cache is None else cache[:2]
    ki_all = ki if cache is None else _pad_to(jnp.concatenate([cache[2].astype(F32), ki], axis=1), 1, grp.s_pad)
    topk = min(TOPK_MAX, grp.s_valid // 4)
    mask = _dsa_select(qit, wit, ki_all.astype(BF16), grp.past, grp.s_valid, topk, grp.dsa_bq, grp.bk)
    o = _dsa_attention(qt, grp.keys(k_rm, ck_), grp.values(vt, cv_), mask, grp.past, grp.fox_bq, grp.bk)[:, :grp.t]
    return o, g, ([k, v], ki)


def _run_trunk(x, p, caches, past):
    depth = p["ln_g"].shape[0]
    alpha = (2 * depth) ** 0.25
    grp = _Group(x, past)
    n_a, n_c = (depth + 2) // 3, depth // 3
    kv_a, kv_c = None, None
    new_a, new_b, new_c = [], [], []
    for i in range(depth):
        j, kind = i // 3, i % 3
        if kind == 0:
            cache = None if caches is None else ((caches["a_k"], j), (caches["a_v"], j), caches["a_logf"][j])
            o, g, (kv_a, logf) = _mixer_a(grp, x, p["w_main_a"][j], p["w_f_a"][j], p["b_f_a"][j], cache,
                                          (j, n_a), kv_a)
            new_a.append(logf)
            w_out = p["w_out_a"][j]
        elif kind == 1:
            state = None if caches is None else (caches["b_conv"][j], caches["b_h"][j])
            o, g, st = _mixer_b(grp, x, p["w_in_b"][j], p["conv_w_b"][j], p["conv_b_b"][j], p["w_rg_b"][j],
                                p["b_rg_b"][j], p["w_ig_b"][j], p["b_ig_b"][j], p["lam_b"][j], state)
            new_b.append(st)
            w_out = p["w_out_b"][j]
        else:
            cache = None if caches is None else ((caches["c_k"], j), (caches["c_v"], j), caches["c_kidx"][j])
            o, g, (kv_c, ki) = _mixer_c(grp, x, p["w_main_c"][j], p["w_idx_c"][j], p["h_c"], p["h_idx"], p["d_idx"],
                                        cache, (j, n_c), kv_c)
            new_c.append(ki)
            w_out = p["w_out_c"][j]
        m = grp.b * grp.t
        x = _outproj_ln(o.reshape(m, -1), g.reshape(m, -1), x.reshape(m, grp.d), w_out,
                        p["ln_g"][i], p["ln_b"][i], alpha).reshape(grp.b, grp.t, grp.d)
    stack = lambda sts, n: jnp.stack([s[n] for s in sts])
    heads = lambda a, nh: a.reshape(a.shape[0], grp.b, grp.t, nh, a.shape[-1] * a.shape[-2] // nh)
    h_a = p["b_f_a"].shape[1]
    return (x, heads(kv_a[0], h_a), heads(kv_a[1], h_a), jnp.stack(new_a), stack(new_b, 0), stack(new_b, 1),
            heads(kv_c[0], p["h_c"]), heads(kv_c[1], p["h_c"]), jnp.stack(new_c))


def kernel(x_prompt, x_sample, cache_a_k, cache_a_v, cache_a_logf, state_b_conv, state_b_h, cache_c_k, cache_c_v, cache_c_kidx, w_in_a, b_f_a, w_out_a, w_in_b, conv_w_b, conv_b_b, w_rg_b, b_rg_b, w_ig_b, b_ig_b, lam_b, w_out_b, w_in_c, w_out_c, ln_g, ln_b):
    h_a = b_f_a.shape[1]
    d_a = w_out_a.shape[1]
    d_c = w_out_c.shape[1]
    h_c = cache_c_k.shape[3]
    d_idx = cache_c_kidx.shape[-1]
    h_idx = (w_in_c.shape[2] - 4 * d_c - d_idx) // (d_idx + 1)
    assert d_a // h_a == LANES and d_c // h_c == LANES, "head width must equal the lane count"
    assert w_in_a.shape[2] == 4 * d_a + h_a and d_idx + h_idx <= LANES
    past = cache_a_k.shape[2]
    assert past % CHUNK == 0 and past > 0

    w_idx = w_in_c[:, :, 4 * d_c:]
    w_idx = _pad_to(w_idx, 2, h_idx * d_idx + LANES)
    p = {"w_main_a": w_in_a[:, :, :4 * d_a].astype(BF16),
         "w_f_a": _pad_to(w_in_a[:, :, 4 * d_a:], 2, LANES).astype(BF16),
         "b_f_a": b_f_a, "w_out_a": w_out_a.astype(BF16),
         "w_in_b": w_in_b.astype(BF16), "conv_w_b": conv_w_b, "conv_b_b": conv_b_b, "w_rg_b": w_rg_b,
         "b_rg_b": b_rg_b, "w_ig_b": w_ig_b, "b_ig_b": b_ig_b, "lam_b": lam_b, "w_out_b": w_out_b.astype(BF16),
         "w_main_c": w_in_c[:, :, :4 * d_c].astype(BF16), "w_idx_c": w_idx.astype(BF16),
         "w_out_c": w_out_c.astype(BF16), "ln_g": ln_g, "ln_b": ln_b,
         "h_c": h_c, "h_idx": h_idx, "d_idx": d_idx}
    caches = {"a_k": cache_a_k, "a_v": cache_a_v, "a_logf": cache_a_logf, "b_conv": state_b_conv,
              "b_h": state_b_h, "c_k": cache_c_k, "c_v": cache_c_v, "c_kidx": cache_c_kidx}
    outs_p = _run_trunk(x_prompt, p, None, 0)
    outs_s = _run_trunk(x_sample, p, caches, past)
    return (outs_p[0], outs_s[0]) + outs_p[1:] + outs_s[1:]
```

```python
import functools
import math

import jax
import jax.numpy as jnp
import numpy as np
from jax import lax
from jax.experimental import pallas as pl
from jax.experimental.pallas import tpu as pltpu

NEG_INF = -1e30
LN_EPS = 1e-5
CHUNK = 64
CHUNK_SHIFT = 6
TOPK_MAX = 256
LRU_C = 8.0
LANES = 128
SUBLANES = 8
BF16_ROWS = 16
PLANE_ROWS = 32 * SUBLANES
INT_MIN = -(2 ** 31)
VMEM_LIMIT = 56 * 1024 * 1024
LOG2E = math.log2(math.e)
FOX_WINDOW = 4
SKIP_BITS = 60.0

F32 = jnp.float32
BF16 = jnp.bfloat16


def _cparams(sem, flags=None):
    return pltpu.CompilerParams(dimension_semantics=sem, vmem_limit_bytes=VMEM_LIMIT, flags=flags)


def _pick(n, pref):
    if n <= pref:
        return n
    b = pref
    while n % b:
        b //= 2
    return b


def _inproj_kernel(x_ref, *refs, n_w, n_alias, layer, plan):
    w_refs, out_refs = refs[:n_w], refs[n_w + n_alias:]
    xb = x_ref[0].astype(BF16)
    for w_idx, c0, width, scale, outs in plan:
        r = jnp.dot(xb, w_refs[w_idx][:, c0:c0 + width], preferred_element_type=F32)
        if scale != 1.0:
            r = r * scale
        for o_idx, kind in outs:
            o = out_refs[o_idx]
            if kind == "row":
                o[0] = r.astype(o.dtype)
            elif kind == "norm":
                rr = jnp.square(r.astype(BF16).astype(F32))
                o[0] = jnp.sqrt(jnp.concatenate(
                    [jnp.sum(rr[:, h * LANES:(h + 1) * LANES], axis=1, keepdims=True) for h in range(width // LANES)],
                    axis=1))
            elif kind == "stack":
                val = r.astype(o.dtype).reshape(o.shape[2:])
                if o.shape[0] == 1:
                    o[0, 0] = val
                else:
                    for l in range(o.shape[0]):
                        o[l, 0] = val if l == layer else jnp.zeros_like(val)
            else:
                for h in range(width // LANES):
                    o[0, h] = r[:, h * LANES:(h + 1) * LANES].T.astype(o.dtype)


def _inproj(x3, ws, plan, out_defs, bm, layer=(0, 1), stacked=None):
    bx, tx, d = x3.shape
    j, n_layers = layer
    grid = (bx, tx // bm)
    in_specs = [pl.BlockSpec((1, bm, d), lambda b, i: (b, i, 0))]
    for w in ws:
        in_specs.append(pl.BlockSpec(w.shape, lambda b, i: (0, 0)))
    stacked = list(stacked or [])
    in_specs += [pl.BlockSpec(memory_space=pl.ANY)] * len(stacked)
    out_shape, out_specs, aliases = [], [], {}
    for o_idx, (kind, width, dt) in enumerate(out_defs):
        if kind == "row":
            out_shape.append(jax.ShapeDtypeStruct((bx, tx, width), dt))
            out_specs.append(pl.BlockSpec((1, bm, width), lambda b, i: (b, i, 0)))
        elif kind == "norm":
            out_shape.append(jax.ShapeDtypeStruct((bx, tx, width // LANES), dt))
            out_specs.append(pl.BlockSpec((1, bm, width // LANES), lambda b, i: (b, i, 0)))
        elif kind == "stack":
            out_shape.append(jax.ShapeDtypeStruct((n_layers, bx, tx, width // LANES, LANES), dt))
            if stacked:
                out_specs.append(pl.BlockSpec((1, 1, bm, width // LANES, LANES), lambda b, i: (j, b, i, 0, 0)))
                aliases[1 + len(ws) + len(aliases)] = o_idx
            else:
                out_specs.append(pl.BlockSpec((n_layers, 1, bm, width // LANES, LANES), lambda b, i: (0, b, i, 0, 0)))
        else:
            nh = width // LANES
            out_shape.append(jax.ShapeDtypeStruct((bx, nh, LANES, tx), dt))
            out_specs.append(pl.BlockSpec((1, nh, LANES, bm), lambda b, i: (b, 0, 0, i)))
    assert len(aliases) == len(stacked)
    return pl.pallas_call(
        functools.partial(_inproj_kernel, n_w=len(ws), n_alias=len(stacked), layer=j, plan=tuple(plan)),
        out_shape=out_shape, grid=grid, in_specs=in_specs, out_specs=out_specs, input_output_aliases=aliases,
        compiler_params=_cparams(("parallel", "parallel")), name="inproj",
    )(x3, *ws, *stacked)


def _outproj_ln_kernel(o_ref, g_ref, x_ref, w_ref, lg_ref, lb_ref, y_ref, *, alpha):
    g = g_ref[...].astype(F32)
    og = (o_ref[...].astype(F32) * (g * jax.nn.sigmoid(g))).astype(BF16)
    y = jnp.dot(og, w_ref[...], preferred_element_type=F32)
    z = alpha * x_ref[...] + y
    mu = jnp.mean(z, axis=-1, keepdims=True)
    zc = z - mu
    var = jnp.mean(zc * zc, axis=-1, keepdims=True)
    y_ref[...] = zc * lax.rsqrt(var + LN_EPS) * lg_ref[...] + lb_ref[...]


def _outproj_ln(o2, g2, x2, w, ln_g, ln_b, alpha):
    m, d = x2.shape
    dk = o2.shape[1]
    bm = _pick(m, 512)
    row = lambda i: (i, 0)
    fixed = lambda i: (0, 0)
    return pl.pallas_call(
        functools.partial(_outproj_ln_kernel, alpha=alpha),
        out_shape=jax.ShapeDtypeStruct((m, d), F32), grid=(m // bm,),
        in_specs=[pl.BlockSpec((bm, dk), row), pl.BlockSpec((bm, dk), row), pl.BlockSpec((bm, d), row),
                  pl.BlockSpec((dk, d), fixed), pl.BlockSpec((1, d), fixed), pl.BlockSpec((1, d), fixed)],
        out_specs=pl.BlockSpec((bm, d), row),
        compiler_params=_cparams(("parallel",)), name="outproj_ln",
    )(o2, g2, x2, w, ln_g.reshape(1, d), ln_b.reshape(1, d))


def _log_sigmoid(x):
    return -(jnp.maximum(-x, 0.0) + jnp.log1p(jnp.exp(-jnp.abs(x))))


def _logf_cumsum_kernel(z_ref, bf_ref, lf_ref, c_ref, *, p0, p1):
    z = z_ref[0]
    pos = lax.broadcasted_iota(jnp.int32, z.shape, 1)
    is_new = jnp.logical_and(pos >= p0, pos < p1)
    lf = jnp.where(is_new, _log_sigmoid(z + bf_ref[...]), z)
    lf_ref[0] = lf
    c = lf
    s = 1
    while s < z.shape[1]:
        c = c + jnp.where(pos >= s, pltpu.roll(c, s, axis=1), 0.0)
        s *= 2
    c_ref[0] = c


def _logf_cumsum(z, b_f, p0, p1):
    b, h, l = z.shape
    blk = pl.BlockSpec((1, h, l), lambda i: (i, 0, 0))
    return pl.pallas_call(
        functools.partial(_logf_cumsum_kernel, p0=p0, p1=p1),
        out_shape=[jax.ShapeDtypeStruct(z.shape, F32)] * 2, grid=(b,),
        in_specs=[blk, pl.BlockSpec((h, 1), lambda i: (0, 0))], out_specs=[blk, blk],
        compiler_params=_cparams(("parallel",)), name="logf_cumsum",
    )(z, b_f.reshape(h, 1))


def _score_store(u, slot, s_sc, mc_sc):
    s_sc[slot] = u
    mc_sc[slot] = jnp.broadcast_to(jnp.max(u, axis=0, keepdims=True), mc_sc.shape[1:])


def _softmax_update(h, slot, vt, s_sc, mc_sc, m_sc, acc_sc, row_shift=None):
    u = s_sc[slot]
    m_prev = m_sc[h]
    m_cur = mc_sc[slot]
    if row_shift is not None:
        m_cur = m_cur + row_shift
    m_new = jnp.maximum(m_prev, m_cur)
    m_row = m_new[0:1]
    p = jnp.exp2(u - (m_row if row_shift is None else m_row - row_shift))
    alpha = jnp.exp2(m_prev - m_new)
    vt1 = jnp.concatenate([vt, jnp.ones((BF16_ROWS, vt.shape[1]), BF16)], axis=0)
    acc_sc[h] = acc_sc[h] * alpha[0:1] + jnp.dot(vt1, p.astype(BF16), preferred_element_type=F32)
    m_sc[h] = m_new


def _head(k_ref, h):
    dh = LANES
    return k_ref[0, :, h * dh:(h + 1) * dh]


def _head_t(v_ref, h, v_rows):
    if not v_rows:
        return v_ref[0, h]
    return _head(v_ref, h).astype(F32).T.astype(BF16)


def _scores(k, kx, qt, qx):
    return jnp.dot(jnp.concatenate([k, kx], axis=1), jnp.concatenate([qt, qx], axis=0),
                   preferred_element_type=F32)


def _pipelined_heads(nh, qk_phase, sm_phase):
    qk_phase(0, 0)
    for h in range(nh - 1):
        qk_phase(h + 1, (h + 1) % 2)
        sm_phase(h, h % 2)
    sm_phase(nh - 1, (nh - 1) % 2)


def _attn_init(m_sc, acc_sc):
    m_sc[...] = jnp.full(m_sc.shape, -jnp.inf, F32)
    acc_sc[...] = jnp.zeros(acc_sc.shape, F32)


def _attn_finish(o_ref, acc_sc, nh, dh):
    for h in range(nh):
        a = acc_sc[h]
        o_ref[0, :, h * dh:(h + 1) * dh] = (a[0:dh] / a[dh:dh + 1]).T.astype(o_ref.dtype)


def _attn_scratch(nh, bq, bk, dh):
    return [pltpu.VMEM((nh, SUBLANES, bq), F32), pltpu.VMEM((nh, dh + BF16_ROWS, bq), F32),
            pltpu.VMEM((2, bk, bq), F32), pltpu.VMEM((2, SUBLANES, bq), F32)]


def _fox_kernel(lo_ref, qt_ref, k_ref, v_ref, cq_ref, kx_ref, qx_ref, o_ref, m_sc, acc_sc, s_sc, mc_sc,
                *, bq, bk, q_off, nh, v_rows, window):
    qi, kstep = pl.program_id(1), pl.program_id(2)
    q_first = q_off + qi * bq
    q_last = q_first + bq - 1
    ki = kstep + _fox_first_step(q_last // bk, window)
    k_first = ki * bk
    k_last = k_first + bk - 1

    @pl.when(kstep == 0)
    def _():
        _attn_init(m_sc, acc_sc)

    def step(masked):
        kx = kx_ref[0]
        if masked:
            kpos = k_first + lax.broadcasted_iota(jnp.int32, (bk, bq), 0)
            qpos = q_first + lax.broadcasted_iota(jnp.int32, (bk, bq), 1)
            causal = kpos <= qpos

        def qk_phase(h, slot):
            u = _scores(_head(k_ref, h), kx, qt_ref[0, h], qx_ref[h])
            if masked:
                u = jnp.where(causal, u, NEG_INF)
            _score_store(u, slot, s_sc, mc_sc)

        def sm_phase(h, slot):
            cq2 = cq_ref[0, pl.ds(h, 1), :] * LOG2E
            _softmax_update(h, slot, _head_t(v_ref, h, v_rows), s_sc, mc_sc, m_sc, acc_sc, row_shift=cq2)

        _pipelined_heads(nh, qk_phase, sm_phase)

    needed = jnp.logical_and(k_first <= q_last, ki >= jnp.maximum(lo_ref[pl.program_id(0), qi], 0))
    straddles = k_last > q_first

    @pl.when(jnp.logical_and(needed, straddles))
    def _():
        step(True)

    @pl.when(jnp.logical_and(needed, jnp.logical_not(straddles)))
    def _():
        step(False)

    @pl.when(kstep == pl.num_programs(2) - 1)
    def _():
        _attn_finish(o_ref, acc_sc, nh, LANES)


def _fox_first_step(last_block, window):
    return 0 if window is None else last_block - (window - 1)


def _split3(x):
    def top_half(v):
        bits = lax.bitcast_convert_type(v, jnp.uint32) & jnp.uint32(0xFFFF0000)
        return lax.bitcast_convert_type(bits, F32)

    hi = top_half(x)
    r1 = x - hi
    mid = top_half(r1)
    return hi.astype(BF16), mid.astype(BF16), (r1 - mid).astype(BF16)


def _v_spec(v, nh, dh, bk, kmap):
    if v.ndim == 3:
        return True, pl.BlockSpec((1, bk, nh * dh), lambda *ix: (ix[0], kmap(*ix), 0))
    return False, pl.BlockSpec((1, nh, dh, bk), lambda *ix: (ix[0], 0, 0, kmap(*ix)))


def _fox_first_block(qn_rows, kn_rows, cq, ck, q_off, bq, bk):
    b, nh, tq = cq.shape
    nq, nk = tq // bq, ck.shape[2] // bk
    if nk == 1 or qn_rows is None:
        return jnp.zeros((b, nq), jnp.int32)
    qn = qn_rows.reshape(b, nq, bq, nh).max(axis=2).transpose(0, 2, 1)
    kn = kn_rows.reshape(b, nk, bk, nh).max(axis=2).transpose(0, 2, 1)
    first_self = [(q_off + qi * bq) // bk for qi in range(nq)]
    last_self = [min((q_off + (qi + 1) * bq - 1) // bk, nk - 1) for qi in range(nq)]
    kn_self = jnp.stack([kn[:, :, f:l + 1].max(axis=-1) for f, l in zip(first_self, last_self)], axis=-1)
    c_first = cq[:, :, ::bq]
    c_last = ck[:, :, bk - 1::bk]
    bound = (qn[..., :, None] * (kn[..., None, :] + kn_self[..., :, None])
             + (c_first[..., :, None] - c_last[..., None, :]) * LOG2E)
    skip = jnp.all(bound < -SKIP_BITS, axis=1)
    lo = jnp.sum(jnp.cumprod(skip.astype(jnp.int32), axis=-1), axis=-1)
    return jnp.minimum(lo, jnp.asarray(first_self, jnp.int32)[None, :]).astype(jnp.int32)


def _fox_attention(qt_hm, k_rm, v, cq, ck, q_off, bq, bk, qn_rows=None, kn_rows=None):
    b, nh, dh, tq = qt_hm.shape
    s = k_rm.shape[1]
    nq, nk = tq // bq, s // bk
    w = LANES // nh
    assert w >= 3
    kx = jnp.stack(_split3(ck * (-LOG2E)), axis=-1)
    kx = _pad_to(kx, 3, w).transpose(0, 2, 1, 3).reshape(b, s, LANES)
    rows = jnp.arange(LANES)[None, :, None]
    heads = jnp.arange(nh)[:, None, None]
    qx = jnp.broadcast_to(jnp.logical_and(rows >= heads * w, rows < heads * w + 3), (nh, LANES, bq)).astype(BF16)

    lo = _fox_first_block(qn_rows, kn_rows, cq, ck, q_off, bq, bk)

    last = [(q_off + (qi + 1) * bq - 1) // bk for qi in range(nq)]

    def run(window):
        def kmap(bb, qi, kstep, lo_ref):
            last_block = (q_off + (qi + 1) * bq - 1) // bk
            ki = kstep + _fox_first_step(last_block, window)
            return jnp.clip(ki, lo_ref[bb, qi], last_block)

        v_rows, v_spec = _v_spec(v, nh, dh, bk, kmap)
        grid_spec = pltpu.PrefetchScalarGridSpec(
            num_scalar_prefetch=1, grid=(b, nq, nk if window is None else window),
            in_specs=[pl.BlockSpec((1, nh, dh, bq), lambda bb, qi, ki, lo_ref: (bb, 0, 0, qi)),
                      pl.BlockSpec((1, bk, nh * dh), lambda *ix: (ix[0], kmap(*ix), 0)),
                      v_spec,
                      pl.BlockSpec((1, nh, bq), lambda bb, qi, ki, lo_ref: (bb, 0, qi)),
                      pl.BlockSpec((1, bk, LANES), lambda *ix: (ix[0], kmap(*ix), 0)),
                      pl.BlockSpec((nh, LANES, bq), lambda bb, qi, ki, lo_ref: (0, 0, 0))],
            out_specs=pl.BlockSpec((1, bq, nh * dh), lambda bb, qi, ki, lo_ref: (bb, qi, 0)),
            scratch_shapes=_attn_scratch(nh, bq, bk, dh))
        return pl.pallas_call(
            functools.partial(_fox_kernel, bq=bq, bk=bk, q_off=q_off, nh=nh, v_rows=v_rows, window=window),
            out_shape=jax.ShapeDtypeStruct((b, tq, nh * dh), BF16), grid_spec=grid_spec,
            compiler_params=_cparams(("parallel", "parallel", "arbitrary")), name="fox_attention",
        )(lo, qt_hm, k_rm, v, cq, kx, qx)

    if nk <= FOX_WINDOW:
        return run(None)
    fits = jnp.all(jnp.asarray(last, jnp.int32)[None, :] - lo < FOX_WINDOW)
    return lax.cond(fits, lambda: run(FOX_WINDOW), lambda: run(None))


def _bit_planes(words):
    a = list(words)
    j, m = 16, 0x0000FFFF
    while j:
        k = 0
        while k < 32:
            t = (a[k] ^ lax.shift_right_logical(a[k + j], jnp.int32(j))) & jnp.int32(m)
            a[k] = a[k] ^ t
            a[k + j] = a[k + j] ^ lax.shift_left(t, jnp.int32(j))
            k = (k + j + 1) & ~j
        j >>= 1
        m = (m ^ (m << j)) & 0xFFFFFFFF
    return a


def _dsa_select_kernel(qit_ref, wit_ref, ki_ref, out_ref, key_sc, pl_sc, act_sc, j_sc,
                       *, bq, kb, q_off, s_valid, topk, nh_idx, idx_bits):
    qb = pl.program_id(1)
    q_first = q_off + qb * bq
    q_last = q_first + bq - 1
    n_adm = jnp.minimum(((q_last >> CHUNK_SHIFT) + 1) * CHUNK, s_valid)
    nkb = (jnp.maximum(n_adm, topk) + kb - 1) // kb
    nkb_total = key_sc.shape[0] // kb
    gpb = kb // PLANE_ROWS
    nkb4 = nkb // 4
    qchunk = (q_first + lax.broadcasted_iota(jnp.int32, (kb, bq), 1)) >> CHUNK_SHIFT
    row = lax.broadcasted_iota(jnp.int32, (kb, bq), 0)

    def admissible(kpos):
        return jnp.logical_and((kpos >> CHUNK_SHIFT) <= qchunk, kpos < s_valid)

    def rows(kblk):
        return pl.ds(pl.multiple_of(kblk * kb, kb), kb)

    def score_block(kblk, carry, all_admissible):
        kblock = ki_ref[0, rows(kblk), :]
        sc = jnp.zeros((kb, bq), F32)
        for h in range(nh_idx):
            d = jnp.dot(kblock, qit_ref[0, h], preferred_element_type=F32)
            sc = sc + wit_ref[0, h:h + 1, :] * jnp.maximum(d, 0.0)
        sc = sc + 0.0
        if not all_admissible:
            sc = jnp.where(admissible(kblk * kb + row), sc, NEG_INF)
        bits = lax.bitcast_convert_type(sc, jnp.int32)
        key = bits ^ ((bits >> 31) & 0x7FFFFFFF)
        key_sc[rows(kblk), :] = key
        for g in range(gpb):
            base = g * PLANE_ROWS
            planes = _bit_planes([key[base + SUBLANES * j:base + SUBLANES * (j + 1), :] for j in range(32)])
            planes[0] = ~planes[0]
            for i in range(32):
                pl_sc[kblk * gpb + g, i] = planes[i]
            act_sc[kblk * gpb + g] = jnp.full((SUBLANES, bq), -1, jnp.int32)
        return carry

    n_full = jnp.minimum(((q_first >> CHUNK_SHIFT) + 1) * CHUNK, s_valid) // kb
    lax.fori_loop(0, n_full, functools.partial(score_block, all_admissible=True), 0)
    lax.fori_loop(n_full, nkb, functools.partial(score_block, all_admissible=False), 0)

    def two_bit_step(it, carry):
        thr_u, above = carry
        i1, i0 = 2 * it, 2 * it + 1

        def split(g):
            a = act_sc[g]
            a1 = a & pl_sc[g, i1]
            x11 = a1 & pl_sc[g, i0]
            return a, x11, a1 ^ x11, (a & pl_sc[g, i0]) ^ x11

        def cnt_groups(g0, n, accs):
            c11, c10, c01 = accs
            for g in range(n):
                _, x11, x10, x01 = split(g0 + g)
                c11 = c11 + lax.population_count(x11)
                c10 = c10 + lax.population_count(x10)
                c01 = c01 + lax.population_count(x01)
            return c11, c10, c01

        zero = jnp.zeros((SUBLANES, bq), jnp.int32)
        accs = lax.fori_loop(0, nkb4, lambda i, c: cnt_groups(i * gpb * 4, gpb * 4, c), (zero, zero, zero))
        accs = lax.fori_loop(nkb4 * 4, nkb, lambda i, c: cnt_groups(i * gpb, gpb, c), accs)
        c11, c10, c01 = [jnp.sum(c.astype(F32), axis=0, keepdims=True) for c in accs]
        s1 = above + c11
        s2 = s1 + c10
        s3 = s2 + c01
        t11, t10, t01 = s1 >= topk, s2 >= topk, s3 >= topk

        def upd_groups(g0, n, carry2):
            for g in range(n):
                a, x11, x10, x01 = split(g0 + g)
                x00 = a ^ x11 ^ x10 ^ x01
                act_sc[g0 + g] = jnp.where(t11, x11, jnp.where(t10, x10, jnp.where(t01, x01, x00)))
            return carry2

        lax.fori_loop(0, nkb4, lambda i, c2: upd_groups(i * gpb * 4, gpb * 4, c2), 0)
        lax.fori_loop(nkb4 * 4, nkb, lambda i, c2: upd_groups(i * gpb, gpb, c2), 0)
        hi = lax.shift_left(jnp.int32(1), 31 - i1)
        lo = lax.shift_left(jnp.int32(1), 31 - i0)
        low_bit = jnp.where(t11, lo, jnp.where(t10, 0, jnp.where(t01, lo, 0)))
        return (thr_u | jnp.where(t10, hi, 0) | low_bit,
                jnp.where(t11, above, jnp.where(t10, s1, jnp.where(t01, s2, s3))))

    thr_u, n_gt = lax.fori_loop(0, 16, two_bit_step, (jnp.zeros((1, bq), jnp.int32), jnp.zeros((1, bq), F32)))
    thr = thr_u ^ INT_MIN

    def eq_blk(kblk, acc):
        for g in range(gpb):
            acc = acc + lax.population_count(act_sc[kblk * gpb + g])
        return acc

    n_eq = jnp.sum(lax.fori_loop(0, nkb, eq_blk, jnp.zeros((SUBLANES, bq), jnp.int32)).astype(F32),
                   axis=0, keepdims=True)
    need = topk - n_gt
    has_excess = jnp.max(n_eq - need) > 0.0
    j_sc[...] = jnp.full(j_sc.shape, 2 ** 30, jnp.int32)

    @pl.when(has_excess)
    def _():
        def count_ties_below(cand):
            def blk(kblk, acc):
                hit = jnp.logical_and(key_sc[rows(kblk), :] == thr, kblk * kb + row < cand)
                ind = jnp.where(hit, 1.0, 0.0)
                for r in range(kb // SUBLANES):
                    acc = acc + ind[r * SUBLANES:(r + 1) * SUBLANES]
                return acc
            return jnp.sum(lax.fori_loop(0, nkb, blk, jnp.zeros((SUBLANES, bq), F32)), axis=0, keepdims=True)

        def idx_step(it, lo):
            cand = lo | lax.shift_left(jnp.int32(1), idx_bits - 1 - it)
            return jnp.where(count_ties_below(cand) < need, cand, lo)

        lo = lax.fori_loop(0, idx_bits, idx_step, jnp.zeros((1, bq), jnp.int32))
        j_sc[...] = jnp.broadcast_to(lo, j_sc.shape)

    j_cut = j_sc[0:1, :]

    def write_fast(kblk, carry):
        out_ref[0, 0, rows(kblk), :] = jnp.where(key_sc[rows(kblk), :] >= thr, 0.0, NEG_INF).astype(out_ref.dtype)
        return carry

    def write_block(kblk, carry):
        key = key_sc[rows(kblk), :]
        kpos = kblk * kb + row
        tie = jnp.logical_and(key == thr, kpos <= j_cut)
        sel = jnp.logical_and(jnp.logical_or(key > thr, tie), admissible(kpos))
        out_ref[0, 0, rows(kblk), :] = jnp.where(sel, 0.0, NEG_INF).astype(out_ref.dtype)
        return carry

    def fill_block(kblk, carry):
        out_ref[0, 0, rows(kblk), :] = jnp.full((kb, bq), NEG_INF, out_ref.dtype)
        return carry

    n_fast = jnp.where(has_excess, 0, n_full)
    lax.fori_loop(0, n_fast, write_fast, 0)
    lax.fori_loop(n_fast, nkb, write_block, 0)
    lax.fori_loop(nkb, nkb_total, fill_block, 0)


def _dsa_select(qit_hm, wit, ki, q_off, s_valid, topk, bq, kb):
    b, nh_idx, d_idx, tq = qit_hm.shape
    s_pad = ki.shape[1]
    return pl.pallas_call(
        functools.partial(_dsa_select_kernel, bq=bq, kb=kb, q_off=q_off, s_valid=s_valid, topk=topk,
                          nh_idx=nh_idx, idx_bits=max(1, (s_pad - 1).bit_length())),
        out_shape=jax.ShapeDtypeStruct((b, tq // bq, s_pad, bq), BF16), grid=(b, tq // bq),
        in_specs=[pl.BlockSpec((1, nh_idx, d_idx, bq), lambda bb, qb: (bb, 0, 0, qb)),
                  pl.BlockSpec((1, nh_idx, bq), lambda bb, qb: (bb, 0, qb)),
                  pl.BlockSpec((1, s_pad, d_idx), lambda bb, qb: (bb, 0, 0))],
        out_specs=pl.BlockSpec((1, 1, s_pad, bq), lambda bb, qb: (bb, qb, 0, 0)),
        scratch_shapes=[pltpu.VMEM((s_pad, bq), jnp.int32),
                        pltpu.VMEM((s_pad // PLANE_ROWS, 32, SUBLANES, bq), jnp.int32),
                        pltpu.VMEM((s_pad // PLANE_ROWS, SUBLANES, bq), jnp.int32),
                        pltpu.VMEM((SUBLANES, bq), jnp.int32)],
        compiler_params=_cparams(("parallel", "arbitrary")), name="dsa_select",
    )(qit_hm, wit, ki)


def _dsa_attn_kernel(sl_ref, qt_ref, k_ref, v_ref, mb_ref, qx_ref, o_ref, m_sc, acc_sc, s_sc, mc_sc,
                     base_sc, t_sc, *, bq, bk, q_off, nh, v_rows):
    qi, ki = pl.program_id(1), pl.program_id(2)
    q_first = q_off + qi * bq
    q_last = q_first + bq - 1
    k_first = ki * bk
    k_last = k_first + bk - 1
    qposf = (q_first + lax.broadcasted_iota(jnp.int32, (1, bq), 1)).astype(F32)

    @pl.when(ki == 0)
    def _():
        _attn_init(m_sc, acc_sc)

    def step(past):
        base_sc[...] = jnp.concatenate([mb_ref[0, j] for j in range(mb_ref.shape[1])], axis=1).astype(F32)
        if past:
            kpos = k_first + lax.broadcasted_iota(jnp.int32, (bk, LANES), 0)
            lane = lax.broadcasted_iota(jnp.int32, (bk, LANES), 1)
            kx = jnp.where(lane < 3, kpos >> CHUNK_SHIFT, jnp.where(lane < 6, kpos & (CHUNK - 1), 0)).astype(BF16)
        else:
            kpos = k_first + lax.broadcasted_iota(jnp.int32, (bk, bq), 0)
            qpos = q_first + lax.broadcasted_iota(jnp.int32, (bk, bq), 1)
            t_sc[...] = jnp.minimum(kpos, 2 * qpos - kpos).astype(F32)

        def qk_phase(h, slot):
            if past:
                u = _scores(_head(k_ref, h), kx, qt_ref[0, h], qx_ref[h]) + base_sc[...]
            else:
                st = jnp.dot(_head(k_ref, h), qt_ref[0, h], preferred_element_type=F32)
                u = st + sl_ref[h] * t_sc[...] + base_sc[...]
            _score_store(u, slot, s_sc, mc_sc)

        def sm_phase(h, slot):
            _softmax_update(h, slot, _head_t(v_ref, h, v_rows), s_sc, mc_sc, m_sc, acc_sc,
                            row_shift=-sl_ref[h] * qposf)

        _pipelined_heads(nh, qk_phase, sm_phase)

    needed = k_first <= ((q_last >> CHUNK_SHIFT) << CHUNK_SHIFT) + CHUNK - 1
    past = k_last <= q_first

    @pl.when(past)
    def _():
        step(True)

    @pl.when(jnp.logical_and(needed, jnp.logical_not(past)))
    def _():
        step(False)

    @pl.when(ki == pl.num_programs(2) - 1)
    def _():
        _attn_finish(o_ref, acc_sc, nh, LANES)


def _dsa_attention(qt_hm, k_rm, v, mask, q_off, bq, bk):
    b, nh, dh, tq = qt_hm.shape
    s, mq = mask.shape[2], mask.shape[3]
    nq, nk, nsub = tq // bq, s // bk, bq // mq
    slopes2_np = (2.0 ** (-8.0 * np.arange(1, nh + 1) / nh) * LOG2E).astype(np.float32)
    slopes2 = jnp.asarray(slopes2_np)
    rest, pieces = slopes2_np, []
    for _ in range(3):
        piece = rest.astype(BF16).astype(np.float32)
        pieces.append(piece)
        rest = rest - piece
    col = np.zeros((nh, LANES), np.float32)
    col[:, 0:3] = np.stack(pieces, axis=1) * CHUNK
    col[:, 3:6] = np.stack(pieces, axis=1)
    qx = jnp.broadcast_to(jnp.asarray(col.astype(BF16))[:, :, None], (nh, LANES, bq))

    def kmap(bb, qi, ki):
        q_last = q_off + (qi + 1) * bq - 1
        return jnp.minimum(ki, (((q_last >> CHUNK_SHIFT) << CHUNK_SHIFT) + CHUNK - 1) // bk)

    v_rows, v_spec = _v_spec(v, nh, dh, bk, kmap)
    return pl.pallas_call(
        functools.partial(_dsa_attn_kernel, bq=bq, bk=bk, q_off=q_off, nh=nh, v_rows=v_rows),
        out_shape=jax.ShapeDtypeStruct((b, tq, nh * dh), BF16), grid=(b, nq, nk),
        in_specs=[pl.BlockSpec(memory_space=pltpu.SMEM),
                  pl.BlockSpec((1, nh, dh, bq), lambda bb, qi, ki: (bb, 0, 0, qi)),
                  pl.BlockSpec((1, bk, nh * dh), lambda bb, qi, ki: (bb, kmap(bb, qi, ki), 0)),
                  v_spec,
                  pl.BlockSpec((1, nsub, bk, mq), lambda bb, qi, ki: (bb, qi, kmap(bb, qi, ki), 0)),
                  pl.BlockSpec((nh, LANES, bq), lambda bb, qi, ki: (0, 0, 0))],
        out_specs=pl.BlockSpec((1, bq, nh * dh), lambda bb, qi, ki: (bb, qi, 0)),
        scratch_shapes=_attn_scratch(nh, bq, bk, dh) + [pltpu.VMEM((bk, bq), F32), pltpu.VMEM((bk, bq), F32)],
        compiler_params=_cparams(("parallel", "parallel", "arbitrary")), name="dsa_attention",
    )(slopes2, qt_hm, k_rm, v, mask, qx)


def _rglru_kernel(u_ref, cw_ref, cb_ref, wr_ref, br_ref, wig_ref, big_ref, lam_ref, buf0_ref, h0_ref,
                  h_ref, conv_ref, hl_ref, ubuf, a_sc, b_sc, hbuf, hcar, *, tt, conv_w, nblk):
    t = pl.program_id(1)
    pad = SUBLANES
    d = u_ref.shape[2]
    blk = d // nblk

    @pl.when(t == 0)
    def _():
        ubuf[0:pad, :] = buf0_ref[0]
        hcar[...] = jnp.broadcast_to(h0_ref[0], (SUBLANES, d))

    ubuf[pad:pad + tt, :] = u_ref[0]
    uc = cb_ref[...] + ubuf[pad:pad + tt, :] * cw_ref[conv_w - 1:conv_w, :]
    for j in range(conv_w - 1):
        off = pad - (conv_w - 1) + j
        uc = uc + ubuf[off:off + tt, :] * cw_ref[j:j + 1, :]

    lam = lam_ref[...]
    neg_sp = -LRU_C * (jnp.maximum(-lam, 0.0) + jnp.log1p(jnp.exp(-jnp.abs(lam))))
    for n in range(nblk):
        cs = slice(n * blk, (n + 1) * blk)
        ucn = uc[:, cs]
        ub = ucn.astype(BF16)
        r = jax.nn.sigmoid(jnp.dot(ub, wr_ref[n], preferred_element_type=F32) + br_ref[:, cs])
        i = jax.nn.sigmoid(jnp.dot(ub, wig_ref[n], preferred_element_type=F32) + big_ref[:, cs])
        log_a = r * neg_sp[:, cs]
        a = jnp.exp(log_a)
        a_sc[:, cs] = a
        b_sc[:, cs] = jnp.sqrt(-jnp.tanh(log_a) * (a * a + 1.0)) * (i * ucn)

    row = lax.broadcasted_iota(jnp.int32, (SUBLANES, d), 0)

    def group(gi, hprev):
        r0 = pl.multiple_of(gi * SUBLANES, SUBLANES)
        av = a_sc[pl.ds(r0, SUBLANES), :]
        bv = b_sc[pl.ds(r0, SUBLANES), :]
        s = 1
        while s < SUBLANES:
            a_sh = pltpu.roll(av, s, axis=0)
            b_sh = pltpu.roll(bv, s, axis=0)
            m = row >= s
            bv = jnp.where(m, av * b_sh + bv, bv)
            av = jnp.where(m, av * a_sh, av)
            s *= 2
        hrows = av * hprev + bv
        hbuf[pl.ds(r0, SUBLANES), :] = hrows
        return jnp.broadcast_to(hrows[SUBLANES - 1:SUBLANES, :], (SUBLANES, d))

    hlast = lax.fori_loop(0, tt // SUBLANES, group, hcar[...])
    h_ref[0] = hbuf[...].astype(h_ref.dtype)
    hcar[...] = hlast
    hl_ref[0] = hlast[0:1, :]
    tail = ubuf[tt:tt + pad, :]
    conv_ref[0] = tail
    ubuf[0:pad, :] = tail


def _rglru(u, conv_w, conv_b, w_rg, b_rg, w_ig, b_ig, lam, buf0, h0, tt):
    b, t, d = u.shape
    cw = conv_w.shape[0]
    nblk, blk = w_rg.shape[0], w_rg.shape[1]
    vec = lambda a: a.reshape(1, d)
    fixed2 = lambda bb, ti: (0, 0)
    fixed3 = lambda bb, ti: (0, 0, 0)
    perb = lambda bb, ti: (bb, 0, 0)
    return pl.pallas_call(
        functools.partial(_rglru_kernel, tt=tt, conv_w=cw, nblk=nblk),
        out_shape=[jax.ShapeDtypeStruct((b, t, d), BF16), jax.ShapeDtypeStruct((b, SUBLANES, d), F32),
                   jax.ShapeDtypeStruct((b, 1, d), F32)],
        grid=(b, t // tt),
        in_specs=[pl.BlockSpec((1, tt, d), lambda bb, ti: (bb, ti, 0)),
                  pl.BlockSpec((cw, d), fixed2), pl.BlockSpec((1, d), fixed2),
                  pl.BlockSpec((nblk, blk, blk), fixed3), pl.BlockSpec((1, d), fixed2),
                  pl.BlockSpec((nblk, blk, blk), fixed3), pl.BlockSpec((1, d), fixed2),
                  pl.BlockSpec((1, d), fixed2),
                  pl.BlockSpec((1, SUBLANES, d), perb), pl.BlockSpec((1, 1, d), perb)],
        out_specs=[pl.BlockSpec((1, tt, d), lambda bb, ti: (bb, ti, 0)),
                   pl.BlockSpec((1, SUBLANES, d), perb), pl.BlockSpec((1, 1, d), perb)],
        scratch_shapes=[pltpu.VMEM((SUBLANES + tt, d), F32), pltpu.VMEM((tt, d), F32),
                        pltpu.VMEM((tt, d), F32), pltpu.VMEM((tt, d), F32), pltpu.VMEM((SUBLANES, d), F32)],
        compiler_params=_cparams(("parallel", "arbitrary")), name="rglru",
    )(u, conv_w, vec(conv_b), w_rg.astype(BF16), vec(b_rg), w_ig.astype(BF16), vec(b_ig), vec(lam), buf0, h0)


def _pack_rows_kernel(c_ref, n_ref, o_ref):
    p, t = c_ref.shape[1], n_ref.shape[1]
    o_ref[0, 0:p, :] = c_ref[0].reshape(p, o_ref.shape[2]).astype(o_ref.dtype)
    o_ref[0, p:p + t, :] = n_ref[0]
    o_ref[0, p + t:, :] = jnp.zeros((o_ref.shape[1] - p - t, o_ref.shape[2]), o_ref.dtype)


def _pack_rows(caches, j, new, s_pad):
    _, b, p, nh, dh = caches.shape
    t, w = new.shape[1], new.shape[2]
    return pl.pallas_call(
        _pack_rows_kernel, out_shape=jax.ShapeDtypeStruct((b, s_pad, w), BF16), grid=(b,),
        in_specs=[pl.BlockSpec((None, 1, p, nh, dh), lambda i: (j, i, 0, 0, 0)),
                  pl.BlockSpec((1, t, w), lambda i: (i, 0, 0))],
        out_specs=pl.BlockSpec((1, s_pad, w), lambda i: (i, 0, 0)),
        compiler_params=_cparams(("parallel",)), name="pack_rows",
    )(caches, new)


def _pad_to(a, axis, n):
    extra = n - a.shape[axis]
    if extra == 0:
        return a
    widths = [(0, 0)] * a.ndim
    widths[axis] = (0, extra)
    return jnp.pad(a, widths)


def _round_up(n, m):
    return (n + m - 1) // m * m


class _Group:
    def __init__(self, x, past):
        self.b, self.t, self.d = x.shape
        self.past = past
        self.flat = past > 0
        self.s_valid = past + self.t
        if self.flat:
            self.tq = _round_up(self.t, LANES)
            self.s_pad = _round_up(self.s_valid, PLANE_ROWS)
            self.fox_bq = self.dsa_bq = self.tq
            self.bk = self.s_pad
        else:
            self.tq = self.s_pad = self.t
            self.fox_bq = _pick(self.t, 512)
            self.dsa_bq = _pick(self.t, 256)
            self.bk = _pick(self.t, 512)

    def proj_view(self, x):
        return x.reshape(1, self.b * self.t, self.d) if self.flat else x

    def proj_bm(self):
        return self.b * self.t if self.flat else _pick(self.t, 512)

    def rows(self, a):
        return a.reshape(self.b, self.t, a.shape[-1])

    def heads_t(self, a):
        if not self.flat:
            return a
        return a.reshape(a.shape[1], a.shape[2], self.b, self.t).transpose(2, 0, 1, 3)

    def pad_q(self, a):
        return _pad_to(a, a.ndim - 1, self.tq)

    def keys(self, new_rm, cache):
        return new_rm if cache is None else _pack_rows(cache[0], cache[1], new_rm, self.s_pad)

    def values(self, new, cache):
        return new if cache is None else self.keys(self.rows(new), cache)


def _qkvg_plan(width, dh, v_rows):
    plan = [(0, 0, width, dh ** -0.5 * LOG2E, [(0, "headT")]),
            (0, width, width, 1.0, [(1, "row"), (3, "stack")]),
            (0, 2 * width, width, 1.0, [(2, "row" if v_rows else "headT"), (4, "stack")]),
            (0, 3 * width, width, 1.0, [(5, "row")])]
    outs = [("headT", width, BF16), ("row", width, BF16), ("row" if v_rows else "headT", width, BF16),
            ("stack", width, F32), ("stack", width, F32), ("row", width, BF16)]
    return plan, outs


def _mixer_a(grp, x, w_main, w_f, b_f, cache, layer, stacked):
    nh = b_f.shape[0]
    da = w_main.shape[1] // 4
    plan, outs = _qkvg_plan(da, da // nh, grp.flat)
    plan = plan + [(1, 0, LANES, 1.0, [(6, "row")])]
    outs = outs + [("row", LANES, F32)]
    qn = kn = None
    if not grp.flat:
        plan[0][4].append((7, "norm"))
        plan[1][4].append((8, "norm"))
        outs = outs + [("norm", da, F32)] * 2
        qt, k_rm, vt, k, v, g, fl, qn, kn = _inproj(grp.proj_view(x), [w_main, w_f], plan, outs, grp.proj_bm(), layer,
                                                    stacked)
    else:
        qt, k_rm, vt, k, v, g, fl = _inproj(grp.proj_view(x), [w_main, w_f], plan, outs, grp.proj_bm(), layer,
                                            stacked)
    qt, k_rm, g = grp.pad_q(grp.heads_t(qt)), grp.rows(k_rm), grp.rows(g)
    z = grp.rows(fl)[:, :, :nh].transpose(0, 2, 1)
    ck_, cv_ = (None, None) if cache is None else cache[:2]
    if cache is not None:
        z = _pad_to(jnp.concatenate([cache[2].astype(F32).transpose(0, 2, 1), z], axis=2), 2, grp.s_pad)
    lf_all, c_all = _logf_cumsum(z, b_f, grp.past, grp.s_valid)
    logf = lf_all[:, :, grp.past:grp.s_valid].transpose(0, 2, 1)
    cq = grp.pad_q(c_all[:, :, grp.past:grp.s_valid])
    o = _fox_attention(qt, grp.keys(k_rm, ck_), grp.values(vt, cv_), cq, c_all,
                       grp.past, grp.fox_bq, grp.bk, qn, kn)[:, :grp.t]
    return o, g, ([k, v], logf)


def _mixer_b(grp, x, w_in, conv_w, conv_b, w_rg, b_rg, w_ig, b_ig, lam, state):
    dr = w_in.shape[1] // 2
    plan = [(0, 0, dr, 1.0, [(0, "row")]), (0, dr, dr, 1.0, [(1, "row")])]
    u, g = _inproj(grp.proj_view(x), [w_in], plan, [("row", dr, F32), ("row", dr, BF16)], grp.proj_bm())
    u, g = grp.rows(u), grp.rows(g)
    cw = conv_w.shape[0]
    if state is None:
        buf0 = jnp.zeros((grp.b, SUBLANES, dr), F32)
        h0 = jnp.zeros((grp.b, 1, dr), F32)
    else:
        buf, h0 = state
        buf0 = jnp.pad(buf.astype(F32), ((0, 0), (SUBLANES - (cw - 1), 0), (0, 0)))
        h0 = h0.astype(F32).reshape(grp.b, 1, dr)
    h, tail, hl = _rglru(u, conv_w, conv_b, w_rg, b_rg, w_ig, b_ig, lam, buf0, h0, _pick(grp.t, 256))
    return h, g, (tail[:, SUBLANES - (cw - 1):], hl[:, 0])


def _mixer_c(grp, x, w_main, w_idx, nh, nh_idx, d_idx, cache, layer, stacked):
    dc = w_main.shape[1] // 4
    wq = nh_idx * d_idx
    plan, outs = _qkvg_plan(dc, dc // nh, grp.flat)
    plan = plan + [(1, 0, wq, 1.0, [(6, "row")]), (1, wq, LANES, 1.0, [(7, "row")])]
    outs = outs + [("row", wq, BF16), ("row", LANES, F32)]
    qt, k_rm, vt, k, v, g, qi, kw = _inproj(grp.proj_view(x), [w_main, w_idx], plan, outs, grp.proj_bm(), layer,
                                            stacked)
    qt, k_rm = grp.pad_q(grp.heads_t(qt)), grp.rows(k_rm)
    g, qi, kw = grp.rows(g), grp.rows(qi), grp.rows(kw)
    ki = kw[:, :, :d_idx]
    wit = grp.pad_q(kw[:, :, d_idx:d_idx + nh_idx].transpose(0, 2, 1))
    qit = grp.pad_q(qi.reshape(grp.b, grp.t, nh_idx, d_idx).transpose(0, 2, 3, 1))
    ck_, cv_ = (None, None) if cache is None else cache[:2]
    ki_all = ki if cache is None else _pad_to(jnp.concatenate([cache[2].astype(F32), ki], axis=1), 1, grp.s_pad)
    topk = min(TOPK_MAX, grp.s_valid // 4)
    mask = _dsa_select(qit, wit, ki_all.astype(BF16), grp.past, grp.s_valid, topk, grp.dsa_bq, grp.bk)
    o = _dsa_attention(qt, grp.keys(k_rm, ck_), grp.values(vt, cv_), mask, grp.past, grp.fox_bq, grp.bk)[:, :grp.t]
    return o, g, ([k, v], ki)


def _run_trunk(x, p, caches, past):
    depth = p["ln_g"].shape[0]
    alpha = (2 * depth) ** 0.25
    grp = _Group(x, past)
    n_a, n_c = (depth + 2) // 3, depth // 3
    kv_a, kv_c = None, None
    new_a, new_b, new_c = [], [], []
    for i in range(depth):
        j, kind = i // 3, i % 3
        if kind == 0:
            cache = None if caches is None else ((caches["a_k"], j), (caches["a_v"], j), caches["a_logf"][j])
            o, g, (kv_a, logf) = _mixer_a(grp, x, p["w_main_a"][j], p["w_f_a"][j], p["b_f_a"][j], cache,
                                          (j, n_a), kv_a)
            new_a.append(logf)
            w_out = p["w_out_a"][j]
        elif kind == 1:
            state = None if caches is None else (caches["b_conv"][j], caches["b_h"][j])
            o, g, st = _mixer_b(grp, x, p["w_in_b"][j], p["conv_w_b"][j], p["conv_b_b"][j], p["w_rg_b"][j],
                                p["b_rg_b"][j], p["w_ig_b"][j], p["b_ig_b"][j], p["lam_b"][j], state)
            new_b.append(st)
            w_out = p["w_out_b"][j]
        else:
            cache = None if caches is None else ((caches["c_k"], j), (caches["c_v"], j), caches["c_kidx"][j])
            o, g, (kv_c, ki) = _mixer_c(grp, x, p["w_main_c"][j], p["w_idx_c"][j], p["h_c"], p["h_idx"], p["d_idx"],
                                        cache, (j, n_c), kv_c)
            new_c.append(ki)
            w_out = p["w_out_c"][j]
        m = grp.b * grp.t
        x = _outproj_ln(o.reshape(m, -1), g.reshape(m, -1), x.reshape(m, grp.d), w_out,
                        p["ln_g"][i], p["ln_b"][i], alpha).reshape(grp.b, grp.t, grp.d)
    stack = lambda sts, n: jnp.stack([s[n] for s in sts])
    heads = lambda a, nh: a.reshape(a.shape[0], grp.b, grp.t, nh, a.shape[-1] * a.shape[-2] // nh)
    h_a = p["b_f_a"].shape[1]
    return (x, heads(kv_a[0], h_a), heads(kv_a[1], h_a), jnp.stack(new_a), stack(new_b, 0), stack(new_b, 1),
            heads(kv_c[0], p["h_c"]), heads(kv_c[1], p["h_c"]), jnp.stack(new_c))


def kernel(x_prompt, x_sample, cache_a_k, cache_a_v, cache_a_logf, state_b_conv, state_b_h, cache_c_k, cache_c_v, cache_c_kidx, w_in_a, b_f_a, w_out_a, w_in_b, conv_w_b, conv_b_b, w_rg_b, b_rg_b, w_ig_b, b_ig_b, lam_b, w_out_b, w_in_c, w_out_c, ln_g, ln_b):
    h_a = b_f_a.shape[1]
    d_a = w_out_a.shape[1]
    d_c = w_out_c.shape[1]
    h_c = cache_c_k.shape[3]
    d_idx = cache_c_kidx.shape[-1]
    h_idx = (w_in_c.shape[2] - 4 * d_c - d_idx) // (d_idx + 1)
    assert d_a // h_a == LANES and d_c // h_c == LANES, "head width must equal the lane count"
    assert w_in_a.shape[2] == 4 * d_a + h_a and d_idx + h_idx <= LANES
    past = cache_a_k.shape[2]
    assert past % CHUNK == 0 and past > 0

    w_idx = w_in_c[:, :, 4 * d_c:]
    w_idx = _pad_to(w_idx, 2, h_idx * d_idx + LANES)
    p = {"w_main_a": w_in_a[:, :, :4 * d_a].astype(BF16),
         "w_f_a": _pad_to(w_in_a[:, :, 4 * d_a:], 2, LANES).astype(BF16),
         "b_f_a": b_f_a, "w_out_a": w_out_a.astype(BF16),
         "w_in_b": w_in_b.astype(BF16), "conv_w_b": conv_w_b, "conv_b_b": conv_b_b, "w_rg_b": w_rg_b,
         "b_rg_b": b_rg_b, "w_ig_b": w_ig_b, "b_ig_b": b_ig_b, "lam_b": lam_b, "w_out_b": w_out_b.astype(BF16),
         "w_main_c": w_in_c[:, :, :4 * d_c].astype(BF16), "w_idx_c": w_idx.astype(BF16),
         "w_out_c": w_out_c.astype(BF16), "ln_g": ln_g, "ln_b": ln_b,
         "h_c": h_c, "h_idx": h_idx, "d_idx": d_idx}
    caches = {"a_k": cache_a_k, "a_v": cache_a_v, "a_logf": cache_a_logf, "b_conv": state_b_conv,
              "b_h": state_b_h, "c_k": cache_c_k, "c_v": cache_c_v, "c_kidx": cache_c_kidx}
    outs_p = _run_trunk(x_prompt, p, None, 0)
    outs_s = _run_trunk(x_sample, p, caches, past)
    return (outs_p[0], outs_s[0]) + outs_p[1:] + outs_s[1:]
```

```python
import functools
import math

import jax
import jax.numpy as jnp
import numpy as np
from jax import lax
from jax.experimental import pallas as pl
from jax.experimental.pallas import tpu as pltpu

NEG_INF = -1e30
LN_EPS = 1e-5
CHUNK = 64
CHUNK_SHIFT = 6
TOPK_MAX = 256
LRU_C = 8.0
LANES = 128
SUBLANES = 8
BF16_ROWS = 16
PLANE_ROWS = 32 * SUBLANES
INT_MIN = -(2 ** 31)
VMEM_LIMIT = 56 * 1024 * 1024
LOG2E = math.log2(math.e)
FOX_WINDOW = 4
SKIP_BITS = 60.0

F32 = jnp.float32
BF16 = jnp.bfloat16


def _cparams(sem, flags=None):
    return pltpu.CompilerParams(dimension_semantics=sem, vmem_limit_bytes=VMEM_LIMIT, flags=flags)


def _pick(n, pref):
    if n <= pref:
        return n
    b = pref
    while n % b:
        b //= 2
    return b


def _inproj_kernel(x_ref, *refs, n_w, n_alias, layer, plan):
    w_refs, out_refs = refs[:n_w], refs[n_w + n_alias:]
    xb = x_ref[0].astype(BF16)
    for w_idx, c0, width, scale, outs in plan:
        r = jnp.dot(xb, w_refs[w_idx][:, c0:c0 + width], preferred_element_type=F32)
        if scale != 1.0:
            r = r * scale
        for o_idx, kind in outs:
            o = out_refs[o_idx]
            if kind == "row":
                o[0] = r.astype(o.dtype)
            elif kind == "norm":
                rr = jnp.square(r.astype(BF16).astype(F32))
                o[0] = jnp.sqrt(jnp.concatenate(
                    [jnp.sum(rr[:, h * LANES:(h + 1) * LANES], axis=1, keepdims=True) for h in range(width // LANES)],
                    axis=1))
            elif kind == "stack":
                val = r.astype(o.dtype).reshape(o.shape[2:])
                if o.shape[0] == 1:
                    o[0, 0] = val
                else:
                    for l in range(o.shape[0]):
                        o[l, 0] = val if l == layer else jnp.zeros_like(val)
            else:
                for h in range(width // LANES):
                    o[0, h] = r[:, h * LANES:(h + 1) * LANES].T.astype(o.dtype)


def _inproj(x3, ws, plan, out_defs, bm, layer=(0, 1), stacked=None):
    bx, tx, d = x3.shape
    j, n_layers = layer
    grid = (bx, tx // bm)
    in_specs = [pl.BlockSpec((1, bm, d), lambda b, i: (b, i, 0))]
    for w in ws:
        in_specs.append(pl.BlockSpec(w.shape, lambda b, i: (0, 0)))
    stacked = list(stacked or [])
    in_specs += [pl.BlockSpec(memory_space=pl.ANY)] * len(stacked)
    out_shape, out_specs, aliases = [], [], {}
    for o_idx, (kind, width, dt) in enumerate(out_defs):
        if kind == "row":
            out_shape.append(jax.ShapeDtypeStruct((bx, tx, width), dt))
            out_specs.append(pl.BlockSpec((1, bm, width), lambda b, i: (b, i, 0)))
        elif kind == "norm":
            out_shape.append(jax.ShapeDtypeStruct((bx, tx, width // LANES), dt))
            out_specs.append(pl.BlockSpec((1, bm, width // LANES), lambda b, i: (b, i, 0)))
        elif kind == "stack":
            out_shape.append(jax.ShapeDtypeStruct((n_layers, bx, tx, width // LANES, LANES), dt))
            if stacked:
                out_specs.append(pl.BlockSpec((1, 1, bm, width // LANES, LANES), lambda b, i: (j, b, i, 0, 0)))
                aliases[1 + len(ws) + len(aliases)] = o_idx
            else:
                out_specs.append(pl.BlockSpec((n_layers, 1, bm, width // LANES, LANES), lambda b, i: (0, b, i, 0, 0)))
        else:
            nh = width // LANES
            out_shape.append(jax.ShapeDtypeStruct((bx, nh, LANES, tx), dt))
            out_specs.append(pl.BlockSpec((1, nh, LANES, bm), lambda b, i: (b, 0, 0, i)))
    assert len(aliases) == len(stacked)
    return pl.pallas_call(
        functools.partial(_inproj_kernel, n_w=len(ws), n_alias=len(stacked), layer=j, plan=tuple(plan)),
        out_shape=out_shape, grid=grid, in_specs=in_specs, out_specs=out_specs, input_output_aliases=aliases,
        compiler_params=_cparams(("parallel", "parallel")), name="inproj",
    )(x3, *ws, *stacked)


def _outproj_ln_kernel(o_ref, g_ref, x_ref, w_ref, lg_ref, lb_ref, y_ref, *, alpha):
    g = g_ref[...].astype(F32)
    og = (o_ref[...].astype(F32) * (g * jax.nn.sigmoid(g))).astype(BF16)
    y = jnp.dot(og, w_ref[...], preferred_element_type=F32)
    z = alpha * x_ref[...] + y
    mu = jnp.mean(z, axis=-1, keepdims=True)
    zc = z - mu
    var = jnp.mean(zc * zc, axis=-1, keepdims=True)
    y_ref[...] = zc * lax.rsqrt(var + LN_EPS) * lg_ref[...] + lb_ref[...]


def _outproj_ln(o2, g2, x2, w, ln_g, ln_b, alpha):
    m, d = x2.shape
    dk = o2.shape[1]
    bm = _pick(m, 1024)
    row = lambda i: (i, 0)
    fixed = lambda i: (0, 0)
    return pl.pallas_call(
        functools.partial(_outproj_ln_kernel, alpha=alpha),
        out_shape=jax.ShapeDtypeStruct((m, d), F32), grid=(m // bm,),
        in_specs=[pl.BlockSpec((bm, dk), row), pl.BlockSpec((bm, dk), row), pl.BlockSpec((bm, d), row),
                  pl.BlockSpec((dk, d), fixed), pl.BlockSpec((1, d), fixed), pl.BlockSpec((1, d), fixed)],
        out_specs=pl.BlockSpec((bm, d), row),
        compiler_params=_cparams(("parallel",)), name="outproj_ln",
    )(o2, g2, x2, w, ln_g.reshape(1, d), ln_b.reshape(1, d))


def _log_sigmoid(x):
    return -(jnp.maximum(-x, 0.0) + jnp.log1p(jnp.exp(-jnp.abs(x))))


def _logf_cumsum_kernel(z_ref, bf_ref, lf_ref, c_ref, *, p0, p1):
    z = z_ref[0]
    pos = lax.broadcasted_iota(jnp.int32, z.shape, 1)
    is_new = jnp.logical_and(pos >= p0, pos < p1)
    lf = jnp.where(is_new, _log_sigmoid(z + bf_ref[...]), z)
    lf_ref[0] = lf
    c = lf
    s = 1
    while s < z.shape[1]:
        c = c + jnp.where(pos >= s, pltpu.roll(c, s, axis=1), 0.0)
        s *= 2
    c_ref[0] = c


def _logf_cumsum(z, b_f, p0, p1):
    b, h, l = z.shape
    blk = pl.BlockSpec((1, h, l), lambda i: (i, 0, 0))
    return pl.pallas_call(
        functools.partial(_logf_cumsum_kernel, p0=p0, p1=p1),
        out_shape=[jax.ShapeDtypeStruct(z.shape, F32)] * 2, grid=(b,),
        in_specs=[blk, pl.BlockSpec((h, 1), lambda i: (0, 0))], out_specs=[blk, blk],
        compiler_params=_cparams(("parallel",)), name="logf_cumsum",
    )(z, b_f.reshape(h, 1))


def _score_store(u, slot, s_sc, mc_sc):
    s_sc[slot] = u
    mc_sc[slot] = jnp.broadcast_to(jnp.max(u, axis=0, keepdims=True), mc_sc.shape[1:])


def _softmax_update(h, slot, vt, s_sc, mc_sc, m_sc, acc_sc, row_shift=None):
    u = s_sc[slot]
    m_prev = m_sc[h]
    m_cur = mc_sc[slot]
    if row_shift is not None:
        m_cur = m_cur + row_shift
    m_new = jnp.maximum(m_prev, m_cur)
    m_row = m_new[0:1]
    p = jnp.exp2(u - (m_row if row_shift is None else m_row - row_shift))
    alpha = jnp.exp2(m_prev - m_new)
    vt1 = jnp.concatenate([vt, jnp.ones((BF16_ROWS, vt.shape[1]), BF16)], axis=0)
    acc_sc[h] = acc_sc[h] * alpha[0:1] + jnp.dot(vt1, p.astype(BF16), preferred_element_type=F32)
    m_sc[h] = m_new


def _head(k_ref, h):
    dh = LANES
    return k_ref[0, :, h * dh:(h + 1) * dh]


def _head_t(v_ref, h, v_rows):
    if not v_rows:
        return v_ref[0, h]
    return _head(v_ref, h).astype(F32).T.astype(BF16)


def _scores(k, kx, qt, qx):
    return jnp.dot(jnp.concatenate([k, kx], axis=1), jnp.concatenate([qt, qx], axis=0),
                   preferred_element_type=F32)


def _pipelined_heads(nh, qk_phase, sm_phase):
    qk_phase(0, 0)
    for h in range(nh - 1):
        qk_phase(h + 1, (h + 1) % 2)
        sm_phase(h, h % 2)
    sm_phase(nh - 1, (nh - 1) % 2)


def _attn_init(m_sc, acc_sc):
    m_sc[...] = jnp.full(m_sc.shape, -jnp.inf, F32)
    acc_sc[...] = jnp.zeros(acc_sc.shape, F32)


def _attn_finish(o_ref, acc_sc, nh, dh):
    for h in range(nh):
        a = acc_sc[h]
        o_ref[0, :, h * dh:(h + 1) * dh] = (a[0:dh] / a[dh:dh + 1]).T.astype(o_ref.dtype)


def _attn_scratch(nh, bq, bk, dh):
    return [pltpu.VMEM((nh, SUBLANES, bq), F32), pltpu.VMEM((nh, dh + BF16_ROWS, bq), F32),
            pltpu.VMEM((2, bk, bq), F32), pltpu.VMEM((2, SUBLANES, bq), F32)]


def _fox_kernel(lo_ref, qt_ref, k_ref, v_ref, cq_ref, kx_ref, qx_ref, o_ref, m_sc, acc_sc, s_sc, mc_sc,
                *, bq, bk, q_off, nh, v_rows, window):
    qi, kstep = pl.program_id(1), pl.program_id(2)
    q_first = q_off + qi * bq
    q_last = q_first + bq - 1
    ki = kstep + _fox_first_step(q_last // bk, window)
    k_first = ki * bk
    k_last = k_first + bk - 1

    @pl.when(kstep == 0)
    def _():
        _attn_init(m_sc, acc_sc)

    def step(masked):
        kx = kx_ref[0]
        if masked:
            kpos = k_first + lax.broadcasted_iota(jnp.int32, (bk, bq), 0)
            qpos = q_first + lax.broadcasted_iota(jnp.int32, (bk, bq), 1)
            causal = kpos <= qpos

        def qk_phase(h, slot):
            u = _scores(_head(k_ref, h), kx, qt_ref[0, h], qx_ref[h])
            if masked:
                u = jnp.where(causal, u, NEG_INF)
            _score_store(u, slot, s_sc, mc_sc)

        def sm_phase(h, slot):
            cq2 = cq_ref[0, pl.ds(h, 1), :] * LOG2E
            _softmax_update(h, slot, _head_t(v_ref, h, v_rows), s_sc, mc_sc, m_sc, acc_sc, row_shift=cq2)

        _pipelined_heads(nh, qk_phase, sm_phase)

    needed = jnp.logical_and(k_first <= q_last, ki >= jnp.maximum(lo_ref[pl.program_id(0), qi], 0))
    straddles = k_last > q_first

    @pl.when(jnp.logical_and(needed, straddles))
    def _():
        step(True)

    @pl.when(jnp.logical_and(needed, jnp.logical_not(straddles)))
    def _():
        step(False)

    @pl.when(kstep == pl.num_programs(2) - 1)
    def _():
        _attn_finish(o_ref, acc_sc, nh, LANES)


def _fox_first_step(last_block, window):
    return 0 if window is None else last_block - (window - 1)


def _split3(x):
    def top_half(v):
        bits = lax.bitcast_convert_type(v, jnp.uint32) & jnp.uint32(0xFFFF0000)
        return lax.bitcast_convert_type(bits, F32)

    hi = top_half(x)
    r1 = x - hi
    mid = top_half(r1)
    return hi.astype(BF16), mid.astype(BF16), (r1 - mid).astype(BF16)


def _v_spec(v, nh, dh, bk, kmap):
    if v.ndim == 3:
        return True, pl.BlockSpec((1, bk, nh * dh), lambda *ix: (ix[0], kmap(*ix), 0))
    return False, pl.BlockSpec((1, nh, dh, bk), lambda *ix: (ix[0], 0, 0, kmap(*ix)))


def _fox_first_block(qn_rows, kn_rows, cq, ck, q_off, bq, bk):
    b, nh, tq = cq.shape
    nq, nk = tq // bq, ck.shape[2] // bk
    if nk == 1 or qn_rows is None:
        return jnp.zeros((b, nq), jnp.int32)
    qn = qn_rows.reshape(b, nq, bq, nh).max(axis=2).transpose(0, 2, 1)
    kn = kn_rows.reshape(b, nk, bk, nh).max(axis=2).transpose(0, 2, 1)
    first_self = [(q_off + qi * bq) // bk for qi in range(nq)]
    last_self = [min((q_off + (qi + 1) * bq - 1) // bk, nk - 1) for qi in range(nq)]
    kn_self = jnp.stack([kn[:, :, f:l + 1].max(axis=-1) for f, l in zip(first_self, last_self)], axis=-1)
    c_first = cq[:, :, ::bq]
    c_last = ck[:, :, bk - 1::bk]
    bound = (qn[..., :, None] * (kn[..., None, :] + kn_self[..., :, None])
             + (c_first[..., :, None] - c_last[..., None, :]) * LOG2E)
    skip = jnp.all(bound < -SKIP_BITS, axis=1)
    lo = jnp.sum(jnp.cumprod(skip.astype(jnp.int32), axis=-1), axis=-1)
    return jnp.minimum(lo, jnp.asarray(first_self, jnp.int32)[None, :]).astype(jnp.int32)


def _fox_attention(qt_hm, k_rm, v, cq, ck, q_off, bq, bk, qn_rows=None, kn_rows=None):
    b, nh, dh, tq = qt_hm.shape
    s = k_rm.shape[1]
    nq, nk = tq // bq, s // bk
    w = LANES // nh
    assert w >= 3
    kx = jnp.stack(_split3(ck * (-LOG2E)), axis=-1)
    kx = _pad_to(kx, 3, w).transpose(0, 2, 1, 3).reshape(b, s, LANES)
    rows = jnp.arange(LANES)[None, :, None]
    heads = jnp.arange(nh)[:, None, None]
    qx = jnp.broadcast_to(jnp.logical_and(rows >= heads * w, rows < heads * w + 3), (nh, LANES, bq)).astype(BF16)

    lo = _fox_first_block(qn_rows, kn_rows, cq, ck, q_off, bq, bk)

    last = [(q_off + (qi + 1) * bq - 1) // bk for qi in range(nq)]

    def run(window):
        def kmap(bb, qi, kstep, lo_ref):
            last_block = (q_off + (qi + 1) * bq - 1) // bk
            ki = kstep + _fox_first_step(last_block, window)
            return jnp.clip(ki, lo_ref[bb, qi], last_block)

        v_rows, v_spec = _v_spec(v, nh, dh, bk, kmap)
        grid_spec = pltpu.PrefetchScalarGridSpec(
            num_scalar_prefetch=1, grid=(b, nq, nk if window is None else window),
            in_specs=[pl.BlockSpec((1, nh, dh, bq), lambda bb, qi, ki, lo_ref: (bb, 0, 0, qi)),
                      pl.BlockSpec((1, bk, nh * dh), lambda *ix: (ix[0], kmap(*ix), 0)),
                      v_spec,
                      pl.BlockSpec((1, nh, bq), lambda bb, qi, ki, lo_ref: (bb, 0, qi)),
                      pl.BlockSpec((1, bk, LANES), lambda *ix: (ix[0], kmap(*ix), 0)),
                      pl.BlockSpec((nh, LANES, bq), lambda bb, qi, ki, lo_ref: (0, 0, 0))],
            out_specs=pl.BlockSpec((1, bq, nh * dh), lambda bb, qi, ki, lo_ref: (bb, qi, 0)),
            scratch_shapes=_attn_scratch(nh, bq, bk, dh))
        return pl.pallas_call(
            functools.partial(_fox_kernel, bq=bq, bk=bk, q_off=q_off, nh=nh, v_rows=v_rows, window=window),
            out_shape=jax.ShapeDtypeStruct((b, tq, nh * dh), BF16), grid_spec=grid_spec,
            compiler_params=_cparams(("parallel", "parallel", "arbitrary")), name="fox_attention",
        )(lo, qt_hm, k_rm, v, cq, kx, qx)

    if nk <= FOX_WINDOW:
        return run(None)
    fits = jnp.all(jnp.asarray(last, jnp.int32)[None, :] - lo < FOX_WINDOW)
    return lax.cond(fits, lambda: run(FOX_WINDOW), lambda: run(None))


def _bit_planes(words):
    a = list(words)
    j, m = 16, 0x0000FFFF
    while j:
        k = 0
        while k < 32:
            t = (a[k] ^ lax.shift_right_logical(a[k + j], jnp.int32(j))) & jnp.int32(m)
            a[k] = a[k] ^ t
            a[k + j] = a[k + j] ^ lax.shift_left(t, jnp.int32(j))
            k = (k + j + 1) & ~j
        j >>= 1
        m = (m ^ (m << j)) & 0xFFFFFFFF
    return a


def _dsa_select_kernel(qit_ref, wit_ref, ki_ref, out_ref, key_sc, pl_sc, act_sc, j_sc,
                       *, bq, kb, q_off, s_valid, topk, nh_idx, idx_bits):
    qb = pl.program_id(1)
    q_first = q_off + qb * bq
    q_last = q_first + bq - 1
    n_adm = jnp.minimum(((q_last >> CHUNK_SHIFT) + 1) * CHUNK, s_valid)
    nkb = (jnp.maximum(n_adm, topk) + kb - 1) // kb
    nkb_total = key_sc.shape[0] // kb
    gpb = kb // PLANE_ROWS
    nkb4 = nkb // 4
    qchunk = (q_first + lax.broadcasted_iota(jnp.int32, (kb, bq), 1)) >> CHUNK_SHIFT
    row = lax.broadcasted_iota(jnp.int32, (kb, bq), 0)

    def admissible(kpos):
        return jnp.logical_and((kpos >> CHUNK_SHIFT) <= qchunk, kpos < s_valid)

    def rows(kblk):
        return pl.ds(pl.multiple_of(kblk * kb, kb), kb)

    def score_block(kblk, carry, all_admissible):
        kblock = ki_ref[0, rows(kblk), :]
        sc = jnp.zeros((kb, bq), F32)
        for h in range(nh_idx):
            d = jnp.dot(kblock, qit_ref[0, h], preferred_element_type=F32)
            sc = sc + wit_ref[0, h:h + 1, :] * jnp.maximum(d, 0.0)
        sc = sc + 0.0
        if not all_admissible:
            sc = jnp.where(admissible(kblk * kb + row), sc, NEG_INF)
        bits = lax.bitcast_convert_type(sc, jnp.int32)
        key = bits ^ ((bits >> 31) & 0x7FFFFFFF)
        key_sc[rows(kblk), :] = key
        for g in range(gpb):
            base = g * PLANE_ROWS
            planes = _bit_planes([key[base + SUBLANES * j:base + SUBLANES * (j + 1), :] for j in range(32)])
            planes[0] = ~planes[0]
            for i in range(32):
                pl_sc[kblk * gpb + g, i] = planes[i]
            act_sc[kblk * gpb + g] = jnp.full((SUBLANES, bq), -1, jnp.int32)
        return carry

    n_full = jnp.minimum(((q_first >> CHUNK_SHIFT) + 1) * CHUNK, s_valid) // kb
    lax.fori_loop(0, n_full, functools.partial(score_block, all_admissible=True), 0)
    lax.fori_loop(n_full, nkb, functools.partial(score_block, all_admissible=False), 0)

    def two_bit_step(it, carry):
        thr_u, above = carry
        i1, i0 = 2 * it, 2 * it + 1

        def split(g):
            a = act_sc[g]
            a1 = a & pl_sc[g, i1]
            x11 = a1 & pl_sc[g, i0]
            return a, x11, a1 ^ x11, (a & pl_sc[g, i0]) ^ x11

        def cnt_groups(g0, n, accs):
            c11, c10, c01 = accs
            for g in range(n):
                _, x11, x10, x01 = split(g0 + g)
                c11 = c11 + lax.population_count(x11)
                c10 = c10 + lax.population_count(x10)
                c01 = c01 + lax.population_count(x01)
            return c11, c10, c01

        zero = jnp.zeros((SUBLANES, bq), jnp.int32)
        accs = lax.fori_loop(0, nkb4, lambda i, c: cnt_groups(i * gpb * 4, gpb * 4, c), (zero, zero, zero))
        accs = lax.fori_loop(nkb4 * 4, nkb, lambda i, c: cnt_groups(i * gpb, gpb, c), accs)
        c11, c10, c01 = [jnp.sum(c.astype(F32), axis=0, keepdims=True) for c in accs]
        s1 = above + c11
        s2 = s1 + c10
        s3 = s2 + c01
        t11, t10, t01 = s1 >= topk, s2 >= topk, s3 >= topk

        def upd_groups(g0, n, carry2):
            for g in range(n):
                a, x11, x10, x01 = split(g0 + g)
                x00 = a ^ x11 ^ x10 ^ x01
                act_sc[g0 + g] = jnp.where(t11, x11, jnp.where(t10, x10, jnp.where(t01, x01, x00)))
            return carry2

        lax.fori_loop(0, nkb4, lambda i, c2: upd_groups(i * gpb * 4, gpb * 4, c2), 0)
        lax.fori_loop(nkb4 * 4, nkb, lambda i, c2: upd_groups(i * gpb, gpb, c2), 0)
        hi = lax.shift_left(jnp.int32(1), 31 - i1)
        lo = lax.shift_left(jnp.int32(1), 31 - i0)
        low_bit = jnp.where(t11, lo, jnp.where(t10, 0, jnp.where(t01, lo, 0)))
        return (thr_u | jnp.where(t10, hi, 0) | low_bit,
                jnp.where(t11, above, jnp.where(t10, s1, jnp.where(t01, s2, s3))))

    thr_u, n_gt = lax.fori_loop(0, 16, two_bit_step, (jnp.zeros((1, bq), jnp.int32), jnp.zeros((1, bq), F32)))
    thr = thr_u ^ INT_MIN

    def eq_blk(kblk, acc):
        for g in range(gpb):
            acc = acc + lax.population_count(act_sc[kblk * gpb + g])
        return acc

    n_eq = jnp.sum(lax.fori_loop(0, nkb, eq_blk, jnp.zeros((SUBLANES, bq), jnp.int32)).astype(F32),
                   axis=0, keepdims=True)
    need = topk - n_gt
    has_excess = jnp.max(n_eq - need) > 0.0
    j_sc[...] = jnp.full(j_sc.shape, 2 ** 30, jnp.int32)

    @pl.when(has_excess)
    def _():
        def count_ties_below(cand):
            def blk(kblk, acc):
                hit = jnp.logical_and(key_sc[rows(kblk), :] == thr, kblk * kb + row < cand)
                ind = jnp.where(hit, 1.0, 0.0)
                for r in range(kb // SUBLANES):
                    acc = acc + ind[r * SUBLANES:(r + 1) * SUBLANES]
                return acc
            return jnp.sum(lax.fori_loop(0, nkb, blk, jnp.zeros((SUBLANES, bq), F32)), axis=0, keepdims=True)

        def idx_step(it, lo):
            cand = lo | lax.shift_left(jnp.int32(1), idx_bits - 1 - it)
            return jnp.where(count_ties_below(cand) < need, cand, lo)

        lo = lax.fori_loop(0, idx_bits, idx_step, jnp.zeros((1, bq), jnp.int32))
        j_sc[...] = jnp.broadcast_to(lo, j_sc.shape)

    j_cut = j_sc[0:1, :]

    def write_fast(kblk, carry):
        out_ref[0, 0, rows(kblk), :] = jnp.where(key_sc[rows(kblk), :] >= thr, 0.0, NEG_INF).astype(out_ref.dtype)
        return carry

    def write_block(kblk, carry):
        key = key_sc[rows(kblk), :]
        kpos = kblk * kb + row
        tie = jnp.logical_and(key == thr, kpos <= j_cut)
        sel = jnp.logical_and(jnp.logical_or(key > thr, tie), admissible(kpos))
        out_ref[0, 0, rows(kblk), :] = jnp.where(sel, 0.0, NEG_INF).astype(out_ref.dtype)
        return carry

    def fill_block(kblk, carry):
        out_ref[0, 0, rows(kblk), :] = jnp.full((kb, bq), NEG_INF, out_ref.dtype)
        return carry

    n_fast = jnp.where(has_excess, 0, n_full)
    lax.fori_loop(0, n_fast, write_fast, 0)
    lax.fori_loop(n_fast, nkb, write_block, 0)
    lax.fori_loop(nkb, nkb_total, fill_block, 0)


def _dsa_select(qit_hm, wit, ki, q_off, s_valid, topk, bq, kb):
    b, nh_idx, d_idx, tq = qit_hm.shape
    s_pad = ki.shape[1]
    return pl.pallas_call(
        functools.partial(_dsa_select_kernel, bq=bq, kb=kb, q_off=q_off, s_valid=s_valid, topk=topk,
                          nh_idx=nh_idx, idx_bits=max(1, (s_pad - 1).bit_length())),
        out_shape=jax.ShapeDtypeStruct((b, tq // bq, s_pad, bq), BF16), grid=(b, tq // bq),
        in_specs=[pl.BlockSpec((1, nh_idx, d_idx, bq), lambda bb, qb: (bb, 0, 0, qb)),
                  pl.BlockSpec((1, nh_idx, bq), lambda bb, qb: (bb, 0, qb)),
                  pl.BlockSpec((1, s_pad, d_idx), lambda bb, qb: (bb, 0, 0))],
        out_specs=pl.BlockSpec((1, 1, s_pad, bq), lambda bb, qb: (bb, qb, 0, 0)),
        scratch_shapes=[pltpu.VMEM((s_pad, bq), jnp.int32),
                        pltpu.VMEM((s_pad // PLANE_ROWS, 32, SUBLANES, bq), jnp.int32),
                        pltpu.VMEM((s_pad // PLANE_ROWS, SUBLANES, bq), jnp.int32),
                        pltpu.VMEM((SUBLANES, bq), jnp.int32)],
        compiler_params=_cparams(("parallel", "arbitrary")), name="dsa_select",
    )(qit_hm, wit, ki)


def _dsa_attn_kernel(sl_ref, qt_ref, k_ref, v_ref, mb_ref, qx_ref, o_ref, m_sc, acc_sc, s_sc, mc_sc,
                     base_sc, t_sc, *, bq, bk, q_off, nh, v_rows):
    qi, ki = pl.program_id(1), pl.program_id(2)
    q_first = q_off + qi * bq
    q_last = q_first + bq - 1
    k_first = ki * bk
    k_last = k_first + bk - 1
    qposf = (q_first + lax.broadcasted_iota(jnp.int32, (1, bq), 1)).astype(F32)

    @pl.when(ki == 0)
    def _():
        _attn_init(m_sc, acc_sc)

    def step(past):
        base_sc[...] = jnp.concatenate([mb_ref[0, j] for j in range(mb_ref.shape[1])], axis=1).astype(F32)
        if past:
            kpos = k_first + lax.broadcasted_iota(jnp.int32, (bk, LANES), 0)
            lane = lax.broadcasted_iota(jnp.int32, (bk, LANES), 1)
            kx = jnp.where(lane < 3, kpos >> CHUNK_SHIFT, jnp.where(lane < 6, kpos & (CHUNK - 1), 0)).astype(BF16)
        else:
            kpos = k_first + lax.broadcasted_iota(jnp.int32, (bk, bq), 0)
            qpos = q_first + lax.broadcasted_iota(jnp.int32, (bk, bq), 1)
            t_sc[...] = jnp.minimum(kpos, 2 * qpos - kpos).astype(F32)

        def qk_phase(h, slot):
            if past:
                u = _scores(_head(k_ref, h), kx, qt_ref[0, h], qx_ref[h]) + base_sc[...]
            else:
                st = jnp.dot(_head(k_ref, h), qt_ref[0, h], preferred_element_type=F32)
                u = st + sl_ref[h] * t_sc[...] + base_sc[...]
            _score_store(u, slot, s_sc, mc_sc)

        def sm_phase(h, slot):
            _softmax_update(h, slot, _head_t(v_ref, h, v_rows), s_sc, mc_sc, m_sc, acc_sc,
                            row_shift=-sl_ref[h] * qposf)

        _pipelined_heads(nh, qk_phase, sm_phase)

    needed = k_first <= ((q_last >> CHUNK_SHIFT) << CHUNK_SHIFT) + CHUNK - 1
    past = k_last <= q_first

    @pl.when(past)
    def _():
        step(True)

    @pl.when(jnp.logical_and(needed, jnp.logical_not(past)))
    def _():
        step(False)

    @pl.when(ki == pl.num_programs(2) - 1)
    def _():
        _attn_finish(o_ref, acc_sc, nh, LANES)


def _dsa_attention(qt_hm, k_rm, v, mask, q_off, bq, bk):
    b, nh, dh, tq = qt_hm.shape
    s, mq = mask.shape[2], mask.shape[3]
    nq, nk, nsub = tq // bq, s // bk, bq // mq
    slopes2_np = (2.0 ** (-8.0 * np.arange(1, nh + 1) / nh) * LOG2E).astype(np.float32)
    slopes2 = jnp.asarray(slopes2_np)
    rest, pieces = slopes2_np, []
    for _ in range(3):
        piece = rest.astype(BF16).astype(np.float32)
        pieces.append(piece)
        rest = rest - piece
    col = np.zeros((nh, LANES), np.float32)
    col[:, 0:3] = np.stack(pieces, axis=1) * CHUNK
    col[:, 3:6] = np.stack(pieces, axis=1)
    qx = jnp.broadcast_to(jnp.asarray(col.astype(BF16))[:, :, None], (nh, LANES, bq))

    def kmap(bb, qi, ki):
        q_last = q_off + (qi + 1) * bq - 1
        return jnp.minimum(ki, (((q_last >> CHUNK_SHIFT) << CHUNK_SHIFT) + CHUNK - 1) // bk)

    v_rows, v_spec = _v_spec(v, nh, dh, bk, kmap)
    return pl.pallas_call(
        functools.partial(_dsa_attn_kernel, bq=bq, bk=bk, q_off=q_off, nh=nh, v_rows=v_rows),
        out_shape=jax.ShapeDtypeStruct((b, tq, nh * dh), BF16), grid=(b, nq, nk),
        in_specs=[pl.BlockSpec(memory_space=pltpu.SMEM),
                  pl.BlockSpec((1, nh, dh, bq), lambda bb, qi, ki: (bb, 0, 0, qi)),
                  pl.BlockSpec((1, bk, nh * dh), lambda bb, qi, ki: (bb, kmap(bb, qi, ki), 0)),
                  v_spec,
                  pl.BlockSpec((1, nsub, bk, mq), lambda bb, qi, ki: (bb, qi, kmap(bb, qi, ki), 0)),
                  pl.BlockSpec((nh, LANES, bq), lambda bb, qi, ki: (0, 0, 0))],
        out_specs=pl.BlockSpec((1, bq, nh * dh), lambda bb, qi, ki: (bb, qi, 0)),
        scratch_shapes=_attn_scratch(nh, bq, bk, dh) + [pltpu.VMEM((bk, bq), F32), pltpu.VMEM((bk, bq), F32)],
        compiler_params=_cparams(("parallel", "parallel", "arbitrary")), name="dsa_attention",
    )(slopes2, qt_hm, k_rm, v, mask, qx)


def _rglru_kernel(u_ref, cw_ref, cb_ref, wr_ref, br_ref, wig_ref, big_ref, lam_ref, buf0_ref, h0_ref,
                  h_ref, conv_ref, hl_ref, ubuf, a_sc, b_sc, hbuf, hcar, *, tt, conv_w, nblk):
    t = pl.program_id(1)
    pad = SUBLANES
    d = u_ref.shape[2]
    blk = d // nblk

    @pl.when(t == 0)
    def _():
        ubuf[0:pad, :] = buf0_ref[0]
        hcar[...] = jnp.broadcast_to(h0_ref[0], (SUBLANES, d))

    ubuf[pad:pad + tt, :] = u_ref[0]
    uc = cb_ref[...] + ubuf[pad:pad + tt, :] * cw_ref[conv_w - 1:conv_w, :]
    for j in range(conv_w - 1):
        off = pad - (conv_w - 1) + j
        uc = uc + ubuf[off:off + tt, :] * cw_ref[j:j + 1, :]

    lam = lam_ref[...]
    neg_sp = -LRU_C * (jnp.maximum(-lam, 0.0) + jnp.log1p(jnp.exp(-jnp.abs(lam))))
    for n in range(nblk):
        cs = slice(n * blk, (n + 1) * blk)
        ucn = uc[:, cs]
        ub = ucn.astype(BF16)
        r = jax.nn.sigmoid(jnp.dot(ub, wr_ref[n], preferred_element_type=F32) + br_ref[:, cs])
        i = jax.nn.sigmoid(jnp.dot(ub, wig_ref[n], preferred_element_type=F32) + big_ref[:, cs])
        log_a = r * neg_sp[:, cs]
        a = jnp.exp(log_a)
        a_sc[:, cs] = a
        b_sc[:, cs] = jnp.sqrt(-jnp.tanh(log_a) * (a * a + 1.0)) * (i * ucn)

    row = lax.broadcasted_iota(jnp.int32, (SUBLANES, d), 0)

    def group(gi, hprev):
        r0 = pl.multiple_of(gi * SUBLANES, SUBLANES)
        av = a_sc[pl.ds(r0, SUBLANES), :]
        bv = b_sc[pl.ds(r0, SUBLANES), :]
        s = 1
        while s < SUBLANES:
            a_sh = pltpu.roll(av, s, axis=0)
            b_sh = pltpu.roll(bv, s, axis=0)
            m = row >= s
            bv = jnp.where(m, av * b_sh + bv, bv)
            av = jnp.where(m, av * a_sh, av)
            s *= 2
        hrows = av * hprev + bv
        hbuf[pl.ds(r0, SUBLANES), :] = hrows
        return jnp.broadcast_to(hrows[SUBLANES - 1:SUBLANES, :], (SUBLANES, d))

    hlast = lax.fori_loop(0, tt // SUBLANES, group, hcar[...])
    h_ref[0] = hbuf[...].astype(h_ref.dtype)
    hcar[...] = hlast
    hl_ref[0] = hlast[0:1, :]
    tail = ubuf[tt:tt + pad, :]
    conv_ref[0] = tail
    ubuf[0:pad, :] = tail


def _rglru(u, conv_w, conv_b, w_rg, b_rg, w_ig, b_ig, lam, buf0, h0, tt):
    b, t, d = u.shape
    cw = conv_w.shape[0]
    nblk, blk = w_rg.shape[0], w_rg.shape[1]
    vec = lambda a: a.reshape(1, d)
    fixed2 = lambda bb, ti: (0, 0)
    fixed3 = lambda bb, ti: (0, 0, 0)
    perb = lambda bb, ti: (bb, 0, 0)
    return pl.pallas_call(
        functools.partial(_rglru_kernel, tt=tt, conv_w=cw, nblk=nblk),
        out_shape=[jax.ShapeDtypeStruct((b, t, d), BF16), jax.ShapeDtypeStruct((b, SUBLANES, d), F32),
                   jax.ShapeDtypeStruct((b, 1, d), F32)],
        grid=(b, t // tt),
        in_specs=[pl.BlockSpec((1, tt, d), lambda bb, ti: (bb, ti, 0)),
                  pl.BlockSpec((cw, d), fixed2), pl.BlockSpec((1, d), fixed2),
                  pl.BlockSpec((nblk, blk, blk), fixed3), pl.BlockSpec((1, d), fixed2),
                  pl.BlockSpec((nblk, blk, blk), fixed3), pl.BlockSpec((1, d), fixed2),
                  pl.BlockSpec((1, d), fixed2),
                  pl.BlockSpec((1, SUBLANES, d), perb), pl.BlockSpec((1, 1, d), perb)],
        out_specs=[pl.BlockSpec((1, tt, d), lambda bb, ti: (bb, ti, 0)),
                   pl.BlockSpec((1, SUBLANES, d), perb), pl.BlockSpec((1, 1, d), perb)],
        scratch_shapes=[pltpu.VMEM((SUBLANES + tt, d), F32), pltpu.VMEM((tt, d), F32),
                        pltpu.VMEM((tt, d), F32), pltpu.VMEM((tt, d), F32), pltpu.VMEM((SUBLANES, d), F32)],
        compiler_params=_cparams(("parallel", "arbitrary")), name="rglru",
    )(u, conv_w, vec(conv_b), w_rg.astype(BF16), vec(b_rg), w_ig.astype(BF16), vec(b_ig), vec(lam), buf0, h0)


def _pack_rows_kernel(c_ref, n_ref, o_ref):
    p, t = c_ref.shape[1], n_ref.shape[1]
    o_ref[0, 0:p, :] = c_ref[0].reshape(p, o_ref.shape[2]).astype(o_ref.dtype)
    o_ref[0, p:p + t, :] = n_ref[0]
    o_ref[0, p + t:, :] = jnp.zeros((o_ref.shape[1] - p - t, o_ref.shape[2]), o_ref.dtype)


def _pack_rows(caches, j, new, s_pad):
    _, b, p, nh, dh = caches.shape
    t, w = new.shape[1], new.shape[2]
    return pl.pallas_call(
        _pack_rows_kernel, out_shape=jax.ShapeDtypeStruct((b, s_pad, w), BF16), grid=(b,),
        in_specs=[pl.BlockSpec((None, 1, p, nh, dh), lambda i: (j, i, 0, 0, 0)),
                  pl.BlockSpec((1, t, w), lambda i: (i, 0, 0))],
        out_specs=pl.BlockSpec((1, s_pad, w), lambda i: (i, 0, 0)),
        compiler_params=_cparams(("parallel",)), name="pack_rows",
    )(caches, new)


def _pad_to(a, axis, n):
    extra = n - a.shape[axis]
    if extra == 0:
        return a
    widths = [(0, 0)] * a.ndim
    widths[axis] = (0, extra)
    return jnp.pad(a, widths)


def _round_up(n, m):
    return (n + m - 1) // m * m


class _Group:
    def __init__(self, x, past):
        self.b, self.t, self.d = x.shape
        self.past = past
        self.flat = past > 0
        self.s_valid = past + self.t
        if self.flat:
            self.tq = _round_up(self.t, LANES)
            self.s_pad = _round_up(self.s_valid, PLANE_ROWS)
            self.fox_bq = self.dsa_bq = self.tq
            self.bk = self.s_pad
        else:
            self.tq = self.s_pad = self.t
            self.fox_bq = _pick(self.t, 512)
            self.dsa_bq = _pick(self.t, 256)
            self.bk = _pick(self.t, 512)

    def proj_view(self, x):
        return x.reshape(1, self.b * self.t, self.d) if self.flat else x

    def proj_bm(self):
        return self.b * self.t if self.flat else _pick(self.t, 512)

    def rows(self, a):
        return a.reshape(self.b, self.t, a.shape[-1])

    def heads_t(self, a):
        if not self.flat:
            return a
        return a.reshape(a.shape[1], a.shape[2], self.b, self.t).transpose(2, 0, 1, 3)

    def pad_q(self, a):
        return _pad_to(a, a.ndim - 1, self.tq)

    def keys(self, new_rm, cache):
        return new_rm if cache is None else _pack_rows(cache[0], cache[1], new_rm, self.s_pad)

    def values(self, new, cache):
        return new if cache is None else self.keys(self.rows(new), cache)


def _qkvg_plan(width, dh, v_rows):
    plan = [(0, 0, width, dh ** -0.5 * LOG2E, [(0, "headT")]),
            (0, width, width, 1.0, [(1, "row"), (3, "stack")]),
            (0, 2 * width, width, 1.0, [(2, "row" if v_rows else "headT"), (4, "stack")]),
            (0, 3 * width, width, 1.0, [(5, "row")])]
    outs = [("headT", width, BF16), ("row", width, BF16), ("row" if v_rows else "headT", width, BF16),
            ("stack", width, F32), ("stack", width, F32), ("row", width, BF16)]
    return plan, outs


def _mixer_a(grp, x, w_main, w_f, b_f, cache, layer, stacked):
    nh = b_f.shape[0]
    da = w_main.shape[1] // 4
    plan, outs = _qkvg_plan(da, da // nh, grp.flat)
    plan = plan + [(1, 0, LANES, 1.0, [(6, "row")])]
    outs = outs + [("row", LANES, F32)]
    qn = kn = None
    if not grp.flat:
        plan[0][4].append((7, "norm"))
        plan[1][4].append((8, "norm"))
        outs = outs + [("norm", da, F32)] * 2
        qt, k_rm, vt, k, v, g, fl, qn, kn = _inproj(grp.proj_view(x), [w_main, w_f], plan, outs, grp.proj_bm(), layer,
                                                    stacked)
    else:
        qt, k_rm, vt, k, v, g, fl = _inproj(grp.proj_view(x), [w_main, w_f], plan, outs, grp.proj_bm(), layer,
                                            stacked)
    qt, k_rm, g = grp.pad_q(grp.heads_t(qt)), grp.rows(k_rm), grp.rows(g)
    z = grp.rows(fl)[:, :, :nh].transpose(0, 2, 1)
    ck_, cv_ = (None, None) if cache is None else cache[:2]
    if cache is not None:
        z = _pad_to(jnp.concatenate([cache[2].astype(F32).transpose(0, 2, 1), z], axis=2), 2, grp.s_pad)
    lf_all, c_all = _logf_cumsum(z, b_f, grp.past, grp.s_valid)
    logf = lf_all[:, :, grp.past:grp.s_valid].transpose(0, 2, 1)
    cq = grp.pad_q(c_all[:, :, grp.past:grp.s_valid])
    o = _fox_attention(qt, grp.keys(k_rm, ck_), grp.values(vt, cv_), cq, c_all,
                       grp.past, grp.fox_bq, grp.bk, qn, kn)[:, :grp.t]
    return o, g, ([k, v], logf)


def _mixer_b(grp, x, w_in, conv_w, conv_b, w_rg, b_rg, w_ig, b_ig, lam, state):
    dr = w_in.shape[1] // 2
    plan = [(0, 0, dr, 1.0, [(0, "row")]), (0, dr, dr, 1.0, [(1, "row")])]
    u, g = _inproj(grp.proj_view(x), [w_in], plan, [("row", dr, F32), ("row", dr, BF16)], grp.proj_bm())
    u, g = grp.rows(u), grp.rows(g)
    cw = conv_w.shape[0]
    if state is None:
        buf0 = jnp.zeros((grp.b, SUBLANES, dr), F32)
        h0 = jnp.zeros((grp.b, 1, dr), F32)
    else:
        buf, h0 = state
        buf0 = jnp.pad(buf.astype(F32), ((0, 0), (SUBLANES - (cw - 1), 0), (0, 0)))
        h0 = h0.astype(F32).reshape(grp.b, 1, dr)
    h, tail, hl = _rglru(u, conv_w, conv_b, w_rg, b_rg, w_ig, b_ig, lam, buf0, h0, _pick(grp.t, 512))
    return h, g, (tail[:, SUBLANES - (cw - 1):], hl[:, 0])


def _mixer_c(grp, x, w_main, w_idx, nh, nh_idx, d_idx, cache, layer, stacked):
    dc = w_main.shape[1] // 4
    wq = nh_idx * d_idx
    plan, outs = _qkvg_plan(dc, dc // nh, grp.flat)
    plan = plan + [(1, 0, wq, 1.0, [(6, "row")]), (1, wq, LANES, 1.0, [(7, "row")])]
    outs = outs + [("row", wq, BF16), ("row", LANES, F32)]
    qt, k_rm, vt, k, v, g, qi, kw = _inproj(grp.proj_view(x), [w_main, w_idx], plan, outs, grp.proj_bm(), layer,
                                            stacked)
    qt, k_rm = grp.pad_q(grp.heads_t(qt)), grp.rows(k_rm)
    g, qi, kw = grp.rows(g), grp.rows(qi), grp.rows(kw)
    ki = kw[:, :, :d_idx]
    wit = grp.pad_q(kw[:, :, d_idx:d_idx + nh_idx].transpose(0, 2, 1))
    qit = grp.pad_q(qi.reshape(grp.b, grp.t, nh_idx, d_idx).transpose(0, 2, 3, 1))
    ck_, cv_ = (None, None) if cache is None else cache[:2]
    ki_all = ki if cache is None else _pad_to(jnp.concatenate([cache[2].astype(F32), ki], axis=1), 1, grp.s_pad)
    topk = min(TOPK_MAX, grp.s_valid // 4)
    mask = _dsa_select(qit, wit, ki_all.astype(BF16), grp.past, grp.s_valid, topk, grp.dsa_bq, grp.bk)
    o = _dsa_attention(qt, grp.keys(k_rm, ck_), grp.values(vt, cv_), mask, grp.past, grp.fox_bq, grp.bk)[:, :grp.t]
    return o, g, ([k, v], ki)


def _run_trunk(x, p, caches, past):
    depth = p["ln_g"].shape[0]
    alpha = (2 * depth) ** 0.25
    grp = _Group(x, past)
    n_a, n_c = (depth + 2) // 3, depth // 3
    kv_a, kv_c = None, None
    new_a, new_b, new_c = [], [], []
    for i in range(depth):
        j, kind = i // 3, i % 3
        if kind == 0:
            cache = None if caches is None else ((caches["a_k"], j), (caches["a_v"], j), caches["a_logf"][j])
            o, g, (kv_a, logf) = _mixer_a(grp, x, p["w_main_a"][j], p["w_f_a"][j], p["b_f_a"][j], cache,
                                          (j, n_a), kv_a)
            new_a.append(logf)
            w_out = p["w_out_a"][j]
        elif kind == 1:
            state = None if caches is None else (caches["b_conv"][j], caches["b_h"][j])
            o, g, st = _mixer_b(grp, x, p["w_in_b"][j], p["conv_w_b"][j], p["conv_b_b"][j], p["w_rg_b"][j],
                                p["b_rg_b"][j], p["w_ig_b"][j], p["b_ig_b"][j], p["lam_b"][j], state)
            new_b.append(st)
            w_out = p["w_out_b"][j]
        else:
            cache = None if caches is None else ((caches["c_k"], j), (caches["c_v"], j), caches["c_kidx"][j])
            o, g, (kv_c, ki) = _mixer_c(grp, x, p["w_main_c"][j], p["w_idx_c"][j], p["h_c"], p["h_idx"], p["d_idx"],
                                        cache, (j, n_c), kv_c)
            new_c.append(ki)
            w_out = p["w_out_c"][j]
        m = grp.b * grp.t
        x = _outproj_ln(o.reshape(m, -1), g.reshape(m, -1), x.reshape(m, grp.d), w_out,
                        p["ln_g"][i], p["ln_b"][i], alpha).reshape(grp.b, grp.t, grp.d)
    stack = lambda sts, n: jnp.stack([s[n] for s in sts])
    heads = lambda a, nh: a.reshape(a.shape[0], grp.b, grp.t, nh, a.shape[-1] * a.shape[-2] // nh)
    h_a = p["b_f_a"].shape[1]
    return (x, heads(kv_a[0], h_a), heads(kv_a[1], h_a), jnp.stack(new_a), stack(new_b, 0), stack(new_b, 1),
            heads(kv_c[0], p["h_c"]), heads(kv_c[1], p["h_c"]), jnp.stack(new_c))


def kernel(x_prompt, x_sample, cache_a_k, cache_a_v, cache_a_logf, state_b_conv, state_b_h, cache_c_k, cache_c_v, cache_c_kidx, w_in_a, b_f_a, w_out_a, w_in_b, conv_w_b, conv_b_b, w_rg_b, b_rg_b, w_ig_b, b_ig_b, lam_b, w_out_b, w_in_c, w_out_c, ln_g, ln_b):
    h_a = b_f_a.shape[1]
    d_a = w_out_a.shape[1]
    d_c = w_out_c.shape[1]
    h_c = cache_c_k.shape[3]
    d_idx = cache_c_kidx.shape[-1]
    h_idx = (w_in_c.shape[2] - 4 * d_c - d_idx) // (d_idx + 1)
    assert d_a // h_a == LANES and d_c // h_c == LANES, "head width must equal the lane count"
    assert w_in_a.shape[2] == 4 * d_a + h_a and d_idx + h_idx <= LANES
    past = cache_a_k.shape[2]
    assert past % CHUNK == 0 and past > 0

    w_idx = w_in_c[:, :, 4 * d_c:]
    w_idx = _pad_to(w_idx, 2, h_idx * d_idx + LANES)
    p = {"w_main_a": w_in_a[:, :, :4 * d_a].astype(BF16),
         "w_f_a": _pad_to(w_in_a[:, :, 4 * d_a:], 2, LANES).astype(BF16),
         "b_f_a": b_f_a, "w_out_a": w_out_a.astype(BF16),
         "w_in_b": w_in_b.astype(BF16), "conv_w_b": conv_w_b, "conv_b_b": conv_b_b, "w_rg_b": w_rg_b,
         "b_rg_b": b_rg_b, "w_ig_b": w_ig_b, "b_ig_b": b_ig_b, "lam_b": lam_b, "w_out_b": w_out_b.astype(BF16),
         "w_main_c": w_in_c[:, :, :4 * d_c].astype(BF16), "w_idx_c": w_idx.astype(BF16),
         "w_out_c": w_out_c.astype(BF16), "ln_g": ln_g, "ln_b": ln_b,
         "h_c": h_c, "h_idx": h_idx, "d_idx": d_idx}
    caches = {"a_k": cache_a_k, "a_v": cache_a_v, "a_logf": cache_a_logf, "b_conv": state_b_conv,
              "b_h": state_b_h, "c_k": cache_c_k, "c_v": cache_c_v, "c_kidx": cache_c_kidx}
    outs_p = _run_trunk(x_prompt, p, None, 0)
    outs_s = _run_trunk(x_sample, p, caches, past)
    return (outs_p[0], outs_s[0]) + outs_p[1:] + outs_s[1:]
```
